```python
import math
import jax
import jax.numpy as jnp
from jax import lax
import numpy as np

D_MODEL = 1024
BATCH = 4
SEQ = 4096
DEPTH = 4

N_MIXERS = 4
N_META = 16
BLOCK = 128
PAD = BLOCK - N_META
ALPHA = (2.0 * DEPTH) ** 0.25
BETA = (8.0 * DEPTH) ** -0.25
LN_EPS = 1e-5
RMS_EPS = 1e-6
NEG_INF = -1e30

D_FF = 2816

SSD_D_INNER = 2 * D_MODEL
SSD_HEAD_DIM = 64
SSD_HEADS = SSD_D_INNER // SSD_HEAD_DIM
SSD_GROUPS = 8
SSD_STATE = 128
SSD_CONV = 4
SSD_CONV_CH = SSD_D_INNER + 2 * SSD_GROUPS * SSD_STATE
SSD_IN = SSD_D_INNER + SSD_CONV_CH + SSD_HEADS

DIFF_HEADS = 8
DIFF_HEAD_DIM = D_MODEL // (2 * DIFF_HEADS)
DIFF_V_DIM = 2 * DIFF_HEAD_DIM

REL_BUCKETS = 32
REL_MAX_DIST = 128

S5_GROUP = 16
S5_GROUPS = D_MODEL // S5_GROUP
S5_STATE = 64

MLA_HEADS = 16
MLA_Q_RANK = 384
MLA_KV_RANK = 256
MLA_NOPE = 64
MLA_ROPE = 32
MLA_V = 64
MLA_IN = MLA_Q_RANK + MLA_KV_RANK + MLA_ROPE
ROPE_BASE = 10000.0

N_SSD = (DEPTH + N_MIXERS - 1) // N_MIXERS
N_DIFF = (DEPTH + N_MIXERS - 2) // N_MIXERS
N_S5 = (DEPTH + N_MIXERS - 3) // N_MIXERS
N_MLA = (DEPTH + N_MIXERS - 4) // N_MIXERS

kernel_name = "hybrid_ssd_diffattn_s5_mla_trunk"


def layer_norm(x, g, b):
    xf = x.astype(jnp.float32)
    mu = jnp.mean(xf, -1, keepdims=True)
    var = jnp.mean(jnp.square(xf - mu), -1, keepdims=True)
    return ((xf - mu) * lax.rsqrt(var + LN_EPS) * g + b).astype(x.dtype)


def rms_norm(x, g):
    xf = x.astype(jnp.float32)
    return (xf * lax.rsqrt(jnp.mean(xf * xf, -1, keepdims=True) + RMS_EPS) * g).astype(x.dtype)


def swiglu(x, w1, w3, w2):
    return (jax.nn.silu(x @ w1) * (x @ w3)) @ w2


def front_pad(t):
    pad = [(0, 0)] * t.ndim
    pad[1] = (PAD, 0)
    return jnp.pad(t, pad)


def block_mask(blk, n_keys):
    q_pos = blk * BLOCK + jnp.arange(BLOCK)
    k_pos = jnp.arange(n_keys)
    valid = (k_pos[None, :] <= q_pos[:, None]) & (k_pos[None, :] >= PAD)
    return q_pos, k_pos, valid


def t5_bucket(dist):
    max_exact = REL_BUCKETS // 2
    d = jnp.maximum(dist, 0)
    df = jnp.maximum(d, max_exact).astype(jnp.float32)
    large = max_exact + (jnp.log(df / max_exact) / math.log(REL_MAX_DIST / max_exact)
                         * (REL_BUCKETS - max_exact)).astype(jnp.int32)
    large = jnp.minimum(large, REL_BUCKETS - 1)
    return jnp.where(d < max_exact, d, large)


def ssd_chunked(x, dt, A, Bm, Cm):
    x, dt, Bm, Cm = front_pad(x), front_pad(dt), front_pad(Bm), front_pad(Cm)
    b, lp, h, p = x.shape
    g, n = Bm.shape[2], Bm.shape[3]
    r = h // g
    c = lp // BLOCK
    a = (dt * A).reshape(b, c, BLOCK, g, r)
    xdt = (x * dt[..., None]).reshape(b, c, BLOCK, g, r, p)
    Bm = Bm.reshape(b, c, BLOCK, g, n).astype(xdt.dtype)
    Cm = Cm.reshape(b, c, BLOCK, g, n).astype(xdt.dtype)
    a_cum = jnp.cumsum(a, axis=2)
    seg = a_cum[:, :, :, None] - a_cum[:, :, None, :]
    causal = jnp.tril(jnp.ones((BLOCK, BLOCK), bool))[:, :, None, None]
    decay = jnp.exp(jnp.where(causal, seg, -jnp.inf))
    cb = jnp.einsum('bctgn,bcsgn->bctsg', Cm, Bm)
    y_diag = jnp.einsum('bctsgr,bcsgrp->bctgrp', cb[..., None] * decay, xdt)
    decay_end = jnp.exp(a_cum[:, :, -1:] - a_cum)
    states = jnp.einsum('bcsgn,bcsgrp->bcgrpn', Bm, xdt * decay_end[..., None])
    chunk_decay = jnp.exp(a_cum[:, :, -1])

    def step(s, inp):
        dec, st = inp
        return s * dec[..., None, None] + st, s

    _, s_in = lax.scan(step, jnp.zeros_like(states[:, 0]),
                       (jnp.moveaxis(chunk_decay, 1, 0).astype(states.dtype), jnp.moveaxis(states, 1, 0)))
    s_in = jnp.moveaxis(s_in, 0, 1)
    y_off = jnp.einsum('bctgn,bcgrpn->bctgrp', Cm, s_in) * jnp.exp(a_cum)[..., None]
    y = (y_diag + y_off).reshape(b, lp, h, p)
    return y[:, PAD:]


def mamba2_mixer(x, w_in, conv_w, conv_b, dt_bias, a_log, d_skip, norm_g, w_out):
    b, l, _ = x.shape
    zxbcdt = x @ w_in
    z, xbc, dt = jnp.split(zxbcdt, [SSD_D_INNER, SSD_D_INNER + SSD_CONV_CH], axis=-1)
    xbc = lax.conv_general_dilated(xbc, conv_w[:, None, :].astype(xbc.dtype), window_strides=(1,),
                                   padding=[(SSD_CONV - 1, 0)],
                                   dimension_numbers=('NWC', 'WIO', 'NWC'),
                                   feature_group_count=SSD_CONV_CH) + conv_b
    xbc = jax.nn.silu(xbc)
    xs, Bm, Cm = jnp.split(xbc, [SSD_D_INNER, SSD_D_INNER + SSD_GROUPS * SSD_STATE], axis=-1)
    dt = jax.nn.softplus((dt + dt_bias).astype(jnp.float32))
    A = -jnp.exp(a_log.astype(jnp.float32))
    xs = xs.reshape(b, l, SSD_HEADS, SSD_HEAD_DIM)
    y = ssd_chunked(xs, dt, A, Bm.reshape(b, l, SSD_GROUPS, SSD_STATE),
                    Cm.reshape(b, l, SSD_GROUPS, SSD_STATE))
    y = (y + d_skip[:, None] * xs).astype(x.dtype).reshape(b, l, SSD_D_INNER) * jax.nn.silu(z)
    gs = SSD_D_INNER // SSD_GROUPS
    y = rms_norm(y.reshape(b, l, SSD_GROUPS, gs), norm_g.reshape(SSD_GROUPS, gs)).reshape(b, l, SSD_D_INNER)
    return y @ w_out


def diff_attention(x, w_qkv, lam_q1, lam_k1, lam_q2, lam_k2, subln_g, w_out, rel_bias, lam_init):
    b, l, _ = x.shape
    h, d, e = DIFF_HEADS, DIFF_HEAD_DIM, DIFF_V_DIM
    qkv = front_pad(x @ w_qkv)
    lp = l + PAD
    q, k, v = jnp.split(qkv, [2 * h * d, 4 * h * d], axis=-1)
    q = q.reshape(b, lp, h, 2, d) * (d ** -0.5)
    k = k.reshape(b, lp, h, 2, d)
    v = v.reshape(b, lp, h, e)
    f32 = jnp.float32
    lam = (jnp.exp(jnp.sum(lam_q1.astype(f32) * lam_k1.astype(f32)))
           - jnp.exp(jnp.sum(lam_q2.astype(f32) * lam_k2.astype(f32))) + lam_init)
    nb = lp // BLOCK
    q_blocks = jnp.moveaxis(q.reshape(b, nb, BLOCK, h, 2, d), 1, 0)

    def one_block(args):
        qb, blk = args
        q_pos, k_pos, valid = block_mask(blk, lp)
        bias = rel_bias[t5_bucket(q_pos[:, None] - k_pos[None, :])]
        s = jnp.einsum('bqhmd,bkhmd->bhmqk', qb, k).astype(f32) \
            + jnp.transpose(bias, (2, 0, 1))[None, :, None].astype(f32)
        s = jnp.where(valid, s, NEG_INF)
        p = jax.nn.softmax(s, axis=-1)
        att = p[:, :, 0] - lam * p[:, :, 1]
        return jnp.einsum('bhqk,bkhe->bqhe', att.astype(v.dtype), v)

    o = lax.map(one_block, (q_blocks, jnp.arange(nb)))
    o = jnp.moveaxis(o, 0, 1).reshape(b, lp, h, e)[:, PAD:]
    o = rms_norm(o, subln_g) * (1.0 - lam_init)
    return o.reshape(b, l, h * e) @ w_out


def s5_mixer(x, lam_re, lam_im, log_step, b_re, b_im, c_re, c_im, d_skip, w_glu, b_glu):
    bsz, l, _ = x.shape
    f32 = jnp.float32
    u = x.reshape(bsz, l, S5_GROUPS, S5_GROUP).astype(f32)
    step = jnp.exp(log_step.astype(f32))[:, None]
    lr = jnp.minimum(lam_re.astype(f32), -1e-4)
    li = lam_im.astype(f32)
    mag = jnp.exp(lr * step)
    ar, ai = mag * jnp.cos(li * step), mag * jnp.sin(li * step)
    den = lr * lr + li * li
    cr = ((ar - 1.0) * lr + ai * li) / den
    ci = (ai * lr - (ar - 1.0) * li) / den
    br, bi = b_re.astype(f32), b_im.astype(f32)
    bbr = cr[..., None] * br - ci[..., None] * bi
    bbi = cr[..., None] * bi + ci[..., None] * br
    bu_r = jnp.einsum('gpc,blgc->lbgp', bbr, u)
    bu_i = jnp.einsum('gpc,blgc->lbgp', bbi, u)
    a_r = jnp.broadcast_to(ar[None, None], (l, 1) + ar.shape)
    a_i = jnp.broadcast_to(ai[None, None], (l, 1) + ai.shape)

    def combine(e1, e2):
        a1r, a1i, b1r, b1i = e1
        a2r, a2i, b2r, b2i = e2
        return (a2r * a1r - a2i * a1i, a2r * a1i + a2i * a1r,
                a2r * b1r - a2i * b1i + b2r, a2r * b1i + a2i * b1r + b2i)

    _, _, sr, si = lax.associative_scan(combine, (a_r, a_i, bu_r, bu_i), axis=0)
    y = (jnp.einsum('gcp,lbgp->blgc', c_re.astype(f32), sr)
         - jnp.einsum('gcp,lbgp->blgc', c_im.astype(f32), si)
         + d_skip.astype(f32) * u)
    y = jax.nn.gelu(y.reshape(bsz, l, D_MODEL)).astype(x.dtype)
    val, gate = jnp.split(y @ w_glu + b_glu, 2, axis=-1)
    return val * jax.nn.sigmoid(gate)


def apply_rope(t, cos, sin):
    t1, t2 = jnp.split(t, 2, axis=-1)
    return jnp.concatenate([t1 * cos - t2 * sin, t1 * sin + t2 * cos], axis=-1)


def mla_mixer(x, w_in, q_norm_g, kv_norm_g, w_uq, w_ukv, w_out):
    b, l, _ = x.shape
    h = MLA_HEADS
    f32 = jnp.float32
    c_q, c_kv, k_r = jnp.split(x @ w_in, [MLA_Q_RANK, MLA_Q_RANK + MLA_KV_RANK], axis=-1)
    q = (rms_norm(c_q, q_norm_g) @ w_uq).reshape(b, l, h, MLA_NOPE + MLA_ROPE)
    kv = (rms_norm(c_kv, kv_norm_g) @ w_ukv).reshape(b, l, h, MLA_NOPE + MLA_V)
    q_n, q_r = jnp.split(q, [MLA_NOPE], axis=-1)
    k_n, v = jnp.split(kv, [MLA_NOPE], axis=-1)
    pos = jnp.arange(l, dtype=f32)
    inv_freq = ROPE_BASE ** (-jnp.arange(0, MLA_ROPE, 2, dtype=f32) / MLA_ROPE)
    ang = pos[:, None] * inv_freq[None, :]
    cos, sin = jnp.cos(ang).astype(x.dtype), jnp.sin(ang).astype(x.dtype)
    q_r = apply_rope(q_r, cos[:, None], sin[:, None])
    k_r = apply_rope(k_r, cos, sin)
    scale = (MLA_NOPE + MLA_ROPE) ** -0.5
    q_n, q_r, k_n, k_r, v = (front_pad(t) for t in (q_n, q_r, k_n, k_r, v))
    lp = l + PAD
    nb = lp // BLOCK
    qn_blocks = jnp.moveaxis(q_n.reshape(b, nb, BLOCK, h, MLA_NOPE), 1, 0)
    qr_blocks = jnp.moveaxis(q_r.reshape(b, nb, BLOCK, h, MLA_ROPE), 1, 0)

    def one_block(args):
        qnb, qrb, blk = args
        _, _, valid = block_mask(blk, lp)
        s = (jnp.einsum('bqhd,bkhd->bhqk', qnb, k_n)
             + jnp.einsum('bqhr,bkr->bhqk', qrb, k_r)).astype(f32) * scale
        s = jnp.where(valid, s, NEG_INF)
        p = jax.nn.softmax(s, axis=-1).astype(v.dtype)
        return jnp.einsum('bhqk,bkhe->bqhe', p, v)

    o = lax.map(one_block, (qn_blocks, qr_blocks, jnp.arange(nb)))
    o = jnp.moveaxis(o, 0, 1).reshape(b, lp, h * MLA_V)[:, PAD:]
    return o @ w_out


def setup_inputs(seed: int = 0) -> dict:
    key = jax.random.key(seed)
    ks = iter(jax.random.split(key, 64))
    f32 = jnp.float32

    def nrm(shape, scale):
        return jax.random.normal(next(ks), shape, f32) * scale

    def unif(shape, lo, hi):
        return jax.random.uniform(next(ks), shape, f32, lo, hi)

    D, F = D_MODEL, D_FF
    inp = {}
    inp['x'] = nrm((BATCH, SEQ, D), 1.0)
    inp['meta'] = nrm((N_META, D), 1.0)
    inp['rel_bias'] = nrm((REL_BUCKETS, DIFF_HEADS), 0.5)
    inp['ln_g'] = 1.0 + nrm((DEPTH, 3, D), 0.02)
    inp['ln_b'] = nrm((DEPTH, 3, D), 0.02)
    inp['ffn_w1'] = nrm((DEPTH, 2, D, F), D ** -0.5)
    inp['ffn_w3'] = nrm((DEPTH, 2, D, F), D ** -0.5)
    inp['ffn_w2'] = nrm((DEPTH, 2, F, D), BETA * F ** -0.5)
    inp['ssd_w_in'] = nrm((N_SSD, D, SSD_IN), D ** -0.5)
    inp['ssd_conv_w'] = nrm((N_SSD, SSD_CONV, SSD_CONV_CH), SSD_CONV ** -0.5)
    inp['ssd_conv_b'] = nrm((N_SSD, SSD_CONV_CH), 0.02)
    dt0 = jnp.exp(unif((N_SSD, SSD_HEADS), math.log(1e-3), math.log(1e-1)))
    inp['ssd_dt_bias'] = dt0 + jnp.log(-jnp.expm1(-dt0))
    inp['ssd_a_log'] = jnp.log(unif((N_SSD, SSD_HEADS), 1.0, 16.0))
    inp['ssd_d'] = 1.0 + nrm((N_SSD, SSD_HEADS), 0.02)
    inp['ssd_norm_g'] = 1.0 + nrm((N_SSD, SSD_D_INNER), 0.02)
    inp['ssd_w_out'] = nrm((N_SSD, SSD_D_INNER, D), BETA * SSD_D_INNER ** -0.5)
    inp['diff_w_qkv'] = nrm((N_DIFF, D, 3 * D), D ** -0.5)
    inp['diff_lam_q1'] = nrm((N_DIFF, DIFF_HEAD_DIM), 0.1)
    inp['diff_lam_k1'] = nrm((N_DIFF, DIFF_HEAD_DIM), 0.1)
    inp['diff_lam_q2'] = nrm((N_DIFF, DIFF_HEAD_DIM), 0.1)
    inp['diff_lam_k2'] = nrm((N_DIFF, DIFF_HEAD_DIM), 0.1)
    inp['diff_subln_g'] = 1.0 + nrm((N_DIFF, DIFF_V_DIM), 0.02)
    inp['diff_w_out'] = nrm((N_DIFF, D, D), BETA * D ** -0.5)
    inp['s5_lam_re'] = -0.5 + nrm((N_S5, S5_GROUPS, S5_STATE), 0.01)
    inp['s5_lam_im'] = jnp.broadcast_to(math.pi * jnp.arange(S5_STATE, dtype=f32),
                                        (N_S5, S5_GROUPS, S5_STATE))
    inp['s5_log_step'] = unif((N_S5, S5_GROUPS), math.log(1e-3), math.log(1e-1))
    inp['s5_b_re'] = nrm((N_S5, S5_GROUPS, S5_STATE, S5_GROUP), (2 * S5_GROUP) ** -0.5)
    inp['s5_b_im'] = nrm((N_S5, S5_GROUPS, S5_STATE, S5_GROUP), (2 * S5_GROUP) ** -0.5)
    inp['s5_c_re'] = nrm((N_S5, S5_GROUPS, S5_GROUP, S5_STATE), S5_STATE ** -0.5)
    inp['s5_c_im'] = nrm((N_S5, S5_GROUPS, S5_GROUP, S5_STATE), S5_STATE ** -0.5)
    inp['s5_d'] = nrm((N_S5, S5_GROUPS, S5_GROUP), 1.0)
    inp['s5_w_glu'] = nrm((N_S5, D, 2 * D), BETA * D ** -0.5)
    inp['s5_b_glu'] = nrm((N_S5, 2 * D), 0.02)
    inp['mla_w_in'] = nrm((N_MLA, D, MLA_IN), D ** -0.5)
    inp['mla_q_norm_g'] = 1.0 + nrm((N_MLA, MLA_Q_RANK), 0.02)
    inp['mla_kv_norm_g'] = 1.0 + nrm((N_MLA, MLA_KV_RANK), 0.02)
    inp['mla_w_uq'] = nrm((N_MLA, MLA_Q_RANK, MLA_HEADS * (MLA_NOPE + MLA_ROPE)), MLA_Q_RANK ** -0.5)
    inp['mla_w_ukv'] = nrm((N_MLA, MLA_KV_RANK, MLA_HEADS * (MLA_NOPE + MLA_V)), MLA_KV_RANK ** -0.5)
    inp['mla_w_out'] = nrm((N_MLA, MLA_HEADS * MLA_V, D), BETA * (MLA_HEADS * MLA_V) ** -0.5)
    return inp


def reference(x, meta, rel_bias, ln_g, ln_b, ffn_w1, ffn_w3, ffn_w2,
              ssd_w_in, ssd_conv_w, ssd_conv_b, ssd_dt_bias, ssd_a_log, ssd_d, ssd_norm_g, ssd_w_out,
              diff_w_qkv, diff_lam_q1, diff_lam_k1, diff_lam_q2, diff_lam_k2, diff_subln_g, diff_w_out,
              s5_lam_re, s5_lam_im, s5_log_step, s5_b_re, s5_b_im, s5_c_re, s5_c_im, s5_d, s5_w_glu, s5_b_glu,
              mla_w_in, mla_q_norm_g, mla_kv_norm_g, mla_w_uq, mla_w_ukv, mla_w_out):
    b = x.shape[0]
    h = jnp.concatenate([jnp.broadcast_to(meta[None].astype(x.dtype), (b,) + meta.shape), x], axis=1)
    for i in range(DEPTH):
        kind, j = i % N_MIXERS, i // N_MIXERS
        h = layer_norm(ALPHA * h + 0.5 * swiglu(h, ffn_w1[i, 0], ffn_w3[i, 0], ffn_w2[i, 0]),
                       ln_g[i, 0], ln_b[i, 0])
        if kind == 0:
            m = mamba2_mixer(h, ssd_w_in[j], ssd_conv_w[j], ssd_conv_b[j], ssd_dt_bias[j],
                             ssd_a_log[j], ssd_d[j], ssd_norm_g[j], ssd_w_out[j])
        elif kind == 1:
            m = diff_attention(h, diff_w_qkv[j], diff_lam_q1[j], diff_lam_k1[j], diff_lam_q2[j],
                               diff_lam_k2[j], diff_subln_g[j], diff_w_out[j], rel_bias,
                               0.8 - 0.6 * math.exp(-0.3 * i))
        elif kind == 2:
            m = s5_mixer(h, s5_lam_re[j], s5_lam_im[j], s5_log_step[j], s5_b_re[j], s5_b_im[j],
                         s5_c_re[j], s5_c_im[j], s5_d[j], s5_w_glu[j], s5_b_glu[j])
        else:
            m = mla_mixer(h, mla_w_in[j], mla_q_norm_g[j], mla_kv_norm_g[j], mla_w_uq[j],
                          mla_w_ukv[j], mla_w_out[j])
        h = layer_norm(ALPHA * h + m.astype(h.dtype), ln_g[i, 1], ln_b[i, 1])
        h = layer_norm(ALPHA * h + 0.5 * swiglu(h, ffn_w1[i, 1], ffn_w3[i, 1], ffn_w2[i, 1]),
                       ln_g[i, 2], ln_b[i, 2])
    return h[:, N_META:]
```

```python
import functools
import math

import jax
import jax.numpy as jnp
from jax import lax
from jax.experimental import pallas as pl
from jax.experimental.pallas import tpu as pltpu

F32 = jnp.float32
BF16 = jnp.bfloat16

N_META = 16
DEPTH = 4
ALPHA = (2.0 * DEPTH) ** 0.25
LN_EPS = 1e-5
RMS_EPS = 1e-6
NEG_INF = -1e30

SSD_HEAD_DIM = 64
SSD_HEADS = 32
SSD_GROUPS = 8
SSD_STATE = 128
SSD_CONV = 4
SSD_CHUNK = 128

DIFF_HEADS = 8
DIFF_HEAD_DIM = 64
DIFF_V_DIM = 128
REL_BUCKETS = 32
REL_MAX_DIST = 128

S5_GROUP = 16
S5_GROUPS = 64
S5_STATE = 64
S5_SLICE_GROUPS = 16
S5_SUBLANES = 8
S5_TCHUNK = 32

MLA_HEADS = 16
MLA_Q_RANK = 384
MLA_KV_RANK = 256
MLA_NOPE = 64
MLA_ROPE = 32
MLA_V = 64
ROPE_BASE = 10000.0

LANES = 128
SEQ_ALIGN = 128
VMEM_LIMIT = 56 * 1024 * 1024


def _pick_tile(n, candidates):
    for c in candidates:
        if n % c == 0:
            return c
    raise ValueError(f"no tile for {n}")


def _layer_norm(z, g, b):
    mu = jnp.mean(z, axis=-1, keepdims=True)
    d = z - mu
    var = jnp.mean(d * d, axis=-1, keepdims=True)
    return d * lax.rsqrt(var + LN_EPS) * g + b


def _rms(z, g):
    return z * lax.rsqrt(jnp.mean(z * z, axis=-1, keepdims=True) + RMS_EPS) * g


def _silu(x):
    return x * (1.0 / (1.0 + jnp.exp(-x)))


def _dot(a, b):
    return jnp.dot(a, b, preferred_element_type=F32)


def _dot_nt(a, b):
    return lax.dot_general(a, b, (((1,), (1,)), ((), ())), preferred_element_type=F32)


def _row_call(body, row_ins, const_ins, outs, tm, name, pos_ins=(), lp=None):
    n = row_ins[0].shape[0]
    assert n % tm == 0
    in_specs = [pl.BlockSpec((tm, a.shape[1]), lambda i: (i, 0)) for a in row_ins]
    if pos_ins:
        assert lp % tm == 0
        nb = lp // tm
        in_specs += [pl.BlockSpec((tm, a.shape[1]), lambda i: (i % nb, 0)) for a in pos_ins]
    in_specs += [pl.BlockSpec(c.shape, lambda i: (0, 0), pipeline_mode=pl.Buffered(1)) for c in const_ins]
    out_specs = [pl.BlockSpec((tm, w), lambda i: (i, 0)) for (w, _) in outs]
    out_shape = [jax.ShapeDtypeStruct((n, w), dt) for (w, dt) in outs]
    res = pl.pallas_call(
        body, grid=(n // tm,), in_specs=in_specs, out_specs=out_specs, out_shape=out_shape, name=name,
        compiler_params=pltpu.CompilerParams(dimension_semantics=("parallel",), vmem_limit_bytes=VMEM_LIMIT),
    )(*row_ins, *pos_ins, *const_ins)
    return res


def _ffn_body(n_chunks, h_ref, w1_ref, w3_ref, w2_ref, g_ref, b_ref, o_ref):
    x = h_ref[...]
    xb = x.astype(BF16)
    fc = w1_ref.shape[1] // n_chunks
    y = None
    for c in range(n_chunks):
        sl = slice(c * fc, (c + 1) * fc)
        gate = _dot(xb, w1_ref[:, sl])
        up = _dot(xb, w3_ref[:, sl])
        a = (_silu(gate) * up).astype(BF16)
        part = _dot(a, w2_ref[sl, :])
        y = part if y is None else y + part
    o_ref[...] = _layer_norm(ALPHA * x + 0.5 * y, g_ref[...], b_ref[...])


def _ffn(h2, w1, w3, w2, g, b):
    n, d = h2.shape
    tm = _pick_tile(n, (512, 384, 256, 128))
    f = w1.shape[1]
    n_chunks = 2 if f % (2 * LANES) == 0 else 1
    (out,) = _row_call(functools.partial(_ffn_body, n_chunks), [h2],
                       [w1.astype(BF16), w3.astype(BF16), w2.astype(BF16), g.reshape(1, d), b.reshape(1, d)],
                       [(d, F32)], tm, "ffn")
    return out


def _proj_ln_body(y_ref, h_ref, w_ref, g_ref, b_ref, o_ref):
    m = _dot(y_ref[...], w_ref[...])
    o_ref[...] = _layer_norm(ALPHA * h_ref[...] + m, g_ref[...], b_ref[...])


def _proj_ln(y2, h2, w, g, b):
    n, d = h2.shape
    tm = _pick_tile(n, (512, 384, 256, 128))
    (out,) = _row_call(_proj_ln_body, [y2, h2], [w.astype(BF16), g.reshape(1, d), b.reshape(1, d)],
                       [(d, F32)], tm, "proj_ln")
    return out


def _ssd_in_body(h_ref, w_ref, xbc_ref, z_ref, dt_ref):
    r = _dot(h_ref[...].astype(BF16), w_ref[...])
    nx, nz = xbc_ref.shape[1], z_ref.shape[1]
    xbc_ref[...] = r[:, :nx]
    z_ref[...] = r[:, nx:nx + nz]
    dt_ref[...] = r[:, nx + nz:]


def _softplus(x):
    return jnp.maximum(x, 0.0) + jnp.log1p(jnp.exp(-jnp.abs(x)))


def _ssd_body(xbc_ref, dt_ref, z_ref, convw_ref, convb_ref, dtb_ref, alog_ref, dskip_ref, normg_ref,
              y_ref, buf_scr, state_scr, y_scr):
    t = SSD_CHUNK
    d_inner = SSD_HEADS * SSD_HEAD_DIM
    gn = SSD_GROUPS * SSD_STATE
    hp = SSD_HEAD_DIM
    r = SSD_HEADS // SSD_GROUPS

    @pl.when(pl.program_id(1) == 0)
    def _():
        buf_scr[0:8, :] = jnp.zeros((8, buf_scr.shape[1]), F32)
        state_scr[...] = jnp.zeros(state_scr.shape, F32)

    x = xbc_ref[...]
    buf_scr[8:8 + t, :] = x
    acc = convb_ref[...] + convw_ref[SSD_CONV - 1:SSD_CONV, :] * x
    for j in range(1, SSD_CONV):
        acc = acc + convw_ref[SSD_CONV - 1 - j:SSD_CONV - j, :] * buf_scr[8 - j:8 - j + t, :]
    buf_scr[0:8, :] = x[t - 8:t, :]
    xbc = _silu(acc)

    dt = _softplus(dt_ref[...] + dtb_ref[...])
    a = dt * (-jnp.exp(alog_ref[...]))
    rows = lax.broadcasted_iota(jnp.int32, (t, t), 0)
    cols = lax.broadcasted_iota(jnp.int32, (t, t), 1)
    tri = rows >= cols
    a_cum = jnp.dot(tri.astype(F32), a, precision=lax.Precision.HIGHEST, preferred_element_type=F32)
    a_cum_t = a_cum.T
    a_last = a_cum[t - 1:t, :]
    exp_cum = jnp.exp(a_cum)
    dec_end = jnp.exp(a_last - a_cum)
    chunk_dec = jnp.exp(a_last)

    for g in range(SSD_GROUPS):
        b_f = xbc[:, d_inner + g * SSD_STATE:d_inner + (g + 1) * SSD_STATE]
        c_g = xbc[:, d_inner + gn + g * SSD_STATE:d_inner + gn + (g + 1) * SSD_STATE].astype(BF16)
        b_g = b_f.astype(BF16)
        b_gt = b_f.T.astype(BF16)
        cb = _dot_nt(c_g, b_g)
        for k in range(r):
            h = g * r + k
            col = a_cum[:, h:h + 1]
            row = a_cum_t[h:h + 1, :]
            decay = jnp.exp(jnp.where(tri, col - row, NEG_INF))
            xh = xbc[:, h * hp:(h + 1) * hp]
            xdt = xh * dt[:, h:h + 1]
            st = state_scr[:, h * hp:(h + 1) * hp]
            y_diag = _dot((cb * decay).astype(BF16), xdt.astype(BF16))
            y_off = _dot(c_g, st.astype(BF16)) * exp_cum[:, h:h + 1]
            y_scr[:, h * hp:(h + 1) * hp] = y_diag + y_off + dskip_ref[:, h * hp:(h + 1) * hp] * xh
            xw = (xdt * dec_end[:, h:h + 1]).astype(BF16)
            state_scr[:, h * hp:(h + 1) * hp] = st * chunk_dec[:, h:h + 1] + _dot(b_gt, xw)

    y = y_scr[...] * _silu(z_ref[...])
    gs = d_inner // SSD_GROUPS
    for g in range(SSD_GROUPS):
        sl = slice(g * gs, (g + 1) * gs)
        y_ref[:, sl] = _rms(y[:, sl], normg_ref[:, sl]).astype(y_ref.dtype)


def _ssd_mixer(h3, w_in, conv_w, conv_b, dt_bias, a_log, d_skip, norm_g):
    b, lp, d = h3.shape
    n = b * lp
    d_inner = SSD_HEADS * SSD_HEAD_DIM
    conv_ch = d_inner + 2 * SSD_GROUPS * SSD_STATE
    w_z, w_xbc, w_dt = w_in[:, :d_inner], w_in[:, d_inner:d_inner + conv_ch], w_in[:, d_inner + conv_ch:]
    w_dt = jnp.pad(w_dt, ((0, 0), (0, LANES - SSD_HEADS)))
    w_all = jnp.concatenate([w_xbc, w_z, w_dt], axis=1).astype(BF16)
    tm = _pick_tile(n, (512, 384, 256, 128))
    xbc, z, dt = _row_call(_ssd_in_body, [h3.reshape(n, d)], [w_all],
                           [(conv_ch, F32), (d_inner, F32), (LANES, F32)], tm, "ssd_in")
    t = SSD_CHUNK
    pad1 = lambda v: jnp.pad(v, (0, LANES - SSD_HEADS)).reshape(1, LANES)
    consts = [conv_w, conv_b.reshape(1, conv_ch), pad1(dt_bias), pad1(a_log),
              jnp.repeat(d_skip, SSD_HEAD_DIM).reshape(1, d_inner), norm_g.reshape(1, d_inner)]
    row_spec = lambda w: pl.BlockSpec((None, t, w), lambda i, c: (i, c, 0))
    y = pl.pallas_call(
        _ssd_body, grid=(b, lp // t),
        in_specs=[row_spec(conv_ch), row_spec(LANES), row_spec(d_inner)]
        + [pl.BlockSpec(c.shape, lambda i, c_: (0, 0)) for c in consts],
        out_specs=row_spec(d_inner),
        out_shape=jax.ShapeDtypeStruct((b, lp, d_inner), BF16),
        scratch_shapes=[pltpu.VMEM((8 + t, conv_ch), F32), pltpu.VMEM((SSD_STATE, d_inner), F32),
                        pltpu.VMEM((t, d_inner), F32)],
        name="ssd_scan",
        compiler_params=pltpu.CompilerParams(dimension_semantics=("parallel", "arbitrary"),
                                             vmem_limit_bytes=VMEM_LIMIT),
    )(xbc.reshape(b, lp, conv_ch), dt.reshape(b, lp, LANES), z.reshape(b, lp, d_inner), *consts)
    return y.reshape(n, d_inner)


def _qkv_body(n_q, scale, h_ref, w_ref, o_ref):
    r = _dot(h_ref[...].astype(BF16), w_ref[...])
    o_ref[:, :n_q] = (r[:, :n_q] * scale).astype(o_ref.dtype)
    o_ref[:, n_q:] = r[:, n_q:].astype(o_ref.dtype)


def _t5_bucket(dist):
    max_exact = REL_BUCKETS // 2
    d = jnp.maximum(dist, 0)
    df = jnp.maximum(d, max_exact).astype(F32)
    large = max_exact + (jnp.log(df / max_exact) / math.log(REL_MAX_DIST / max_exact)
                         * (REL_BUCKETS - max_exact)).astype(jnp.int32)
    large = jnp.minimum(large, REL_BUCKETS - 1)
    return jnp.where(d < max_exact, d, large)


def _diff_attn_body(tq, lam_init, q_ref, k_ref, v_ref, bias_ref, lam_ref, g_ref, o_ref, m_scr, l_scr, acc_scr):
    qi = pl.program_id(2)
    q = q_ref[...]
    lane = lax.broadcasted_iota(jnp.int32, q.shape, 1)
    zero = jnp.zeros_like(q)
    qs = (jnp.where(lane < DIFF_HEAD_DIM, q, zero), jnp.where(lane >= DIFF_HEAD_DIM, q, zero))
    m_scr[...] = jnp.full(m_scr.shape, NEG_INF, F32)
    l_scr[...] = jnp.zeros(l_scr.shape, F32)
    acc_scr[...] = jnp.zeros(acc_scr.shape, F32)

    def tile(kj, masked):
        start = pl.multiple_of(kj * tq, tq)
        k = k_ref[pl.ds(start, tq), :]
        v = v_ref[pl.ds(start, tq), :]
        bias = bias_ref[jnp.minimum(qi - kj, 2)]
        for m in range(2):
            s = _dot_nt(qs[m], k) + bias
            if masked:
                rows = lax.broadcasted_iota(jnp.int32, s.shape, 0)
                cols = lax.broadcasted_iota(jnp.int32, s.shape, 1)
                s = jnp.where(rows >= cols, s, NEG_INF)
            m_prev = m_scr[m]
            m_new = jnp.maximum(m_prev, jnp.max(s, axis=-1, keepdims=True))
            alpha = jnp.exp(m_prev - m_new)
            p = jnp.exp(s - m_new)
            l_scr[m] = alpha * l_scr[m] + jnp.sum(p, axis=-1, keepdims=True)
            acc_scr[m] = alpha * acc_scr[m] + _dot(p.astype(BF16), v)
            m_scr[m] = m_new

    def body(kj, carry):
        tile(kj, False)
        return carry

    lax.fori_loop(0, qi, body, 0)
    tile(qi, True)
    o = acc_scr[0] / l_scr[0] - lam_ref[...] * (acc_scr[1] / l_scr[1])
    o_ref[...] = (_rms(o, g_ref[...]) * (1.0 - lam_init)).astype(o_ref.dtype)


def _diff_mixer(h3, w_qkv, lam_q1, lam_k1, lam_q2, lam_k2, subln_g, rel_bias, lam_init):
    b, lp, d = h3.shape
    n = b * lp
    hd = DIFF_HEADS * 2 * DIFF_HEAD_DIM
    tm = _pick_tile(n, (512, 384, 256, 128))
    (qkv,) = _row_call(functools.partial(_qkv_body, hd, DIFF_HEAD_DIM ** -0.5), [h3.reshape(n, d)],
                       [w_qkv.astype(BF16)], [(3 * hd, BF16)], tm, "diff_qkv")
    qkv = qkv.reshape(b, lp, 3 * hd)
    tq = _pick_tile(lp, (384, 256, 128))
    assert tq >= REL_MAX_DIST
    i = jnp.arange(tq)
    dist = (i[:, None] - i[None, :])[None] + (jnp.arange(3) * tq)[:, None, None]
    bias = jnp.transpose(rel_bias[_t5_bucket(dist)], (3, 0, 1, 2)).astype(F32)
    lam = (jnp.exp(jnp.sum(lam_q1.astype(F32) * lam_k1.astype(F32)))
           - jnp.exp(jnp.sum(lam_q2.astype(F32) * lam_k2.astype(F32))) + lam_init)
    lam_row = jnp.full((1, DIFF_V_DIM), lam, F32)
    w = 2 * DIFF_HEAD_DIM
    nh = DIFF_HEADS
    o = pl.pallas_call(
        functools.partial(_diff_attn_body, tq, lam_init),
        grid=(b, nh, lp // tq),
        in_specs=[pl.BlockSpec((None, tq, w), lambda bi, hi, qi: (bi, qi, hi)),
                  pl.BlockSpec((None, lp, w), lambda bi, hi, qi: (bi, 0, nh + hi)),
                  pl.BlockSpec((None, lp, DIFF_V_DIM), lambda bi, hi, qi: (bi, 0, 2 * nh + hi)),
                  pl.BlockSpec((None, 3, tq, tq), lambda bi, hi, qi: (hi, 0, 0, 0)),
                  pl.BlockSpec((1, DIFF_V_DIM), lambda bi, hi, qi: (0, 0)),
                  pl.BlockSpec((1, DIFF_V_DIM), lambda bi, hi, qi: (0, 0))],
        out_specs=pl.BlockSpec((None, tq, DIFF_V_DIM), lambda bi, hi, qi: (bi, qi, hi)),
        out_shape=jax.ShapeDtypeStruct((b, lp, nh * DIFF_V_DIM), BF16),
        scratch_shapes=[pltpu.VMEM((2, tq, 1), F32), pltpu.VMEM((2, tq, 1), F32),
                        pltpu.VMEM((2, tq, DIFF_V_DIM), F32)],
        name="diff_attn",
        compiler_params=pltpu.CompilerParams(dimension_semantics=("parallel", "parallel", "arbitrary"),
                                             vmem_limit_bytes=VMEM_LIMIT),
    )(qkv, qkv, qkv, bias, lam_row, subln_g.reshape(1, DIFF_V_DIM))
    return o.reshape(n, nh * DIFF_V_DIM)


def _gelu_tanh(x):
    return 0.5 * x * (1.0 + jnp.tanh(math.sqrt(2.0 / math.pi) * (x + 0.044715 * (x * x * x))))


def _s5_body(u_ref, wb_ref, wc_ref, are_ref, aim_ref, d_ref, y_ref, s_scr, st_scr):
    rows = u_ref.shape[0]
    tc = rows // S5_SUBLANES
    n_sl = wb_ref.shape[0]
    kw = wb_ref.shape[1]
    sw = wb_ref.shape[2] // 2
    half = n_sl * sw

    @pl.when(pl.program_id(0) == 0)
    def _():
        st_scr[...] = jnp.zeros(st_scr.shape, F32)

    u = u_ref[...]
    ub = u.astype(BF16)
    for q in range(n_sl):
        bu = _dot(ub[:, q * kw:(q + 1) * kw], wb_ref[q])
        s_scr[:, q * sw:(q + 1) * sw] = bu[:, :sw]
        s_scr[:, half + q * sw:half + (q + 1) * sw] = bu[:, sw:]

    for q in range(n_sl):
        re_sl = slice(q * sw, (q + 1) * sw)
        im_sl = slice(half + q * sw, half + (q + 1) * sw)
        ar = are_ref[:, re_sl]
        ai = aim_ref[:, re_sl]

        def step(t, carry):
            sr, si = carry
            r0 = pl.multiple_of(t * S5_SUBLANES, S5_SUBLANES)
            nr = ar * sr - ai * si + s_scr[pl.ds(r0, S5_SUBLANES), re_sl]
            ni = ar * si + ai * sr + s_scr[pl.ds(r0, S5_SUBLANES), im_sl]
            s_scr[pl.ds(r0, S5_SUBLANES), re_sl] = nr
            s_scr[pl.ds(r0, S5_SUBLANES), im_sl] = ni
            return nr, ni

        sr, si = lax.fori_loop(0, tc, step, (st_scr[:, re_sl], st_scr[:, im_sl]))
        st_scr[:, re_sl] = sr
        st_scr[:, im_sl] = si

    for q in range(n_sl):
        s_re = s_scr[:, q * sw:(q + 1) * sw].astype(BF16)
        s_im = s_scr[:, half + q * sw:half + (q + 1) * sw].astype(BF16)
        y = _dot(s_re, wc_ref[q, :sw, :]) + _dot(s_im, wc_ref[q, sw:, :])
        y = y + d_ref[:, q * kw:(q + 1) * kw] * u[:, q * kw:(q + 1) * kw]
        y_ref[:, q * kw:(q + 1) * kw] = _gelu_tanh(y).astype(y_ref.dtype)


def _s5_glu_ln_body(y_ref, h_ref, w_ref, bg_ref, g_ref, b_ref, o_ref):
    r = _dot(y_ref[...], w_ref[...]) + bg_ref[...]
    d = o_ref.shape[1]
    m = r[:, :d] * (1.0 / (1.0 + jnp.exp(-r[:, d:])))
    o_ref[...] = _layer_norm(ALPHA * h_ref[...] + m, g_ref[...], b_ref[...])


def _block_diag(blocks):
    s, g, a, b_ = blocks.shape
    eye = jnp.eye(g, dtype=blocks.dtype)
    return jnp.einsum('sgab,gh->sgahb', blocks, eye).reshape(s, g * a, g * b_)


def _s5_mixer(h3, lam_re, lam_im, log_step, b_re, b_im, c_re, c_im, d_skip, w_glu, b_glu, ln_g, ln_b):
    b, lp, d = h3.shape
    assert b <= S5_SUBLANES
    step = jnp.exp(log_step.astype(F32))[:, None]
    lr = jnp.minimum(lam_re.astype(F32), -1e-4)
    li = lam_im.astype(F32)
    mag = jnp.exp(lr * step)
    ar, ai = mag * jnp.cos(li * step), mag * jnp.sin(li * step)
    den = lr * lr + li * li
    cr = ((ar - 1.0) * lr + ai * li) / den
    ci = (ai * lr - (ar - 1.0) * li) / den
    br, bi = b_re.astype(F32), b_im.astype(F32)
    bbr = cr[..., None] * br - ci[..., None] * bi
    bbi = cr[..., None] * bi + ci[..., None] * br
    n_sl = S5_GROUPS // S5_SLICE_GROUPS
    sg = S5_SLICE_GROUPS
    to_sl = lambda m: m.reshape(n_sl, sg, *m.shape[1:])
    wb = jnp.concatenate([_block_diag(to_sl(jnp.transpose(bbr, (0, 2, 1)))),
                          _block_diag(to_sl(jnp.transpose(bbi, (0, 2, 1))))], axis=2).astype(BF16)
    wc = jnp.concatenate([_block_diag(to_sl(jnp.transpose(c_re.astype(F32), (0, 2, 1)))),
                          _block_diag(to_sl(-jnp.transpose(c_im.astype(F32), (0, 2, 1))))], axis=1).astype(BF16)
    n_state = S5_GROUPS * S5_STATE
    a_re = jnp.broadcast_to(ar.reshape(1, n_state), (S5_SUBLANES, n_state))
    a_im = jnp.broadcast_to(ai.reshape(1, n_state), (S5_SUBLANES, n_state))
    u = jnp.pad(jnp.transpose(h3, (1, 0, 2)), ((0, 0), (0, S5_SUBLANES - b), (0, 0))).reshape(lp * S5_SUBLANES, d)
    tc = S5_TCHUNK
    rows = tc * S5_SUBLANES
    const = lambda a: pl.BlockSpec(a.shape, lambda i: (0,) * a.ndim)
    d_row = d_skip.astype(F32).reshape(1, d)
    y = pl.pallas_call(
        _s5_body, grid=(lp // tc,),
        in_specs=[pl.BlockSpec((rows, d), lambda i: (i, 0)), const(wb), const(wc), const(a_re), const(a_im),
                  const(d_row)],
        out_specs=pl.BlockSpec((rows, d), lambda i: (i, 0)),
        out_shape=jax.ShapeDtypeStruct((lp * S5_SUBLANES, d), BF16),
        scratch_shapes=[pltpu.VMEM((rows, 2 * n_state), F32), pltpu.VMEM((S5_SUBLANES, 2 * n_state), F32)],
        name="s5_scan",
        compiler_params=pltpu.CompilerParams(dimension_semantics=("arbitrary",), vmem_limit_bytes=VMEM_LIMIT),
    )(u, wb, wc, a_re, a_im, d_row)
    y = jnp.transpose(y.reshape(lp, S5_SUBLANES, d)[:, :b], (1, 0, 2)).reshape(b * lp, d)
    n = b * lp
    tm = _pick_tile(n, (512, 384, 256, 128))
    (out,) = _row_call(_s5_glu_ln_body, [y, h3.reshape(n, d)],
                       [w_glu.astype(BF16), b_glu.reshape(1, 2 * d), ln_g.reshape(1, d), ln_b.reshape(1, d)],
                       [(d, F32)], tm, "s5_glu_ln")
    return out


def _rope_block(x, cm, s1, s2):
    return (x * cm + pltpu.roll(x, LANES - MLA_ROPE // 2, axis=1) * s1 + pltpu.roll(x, MLA_ROPE // 2, axis=1) * s2)


def _mla_in_body(h_ref, qcm_ref, qs1_ref, qs2_ref, kcm_ref, ks1_ref, ks2_ref,
                 win_ref, qg_ref, kvg_ref, wuq_ref, wuk_ref, wuv_ref, q_ref, k_ref, v_ref):
    c = _dot(h_ref[...].astype(BF16), win_ref[...])
    c_q = _rms(c[:, :MLA_Q_RANK], qg_ref[...]).astype(BF16)
    c_kv = _rms(c[:, MLA_Q_RANK:MLA_Q_RANK + MLA_KV_RANK], kvg_ref[...]).astype(BF16)
    k_r = _rope_block(c[:, MLA_Q_RANK + MLA_KV_RANK:], kcm_ref[...], ks1_ref[...], ks2_ref[...])
    q = _dot(c_q, wuq_ref[...])
    k = _dot(c_kv, wuk_ref[...])
    qcm, qs1, qs2 = qcm_ref[...], qs1_ref[...], qs2_ref[...]
    for h in range(MLA_HEADS):
        sl = slice(h * LANES, (h + 1) * LANES)
        q_ref[:, sl] = _rope_block(q[:, sl], qcm, qs1, qs2).astype(q_ref.dtype)
        k_ref[:, sl] = (k[:, sl] + k_r).astype(k_ref.dtype)
    v_ref[...] = _dot(c_kv, wuv_ref[...]).astype(v_ref.dtype)


def _mla_attn_body(tq, q_ref, k_ref, v_ref, o_ref, m_scr, l_scr, acc_scr):
    qi = pl.program_id(2)
    m_scr[...] = jnp.full(m_scr.shape, NEG_INF, F32)
    l_scr[...] = jnp.zeros(l_scr.shape, F32)
    acc_scr[...] = jnp.zeros(acc_scr.shape, F32)

    def tile(kj, masked):
        start = pl.multiple_of(kj * tq, tq)
        v = v_ref[pl.ds(start, tq), :]
        for hh in range(2):
            q = q_ref[:, hh * LANES:(hh + 1) * LANES]
            k = k_ref[pl.ds(start, tq), hh * LANES:(hh + 1) * LANES]
            s = _dot_nt(q, k)
            if masked:
                rows = lax.broadcasted_iota(jnp.int32, s.shape, 0)
                cols = lax.broadcasted_iota(jnp.int32, s.shape, 1)
                s = jnp.where(rows >= cols, s, NEG_INF)
            m_prev = m_scr[hh]
            m_new = jnp.maximum(m_prev, jnp.max(s, axis=-1, keepdims=True))
            alpha = jnp.exp(m_prev - m_new)
            p = jnp.exp(s - m_new)
            l_scr[hh] = alpha * l_scr[hh] + jnp.sum(p, axis=-1, keepdims=True)
            acc_scr[hh] = alpha * acc_scr[hh] + _dot(p.astype(BF16), v)
            m_scr[hh] = m_new

    def body(kj, carry):
        tile(kj, False)
        return carry

    lax.fori_loop(0, qi, body, 0)
    tile(qi, True)
    lane = lax.broadcasted_iota(jnp.int32, (tq, LANES), 1)
    o = jnp.where(lane < MLA_V, acc_scr[0] / l_scr[0], acc_scr[1] / l_scr[1])
    o_ref[...] = o.astype(o_ref.dtype)


def _mla_mixer(h3, w_in, q_norm_g, kv_norm_g, w_uq, w_ukv):
    b, lp, d = h3.shape
    n = b * lp
    nh = MLA_HEADS
    qk = MLA_NOPE + MLA_ROPE
    half = MLA_ROPE // 2
    w_kr = jnp.pad(w_in[:, MLA_Q_RANK + MLA_KV_RANK:], ((0, 0), (MLA_NOPE, LANES - qk)))
    w_in_all = jnp.concatenate([w_in[:, :MLA_Q_RANK + MLA_KV_RANK], w_kr], axis=1).astype(BF16)
    w_uq_p = jnp.pad(w_uq.reshape(MLA_Q_RANK, nh, qk), ((0, 0), (0, 0), (0, LANES - qk)))
    w_uq_p = w_uq_p.reshape(MLA_Q_RANK, nh * LANES).astype(BF16)
    w_ukv3 = w_ukv.reshape(MLA_KV_RANK, nh, MLA_NOPE + MLA_V)
    w_uk_p = jnp.pad(w_ukv3[:, :, :MLA_NOPE], ((0, 0), (0, 0), (0, LANES - MLA_NOPE)))
    w_uk_p = w_uk_p.reshape(MLA_KV_RANK, nh * LANES).astype(BF16)
    w_uv = w_ukv3[:, :, MLA_NOPE:].reshape(MLA_KV_RANK, nh * MLA_V).astype(BF16)
    pos = jnp.arange(lp, dtype=F32)
    inv_freq = ROPE_BASE ** (-jnp.arange(0, MLA_ROPE, 2, dtype=F32) / MLA_ROPE)
    ang = pos[:, None] * inv_freq[None, :]
    cos, sin = jnp.cos(ang), jnp.sin(ang)
    z = lambda w: jnp.zeros((lp, w), F32)
    cm = jnp.concatenate([jnp.ones((lp, MLA_NOPE), F32), cos, cos, z(LANES - qk)], axis=1)
    s1 = jnp.concatenate([z(MLA_NOPE), -sin, z(LANES - MLA_NOPE - half)], axis=1)
    s2 = jnp.concatenate([z(MLA_NOPE + half), sin, z(LANES - qk)], axis=1)
    scale = qk ** -0.5
    tm = _pick_tile(lp, (384, 256, 128))
    q, k, v = _row_call(
        _mla_in_body, [h3.reshape(n, d)],
        [w_in_all, q_norm_g.reshape(1, MLA_Q_RANK), kv_norm_g.reshape(1, MLA_KV_RANK), w_uq_p, w_uk_p, w_uv],
        [(nh * LANES, BF16), (nh * LANES, BF16), (nh * MLA_V, BF16)], tm, "mla_in",
        pos_ins=[cm * scale, s1 * scale, s2 * scale, cm, s1, s2], lp=lp)
    tq = _pick_tile(lp, (384, 256, 128))
    q, k, v = q.reshape(b, lp, nh * LANES), k.reshape(b, lp, nh * LANES), v.reshape(b, lp, nh * MLA_V)
    o = pl.pallas_call(
        functools.partial(_mla_attn_body, tq),
        grid=(b, nh // 2, lp // tq),
        in_specs=[pl.BlockSpec((None, tq, 2 * LANES), lambda bi, hi, qi: (bi, qi, hi)),
                  pl.BlockSpec((None, lp, 2 * LANES), lambda bi, hi, qi: (bi, 0, hi)),
                  pl.BlockSpec((None, lp, 2 * MLA_V), lambda bi, hi, qi: (bi, 0, hi))],
        out_specs=pl.BlockSpec((None, tq, 2 * MLA_V), lambda bi, hi, qi: (bi, qi, hi)),
        out_shape=jax.ShapeDtypeStruct((b, lp, nh * MLA_V), BF16),
        scratch_shapes=[pltpu.VMEM((2, tq, 1), F32), pltpu.VMEM((2, tq, 1), F32),
                        pltpu.VMEM((2, tq, 2 * MLA_V), F32)],
        name="mla_attn",
        compiler_params=pltpu.CompilerParams(dimension_semantics=("parallel", "parallel", "arbitrary"),
                                             vmem_limit_bytes=VMEM_LIMIT),
    )(q, k, v)
    return o.reshape(n, nh * MLA_V)


def kernel(x, meta, rel_bias, ln_g, ln_b, ffn_w1, ffn_w3, ffn_w2, ssd_w_in, ssd_conv_w, ssd_conv_b, ssd_dt_bias, ssd_a_log, ssd_d, ssd_norm_g, ssd_w_out, diff_w_qkv, diff_lam_q1, diff_lam_k1, diff_lam_q2, diff_lam_k2, diff_subln_g, diff_w_out, s5_lam_re, s5_lam_im, s5_log_step, s5_b_re, s5_b_im, s5_c_re, s5_c_im, s5_d, s5_w_glu, s5_b_glu, mla_w_in, mla_q_norm_g, mla_kv_norm_g, mla_w_uq, mla_w_ukv, mla_w_out):
    b, seq, d = x.shape
    l = seq + N_META
    lp = -(-l // SEQ_ALIGN) * SEQ_ALIGN
    n = b * lp
    h = jnp.concatenate([jnp.broadcast_to(meta[None].astype(x.dtype), (b, N_META, d)), x,
                         jnp.zeros((b, lp - l, d), x.dtype)], axis=1).reshape(n, d)
    for i in range(DEPTH):
        kind, j = i % 4, i // 4
        h = _ffn(h, ffn_w1[i, 0], ffn_w3[i, 0], ffn_w2[i, 0], ln_g[i, 0], ln_b[i, 0])
        h3 = h.reshape(b, lp, d)
        if kind == 0:
            y = _ssd_mixer(h3, ssd_w_in[j], ssd_conv_w[j], ssd_conv_b[j], ssd_dt_bias[j], ssd_a_log[j],
                           ssd_d[j], ssd_norm_g[j])
            h = _proj_ln(y, h, ssd_w_out[j], ln_g[i, 1], ln_b[i, 1])
        elif kind == 1:
            lam_init = 0.8 - 0.6 * math.exp(-0.3 * i)
            y = _diff_mixer(h3, diff_w_qkv[j], diff_lam_q1[j], diff_lam_k1[j], diff_lam_q2[j], diff_lam_k2[j],
                            diff_subln_g[j], rel_bias, lam_init)
            h = _proj_ln(y, h, diff_w_out[j], ln_g[i, 1], ln_b[i, 1])
        elif kind == 2:
            h = _s5_mixer(h3, s5_lam_re[j], s5_lam_im[j], s5_log_step[j], s5_b_re[j], s5_b_im[j], s5_c_re[j],
                          s5_c_im[j], s5_d[j], s5_w_glu[j], s5_b_glu[j], ln_g[i, 1], ln_b[i, 1])
        else:
            y = _mla_mixer(h3, mla_w_in[j], mla_q_norm_g[j], mla_kv_norm_g[j], mla_w_uq[j], mla_w_ukv[j])
            h = _proj_ln(y, h, mla_w_out[j], ln_g[i, 1], ln_b[i, 1])
        h = _ffn(h, ffn_w1[i, 1], ffn_w3[i, 1], ffn_w2[i, 1], ln_g[i, 2], ln_b[i, 2])
    return h.reshape(b, lp, d)[:, N_META:l]
```

```python
import functools
import math

import jax
import jax.numpy as jnp
from jax import lax
from jax.experimental import pallas as pl
from jax.experimental.pallas import tpu as pltpu

F32 = jnp.float32
BF16 = jnp.bfloat16

N_META = 16
DEPTH = 4
ALPHA = (2.0 * DEPTH) ** 0.25
LN_EPS = 1e-5
RMS_EPS = 1e-6
NEG_INF = -1e30
LOG2E = math.log2(math.e)

SSD_HEAD_DIM = 64
SSD_HEADS = 32
SSD_GROUPS = 8
SSD_STATE = 128
SSD_CONV = 4
SSD_CHUNK = 128

DIFF_HEADS = 8
DIFF_HEAD_DIM = 64
DIFF_V_DIM = 128
REL_BUCKETS = 32
REL_MAX_DIST = 128

S5_GROUP = 16
S5_GROUPS = 64
S5_STATE = 64
S5_SLICE_GROUPS = 16
S5_SUBLANES = 8
S5_TCHUNK = 32

MLA_HEADS = 16
MLA_Q_RANK = 384
MLA_KV_RANK = 256
MLA_NOPE = 64
MLA_ROPE = 32
MLA_V = 64
ROPE_BASE = 10000.0

LANES = 128
SEQ_ALIGN = 128
VMEM_LIMIT = 56 * 1024 * 1024


def _pick_tile(n, candidates):
    for c in candidates:
        if n % c == 0:
            return c
    raise ValueError(f"no tile for {n}")


def _layer_norm(z, g, b):
    mu = jnp.mean(z, axis=-1, keepdims=True)
    d = z - mu
    var = jnp.mean(d * d, axis=-1, keepdims=True)
    return d * lax.rsqrt(var + LN_EPS) * g + b


def _rms(z, g):
    return z * lax.rsqrt(jnp.mean(z * z, axis=-1, keepdims=True) + RMS_EPS) * g


def _silu(x):
    return x * (1.0 / (1.0 + jnp.exp(-x)))


def _dot(a, b):
    return jnp.dot(a, b, preferred_element_type=F32)


def _dot_nt(a, b):
    return lax.dot_general(a, b, (((1,), (1,)), ((), ())), preferred_element_type=F32)


def _row_call(body, row_ins, const_ins, outs, tm, name, pos_ins=(), lp=None, t_outs=()):
    n = row_ins[0].shape[0]
    assert n % tm == 0
    in_specs = [pl.BlockSpec((tm, a.shape[1]), lambda i: (i, 0)) for a in row_ins]
    if pos_ins or t_outs:
        assert lp % tm == 0
        nb = lp // tm
        in_specs += [pl.BlockSpec((tm, a.shape[1]), lambda i: (i % nb, 0)) for a in pos_ins]
    in_specs += [pl.BlockSpec(c.shape, lambda i: (0, 0), pipeline_mode=pl.Buffered(1)) for c in const_ins]
    out_specs = [pl.BlockSpec((tm, w), lambda i: (i, 0)) for (w, _) in outs]
    out_shape = [jax.ShapeDtypeStruct((n, w), dt) for (w, dt) in outs]
    out_specs += [pl.BlockSpec((None, None, w, tm), lambda i: (i // nb, i % nb, 0, 0)) for (w, _) in t_outs]
    out_shape += [jax.ShapeDtypeStruct((n // lp, nb, w, tm), dt) for (w, dt) in t_outs]
    res = pl.pallas_call(
        body, grid=(n // tm,), in_specs=in_specs, out_specs=out_specs, out_shape=out_shape, name=name,
        compiler_params=pltpu.CompilerParams(dimension_semantics=("parallel",), vmem_limit_bytes=VMEM_LIMIT),
    )(*row_ins, *pos_ins, *const_ins)
    return res


def _ffn_body(n_chunks, h_ref, w1_ref, w3_ref, w2_ref, g_ref, b_ref, o_ref):
    x = h_ref[...]
    xb = x.astype(BF16)
    fc = w1_ref.shape[1] // n_chunks
    y = None
    for c in range(n_chunks):
        sl = slice(c * fc, (c + 1) * fc)
        gate = _dot(xb, w1_ref[:, sl])
        up = _dot(xb, w3_ref[:, sl])
        a = (_silu(gate) * up).astype(BF16)
        part = _dot(a, w2_ref[sl, :])
        y = part if y is None else y + part
    o_ref[...] = _layer_norm(ALPHA * x + 0.5 * y, g_ref[...], b_ref[...])


def _ffn(h2, w1, w3, w2, g, b):
    n, d = h2.shape
    tm = _pick_tile(n, (512, 384, 256, 128))
    f = w1.shape[1]
    n_chunks = 2 if f % (2 * LANES) == 0 else 1
    (out,) = _row_call(functools.partial(_ffn_body, n_chunks), [h2],
                       [w1.astype(BF16), w3.astype(BF16), w2.astype(BF16), g.reshape(1, d), b.reshape(1, d)],
                       [(d, F32)], tm, "ffn")
    return out


def _proj_ln_body(y_ref, h_ref, w_ref, g_ref, b_ref, o_ref):
    m = _dot(y_ref[...], w_ref[...])
    o_ref[...] = _layer_norm(ALPHA * h_ref[...] + m, g_ref[...], b_ref[...])


def _proj_ln(y2, h2, w, g, b):
    n, d = h2.shape
    tm = _pick_tile(n, (512, 384, 256, 128))
    (out,) = _row_call(_proj_ln_body, [y2, h2], [w.astype(BF16), g.reshape(1, d), b.reshape(1, d)],
                       [(d, F32)], tm, "proj_ln")
    return out


def _ssd_in_body(h_ref, w_ref, xbc_ref, z_ref, dt_ref):
    r = _dot(h_ref[...].astype(BF16), w_ref[...])
    nx, nz = xbc_ref.shape[1], z_ref.shape[1]
    xbc_ref[...] = r[:, :nx]
    z_ref[...] = r[:, nx:nx + nz]
    dt_ref[...] = r[:, nx + nz:]


def _softplus(x):
    return jnp.maximum(x, 0.0) + jnp.log1p(jnp.exp(-jnp.abs(x)))


def _ssd_body(xbc_ref, dt_ref, z_ref, convw_ref, convb_ref, dtb_ref, alog_ref, dskip_ref, normg_ref,
              y_ref, buf_scr, state_scr, y_scr):
    t = SSD_CHUNK
    d_inner = SSD_HEADS * SSD_HEAD_DIM
    gn = SSD_GROUPS * SSD_STATE
    hp = SSD_HEAD_DIM
    r = SSD_HEADS // SSD_GROUPS

    @pl.when(pl.program_id(1) == 0)
    def _():
        buf_scr[0:8, :] = jnp.zeros((8, buf_scr.shape[1]), F32)
        state_scr[...] = jnp.zeros(state_scr.shape, F32)

    x = xbc_ref[...]
    buf_scr[8:8 + t, :] = x
    acc = convb_ref[...] + convw_ref[SSD_CONV - 1:SSD_CONV, :] * x
    for j in range(1, SSD_CONV):
        acc = acc + convw_ref[SSD_CONV - 1 - j:SSD_CONV - j, :] * buf_scr[8 - j:8 - j + t, :]
    buf_scr[0:8, :] = x[t - 8:t, :]
    xbc = _silu(acc)

    dt = _softplus(dt_ref[...] + dtb_ref[...])
    a = dt * (-jnp.exp(alog_ref[...]))
    rows = lax.broadcasted_iota(jnp.int32, (t, t), 0)
    cols = lax.broadcasted_iota(jnp.int32, (t, t), 1)
    tri = rows >= cols
    a_cum = jnp.dot(tri.astype(F32), a, precision=lax.Precision.HIGHEST, preferred_element_type=F32)
    a_cum_t = a_cum.T
    a_last = a_cum[t - 1:t, :]
    exp_cum = jnp.exp(a_cum)
    dec_end = jnp.exp(a_last - a_cum)
    chunk_dec = jnp.exp(a_last)

    for g in range(SSD_GROUPS):
        b_f = xbc[:, d_inner + g * SSD_STATE:d_inner + (g + 1) * SSD_STATE]
        c_g = xbc[:, d_inner + gn + g * SSD_STATE:d_inner + gn + (g + 1) * SSD_STATE].astype(BF16)
        b_g = b_f.astype(BF16)
        b_gt = b_f.T.astype(BF16)
        cb = _dot_nt(c_g, b_g)
        for k in range(r):
            h = g * r + k
            col = a_cum[:, h:h + 1]
            row = a_cum_t[h:h + 1, :]
            decay = jnp.exp(jnp.where(tri, col - row, NEG_INF))
            xh = xbc[:, h * hp:(h + 1) * hp]
            xdt = xh * dt[:, h:h + 1]
            st = state_scr[:, h * hp:(h + 1) * hp]
            y_diag = _dot((cb * decay).astype(BF16), xdt.astype(BF16))
            y_off = _dot(c_g, st.astype(BF16)) * exp_cum[:, h:h + 1]
            y_scr[:, h * hp:(h + 1) * hp] = y_diag + y_off + dskip_ref[:, h * hp:(h + 1) * hp] * xh
            xw = (xdt * dec_end[:, h:h + 1]).astype(BF16)
            state_scr[:, h * hp:(h + 1) * hp] = st * chunk_dec[:, h:h + 1] + _dot(b_gt, xw)

    y = y_scr[...] * _silu(z_ref[...])
    gs = d_inner // SSD_GROUPS
    for g in range(SSD_GROUPS):
        sl = slice(g * gs, (g + 1) * gs)
        y_ref[:, sl] = _rms(y[:, sl], normg_ref[:, sl]).astype(y_ref.dtype)


def _ssd_mixer(h3, w_in, conv_w, conv_b, dt_bias, a_log, d_skip, norm_g):
    b, lp, d = h3.shape
    n = b * lp
    d_inner = SSD_HEADS * SSD_HEAD_DIM
    conv_ch = d_inner + 2 * SSD_GROUPS * SSD_STATE
    w_z, w_xbc, w_dt = w_in[:, :d_inner], w_in[:, d_inner:d_inner + conv_ch], w_in[:, d_inner + conv_ch:]
    w_dt = jnp.pad(w_dt, ((0, 0), (0, LANES - SSD_HEADS)))
    w_all = jnp.concatenate([w_xbc, w_z, w_dt], axis=1).astype(BF16)
    tm = _pick_tile(n, (512, 384, 256, 128))
    xbc, z, dt = _row_call(_ssd_in_body, [h3.reshape(n, d)], [w_all],
                           [(conv_ch, F32), (d_inner, F32), (LANES, F32)], tm, "ssd_in")
    t = SSD_CHUNK
    pad1 = lambda v: jnp.pad(v, (0, LANES - SSD_HEADS)).reshape(1, LANES)
    consts = [conv_w, conv_b.reshape(1, conv_ch), pad1(dt_bias), pad1(a_log),
              jnp.repeat(d_skip, SSD_HEAD_DIM).reshape(1, d_inner), norm_g.reshape(1, d_inner)]
    row_spec = lambda w: pl.BlockSpec((None, t, w), lambda i, c: (i, c, 0))
    y = pl.pallas_call(
        _ssd_body, grid=(b, lp // t),
        in_specs=[row_spec(conv_ch), row_spec(LANES), row_spec(d_inner)]
        + [pl.BlockSpec(c.shape, lambda i, c_: (0, 0)) for c in consts],
        out_specs=row_spec(d_inner),
        out_shape=jax.ShapeDtypeStruct((b, lp, d_inner), BF16),
        scratch_shapes=[pltpu.VMEM((8 + t, conv_ch), F32), pltpu.VMEM((SSD_STATE, d_inner), F32),
                        pltpu.VMEM((t, d_inner), F32)],
        name="ssd_scan",
        compiler_params=pltpu.CompilerParams(dimension_semantics=("parallel", "arbitrary"),
                                             vmem_limit_bytes=VMEM_LIMIT),
    )(xbc.reshape(b, lp, conv_ch), dt.reshape(b, lp, LANES), z.reshape(b, lp, d_inner), *consts)
    return y.reshape(n, d_inner)


def _qkv_body(n_q, scale, h_ref, w_ref, qk_ref, vt_ref):
    r = _dot(h_ref[...].astype(BF16), w_ref[...])
    n_qk = qk_ref.shape[1]
    qk_ref[:, :n_q] = (r[:, :n_q] * scale).astype(qk_ref.dtype)
    qk_ref[:, n_q:] = r[:, n_q:n_qk].astype(qk_ref.dtype)
    vt_ref[...] = r[:, n_qk:].T.astype(vt_ref.dtype)


def _online_softmax_tile(s, vt, m_scr, l_scr, acc_scr):
    m_prev = m_scr[...]
    m_new = jnp.maximum(m_prev, jnp.max(s, axis=0, keepdims=True))
    alpha = jnp.exp2(m_prev - m_new)
    p = jnp.exp2(s - m_new)
    l_scr[...] = alpha * l_scr[...] + jnp.sum(p, axis=0, keepdims=True)
    acc_scr[...] = alpha * acc_scr[...] + _dot(vt, p.astype(BF16))
    m_scr[...] = m_new


def _causal_mask_pair(s, tq):
    rows = lax.broadcasted_iota(jnp.int32, s.shape, 0)
    cols = lax.broadcasted_iota(jnp.int32, s.shape, 1)
    cols = jnp.where(cols >= tq, cols - tq, cols)
    return jnp.where(rows <= cols, s, NEG_INF)


def _t5_bucket(dist):
    max_exact = REL_BUCKETS // 2
    d = jnp.maximum(dist, 0)
    df = jnp.maximum(d, max_exact).astype(F32)
    large = max_exact + (jnp.log(df / max_exact) / math.log(REL_MAX_DIST / max_exact)
                         * (REL_BUCKETS - max_exact)).astype(jnp.int32)
    large = jnp.minimum(large, REL_BUCKETS - 1)
    return jnp.where(d < max_exact, d, large)


def _diff_attn_body(tq, lam_init, q_ref, k_ref, vt_ref, bias_ref, lam_ref, g_ref, o_ref, m_scr, l_scr, acc_scr):
    qi = pl.program_id(2)
    q = q_ref[...]
    lane = lax.broadcasted_iota(jnp.int32, q.shape, 1)
    zero = jnp.zeros_like(q)
    q_cat = jnp.concatenate([jnp.where(lane < DIFF_HEAD_DIM, q, zero),
                             jnp.where(lane >= DIFF_HEAD_DIM, q, zero)], axis=0)
    m_scr[...] = jnp.full(m_scr.shape, NEG_INF, F32)
    l_scr[...] = jnp.zeros(l_scr.shape, F32)
    acc_scr[...] = jnp.zeros(acc_scr.shape, F32)

    def tile(kj, masked):
        start = pl.multiple_of(kj * tq, tq)
        k = k_ref[pl.ds(start, tq), :]
        bias = bias_ref[jnp.minimum(qi - kj, 2)]
        s = _dot_nt(k, q_cat) + jnp.concatenate([bias, bias], axis=1)
        if masked:
            s = _causal_mask_pair(s, tq)
        _online_softmax_tile(s, vt_ref[kj], m_scr, l_scr, acc_scr)

    def body(kj, carry):
        tile(kj, False)
        return carry

    lax.fori_loop(0, qi, body, 0)
    tile(qi, True)
    acc = acc_scr[...] * (1.0 / l_scr[...])
    o_t = acc[:, :tq] - lam_ref[...] * acc[:, tq:]
    o_ref[...] = (_rms(o_t.T, g_ref[...]) * (1.0 - lam_init)).astype(o_ref.dtype)


def _diff_mixer(h3, w_qkv, lam_q1, lam_k1, lam_q2, lam_k2, subln_g, rel_bias, lam_init):
    b, lp, d = h3.shape
    n = b * lp
    hd = DIFF_HEADS * 2 * DIFF_HEAD_DIM
    tq = _pick_tile(lp, (384, 256, 128))
    assert tq >= REL_MAX_DIST
    nk = lp // tq
    qk, vt = _row_call(functools.partial(_qkv_body, hd, DIFF_HEAD_DIM ** -0.5 * LOG2E), [h3.reshape(n, d)],
                       [w_qkv.astype(BF16)], [(2 * hd, BF16)], tq, "diff_qkv", lp=lp, t_outs=[(hd, BF16)])
    qk = qk.reshape(b, lp, 2 * hd)
    m2 = 2 * tq
    kk = jnp.arange(m2)
    dist = jnp.where(kk < tq, kk, kk - m2)[None, :] + (jnp.arange(3) * tq)[:, None]
    tab = jnp.transpose(rel_bias[_t5_bucket(dist)].astype(F32) * LOG2E, (2, 0, 1))
    nh = DIFF_HEADS
    bias = jnp.tile(tab, (1, 1, tq))[..., :tq * (m2 - 1)].reshape(nh, 3, tq, m2 - 1)[..., :tq]
    lam = (jnp.exp(jnp.sum(lam_q1.astype(F32) * lam_k1.astype(F32)))
           - jnp.exp(jnp.sum(lam_q2.astype(F32) * lam_k2.astype(F32))) + lam_init)
    w = 2 * DIFF_HEAD_DIM
    o = pl.pallas_call(
        functools.partial(_diff_attn_body, tq, lam_init),
        grid=(b, nh, nk),
        in_specs=[pl.BlockSpec((None, tq, w), lambda bi, hi, qi: (bi, qi, hi)),
                  pl.BlockSpec((None, lp, w), lambda bi, hi, qi: (bi, 0, nh + hi)),
                  pl.BlockSpec((None, nk, DIFF_V_DIM, tq), lambda bi, hi, qi: (bi, 0, hi, 0)),
                  pl.BlockSpec((None, 3, tq, tq), lambda bi, hi, qi: (hi, 0, 0, 0)),
                  pl.BlockSpec((1, 1), lambda bi, hi, qi: (0, 0)),
                  pl.BlockSpec((1, DIFF_V_DIM), lambda bi, hi, qi: (0, 0))],
        out_specs=pl.BlockSpec((None, tq, DIFF_V_DIM), lambda bi, hi, qi: (bi, qi, hi)),
        out_shape=jax.ShapeDtypeStruct((b, lp, nh * DIFF_V_DIM), BF16),
        scratch_shapes=[pltpu.VMEM((1, 2 * tq), F32), pltpu.VMEM((1, 2 * tq), F32),
                        pltpu.VMEM((DIFF_V_DIM, 2 * tq), F32)],
        name="diff_attn",
        compiler_params=pltpu.CompilerParams(dimension_semantics=("parallel", "parallel", "arbitrary"),
                                             vmem_limit_bytes=VMEM_LIMIT),
    )(qk, qk, vt, bias, lam.reshape(1, 1), subln_g.reshape(1, DIFF_V_DIM))
    return o.reshape(n, nh * DIFF_V_DIM)


def _gelu_tanh(x):
    return 0.5 * x * (1.0 + jnp.tanh(math.sqrt(2.0 / math.pi) * (x + 0.044715 * (x * x * x))))


def _s5_body(u_ref, wb_ref, wc_ref, are_ref, aim_ref, d_ref, y_ref, s_scr, st_scr):
    rows = u_ref.shape[0]
    tc = rows // S5_SUBLANES
    n_sl = wb_ref.shape[0]
    kw = wb_ref.shape[1]
    sw = wb_ref.shape[2] // 2
    half = n_sl * sw

    @pl.when(pl.program_id(0) == 0)
    def _():
        st_scr[...] = jnp.zeros(st_scr.shape, F32)

    u = u_ref[...]
    ub = u.astype(BF16)
    for q in range(n_sl):
        bu = _dot(ub[:, q * kw:(q + 1) * kw], wb_ref[q])
        s_scr[:, q * sw:(q + 1) * sw] = bu[:, :sw]
        s_scr[:, half + q * sw:half + (q + 1) * sw] = bu[:, sw:]

    for q in range(n_sl):
        re_sl = slice(q * sw, (q + 1) * sw)
        im_sl = slice(half + q * sw, half + (q + 1) * sw)
        ar = are_ref[:, re_sl]
        ai = aim_ref[:, re_sl]

        def step(t, carry):
            sr, si = carry
            r0 = pl.multiple_of(t * S5_SUBLANES, S5_SUBLANES)
            nr = ar * sr - ai * si + s_scr[pl.ds(r0, S5_SUBLANES), re_sl]
            ni = ar * si + ai * sr + s_scr[pl.ds(r0, S5_SUBLANES), im_sl]
            s_scr[pl.ds(r0, S5_SUBLANES), re_sl] = nr
            s_scr[pl.ds(r0, S5_SUBLANES), im_sl] = ni
            return nr, ni

        sr, si = lax.fori_loop(0, tc, step, (st_scr[:, re_sl], st_scr[:, im_sl]))
        st_scr[:, re_sl] = sr
        st_scr[:, im_sl] = si

    for q in range(n_sl):
        s_re = s_scr[:, q * sw:(q + 1) * sw].astype(BF16)
        s_im = s_scr[:, half + q * sw:half + (q + 1) * sw].astype(BF16)
        y = _dot(s_re, wc_ref[q, :sw, :]) + _dot(s_im, wc_ref[q, sw:, :])
        y = y + d_ref[:, q * kw:(q + 1) * kw] * u[:, q * kw:(q + 1) * kw]
        y_ref[:, q * kw:(q + 1) * kw] = _gelu_tanh(y).astype(y_ref.dtype)


def _s5_glu_ln_body(y_ref, h_ref, w_ref, bg_ref, g_ref, b_ref, o_ref):
    r = _dot(y_ref[...], w_ref[...]) + bg_ref[...]
    d = o_ref.shape[1]
    m = r[:, :d] * (1.0 / (1.0 + jnp.exp(-r[:, d:])))
    o_ref[...] = _layer_norm(ALPHA * h_ref[...] + m, g_ref[...], b_ref[...])


def _block_diag(blocks):
    s, g, a, b_ = blocks.shape
    eye = jnp.eye(g, dtype=blocks.dtype)
    return jnp.einsum('sgab,gh->sgahb', blocks, eye).reshape(s, g * a, g * b_)


def _s5_mixer(h3, lam_re, lam_im, log_step, b_re, b_im, c_re, c_im, d_skip, w_glu, b_glu, ln_g, ln_b):
    b, lp, d = h3.shape
    assert b <= S5_SUBLANES
    step = jnp.exp(log_step.astype(F32))[:, None]
    lr = jnp.minimum(lam_re.astype(F32), -1e-4)
    li = lam_im.astype(F32)
    mag = jnp.exp(lr * step)
    ar, ai = mag * jnp.cos(li * step), mag * jnp.sin(li * step)
    den = lr * lr + li * li
    cr = ((ar - 1.0) * lr + ai * li) / den
    ci = (ai * lr - (ar - 1.0) * li) / den
    br, bi = b_re.astype(F32), b_im.astype(F32)
    bbr = cr[..., None] * br - ci[..., None] * bi
    bbi = cr[..., None] * bi + ci[..., None] * br
    n_sl = S5_GROUPS // S5_SLICE_GROUPS
    sg = S5_SLICE_GROUPS
    to_sl = lambda m: m.reshape(n_sl, sg, *m.shape[1:])
    wb = jnp.concatenate([_block_diag(to_sl(jnp.transpose(bbr, (0, 2, 1)))),
                          _block_diag(to_sl(jnp.transpose(bbi, (0, 2, 1))))], axis=2).astype(BF16)
    wc = jnp.concatenate([_block_diag(to_sl(jnp.transpose(c_re.astype(F32), (0, 2, 1)))),
                          _block_diag(to_sl(-jnp.transpose(c_im.astype(F32), (0, 2, 1))))], axis=1).astype(BF16)
    n_state = S5_GROUPS * S5_STATE
    a_re = jnp.broadcast_to(ar.reshape(1, n_state), (S5_SUBLANES, n_state))
    a_im = jnp.broadcast_to(ai.reshape(1, n_state), (S5_SUBLANES, n_state))
    u = jnp.pad(jnp.transpose(h3, (1, 0, 2)), ((0, 0), (0, S5_SUBLANES - b), (0, 0))).reshape(lp * S5_SUBLANES, d)
    tc = S5_TCHUNK
    rows = tc * S5_SUBLANES
    const = lambda a: pl.BlockSpec(a.shape, lambda i: (0,) * a.ndim)
    d_row = d_skip.astype(F32).reshape(1, d)
    y = pl.pallas_call(
        _s5_body, grid=(lp // tc,),
        in_specs=[pl.BlockSpec((rows, d), lambda i: (i, 0)), const(wb), const(wc), const(a_re), const(a_im),
                  const(d_row)],
        out_specs=pl.BlockSpec((rows, d), lambda i: (i, 0)),
        out_shape=jax.ShapeDtypeStruct((lp * S5_SUBLANES, d), BF16),
        scratch_shapes=[pltpu.VMEM((rows, 2 * n_state), F32), pltpu.VMEM((S5_SUBLANES, 2 * n_state), F32)],
        name="s5_scan",
        compiler_params=pltpu.CompilerParams(dimension_semantics=("arbitrary",), vmem_limit_bytes=VMEM_LIMIT),
    )(u, wb, wc, a_re, a_im, d_row)
    y = jnp.transpose(y.reshape(lp, S5_SUBLANES, d)[:, :b], (1, 0, 2)).reshape(b * lp, d)
    n = b * lp
    tm = _pick_tile(n, (512, 384, 256, 128))
    (out,) = _row_call(_s5_glu_ln_body, [y, h3.reshape(n, d)],
                       [w_glu.astype(BF16), b_glu.reshape(1, 2 * d), ln_g.reshape(1, d), ln_b.reshape(1, d)],
                       [(d, F32)], tm, "s5_glu_ln")
    return out


def _rope_block(x, cm, s1, s2):
    return (x * cm + pltpu.roll(x, LANES - MLA_ROPE // 2, axis=1) * s1 + pltpu.roll(x, MLA_ROPE // 2, axis=1) * s2)


def _mla_in_body(h_ref, qcm_ref, qs1_ref, qs2_ref, kcm_ref, ks1_ref, ks2_ref,
                 win_ref, qg_ref, kvg_ref, wuq_ref, wuk_ref, wuv_ref, q_ref, k_ref, vt_ref):
    c = _dot(h_ref[...].astype(BF16), win_ref[...])
    c_q = _rms(c[:, :MLA_Q_RANK], qg_ref[...]).astype(BF16)
    c_kv = _rms(c[:, MLA_Q_RANK:MLA_Q_RANK + MLA_KV_RANK], kvg_ref[...]).astype(BF16)
    k_r = _rope_block(c[:, MLA_Q_RANK + MLA_KV_RANK:], kcm_ref[...], ks1_ref[...], ks2_ref[...])
    q = _dot(c_q, wuq_ref[...])
    k = _dot(c_kv, wuk_ref[...])
    qcm, qs1, qs2 = qcm_ref[...], qs1_ref[...], qs2_ref[...]
    for h in range(MLA_HEADS):
        sl = slice(h * LANES, (h + 1) * LANES)
        q_ref[:, sl] = _rope_block(q[:, sl], qcm, qs1, qs2).astype(q_ref.dtype)
        k_ref[:, sl] = (k[:, sl] + k_r).astype(k_ref.dtype)
    vt_ref[...] = _dot(c_kv, wuv_ref[...]).T.astype(vt_ref.dtype)


def _mla_attn_body(tq, q_ref, k_ref, vt_ref, o_ref, m_scr, l_scr, acc_scr):
    qi = pl.program_id(2)
    m_scr[...] = jnp.full(m_scr.shape, NEG_INF, F32)
    l_scr[...] = jnp.zeros(l_scr.shape, F32)
    acc_scr[...] = jnp.zeros(acc_scr.shape, F32)

    def tile(kj, masked):
        start = pl.multiple_of(kj * tq, tq)
        s = jnp.concatenate(
            [_dot_nt(k_ref[pl.ds(start, tq), hh * LANES:(hh + 1) * LANES], q_ref[:, hh * LANES:(hh + 1) * LANES])
             for hh in range(2)], axis=1)
        if masked:
            s = _causal_mask_pair(s, tq)
        _online_softmax_tile(s, vt_ref[kj], m_scr, l_scr, acc_scr)

    def body(kj, carry):
        tile(kj, False)
        return carry

    lax.fori_loop(0, qi, body, 0)
    tile(qi, True)
    acc = acc_scr[...] * (1.0 / l_scr[...])
    row = lax.broadcasted_iota(jnp.int32, (2 * MLA_V, tq), 0)
    o_t = jnp.where(row < MLA_V, acc[:, :tq], acc[:, tq:])
    o_ref[...] = o_t.T.astype(o_ref.dtype)


def _mla_mixer(h3, w_in, q_norm_g, kv_norm_g, w_uq, w_ukv):
    b, lp, d = h3.shape
    n = b * lp
    nh = MLA_HEADS
    qk = MLA_NOPE + MLA_ROPE
    half = MLA_ROPE // 2
    w_kr = jnp.pad(w_in[:, MLA_Q_RANK + MLA_KV_RANK:], ((0, 0), (MLA_NOPE, LANES - qk)))
    w_in_all = jnp.concatenate([w_in[:, :MLA_Q_RANK + MLA_KV_RANK], w_kr], axis=1).astype(BF16)
    w_uq_p = jnp.pad(w_uq.reshape(MLA_Q_RANK, nh, qk), ((0, 0), (0, 0), (0, LANES - qk)))
    w_uq_p = w_uq_p.reshape(MLA_Q_RANK, nh * LANES).astype(BF16)
    w_ukv3 = w_ukv.reshape(MLA_KV_RANK, nh, MLA_NOPE + MLA_V)
    w_uk_p = jnp.pad(w_ukv3[:, :, :MLA_NOPE], ((0, 0), (0, 0), (0, LANES - MLA_NOPE)))
    w_uk_p = w_uk_p.reshape(MLA_KV_RANK, nh * LANES).astype(BF16)
    w_uv = w_ukv3[:, :, MLA_NOPE:].reshape(MLA_KV_RANK, nh * MLA_V).astype(BF16)
    pos = jnp.arange(lp, dtype=F32)
    inv_freq = ROPE_BASE ** (-jnp.arange(0, MLA_ROPE, 2, dtype=F32) / MLA_ROPE)
    ang = pos[:, None] * inv_freq[None, :]
    cos, sin = jnp.cos(ang), jnp.sin(ang)
    z = lambda w: jnp.zeros((lp, w), F32)
    cm = jnp.concatenate([jnp.ones((lp, MLA_NOPE), F32), cos, cos, z(LANES - qk)], axis=1)
    s1 = jnp.concatenate([z(MLA_NOPE), -sin, z(LANES - MLA_NOPE - half)], axis=1)
    s2 = jnp.concatenate([z(MLA_NOPE + half), sin, z(LANES - qk)], axis=1)
    scale = qk ** -0.5 * LOG2E
    tq = _pick_tile(lp, (384, 256, 128))
    nk = lp // tq
    q, k, vt = _row_call(
        _mla_in_body, [h3.reshape(n, d)],
        [w_in_all, q_norm_g.reshape(1, MLA_Q_RANK), kv_norm_g.reshape(1, MLA_KV_RANK), w_uq_p, w_uk_p, w_uv],
        [(nh * LANES, BF16), (nh * LANES, BF16)], tq, "mla_in",
        pos_ins=[cm * scale, s1 * scale, s2 * scale, cm, s1, s2], lp=lp, t_outs=[(nh * MLA_V, BF16)])
    q, k = q.reshape(b, lp, nh * LANES), k.reshape(b, lp, nh * LANES)
    o = pl.pallas_call(
        functools.partial(_mla_attn_body, tq),
        grid=(b, nh // 2, nk),
        in_specs=[pl.BlockSpec((None, tq, 2 * LANES), lambda bi, hi, qi: (bi, qi, hi)),
                  pl.BlockSpec((None, lp, 2 * LANES), lambda bi, hi, qi: (bi, 0, hi)),
                  pl.BlockSpec((None, nk, 2 * MLA_V, tq), lambda bi, hi, qi: (bi, 0, hi, 0))],
        out_specs=pl.BlockSpec((None, tq, 2 * MLA_V), lambda bi, hi, qi: (bi, qi, hi)),
        out_shape=jax.ShapeDtypeStruct((b, lp, nh * MLA_V), BF16),
        scratch_shapes=[pltpu.VMEM((1, 2 * tq), F32), pltpu.VMEM((1, 2 * tq), F32),
                        pltpu.VMEM((2 * MLA_V, 2 * tq), F32)],
        name="mla_attn",
        compiler_params=pltpu.CompilerParams(dimension_semantics=("parallel", "parallel", "arbitrary"),
                                             vmem_limit_bytes=VMEM_LIMIT),
    )(q, k, vt)
    return o.reshape(n, nh * MLA_V)


def kernel(x, meta, rel_bias, ln_g, ln_b, ffn_w1, ffn_w3, ffn_w2, ssd_w_in, ssd_conv_w, ssd_conv_b, ssd_dt_bias, ssd_a_log, ssd_d, ssd_norm_g, ssd_w_out, diff_w_qkv, diff_lam_q1, diff_lam_k1, diff_lam_q2, diff_lam_k2, diff_subln_g, diff_w_out, s5_lam_re, s5_lam_im, s5_log_step, s5_b_re, s5_b_im, s5_c_re, s5_c_im, s5_d, s5_w_glu, s5_b_glu, mla_w_in, mla_q_norm_g, mla_kv_norm_g, mla_w_uq, mla_w_ukv, mla_w_out):
    b, seq, d = x.shape
    l = seq + N_META
    lp = -(-l // SEQ_ALIGN) * SEQ_ALIGN
    n = b * lp
    h = jnp.concatenate([jnp.broadcast_to(meta[None].astype(x.dtype), (b, N_META, d)), x,
                         jnp.zeros((b, lp - l, d), x.dtype)], axis=1).reshape(n, d)
    for i in range(DEPTH):
        kind, j = i % 4, i // 4
        h = _ffn(h, ffn_w1[i, 0], ffn_w3[i, 0], ffn_w2[i, 0], ln_g[i, 0], ln_b[i, 0])
        h3 = h.reshape(b, lp, d)
        if kind == 0:
            y = _ssd_mixer(h3, ssd_w_in[j], ssd_conv_w[j], ssd_conv_b[j], ssd_dt_bias[j], ssd_a_log[j],
                           ssd_d[j], ssd_norm_g[j])
            h = _proj_ln(y, h, ssd_w_out[j], ln_g[i, 1], ln_b[i, 1])
        elif kind == 1:
            lam_init = 0.8 - 0.6 * math.exp(-0.3 * i)
            y = _diff_mixer(h3, diff_w_qkv[j], diff_lam_q1[j], diff_lam_k1[j], diff_lam_q2[j], diff_lam_k2[j],
                            diff_subln_g[j], rel_bias, lam_init)
            h = _proj_ln(y, h, diff_w_out[j], ln_g[i, 1], ln_b[i, 1])
        elif kind == 2:
            h = _s5_mixer(h3, s5_lam_re[j], s5_lam_im[j], s5_log_step[j], s5_b_re[j], s5_b_im[j], s5_c_re[j],
                          s5_c_im[j], s5_d[j], s5_w_glu[j], s5_b_glu[j], ln_g[i, 1], ln_b[i, 1])
        else:
            y = _mla_mixer(h3, mla_w_in[j], mla_q_norm_g[j], mla_kv_norm_g[j], mla_w_uq[j], mla_w_ukv[j])
            h = _proj_ln(y, h, mla_w_out[j], ln_g[i, 1], ln_b[i, 1])
        h = _ffn(h, ffn_w1[i, 1], ffn_w3[i, 1], ffn_w2[i, 1], ln_g[i, 2], ln_b[i, 2])
    return h.reshape(b, lp, d)[:, N_META:l]
```

```python
import functools
import math

import jax
import jax.numpy as jnp
from jax import lax
from jax.experimental import pallas as pl
from jax.experimental.pallas import tpu as pltpu

F32 = jnp.float32
BF16 = jnp.bfloat16

N_META = 16
DEPTH = 4
ALPHA = (2.0 * DEPTH) ** 0.25
LN_EPS = 1e-5
RMS_EPS = 1e-6
NEG_INF = -1e30
LOG2E = math.log2(math.e)

SSD_HEAD_DIM = 64
SSD_HEADS = 32
SSD_GROUPS = 8
SSD_STATE = 128
SSD_CONV = 4
SSD_CHUNK = 128

DIFF_HEADS = 8
DIFF_HEAD_DIM = 64
DIFF_V_DIM = 128
REL_BUCKETS = 32
REL_MAX_DIST = 128

S5_GROUP = 16
S5_GROUPS = 64
S5_STATE = 64
S5_SLICE_GROUPS = 16
S5_SUBLANES = 8
S5_TCHUNK = 32

MLA_HEADS = 16
MLA_Q_RANK = 384
MLA_KV_RANK = 256
MLA_NOPE = 64
MLA_ROPE = 32
MLA_V = 64
ROPE_BASE = 10000.0

LANES = 128
SEQ_ALIGN = 128
VMEM_LIMIT = 56 * 1024 * 1024


def _pick_tile(n, candidates):
    for c in candidates:
        if n % c == 0:
            return c
    raise ValueError(f"no tile for {n}")


def _layer_norm(z, g, b):
    mu = jnp.mean(z, axis=-1, keepdims=True)
    d = z - mu
    var = jnp.mean(d * d, axis=-1, keepdims=True)
    return d * lax.rsqrt(var + LN_EPS) * g + b


def _rms(z, g):
    return z * lax.rsqrt(jnp.mean(z * z, axis=-1, keepdims=True) + RMS_EPS) * g


def _silu(x):
    return x * (1.0 / (1.0 + jnp.exp(-x)))


def _dot(a, b):
    return jnp.dot(a, b, preferred_element_type=F32)


def _dot_nt(a, b):
    return lax.dot_general(a, b, (((1,), (1,)), ((), ())), preferred_element_type=F32)


def _row_call(body, row_ins, const_ins, outs, tm, name, pos_ins=(), lp=None, t_outs=()):
    n = row_ins[0].shape[0]
    assert n % tm == 0
    in_specs = [pl.BlockSpec((tm, a.shape[1]), lambda i: (i, 0)) for a in row_ins]
    if pos_ins or t_outs:
        assert lp % tm == 0
        nb = lp // tm
        in_specs += [pl.BlockSpec((tm, a.shape[1]), lambda i: (i % nb, 0)) for a in pos_ins]
    in_specs += [pl.BlockSpec(c.shape, lambda i: (0, 0), pipeline_mode=pl.Buffered(1)) for c in const_ins]
    out_specs = [pl.BlockSpec((tm, w), lambda i: (i, 0)) for (w, _) in outs]
    out_shape = [jax.ShapeDtypeStruct((n, w), dt) for (w, dt) in outs]
    out_specs += [pl.BlockSpec((None, None, w, tm), lambda i: (i // nb, i % nb, 0, 0)) for (w, _) in t_outs]
    out_shape += [jax.ShapeDtypeStruct((n // lp, nb, w, tm), dt) for (w, dt) in t_outs]
    res = pl.pallas_call(
        body, grid=(n // tm,), in_specs=in_specs, out_specs=out_specs, out_shape=out_shape, name=name,
        compiler_params=pltpu.CompilerParams(dimension_semantics=("parallel",), vmem_limit_bytes=VMEM_LIMIT),
    )(*row_ins, *pos_ins, *const_ins)
    return res


def _ffn_body(n_chunks, h_ref, w1_ref, w3_ref, w2_ref, g_ref, b_ref, o_ref):
    x = h_ref[...]
    xb = x.astype(BF16)
    fc = w1_ref.shape[1] // n_chunks
    y = None
    for c in range(n_chunks):
        sl = slice(c * fc, (c + 1) * fc)
        gate = _dot(xb, w1_ref[:, sl])
        up = _dot(xb, w3_ref[:, sl])
        a = (_silu(gate) * up).astype(BF16)
        part = _dot(a, w2_ref[sl, :])
        y = part if y is None else y + part
    o_ref[...] = _layer_norm(ALPHA * x + 0.5 * y, g_ref[...], b_ref[...])


def _ffn(h2, w1, w3, w2, g, b):
    n, d = h2.shape
    tm = _pick_tile(n, (512, 384, 256, 128))
    f = w1.shape[1]
    n_chunks = 2 if f % (2 * LANES) == 0 else 1
    (out,) = _row_call(functools.partial(_ffn_body, n_chunks), [h2],
                       [w1.astype(BF16), w3.astype(BF16), w2.astype(BF16), g.reshape(1, d), b.reshape(1, d)],
                       [(d, F32)], tm, "ffn")
    return out


def _proj_ln_body(y_ref, h_ref, w_ref, g_ref, b_ref, o_ref):
    m = _dot(y_ref[...], w_ref[...])
    o_ref[...] = _layer_norm(ALPHA * h_ref[...] + m, g_ref[...], b_ref[...])


def _proj_ln(y2, h2, w, g, b):
    n, d = h2.shape
    tm = _pick_tile(n, (512, 384, 256, 128))
    (out,) = _row_call(_proj_ln_body, [y2, h2], [w.astype(BF16), g.reshape(1, d), b.reshape(1, d)],
                       [(d, F32)], tm, "proj_ln")
    return out


def _ssd_in_body(h_ref, w_ref, xbc_ref, z_ref, dt_ref):
    r = _dot(h_ref[...].astype(BF16), w_ref[...])
    nx, nz = xbc_ref.shape[1], z_ref.shape[1]
    xbc_ref[...] = r[:, :nx]
    z_ref[...] = r[:, nx:nx + nz]
    dt_ref[...] = r[:, nx + nz:]


def _softplus(x):
    return jnp.maximum(x, 0.0) + jnp.log1p(jnp.exp(-jnp.abs(x)))


def _ssd_body(xbc_ref, dt_ref, z_ref, convw_ref, convb_ref, dtb_ref, alog_ref, dskip_ref, normg_ref,
              y_ref, buf_scr, state_scr, y_scr):
    t = SSD_CHUNK
    d_inner = SSD_HEADS * SSD_HEAD_DIM
    gn = SSD_GROUPS * SSD_STATE
    hp = SSD_HEAD_DIM
    r = SSD_HEADS // SSD_GROUPS

    @pl.when(pl.program_id(1) == 0)
    def _():
        buf_scr[0:8, :] = jnp.zeros((8, buf_scr.shape[1]), F32)
        state_scr[...] = jnp.zeros(state_scr.shape, F32)

    x = xbc_ref[...]
    buf_scr[8:8 + t, :] = x
    acc = convb_ref[...] + convw_ref[SSD_CONV - 1:SSD_CONV, :] * x
    for j in range(1, SSD_CONV):
        acc = acc + convw_ref[SSD_CONV - 1 - j:SSD_CONV - j, :] * buf_scr[8 - j:8 - j + t, :]
    buf_scr[0:8, :] = x[t - 8:t, :]
    xbc = _silu(acc)

    dt = _softplus(dt_ref[...] + dtb_ref[...])
    a = dt * (-jnp.exp(alog_ref[...]))
    rows = lax.broadcasted_iota(jnp.int32, (t, t), 0)
    cols = lax.broadcasted_iota(jnp.int32, (t, t), 1)
    tri = rows >= cols
    a_cum = jnp.dot(tri.astype(F32), a, precision=lax.Precision.HIGHEST, preferred_element_type=F32)
    a_last = a_cum[t - 1:t, :]
    a_cum_t = a_cum.T
    dt_t = dt.T
    w_end_t = (dt * jnp.exp(a_last - a_cum)).T
    chunk_dec = jnp.exp(a_last)

    for g in range(SSD_GROUPS):
        b_f = xbc[:, d_inner + g * SSD_STATE:d_inner + (g + 1) * SSD_STATE]
        c_g = xbc[:, d_inner + gn + g * SSD_STATE:d_inner + gn + (g + 1) * SSD_STATE].astype(BF16)
        b_g = b_f.astype(BF16)
        b_ft = b_f.T
        cb = _dot_nt(c_g, b_g)
        y_off_g = _dot(c_g, state_scr[:, g * r * hp:(g + 1) * r * hp].astype(BF16))
        for k in range(r):
            h = g * r + k
            col = jnp.broadcast_to(a_cum[:, h:h + 1], (t, t))
            decay = jnp.exp(jnp.where(tri, col - a_cum_t[h:h + 1, :], NEG_INF))
            xh = xbc[:, h * hp:(h + 1) * hp]
            xb = xh.astype(BF16)
            y_diag = _dot((cb * decay * dt_t[h:h + 1, :]).astype(BF16), xb)
            y_off = y_off_g[:, k * hp:(k + 1) * hp] * jnp.exp(col[:, :hp])
            y_scr[:, h * hp:(h + 1) * hp] = y_diag + y_off + dskip_ref[:, h * hp:(h + 1) * hp] * xh
            st = state_scr[:, h * hp:(h + 1) * hp]
            b_w = (b_ft * w_end_t[h:h + 1, :]).astype(BF16)
            state_scr[:, h * hp:(h + 1) * hp] = st * chunk_dec[:, h:h + 1] + _dot(b_w, xb)

    y = y_scr[...] * _silu(z_ref[...])
    gs = d_inner // SSD_GROUPS
    for g in range(SSD_GROUPS):
        sl = slice(g * gs, (g + 1) * gs)
        y_ref[:, sl] = _rms(y[:, sl], normg_ref[:, sl]).astype(y_ref.dtype)


def _ssd_mixer(h3, w_in, conv_w, conv_b, dt_bias, a_log, d_skip, norm_g):
    b, lp, d = h3.shape
    n = b * lp
    d_inner = SSD_HEADS * SSD_HEAD_DIM
    conv_ch = d_inner + 2 * SSD_GROUPS * SSD_STATE
    w_z, w_xbc, w_dt = w_in[:, :d_inner], w_in[:, d_inner:d_inner + conv_ch], w_in[:, d_inner + conv_ch:]
    w_dt = jnp.pad(w_dt, ((0, 0), (0, LANES - SSD_HEADS)))
    w_all = jnp.concatenate([w_xbc, w_z, w_dt], axis=1).astype(BF16)
    tm = _pick_tile(n, (512, 384, 256, 128))
    xbc, z, dt = _row_call(_ssd_in_body, [h3.reshape(n, d)], [w_all],
                           [(conv_ch, F32), (d_inner, F32), (LANES, F32)], tm, "ssd_in")
    t = SSD_CHUNK
    pad1 = lambda v: jnp.pad(v, (0, LANES - SSD_HEADS)).reshape(1, LANES)
    consts = [conv_w, conv_b.reshape(1, conv_ch), pad1(dt_bias), pad1(a_log),
              jnp.repeat(d_skip, SSD_HEAD_DIM).reshape(1, d_inner), norm_g.reshape(1, d_inner)]
    row_spec = lambda w: pl.BlockSpec((None, t, w), lambda i, c: (i, c, 0))
    y = pl.pallas_call(
        _ssd_body, grid=(b, lp // t),
        in_specs=[row_spec(conv_ch), row_spec(LANES), row_spec(d_inner)]
        + [pl.BlockSpec(c.shape, lambda i, c_: (0, 0)) for c in consts],
        out_specs=row_spec(d_inner),
        out_shape=jax.ShapeDtypeStruct((b, lp, d_inner), BF16),
        scratch_shapes=[pltpu.VMEM((8 + t, conv_ch), F32), pltpu.VMEM((SSD_STATE, d_inner), F32),
                        pltpu.VMEM((t, d_inner), F32)],
        name="ssd_scan",
        compiler_params=pltpu.CompilerParams(dimension_semantics=("parallel", "arbitrary"),
                                             vmem_limit_bytes=VMEM_LIMIT),
    )(xbc.reshape(b, lp, conv_ch), dt.reshape(b, lp, LANES), z.reshape(b, lp, d_inner), *consts)
    return y.reshape(n, d_inner)


def _qkv_body(n_q, scale, h_ref, w_ref, qk_ref, vt_ref):
    r = _dot(h_ref[...].astype(BF16), w_ref[...])
    n_qk = qk_ref.shape[1]
    qk_ref[:, :n_q] = (r[:, :n_q] * scale).astype(qk_ref.dtype)
    qk_ref[:, n_q:] = r[:, n_q:n_qk].astype(qk_ref.dtype)
    vt_ref[...] = r[:, n_qk:].T.astype(vt_ref.dtype)


def _causal_mask_pair(s, tq):
    rows = lax.broadcasted_iota(jnp.int32, s.shape, 0)
    cols = lax.broadcasted_iota(jnp.int32, s.shape, 1)
    cols = jnp.where(cols >= tq, cols - tq, cols)
    return jnp.where(rows <= cols, s, NEG_INF)


def _flash_causal(qi, tq, scores, vt_ref, m_scr, l_scr, acc_scr, s_scrs, p_scrs):
    m_scr[...] = jnp.full(m_scr.shape, NEG_INF, F32)
    l_scr[...] = jnp.zeros(l_scr.shape, F32)
    acc_scr[...] = jnp.zeros(acc_scr.shape, F32)
    s_scrs[0][...] = scores(0)
    p_scrs[1][...] = jnp.zeros(p_scrs[1].shape, p_scrs[1].dtype)

    def softmax(s, pv):
        m_prev = m_scr[...]
        m_new = jnp.maximum(m_prev, jnp.max(s, axis=0, keepdims=True))
        alpha = jnp.exp2(m_prev - m_new)
        p = jnp.exp2(s - m_new)
        l_scr[...] = alpha * l_scr[...] + jnp.sum(p, axis=0, keepdims=True)
        m_scr[...] = m_new
        acc_scr[...] = alpha * (acc_scr[...] + pv)
        return p.astype(BF16)

    def step(kj, cur, nxt):
        s_scrs[nxt][...] = scores(kj + 1)
        pv = _dot(vt_ref[jnp.maximum(kj - 1, 0)], p_scrs[nxt][...])
        p_scrs[cur][...] = softmax(s_scrs[cur][...], pv)

    def body(kj, carry):
        for par in range(2):
            pl.when((kj & 1) == par)(functools.partial(step, kj, par, 1 - par))
        return carry

    lax.fori_loop(0, qi, body, 0)

    def last(cur, nxt):
        pv = _dot(vt_ref[jnp.maximum(qi - 1, 0)], p_scrs[nxt][...])
        p = softmax(_causal_mask_pair(s_scrs[cur][...], tq), pv)
        acc_scr[...] = acc_scr[...] + _dot(vt_ref[qi], p)

    for par in range(2):
        pl.when((qi & 1) == par)(functools.partial(last, par, 1 - par))


def _flash_scratch(e, tq):
    w = 2 * tq
    return [pltpu.VMEM((1, w), F32), pltpu.VMEM((1, w), F32), pltpu.VMEM((e, w), F32),
            pltpu.VMEM((tq, w), F32), pltpu.VMEM((tq, w), F32), pltpu.VMEM((tq, w), BF16), pltpu.VMEM((tq, w), BF16)]


def _t5_bucket(dist):
    max_exact = REL_BUCKETS // 2
    d = jnp.maximum(dist, 0)
    df = jnp.maximum(d, max_exact).astype(F32)
    large = max_exact + (jnp.log(df / max_exact) / math.log(REL_MAX_DIST / max_exact)
                         * (REL_BUCKETS - max_exact)).astype(jnp.int32)
    large = jnp.minimum(large, REL_BUCKETS - 1)
    return jnp.where(d < max_exact, d, large)


def _diff_attn_body(tq, lam_init, q_ref, k_ref, vt_ref, bias_ref, lam_ref, g_ref, o_ref,
                    m_scr, l_scr, acc_scr, s0_scr, s1_scr, p0_scr, p1_scr, qcat_scr):
    qi = pl.program_id(2)
    q = q_ref[...]
    lane = lax.broadcasted_iota(jnp.int32, q.shape, 1)
    zero = jnp.zeros_like(q)
    qcat_scr[0:tq, :] = jnp.where(lane < DIFF_HEAD_DIM, q, zero)
    qcat_scr[tq:2 * tq, :] = jnp.where(lane >= DIFF_HEAD_DIM, q, zero)

    def scores(kj):
        start = pl.multiple_of(kj * tq, tq)
        bias = bias_ref[jnp.minimum(qi - kj, 2)]
        return _dot_nt(k_ref[pl.ds(start, tq), :], qcat_scr[...]) + jnp.concatenate([bias, bias], axis=1)

    _flash_causal(qi, tq, scores, vt_ref, m_scr, l_scr, acc_scr, (s0_scr, s1_scr), (p0_scr, p1_scr))
    acc = acc_scr[...] * (1.0 / l_scr[...])
    o_t = acc[:, :tq] - lam_ref[...] * acc[:, tq:]
    o_ref[...] = (_rms(o_t.T, g_ref[...]) * (1.0 - lam_init)).astype(o_ref.dtype)


def _diff_mixer(h3, w_qkv, lam_q1, lam_k1, lam_q2, lam_k2, subln_g, rel_bias, lam_init):
    b, lp, d = h3.shape
    n = b * lp
    hd = DIFF_HEADS * 2 * DIFF_HEAD_DIM
    tq = _pick_tile(lp, (384, 256, 128))
    assert tq >= REL_MAX_DIST
    nk = lp // tq
    qk, vt = _row_call(functools.partial(_qkv_body, hd, DIFF_HEAD_DIM ** -0.5 * LOG2E), [h3.reshape(n, d)],
                       [w_qkv.astype(BF16)], [(2 * hd, BF16)], tq, "diff_qkv", lp=lp, t_outs=[(hd, BF16)])
    qk = qk.reshape(b, lp, 2 * hd)
    m2 = 2 * tq
    kk = jnp.arange(m2)
    dist = jnp.where(kk < tq, kk, kk - m2)[None, :] + (jnp.arange(3) * tq)[:, None]
    tab = jnp.transpose(rel_bias[_t5_bucket(dist)].astype(F32) * LOG2E, (2, 0, 1))
    nh = DIFF_HEADS
    bias = jnp.tile(tab, (1, 1, tq))[..., :tq * (m2 - 1)].reshape(nh, 3, tq, m2 - 1)[..., :tq]
    lam = (jnp.exp(jnp.sum(lam_q1.astype(F32) * lam_k1.astype(F32)))
           - jnp.exp(jnp.sum(lam_q2.astype(F32) * lam_k2.astype(F32))) + lam_init)
    w = 2 * DIFF_HEAD_DIM
    o = pl.pallas_call(
        functools.partial(_diff_attn_body, tq, lam_init),
        grid=(b, nh, nk),
        in_specs=[pl.BlockSpec((None, tq, w), lambda bi, hi, qi: (bi, qi, hi)),
                  pl.BlockSpec((None, lp, w), lambda bi, hi, qi: (bi, 0, nh + hi)),
                  pl.BlockSpec((None, nk, DIFF_V_DIM, tq), lambda bi, hi, qi: (bi, 0, hi, 0)),
                  pl.BlockSpec((None, 3, tq, tq), lambda bi, hi, qi: (hi, 0, 0, 0)),
                  pl.BlockSpec((1, 1), lambda bi, hi, qi: (0, 0)),
                  pl.BlockSpec((1, DIFF_V_DIM), lambda bi, hi, qi: (0, 0))],
        out_specs=pl.BlockSpec((None, tq, DIFF_V_DIM), lambda bi, hi, qi: (bi, qi, hi)),
        out_shape=jax.ShapeDtypeStruct((b, lp, nh * DIFF_V_DIM), BF16),
        scratch_shapes=_flash_scratch(DIFF_V_DIM, tq) + [pltpu.VMEM((2 * tq, w), BF16)],
        name="diff_attn",
        compiler_params=pltpu.CompilerParams(dimension_semantics=("parallel", "parallel", "arbitrary"),
                                             vmem_limit_bytes=VMEM_LIMIT),
    )(qk, qk, vt, bias, lam.reshape(1, 1), subln_g.reshape(1, DIFF_V_DIM))
    return o.reshape(n, nh * DIFF_V_DIM)


def _gelu_tanh(x):
    return 0.5 * x * (1.0 + jnp.tanh(math.sqrt(2.0 / math.pi) * (x + 0.044715 * (x * x * x))))


def _s5_body(u_ref, wb_ref, wc_ref, are_ref, aim_ref, d_ref, y_ref, s_scr, st_scr):
    rows = u_ref.shape[0]
    tc = rows // S5_SUBLANES
    n_sl = wb_ref.shape[0]
    kw = wb_ref.shape[1]
    sw = wb_ref.shape[2] // 2
    half = n_sl * sw

    @pl.when(pl.program_id(0) == 0)
    def _():
        st_scr[...] = jnp.zeros(st_scr.shape, F32)

    u = u_ref[...]
    ub = u.astype(BF16)
    for q in range(n_sl):
        bu = _dot(ub[:, q * kw:(q + 1) * kw], wb_ref[q])
        s_scr[:, q * sw:(q + 1) * sw] = bu[:, :sw]
        s_scr[:, half + q * sw:half + (q + 1) * sw] = bu[:, sw:]

    for q in range(n_sl):
        re_sl = slice(q * sw, (q + 1) * sw)
        im_sl = slice(half + q * sw, half + (q + 1) * sw)
        ar = are_ref[:, re_sl]
        ai = aim_ref[:, re_sl]

        def step(t, carry):
            sr, si = carry
            r0 = pl.multiple_of(t * S5_SUBLANES, S5_SUBLANES)
            nr = ar * sr - ai * si + s_scr[pl.ds(r0, S5_SUBLANES), re_sl]
            ni = ar * si + ai * sr + s_scr[pl.ds(r0, S5_SUBLANES), im_sl]
            s_scr[pl.ds(r0, S5_SUBLANES), re_sl] = nr
            s_scr[pl.ds(r0, S5_SUBLANES), im_sl] = ni
            return nr, ni

        sr, si = lax.fori_loop(0, tc, step, (st_scr[:, re_sl], st_scr[:, im_sl]))
        st_scr[:, re_sl] = sr
        st_scr[:, im_sl] = si

    for q in range(n_sl):
        s_re = s_scr[:, q * sw:(q + 1) * sw].astype(BF16)
        s_im = s_scr[:, half + q * sw:half + (q + 1) * sw].astype(BF16)
        y = _dot(s_re, wc_ref[q, :sw, :]) + _dot(s_im, wc_ref[q, sw:, :])
        y = y + d_ref[:, q * kw:(q + 1) * kw] * u[:, q * kw:(q + 1) * kw]
        y_ref[:, q * kw:(q + 1) * kw] = _gelu_tanh(y).astype(y_ref.dtype)


def _s5_glu_ln_body(y_ref, h_ref, w_ref, bg_ref, g_ref, b_ref, o_ref):
    r = _dot(y_ref[...], w_ref[...]) + bg_ref[...]
    d = o_ref.shape[1]
    m = r[:, :d] * (1.0 / (1.0 + jnp.exp(-r[:, d:])))
    o_ref[...] = _layer_norm(ALPHA * h_ref[...] + m, g_ref[...], b_ref[...])


def _block_diag(blocks):
    s, g, a, b_ = blocks.shape
    eye = jnp.eye(g, dtype=blocks.dtype)
    return jnp.einsum('sgab,gh->sgahb', blocks, eye).reshape(s, g * a, g * b_)


def _s5_mixer(h3, lam_re, lam_im, log_step, b_re, b_im, c_re, c_im, d_skip, w_glu, b_glu, ln_g, ln_b):
    b, lp, d = h3.shape
    assert b <= S5_SUBLANES
    step = jnp.exp(log_step.astype(F32))[:, None]
    lr = jnp.minimum(lam_re.astype(F32), -1e-4)
    li = lam_im.astype(F32)
    mag = jnp.exp(lr * step)
    ar, ai = mag * jnp.cos(li * step), mag * jnp.sin(li * step)
    den = lr * lr + li * li
    cr = ((ar - 1.0) * lr + ai * li) / den
    ci = (ai * lr - (ar - 1.0) * li) / den
    br, bi = b_re.astype(F32), b_im.astype(F32)
    bbr = cr[..., None] * br - ci[..., None] * bi
    bbi = cr[..., None] * bi + ci[..., None] * br
    n_sl = S5_GROUPS // S5_SLICE_GROUPS
    sg = S5_SLICE_GROUPS
    to_sl = lambda m: m.reshape(n_sl, sg, *m.shape[1:])
    wb = jnp.concatenate([_block_diag(to_sl(jnp.transpose(bbr, (0, 2, 1)))),
                          _block_diag(to_sl(jnp.transpose(bbi, (0, 2, 1))))], axis=2).astype(BF16)
    wc = jnp.concatenate([_block_diag(to_sl(jnp.transpose(c_re.astype(F32), (0, 2, 1)))),
                          _block_diag(to_sl(-jnp.transpose(c_im.astype(F32), (0, 2, 1))))], axis=1).astype(BF16)
    n_state = S5_GROUPS * S5_STATE
    a_re = jnp.broadcast_to(ar.reshape(1, n_state), (S5_SUBLANES, n_state))
    a_im = jnp.broadcast_to(ai.reshape(1, n_state), (S5_SUBLANES, n_state))
    u = jnp.pad(jnp.transpose(h3, (1, 0, 2)), ((0, 0), (0, S5_SUBLANES - b), (0, 0))).reshape(lp * S5_SUBLANES, d)
    tc = S5_TCHUNK
    rows = tc * S5_SUBLANES
    const = lambda a: pl.BlockSpec(a.shape, lambda i: (0,) * a.ndim)
    d_row = d_skip.astype(F32).reshape(1, d)
    y = pl.pallas_call(
        _s5_body, grid=(lp // tc,),
        in_specs=[pl.BlockSpec((rows, d), lambda i: (i, 0)), const(wb), const(wc), const(a_re), const(a_im),
                  const(d_row)],
        out_specs=pl.BlockSpec((rows, d), lambda i: (i, 0)),
        out_shape=jax.ShapeDtypeStruct((lp * S5_SUBLANES, d), BF16),
        scratch_shapes=[pltpu.VMEM((rows, 2 * n_state), F32), pltpu.VMEM((S5_SUBLANES, 2 * n_state), F32)],
        name="s5_scan",
        compiler_params=pltpu.CompilerParams(dimension_semantics=("arbitrary",), vmem_limit_bytes=VMEM_LIMIT),
    )(u, wb, wc, a_re, a_im, d_row)
    y = jnp.transpose(y.reshape(lp, S5_SUBLANES, d)[:, :b], (1, 0, 2)).reshape(b * lp, d)
    n = b * lp
    tm = _pick_tile(n, (512, 384, 256, 128))
    (out,) = _row_call(_s5_glu_ln_body, [y, h3.reshape(n, d)],
                       [w_glu.astype(BF16), b_glu.reshape(1, 2 * d), ln_g.reshape(1, d), ln_b.reshape(1, d)],
                       [(d, F32)], tm, "s5_glu_ln")
    return out


def _rope_block(x, cm, s1, s2):
    return (x * cm + pltpu.roll(x, LANES - MLA_ROPE // 2, axis=1) * s1 + pltpu.roll(x, MLA_ROPE // 2, axis=1) * s2)


def _mla_in_body(h_ref, qcm_ref, qs1_ref, qs2_ref, kcm_ref, ks1_ref, ks2_ref,
                 win_ref, qg_ref, kvg_ref, wuq_ref, wuk_ref, wuv_ref, q_ref, k_ref, vt_ref):
    c = _dot(h_ref[...].astype(BF16), win_ref[...])
    c_q = _rms(c[:, :MLA_Q_RANK], qg_ref[...]).astype(BF16)
    c_kv = _rms(c[:, MLA_Q_RANK:MLA_Q_RANK + MLA_KV_RANK], kvg_ref[...]).astype(BF16)
    k_r = _rope_block(c[:, MLA_Q_RANK + MLA_KV_RANK:], kcm_ref[...], ks1_ref[...], ks2_ref[...])
    q = _dot(c_q, wuq_ref[...])
    k = _dot(c_kv, wuk_ref[...])
    qcm, qs1, qs2 = qcm_ref[...], qs1_ref[...], qs2_ref[...]
    for h in range(MLA_HEADS):
        sl = slice(h * LANES, (h + 1) * LANES)
        q_ref[:, sl] = _rope_block(q[:, sl], qcm, qs1, qs2).astype(q_ref.dtype)
        k_ref[:, sl] = (k[:, sl] + k_r).astype(k_ref.dtype)
    vt_ref[...] = _dot(c_kv, wuv_ref[...]).T.astype(vt_ref.dtype)


def _mla_attn_body(tq, q_ref, k_ref, vt_ref, o_ref, m_scr, l_scr, acc_scr, s0_scr, s1_scr, p0_scr, p1_scr):
    qi = pl.program_id(2)

    def scores(kj):
        start = pl.multiple_of(kj * tq, tq)
        return jnp.concatenate(
            [_dot_nt(k_ref[pl.ds(start, tq), hh * LANES:(hh + 1) * LANES], q_ref[:, hh * LANES:(hh + 1) * LANES])
             for hh in range(2)], axis=1)

    _flash_causal(qi, tq, scores, vt_ref, m_scr, l_scr, acc_scr, (s0_scr, s1_scr), (p0_scr, p1_scr))
    acc = acc_scr[...] * (1.0 / l_scr[...])
    row = lax.broadcasted_iota(jnp.int32, (2 * MLA_V, tq), 0)
    o_t = jnp.where(row < MLA_V, acc[:, :tq], acc[:, tq:])
    o_ref[...] = o_t.T.astype(o_ref.dtype)


def _mla_mixer(h3, w_in, q_norm_g, kv_norm_g, w_uq, w_ukv):
    b, lp, d = h3.shape
    n = b * lp
    nh = MLA_HEADS
    qk = MLA_NOPE + MLA_ROPE
    half = MLA_ROPE // 2
    w_kr = jnp.pad(w_in[:, MLA_Q_RANK + MLA_KV_RANK:], ((0, 0), (MLA_NOPE, LANES - qk)))
    w_in_all = jnp.concatenate([w_in[:, :MLA_Q_RANK + MLA_KV_RANK], w_kr], axis=1).astype(BF16)
    w_uq_p = jnp.pad(w_uq.reshape(MLA_Q_RANK, nh, qk), ((0, 0), (0, 0), (0, LANES - qk)))
    w_uq_p = w_uq_p.reshape(MLA_Q_RANK, nh * LANES).astype(BF16)
    w_ukv3 = w_ukv.reshape(MLA_KV_RANK, nh, MLA_NOPE + MLA_V)
    w_uk_p = jnp.pad(w_ukv3[:, :, :MLA_NOPE], ((0, 0), (0, 0), (0, LANES - MLA_NOPE)))
    w_uk_p = w_uk_p.reshape(MLA_KV_RANK, nh * LANES).astype(BF16)
    w_uv = w_ukv3[:, :, MLA_NOPE:].reshape(MLA_KV_RANK, nh * MLA_V).astype(BF16)
    pos = jnp.arange(lp, dtype=F32)
    inv_freq = ROPE_BASE ** (-jnp.arange(0, MLA_ROPE, 2, dtype=F32) / MLA_ROPE)
    ang = pos[:, None] * inv_freq[None, :]
    cos, sin = jnp.cos(ang), jnp.sin(ang)
    z = lambda w: jnp.zeros((lp, w), F32)
    cm = jnp.concatenate([jnp.ones((lp, MLA_NOPE), F32), cos, cos, z(LANES - qk)], axis=1)
    s1 = jnp.concatenate([z(MLA_NOPE), -sin, z(LANES - MLA_NOPE - half)], axis=1)
    s2 = jnp.concatenate([z(MLA_NOPE + half), sin, z(LANES - qk)], axis=1)
    scale = qk ** -0.5 * LOG2E
    tq = _pick_tile(lp, (384, 256, 128))
    nk = lp // tq
    q, k, vt = _row_call(
        _mla_in_body, [h3.reshape(n, d)],
        [w_in_all, q_norm_g.reshape(1, MLA_Q_RANK), kv_norm_g.reshape(1, MLA_KV_RANK), w_uq_p, w_uk_p, w_uv],
        [(nh * LANES, BF16), (nh * LANES, BF16)], tq, "mla_in",
        pos_ins=[cm * scale, s1 * scale, s2 * scale, cm, s1, s2], lp=lp, t_outs=[(nh * MLA_V, BF16)])
    q, k = q.reshape(b, lp, nh * LANES), k.reshape(b, lp, nh * LANES)
    o = pl.pallas_call(
        functools.partial(_mla_attn_body, tq),
        grid=(b, nh // 2, nk),
        in_specs=[pl.BlockSpec((None, tq, 2 * LANES), lambda bi, hi, qi: (bi, qi, hi)),
                  pl.BlockSpec((None, lp, 2 * LANES), lambda bi, hi, qi: (bi, 0, hi)),
                  pl.BlockSpec((None, nk, 2 * MLA_V, tq), lambda bi, hi, qi: (bi, 0, hi, 0))],
        out_specs=pl.BlockSpec((None, tq, 2 * MLA_V), lambda bi, hi, qi: (bi, qi, hi)),
        out_shape=jax.ShapeDtypeStruct((b, lp, nh * MLA_V), BF16),
        scratch_shapes=_flash_scratch(2 * MLA_V, tq),
        name="mla_attn",
        compiler_params=pltpu.CompilerParams(dimension_semantics=("parallel", "parallel", "arbitrary"),
                                             vmem_limit_bytes=VMEM_LIMIT),
    )(q, k, vt)
    return o.reshape(n, nh * MLA_V)


def kernel(x, meta, rel_bias, ln_g, ln_b, ffn_w1, ffn_w3, ffn_w2, ssd_w_in, ssd_conv_w, ssd_conv_b, ssd_dt_bias, ssd_a_log, ssd_d, ssd_norm_g, ssd_w_out, diff_w_qkv, diff_lam_q1, diff_lam_k1, diff_lam_q2, diff_lam_k2, diff_subln_g, diff_w_out, s5_lam_re, s5_lam_im, s5_log_step, s5_b_re, s5_b_im, s5_c_re, s5_c_im, s5_d, s5_w_glu, s5_b_glu, mla_w_in, mla_q_norm_g, mla_kv_norm_g, mla_w_uq, mla_w_ukv, mla_w_out):
    b, seq, d = x.shape
    l = seq + N_META
    lp = -(-l // SEQ_ALIGN) * SEQ_ALIGN
    n = b * lp
    h = jnp.concatenate([jnp.broadcast_to(meta[None].astype(x.dtype), (b, N_META, d)), x,
                         jnp.zeros((b, lp - l, d), x.dtype)], axis=1).reshape(n, d)
    for i in range(DEPTH):
        kind, j = i % 4, i // 4
        h = _ffn(h, ffn_w1[i, 0], ffn_w3[i, 0], ffn_w2[i, 0], ln_g[i, 0], ln_b[i, 0])
        h3 = h.reshape(b, lp, d)
        if kind == 0:
            y = _ssd_mixer(h3, ssd_w_in[j], ssd_conv_w[j], ssd_conv_b[j], ssd_dt_bias[j], ssd_a_log[j],
                           ssd_d[j], ssd_norm_g[j])
            h = _proj_ln(y, h, ssd_w_out[j], ln_g[i, 1], ln_b[i, 1])
        elif kind == 1:
            lam_init = 0.8 - 0.6 * math.exp(-0.3 * i)
            y = _diff_mixer(h3, diff_w_qkv[j], diff_lam_q1[j], diff_lam_k1[j], diff_lam_q2[j], diff_lam_k2[j],
                            diff_subln_g[j], rel_bias, lam_init)
            h = _proj_ln(y, h, diff_w_out[j], ln_g[i, 1], ln_b[i, 1])
        elif kind == 2:
            h = _s5_mixer(h3, s5_lam_re[j], s5_lam_im[j], s5_log_step[j], s5_b_re[j], s5_b_im[j], s5_c_re[j],
                          s5_c_im[j], s5_d[j], s5_w_glu[j], s5_b_glu[j], ln_g[i, 1], ln_b[i, 1])
        else:
            y = _mla_mixer(h3, mla_w_in[j], mla_q_norm_g[j], mla_kv_norm_g[j], mla_w_uq[j], mla_w_ukv[j])
            h = _proj_ln(y, h, mla_w_out[j], ln_g[i, 1], ln_b[i, 1])
        h = _ffn(h, ffn_w1[i, 1], ffn_w3[i, 1], ffn_w2[i, 1], ln_g[i, 2], ln_b[i, 2])
    return h.reshape(b, lp, d)[:, N_META:l]
```

```python
import functools
import math

import jax
import jax.numpy as jnp
from jax import lax
from jax.experimental import pallas as pl
from jax.experimental.pallas import tpu as pltpu

F32 = jnp.float32
BF16 = jnp.bfloat16

N_META = 16
DEPTH = 4
ALPHA = (2.0 * DEPTH) ** 0.25
LN_EPS = 1e-5
RMS_EPS = 1e-6
NEG_INF = -1e30
LOG2E = math.log2(math.e)

SSD_HEAD_DIM = 64
SSD_HEADS = 32
SSD_GROUPS = 8
SSD_STATE = 128
SSD_CONV = 4
SSD_CHUNK = 128

DIFF_HEADS = 8
DIFF_HEAD_DIM = 64
DIFF_V_DIM = 128
REL_BUCKETS = 32
REL_MAX_DIST = 128

S5_GROUP = 16
S5_GROUPS = 64
S5_STATE = 64
S5_SLICE_GROUPS = 16
S5_SUBLANES = 8
S5_TCHUNK = 128
S5_UNROLL = 4

MLA_HEADS = 16
MLA_Q_RANK = 384
MLA_KV_RANK = 256
MLA_NOPE = 64
MLA_ROPE = 32
MLA_V = 64
ROPE_BASE = 10000.0

FFN_ROW_TILES = (512, 384, 256, 128)
FFN_CHUNKS = 11

LANES = 128
SEQ_ALIGN = 128
VMEM_LIMIT = 56 * 1024 * 1024


def _pick_tile(n, candidates):
    for c in candidates:
        if n % c == 0:
            return c
    raise ValueError(f"no tile for {n}")


def _layer_norm(z, g, b):
    mu = jnp.mean(z, axis=-1, keepdims=True)
    d = z - mu
    var = jnp.mean(d * d, axis=-1, keepdims=True)
    return d * lax.rsqrt(var + LN_EPS) * g + b


def _rms(z, g):
    return z * lax.rsqrt(jnp.mean(z * z, axis=-1, keepdims=True) + RMS_EPS) * g


def _silu(x):
    return x * (1.0 / (1.0 + jnp.exp(-x)))


def _dot(a, b):
    return jnp.dot(a, b, preferred_element_type=F32)


def _dot_nt(a, b):
    return lax.dot_general(a, b, (((1,), (1,)), ((), ())), preferred_element_type=F32)


def _row_call(body, row_ins, const_ins, outs, tm, name, pos_ins=(), lp=None, t_outs=()):
    n = row_ins[0].shape[0]
    assert n % tm == 0
    in_specs = [pl.BlockSpec((tm, a.shape[1]), lambda i: (i, 0)) for a in row_ins]
    if pos_ins or t_outs:
        assert lp % tm == 0
        nb = lp // tm
        in_specs += [pl.BlockSpec((tm, a.shape[1]), lambda i: (i % nb, 0)) for a in pos_ins]
    const_arrays = []
    for c in const_ins:
        arr, lead = c if isinstance(c, tuple) else (c, ())
        const_arrays.append(arr)
        in_specs.append(pl.BlockSpec((None,) * len(lead) + arr.shape[len(lead):],
                                     functools.partial(lambda lead_, i: lead_ + (0, 0), tuple(lead)),
                                     pipeline_mode=pl.Buffered(1)))
    out_specs = [pl.BlockSpec((tm, w), lambda i: (i, 0)) for (w, _) in outs]
    out_shape = [jax.ShapeDtypeStruct((n, w), dt) for (w, dt) in outs]
    out_specs += [pl.BlockSpec((None, None, w, tm), lambda i: (i // nb, i % nb, 0, 0)) for (w, _) in t_outs]
    out_shape += [jax.ShapeDtypeStruct((n // lp, nb, w, tm), dt) for (w, dt) in t_outs]
    res = pl.pallas_call(
        body, grid=(n // tm,), in_specs=in_specs, out_specs=out_specs, out_shape=out_shape, name=name,
        compiler_params=pltpu.CompilerParams(dimension_semantics=("parallel",), vmem_limit_bytes=VMEM_LIMIT),
    )(*row_ins, *pos_ins, *const_arrays)
    return res


def _ffn_body(n_chunks, h_ref, w1_ref, w3_ref, w2_ref, g_ref, b_ref, o_ref):
    x = h_ref[...]
    xb = x.astype(BF16)
    fc = w1_ref.shape[1] // n_chunks
    y = None
    for c in range(n_chunks):
        sl = slice(c * fc, (c + 1) * fc)
        gate = _dot(xb, w1_ref[:, sl])
        up = _dot(xb, w3_ref[:, sl])
        a = (_silu(gate) * up).astype(BF16)
        part = _dot(a, w2_ref[sl, :])
        y = part if y is None else y + part
    o_ref[...] = _layer_norm(ALPHA * x + 0.5 * y, g_ref[...], b_ref[...])


def _ffn(h2, w1, w3, w2, g, b, idx, tm=None, n_chunks=None, name="ffn"):
    n, d = h2.shape
    tm = tm or _pick_tile(n, FFN_ROW_TILES)
    f = w1.shape[-1]
    n_chunks = n_chunks or (FFN_CHUNKS if f % (FFN_CHUNKS * LANES) == 0 else 1)
    (out,) = _row_call(functools.partial(_ffn_body, n_chunks), [h2],
                       [(w1, idx), (w3, idx), (w2, idx), g.reshape(1, d), b.reshape(1, d)],
                       [(d, F32)], tm, name)
    return out


def _proj_ln_body(y_ref, h_ref, w_ref, g_ref, b_ref, o_ref):
    m = _dot(y_ref[...], w_ref[...])
    o_ref[...] = _layer_norm(ALPHA * h_ref[...] + m, g_ref[...], b_ref[...])


def _proj_ln(y2, h2, w, g, b):
    n, d = h2.shape
    tm = _pick_tile(n, (512, 384, 256, 128))
    (out,) = _row_call(_proj_ln_body, [y2, h2], [w.astype(BF16), g.reshape(1, d), b.reshape(1, d)],
                       [(d, F32)], tm, "proj_ln")
    return out


def _ssd_in_body(h_ref, w_ref, xbc_ref, z_ref, dt_ref):
    r = _dot(h_ref[...].astype(BF16), w_ref[...])
    nx, nz = xbc_ref.shape[1], z_ref.shape[1]
    xbc_ref[...] = r[:, :nx]
    z_ref[...] = r[:, nx:nx + nz]
    dt_ref[...] = r[:, nx + nz:]


def _softplus(x):
    return jnp.maximum(x, 0.0) + jnp.log1p(jnp.exp(-jnp.abs(x)))


def _ssd_body(xbc_ref, dt_ref, z_ref, convw_ref, convb_ref, dtb_ref, alog_ref, dskip_ref, normg_ref,
              y_ref, buf_scr, state_scr, y_scr):
    t = SSD_CHUNK
    d_inner = SSD_HEADS * SSD_HEAD_DIM
    gn = SSD_GROUPS * SSD_STATE
    hp = SSD_HEAD_DIM
    r = SSD_HEADS // SSD_GROUPS

    @pl.when(pl.program_id(1) == 0)
    def _():
        buf_scr[0:8, :] = jnp.zeros((8, buf_scr.shape[1]), F32)
        state_scr[...] = jnp.zeros(state_scr.shape, F32)

    x = xbc_ref[...]
    buf_scr[8:8 + t, :] = x
    acc = convb_ref[...] + convw_ref[SSD_CONV - 1:SSD_CONV, :] * x
    for j in range(1, SSD_CONV):
        acc = acc + convw_ref[SSD_CONV - 1 - j:SSD_CONV - j, :] * buf_scr[8 - j:8 - j + t, :]
    buf_scr[0:8, :] = x[t - 8:t, :]
    xbc = _silu(acc)

    dt = _softplus(dt_ref[...] + dtb_ref[...])
    a = dt * (-jnp.exp(alog_ref[...]))
    rows = lax.broadcasted_iota(jnp.int32, (t, t), 0)
    cols = lax.broadcasted_iota(jnp.int32, (t, t), 1)
    tri = rows >= cols
    a_cum = jnp.dot(tri.astype(F32), a, precision=lax.Precision.HIGHEST, preferred_element_type=F32)
    a_last = a_cum[t - 1:t, :]
    a_cum_t = a_cum.T
    dt_t = dt.T
    w_end_t = (dt * jnp.exp(a_last - a_cum)).T
    chunk_dec = jnp.exp(a_last)

    for g in range(SSD_GROUPS):
        b_f = xbc[:, d_inner + g * SSD_STATE:d_inner + (g + 1) * SSD_STATE]
        c_g = xbc[:, d_inner + gn + g * SSD_STATE:d_inner + gn + (g + 1) * SSD_STATE].astype(BF16)
        b_g = b_f.astype(BF16)
        b_ft = b_f.T
        cb = _dot_nt(c_g, b_g)
        y_off_g = _dot(c_g, state_scr[:, g * r * hp:(g + 1) * r * hp].astype(BF16))
        for k in range(r):
            h = g * r + k
            col = jnp.broadcast_to(a_cum[:, h:h + 1], (t, t))
            decay = jnp.exp(jnp.where(tri, col - a_cum_t[h:h + 1, :], NEG_INF))
            xh = xbc[:, h * hp:(h + 1) * hp]
            xb = xh.astype(BF16)
            y_diag = _dot((cb * decay * dt_t[h:h + 1, :]).astype(BF16), xb)
            y_off = y_off_g[:, k * hp:(k + 1) * hp] * jnp.exp(col[:, :hp])
            y_scr[:, h * hp:(h + 1) * hp] = y_diag + y_off + dskip_ref[:, h * hp:(h + 1) * hp] * xh
            st = state_scr[:, h * hp:(h + 1) * hp]
            b_w = (b_ft * w_end_t[h:h + 1, :]).astype(BF16)
            state_scr[:, h * hp:(h + 1) * hp] = st * chunk_dec[:, h:h + 1] + _dot(b_w, xb)

    y = y_scr[...] * _silu(z_ref[...])
    gs = d_inner // SSD_GROUPS
    for g in range(SSD_GROUPS):
        sl = slice(g * gs, (g + 1) * gs)
        y_ref[:, sl] = _rms(y[:, sl], normg_ref[:, sl]).astype(y_ref.dtype)


def _ssd_mixer(h3, w_in, conv_w, conv_b, dt_bias, a_log, d_skip, norm_g):
    b, lp, d = h3.shape
    n = b * lp
    d_inner = SSD_HEADS * SSD_HEAD_DIM
    conv_ch = d_inner + 2 * SSD_GROUPS * SSD_STATE
    w_z, w_xbc, w_dt = w_in[:, :d_inner], w_in[:, d_inner:d_inner + conv_ch], w_in[:, d_inner + conv_ch:]
    w_dt = jnp.pad(w_dt, ((0, 0), (0, LANES - SSD_HEADS)))
    w_all = jnp.concatenate([w_xbc, w_z, w_dt], axis=1).astype(BF16)
    tm = _pick_tile(n, (512, 384, 256, 128))
    xbc, z, dt = _row_call(_ssd_in_body, [h3.reshape(n, d)], [w_all],
                           [(conv_ch, F32), (d_inner, F32), (LANES, F32)], tm, "ssd_in")
    t = SSD_CHUNK
    pad1 = lambda v: jnp.pad(v, (0, LANES - SSD_HEADS)).reshape(1, LANES)
    consts = [conv_w, conv_b.reshape(1, conv_ch), pad1(dt_bias), pad1(a_log),
              jnp.repeat(d_skip, SSD_HEAD_DIM).reshape(1, d_inner), norm_g.reshape(1, d_inner)]
    row_spec = lambda w: pl.BlockSpec((None, t, w), lambda i, c: (i, c, 0))
    y = pl.pallas_call(
        _ssd_body, grid=(b, lp // t),
        in_specs=[row_spec(conv_ch), row_spec(LANES), row_spec(d_inner)]
        + [pl.BlockSpec(c.shape, lambda i, c_: (0, 0)) for c in consts],
        out_specs=row_spec(d_inner),
        out_shape=jax.ShapeDtypeStruct((b, lp, d_inner), BF16),
        scratch_shapes=[pltpu.VMEM((8 + t, conv_ch), F32), pltpu.VMEM((SSD_STATE, d_inner), F32),
                        pltpu.VMEM((t, d_inner), F32)],
        name="ssd_scan",
        compiler_params=pltpu.CompilerParams(dimension_semantics=("parallel", "arbitrary"),
                                             vmem_limit_bytes=VMEM_LIMIT),
    )(xbc.reshape(b, lp, conv_ch), dt.reshape(b, lp, LANES), z.reshape(b, lp, d_inner), *consts)
    return y.reshape(n, d_inner)


def _qkv_body(n_q, scale, h_ref, w_ref, qk_ref, vt_ref):
    r = _dot(h_ref[...].astype(BF16), w_ref[...])
    n_qk = qk_ref.shape[1]
    qk_ref[:, :n_q] = (r[:, :n_q] * scale).astype(qk_ref.dtype)
    qk_ref[:, n_q:] = r[:, n_q:n_qk].astype(qk_ref.dtype)
    vt_ref[...] = r[:, n_qk:].T.astype(vt_ref.dtype)


def _causal_mask_pair(s, tq):
    rows = lax.broadcasted_iota(jnp.int32, s.shape, 0)
    cols = lax.broadcasted_iota(jnp.int32, s.shape, 1)
    cols = jnp.where(cols >= tq, cols - tq, cols)
    return jnp.where(rows <= cols, s, NEG_INF)


def _flash_causal(qi, tq, scores, vt_ref, m_scr, l_scr, acc_scr, s_scrs, p_scrs, mx_scrs):
    m_scr[...] = jnp.full(m_scr.shape, NEG_INF, F32)
    l_scr[...] = jnp.zeros(l_scr.shape, F32)
    acc_scr[...] = jnp.zeros(acc_scr.shape, F32)
    p_scrs[1][...] = jnp.zeros(p_scrs[1].shape, p_scrs[1].dtype)

    def softmax(s, s_max, pv):
        m_prev = m_scr[...]
        m_new = jnp.maximum(m_prev, s_max)
        alpha = jnp.exp2(m_prev - m_new)
        p = jnp.exp2(s - m_new)
        l_scr[...] = alpha * l_scr[...] + jnp.sum(p, axis=0, keepdims=True)
        m_scr[...] = m_new
        acc_scr[...] = alpha * (acc_scr[...] + pv)
        return p.astype(BF16)

    def produce(kj, slot):
        s = scores(kj)
        s_scrs[slot][...] = s
        mx_scrs[slot][...] = jnp.max(s, axis=0, keepdims=True)

    def step(kj, cur, nxt):
        pv = _dot(vt_ref[jnp.maximum(kj - 1, 0)], p_scrs[nxt][...])
        p_scrs[cur][...] = softmax(s_scrs[cur][...], mx_scrs[cur][...], pv)
        produce(kj + 1, nxt)

    produce(0, 0)

    def body(t, carry):
        step(2 * t, 0, 1)
        step(2 * t + 1, 1, 0)
        return carry

    lax.fori_loop(0, lax.shift_right_logical(qi, 1), body, 0)
    pl.when((qi & 1) == 1)(functools.partial(step, qi - 1, 0, 1))

    def last(cur, nxt):
        pv = _dot(vt_ref[jnp.maximum(qi - 1, 0)], p_scrs[nxt][...])
        s = _causal_mask_pair(s_scrs[cur][...], tq)
        p = softmax(s, jnp.max(s, axis=0, keepdims=True), pv)
        acc_scr[...] = acc_scr[...] + _dot(vt_ref[qi], p)

    for par in range(2):
        pl.when((qi & 1) == par)(functools.partial(last, par, 1 - par))


def _flash_scratch(e, tq):
    w = 2 * tq
    return [pltpu.VMEM((1, w), F32), pltpu.VMEM((1, w), F32), pltpu.VMEM((e, w), F32),
            pltpu.VMEM((tq, w), F32), pltpu.VMEM((tq, w), F32), pltpu.VMEM((tq, w), BF16), pltpu.VMEM((tq, w), BF16),
            pltpu.VMEM((1, w), F32), pltpu.VMEM((1, w), F32)]


def _t5_bucket(dist):
    max_exact = REL_BUCKETS // 2
    d = jnp.maximum(dist, 0)
    df = jnp.maximum(d, max_exact).astype(F32)
    large = max_exact + (jnp.log(df / max_exact) / math.log(REL_MAX_DIST / max_exact)
                         * (REL_BUCKETS - max_exact)).astype(jnp.int32)
    large = jnp.minimum(large, REL_BUCKETS - 1)
    return jnp.where(d < max_exact, d, large)


def _diff_attn_body(tq, lam_init, q_ref, k_ref, vt_ref, bias_ref, lam_ref, g_ref, o_ref,
                    m_scr, l_scr, acc_scr, s0_scr, s1_scr, p0_scr, p1_scr, mx0_scr, mx1_scr, qcat_scr):
    qi = pl.program_id(2)
    q = q_ref[...]
    lane = lax.broadcasted_iota(jnp.int32, q.shape, 1)
    zero = jnp.zeros_like(q)
    qcat_scr[0:tq, :] = jnp.where(lane < DIFF_HEAD_DIM, q, zero)
    qcat_scr[tq:2 * tq, :] = jnp.where(lane >= DIFF_HEAD_DIM, q, zero)

    def scores(kj):
        start = pl.multiple_of(kj * tq, tq)
        bias = bias_ref[jnp.minimum(qi - kj, 2)]
        return _dot_nt(k_ref[pl.ds(start, tq), :], qcat_scr[...]) + jnp.concatenate([bias, bias], axis=1)

    _flash_causal(qi, tq, scores, vt_ref, m_scr, l_scr, acc_scr, (s0_scr, s1_scr), (p0_scr, p1_scr), (mx0_scr, mx1_scr))
    acc = acc_scr[...] * (1.0 / l_scr[...])
    o_t = acc[:, :tq] - lam_ref[...] * acc[:, tq:]
    o_ref[...] = (_rms(o_t.T, g_ref[...]) * (1.0 - lam_init)).astype(o_ref.dtype)


def _diff_mixer(h3, w_qkv, lam_q1, lam_k1, lam_q2, lam_k2, subln_g, rel_bias, lam_init):
    b, lp, d = h3.shape
    n = b * lp
    hd = DIFF_HEADS * 2 * DIFF_HEAD_DIM
    tq = _pick_tile(lp, (384, 256, 128))
    assert tq >= REL_MAX_DIST
    nk = lp // tq
    qk, vt = _row_call(functools.partial(_qkv_body, hd, DIFF_HEAD_DIM ** -0.5 * LOG2E), [h3.reshape(n, d)],
                       [w_qkv.astype(BF16)], [(2 * hd, BF16)], tq, "diff_qkv", lp=lp, t_outs=[(hd, BF16)])
    qk = qk.reshape(b, lp, 2 * hd)
    m2 = 2 * tq
    kk = jnp.arange(m2)
    dist = jnp.where(kk < tq, kk, kk - m2)[None, :] + (jnp.arange(3) * tq)[:, None]
    tab = jnp.transpose(rel_bias[_t5_bucket(dist)].astype(F32) * LOG2E, (2, 0, 1))
    nh = DIFF_HEADS
    bias = jnp.tile(tab, (1, 1, tq))[..., :tq * (m2 - 1)].reshape(nh, 3, tq, m2 - 1)[..., :tq]
    lam = (jnp.exp(jnp.sum(lam_q1.astype(F32) * lam_k1.astype(F32)))
           - jnp.exp(jnp.sum(lam_q2.astype(F32) * lam_k2.astype(F32))) + lam_init)
    w = 2 * DIFF_HEAD_DIM
    o = pl.pallas_call(
        functools.partial(_diff_attn_body, tq, lam_init),
        grid=(b, nh, nk),
        in_specs=[pl.BlockSpec((None, tq, w), lambda bi, hi, qi: (bi, qi, hi)),
                  pl.BlockSpec((None, lp, w), lambda bi, hi, qi: (bi, 0, nh + hi)),
                  pl.BlockSpec((None, nk, DIFF_V_DIM, tq), lambda bi, hi, qi: (bi, 0, hi, 0)),
                  pl.BlockSpec((None, 3, tq, tq), lambda bi, hi, qi: (hi, 0, 0, 0)),
                  pl.BlockSpec((1, 1), lambda bi, hi, qi: (0, 0)),
                  pl.BlockSpec((1, DIFF_V_DIM), lambda bi, hi, qi: (0, 0))],
        out_specs=pl.BlockSpec((None, tq, DIFF_V_DIM), lambda bi, hi, qi: (bi, qi, hi)),
        out_shape=jax.ShapeDtypeStruct((b, lp, nh * DIFF_V_DIM), BF16),
        scratch_shapes=_flash_scratch(DIFF_V_DIM, tq) + [pltpu.VMEM((2 * tq, w), BF16)],
        name="diff_attn",
        compiler_params=pltpu.CompilerParams(dimension_semantics=("parallel", "parallel", "arbitrary"),
                                             vmem_limit_bytes=VMEM_LIMIT),
    )(qk, qk, vt, bias, lam.reshape(1, 1), subln_g.reshape(1, DIFF_V_DIM))
    return o.reshape(n, nh * DIFF_V_DIM)


def _gelu_tanh(x):
    return 0.5 * x * (1.0 + jnp.tanh(math.sqrt(2.0 / math.pi) * (x + 0.044715 * (x * x * x))))


def _s5_body(u_ref, wb_ref, wc_ref, are_ref, aim_ref, d_ref, y_ref, s0_scr, s1_scr, st_scr):
    rows = u_ref.shape[0]
    tc = rows // S5_SUBLANES
    n_sl = wb_ref.shape[0]
    kw = wb_ref.shape[1]
    sw = wb_ref.shape[2] // 2
    half = n_sl * sw

    @pl.when(pl.program_id(0) == 0)
    def _():
        st_scr[...] = jnp.zeros(st_scr.shape, F32)

    for q in range(n_sl):
        s_scr = (s0_scr, s1_scr)[q % 2]
        in_sl = slice(q * kw, (q + 1) * kw)
        re_sl = slice(q * sw, (q + 1) * sw)
        im_sl = slice(half + q * sw, half + (q + 1) * sw)
        s_scr[...] = _dot(u_ref[:, in_sl].astype(BF16), wb_ref[q])
        ar = are_ref[:, re_sl]
        ai = aim_ref[:, re_sl]

        def step(t, carry):
            sr, si = carry
            r0 = pl.multiple_of(t * S5_SUBLANES, S5_SUBLANES)
            nr = ar * sr - ai * si + s_scr[pl.ds(r0, S5_SUBLANES), :sw]
            ni = ar * si + ai * sr + s_scr[pl.ds(r0, S5_SUBLANES), sw:]
            s_scr[pl.ds(r0, S5_SUBLANES), :sw] = nr
            s_scr[pl.ds(r0, S5_SUBLANES), sw:] = ni
            return nr, ni

        sr, si = lax.fori_loop(0, tc, step, (st_scr[:, re_sl], st_scr[:, im_sl]), unroll=S5_UNROLL)
        st_scr[:, re_sl] = sr
        st_scr[:, im_sl] = si
        y = _dot(s_scr[...].astype(BF16), wc_ref[q]) + d_ref[:, in_sl] * u_ref[:, in_sl]
        y_ref[:, in_sl] = _gelu_tanh(y).astype(y_ref.dtype)


def _s5_glu_ln_body(y_ref, h_ref, w_ref, bg_ref, g_ref, b_ref, o_ref):
    r = _dot(y_ref[...], w_ref[...]) + bg_ref[...]
    d = o_ref.shape[1]
    m = r[:, :d] * (1.0 / (1.0 + jnp.exp(-r[:, d:])))
    o_ref[...] = _layer_norm(ALPHA * h_ref[...] + m, g_ref[...], b_ref[...])


def _block_diag(blocks):
    s, g, a, b_ = blocks.shape
    eye = jnp.eye(g, dtype=blocks.dtype)
    return jnp.einsum('sgab,gh->sgahb', blocks, eye).reshape(s, g * a, g * b_)


def _s5_mixer(h3, lam_re, lam_im, log_step, b_re, b_im, c_re, c_im, d_skip, w_glu, b_glu, ln_g, ln_b):
    b, lp, d = h3.shape
    assert b <= S5_SUBLANES
    step = jnp.exp(log_step.astype(F32))[:, None]
    lr = jnp.minimum(lam_re.astype(F32), -1e-4)
    li = lam_im.astype(F32)
    mag = jnp.exp(lr * step)
    ar, ai = mag * jnp.cos(li * step), mag * jnp.sin(li * step)
    den = lr * lr + li * li
    cr = ((ar - 1.0) * lr + ai * li) / den
    ci = (ai * lr - (ar - 1.0) * li) / den
    br, bi = b_re.astype(F32), b_im.astype(F32)
    bbr = cr[..., None] * br - ci[..., None] * bi
    bbi = cr[..., None] * bi + ci[..., None] * br
    n_sl = S5_GROUPS // S5_SLICE_GROUPS
    sg = S5_SLICE_GROUPS
    to_sl = lambda m: m.reshape(n_sl, sg, *m.shape[1:])
    wb = jnp.concatenate([_block_diag(to_sl(jnp.transpose(bbr, (0, 2, 1)))),
                          _block_diag(to_sl(jnp.transpose(bbi, (0, 2, 1))))], axis=2).astype(BF16)
    wc = jnp.concatenate([_block_diag(to_sl(jnp.transpose(c_re.astype(F32), (0, 2, 1)))),
                          _block_diag(to_sl(-jnp.transpose(c_im.astype(F32), (0, 2, 1))))], axis=1).astype(BF16)
    n_state = S5_GROUPS * S5_STATE
    a_re = jnp.broadcast_to(ar.reshape(1, n_state), (S5_SUBLANES, n_state))
    a_im = jnp.broadcast_to(ai.reshape(1, n_state), (S5_SUBLANES, n_state))
    u = jnp.pad(jnp.transpose(h3, (1, 0, 2)), ((0, 0), (0, S5_SUBLANES - b), (0, 0))).reshape(lp * S5_SUBLANES, d)
    tc = S5_TCHUNK
    rows = tc * S5_SUBLANES
    const = lambda a: pl.BlockSpec(a.shape, lambda i: (0,) * a.ndim, pipeline_mode=pl.Buffered(1))
    d_row = d_skip.astype(F32).reshape(1, d)
    slice_state = 2 * sg * S5_STATE
    y = pl.pallas_call(
        _s5_body, grid=(lp // tc,),
        in_specs=[pl.BlockSpec((rows, d), lambda i: (i, 0)), const(wb), const(wc), const(a_re), const(a_im),
                  const(d_row)],
        out_specs=pl.BlockSpec((rows, d), lambda i: (i, 0)),
        out_shape=jax.ShapeDtypeStruct((lp * S5_SUBLANES, d), BF16),
        scratch_shapes=[pltpu.VMEM((rows, slice_state), F32), pltpu.VMEM((rows, slice_state), F32),
                        pltpu.VMEM((S5_SUBLANES, 2 * n_state), F32)],
        name="s5_scan",
        compiler_params=pltpu.CompilerParams(dimension_semantics=("arbitrary",), vmem_limit_bytes=VMEM_LIMIT),
    )(u, wb, wc, a_re, a_im, d_row)
    y = jnp.transpose(y.reshape(lp, S5_SUBLANES, d)[:, :b], (1, 0, 2)).reshape(b * lp, d)
    n = b * lp
    tm = _pick_tile(n, (512, 384, 256, 128))
    (out,) = _row_call(_s5_glu_ln_body, [y, h3.reshape(n, d)],
                       [w_glu.astype(BF16), b_glu.reshape(1, 2 * d), ln_g.reshape(1, d), ln_b.reshape(1, d)],
                       [(d, F32)], tm, "s5_glu_ln")
    return out


def _rope_block(x, cm, s1, s2):
    return (x * cm + pltpu.roll(x, LANES - MLA_ROPE // 2, axis=1) * s1 + pltpu.roll(x, MLA_ROPE // 2, axis=1) * s2)


def _mla_in_body(h_ref, qcm_ref, qs1_ref, qs2_ref, kcm_ref, ks1_ref, ks2_ref,
                 win_ref, qg_ref, kvg_ref, wuq_ref, wuk_ref, wuv_ref, q_ref, k_ref, vt_ref):
    c = _dot(h_ref[...].astype(BF16), win_ref[...])
    c_q = _rms(c[:, :MLA_Q_RANK], qg_ref[...]).astype(BF16)
    c_kv = _rms(c[:, MLA_Q_RANK:MLA_Q_RANK + MLA_KV_RANK], kvg_ref[...]).astype(BF16)
    k_r = _rope_block(c[:, MLA_Q_RANK + MLA_KV_RANK:], kcm_ref[...], ks1_ref[...], ks2_ref[...])
    q = _dot(c_q, wuq_ref[...])
    k = _dot(c_kv, wuk_ref[...])
    qcm, qs1, qs2 = qcm_ref[...], qs1_ref[...], qs2_ref[...]
    for h in range(MLA_HEADS):
        sl = slice(h * LANES, (h + 1) * LANES)
        q_ref[:, sl] = _rope_block(q[:, sl], qcm, qs1, qs2).astype(q_ref.dtype)
        k_ref[:, sl] = (k[:, sl] + k_r).astype(k_ref.dtype)
    vt_ref[...] = _dot(c_kv, wuv_ref[...]).T.astype(vt_ref.dtype)


def _mla_attn_body(tq, q_ref, k_ref, vt_ref, o_ref,
                   m_scr, l_scr, acc_scr, s0_scr, s1_scr, p0_scr, p1_scr, mx0_scr, mx1_scr):
    qi = pl.program_id(2)

    def scores(kj):
        start = pl.multiple_of(kj * tq, tq)
        return jnp.concatenate(
            [_dot_nt(k_ref[pl.ds(start, tq), hh * LANES:(hh + 1) * LANES], q_ref[:, hh * LANES:(hh + 1) * LANES])
             for hh in range(2)], axis=1)

    _flash_causal(qi, tq, scores, vt_ref, m_scr, l_scr, acc_scr, (s0_scr, s1_scr), (p0_scr, p1_scr), (mx0_scr, mx1_scr))
    acc = acc_scr[...] * (1.0 / l_scr[...])
    row = lax.broadcasted_iota(jnp.int32, (2 * MLA_V, tq), 0)
    o_t = jnp.where(row < MLA_V, acc[:, :tq], acc[:, tq:])
    o_ref[...] = o_t.T.astype(o_ref.dtype)


def _mla_mixer(h3, w_in, q_norm_g, kv_norm_g, w_uq, w_ukv):
    b, lp, d = h3.shape
    n = b * lp
    nh = MLA_HEADS
    qk = MLA_NOPE + MLA_ROPE
    half = MLA_ROPE // 2
    w_kr = jnp.pad(w_in[:, MLA_Q_RANK + MLA_KV_RANK:], ((0, 0), (MLA_NOPE, LANES - qk)))
    w_in_all = jnp.concatenate([w_in[:, :MLA_Q_RANK + MLA_KV_RANK], w_kr], axis=1).astype(BF16)
    w_uq_p = jnp.pad(w_uq.reshape(MLA_Q_RANK, nh, qk), ((0, 0), (0, 0), (0, LANES - qk)))
    w_uq_p = w_uq_p.reshape(MLA_Q_RANK, nh * LANES).astype(BF16)
    w_ukv3 = w_ukv.reshape(MLA_KV_RANK, nh, MLA_NOPE + MLA_V)
    w_uk_p = jnp.pad(w_ukv3[:, :, :MLA_NOPE], ((0, 0), (0, 0), (0, LANES - MLA_NOPE)))
    w_uk_p = w_uk_p.reshape(MLA_KV_RANK, nh * LANES).astype(BF16)
    w_uv = w_ukv3[:, :, MLA_NOPE:].reshape(MLA_KV_RANK, nh * MLA_V).astype(BF16)
    pos = jnp.arange(lp, dtype=F32)
    inv_freq = ROPE_BASE ** (-jnp.arange(0, MLA_ROPE, 2, dtype=F32) / MLA_ROPE)
    ang = pos[:, None] * inv_freq[None, :]
    cos, sin = jnp.cos(ang), jnp.sin(ang)
    z = lambda w: jnp.zeros((lp, w), F32)
    cm = jnp.concatenate([jnp.ones((lp, MLA_NOPE), F32), cos, cos, z(LANES - qk)], axis=1)
    s1 = jnp.concatenate([z(MLA_NOPE), -sin, z(LANES - MLA_NOPE - half)], axis=1)
    s2 = jnp.concatenate([z(MLA_NOPE + half), sin, z(LANES - qk)], axis=1)
    scale = qk ** -0.5 * LOG2E
    tq = _pick_tile(lp, (384, 256, 128))
    nk = lp // tq
    q, k, vt = _row_call(
        _mla_in_body, [h3.reshape(n, d)],
        [w_in_all, q_norm_g.reshape(1, MLA_Q_RANK), kv_norm_g.reshape(1, MLA_KV_RANK), w_uq_p, w_uk_p, w_uv],
        [(nh * LANES, BF16), (nh * LANES, BF16)], tq, "mla_in",
        pos_ins=[cm * scale, s1 * scale, s2 * scale, cm, s1, s2], lp=lp, t_outs=[(nh * MLA_V, BF16)])
    q, k = q.reshape(b, lp, nh * LANES), k.reshape(b, lp, nh * LANES)
    o = pl.pallas_call(
        functools.partial(_mla_attn_body, tq),
        grid=(b, nh // 2, nk),
        in_specs=[pl.BlockSpec((None, tq, 2 * LANES), lambda bi, hi, qi: (bi, qi, hi)),
                  pl.BlockSpec((None, lp, 2 * LANES), lambda bi, hi, qi: (bi, 0, hi)),
                  pl.BlockSpec((None, nk, 2 * MLA_V, tq), lambda bi, hi, qi: (bi, 0, hi, 0))],
        out_specs=pl.BlockSpec((None, tq, 2 * MLA_V), lambda bi, hi, qi: (bi, qi, hi)),
        out_shape=jax.ShapeDtypeStruct((b, lp, nh * MLA_V), BF16),
        scratch_shapes=_flash_scratch(2 * MLA_V, tq),
        name="mla_attn",
        compiler_params=pltpu.CompilerParams(dimension_semantics=("parallel", "parallel", "arbitrary"),
                                             vmem_limit_bytes=VMEM_LIMIT),
    )(q, k, vt)
    return o.reshape(n, nh * MLA_V)


def kernel(x, meta, rel_bias, ln_g, ln_b, ffn_w1, ffn_w3, ffn_w2, ssd_w_in, ssd_conv_w, ssd_conv_b, ssd_dt_bias, ssd_a_log, ssd_d, ssd_norm_g, ssd_w_out, diff_w_qkv, diff_lam_q1, diff_lam_k1, diff_lam_q2, diff_lam_k2, diff_subln_g, diff_w_out, s5_lam_re, s5_lam_im, s5_log_step, s5_b_re, s5_b_im, s5_c_re, s5_c_im, s5_d, s5_w_glu, s5_b_glu, mla_w_in, mla_q_norm_g, mla_kv_norm_g, mla_w_uq, mla_w_ukv, mla_w_out):
    b, seq, d = x.shape
    l = seq + N_META
    lp = -(-l // SEQ_ALIGN) * SEQ_ALIGN
    n = b * lp
    h = jnp.concatenate([jnp.broadcast_to(meta[None].astype(x.dtype), (b, N_META, d)), x,
                         jnp.zeros((b, lp - l, d), x.dtype)], axis=1).reshape(n, d)
    w1b, w3b, w2b = ffn_w1.astype(BF16), ffn_w3.astype(BF16), ffn_w2.astype(BF16)
    for i in range(DEPTH):
        kind, j = i % 4, i // 4
        h = _ffn(h, w1b, w3b, w2b, ln_g[i, 0], ln_b[i, 0], (i, 0))
        h3 = h.reshape(b, lp, d)
        if kind == 0:
            y = _ssd_mixer(h3, ssd_w_in[j], ssd_conv_w[j], ssd_conv_b[j], ssd_dt_bias[j], ssd_a_log[j],
                           ssd_d[j], ssd_norm_g[j])
            h = _proj_ln(y, h, ssd_w_out[j], ln_g[i, 1], ln_b[i, 1])
        elif kind == 1:
            lam_init = 0.8 - 0.6 * math.exp(-0.3 * i)
            y = _diff_mixer(h3, diff_w_qkv[j], diff_lam_q1[j], diff_lam_k1[j], diff_lam_q2[j], diff_lam_k2[j],
                            diff_subln_g[j], rel_bias, lam_init)
            h = _proj_ln(y, h, diff_w_out[j], ln_g[i, 1], ln_b[i, 1])
        elif kind == 2:
            h = _s5_mixer(h3, s5_lam_re[j], s5_lam_im[j], s5_log_step[j], s5_b_re[j], s5_b_im[j], s5_c_re[j],
                          s5_c_im[j], s5_d[j], s5_w_glu[j], s5_b_glu[j], ln_g[i, 1], ln_b[i, 1])
        else:
            y = _mla_mixer(h3, mla_w_in[j], mla_q_norm_g[j], mla_kv_norm_g[j], mla_w_uq[j], mla_w_ukv[j])
            h = _proj_ln(y, h, mla_w_out[j], ln_g[i, 1], ln_b[i, 1])
        h = _ffn(h, w1b, w3b, w2b, ln_g[i, 2], ln_b[i, 2], (i, 1))
    return h.reshape(b, lp, d)[:, N_META:l]
```

```python
import functools
import math

import jax
import jax.numpy as jnp
from jax import lax
from jax.experimental import pallas as pl
from jax.experimental.pallas import tpu as pltpu

F32 = jnp.float32
BF16 = jnp.bfloat16

N_META = 16
DEPTH = 4
ALPHA = (2.0 * DEPTH) ** 0.25
LN_EPS = 1e-5
RMS_EPS = 1e-6
NEG_INF = -1e30
LOG2E = math.log2(math.e)

SSD_HEAD_DIM = 64
SSD_HEADS = 32
SSD_GROUPS = 8
SSD_STATE = 128
SSD_CONV = 4
SSD_CHUNK = 128

DIFF_HEADS = 8
DIFF_HEAD_DIM = 64
DIFF_V_DIM = 128
REL_BUCKETS = 32
REL_MAX_DIST = 128

S5_GROUP = 16
S5_GROUPS = 64
S5_STATE = 64
S5_CHUNK = 16
S5_GROUP_BLOCK = 8
S5_SUBLANES = 8
S5_UNROLL = 4

MLA_HEADS = 16
MLA_Q_RANK = 384
MLA_KV_RANK = 256
MLA_NOPE = 64
MLA_ROPE = 32
MLA_V = 64
ROPE_BASE = 10000.0

FFN_ROW_TILES = (512, 384, 256, 128)
FFN_CHUNKS = 11

LANES = 128
SEQ_ALIGN = 128
VMEM_LIMIT = 56 * 1024 * 1024


def _pick_tile(n, candidates):
    for c in candidates:
        if n % c == 0:
            return c
    raise ValueError(f"no tile for {n}")


def _layer_norm(z, g, b):
    mu = jnp.mean(z, axis=-1, keepdims=True)
    d = z - mu
    var = jnp.mean(d * d, axis=-1, keepdims=True)
    return d * lax.rsqrt(var + LN_EPS) * g + b


def _rms(z, g):
    return z * lax.rsqrt(jnp.mean(z * z, axis=-1, keepdims=True) + RMS_EPS) * g


def _silu(x):
    return x * (1.0 / (1.0 + jnp.exp(-x)))


def _dot(a, b):
    return jnp.dot(a, b, preferred_element_type=F32)


def _dot_nt(a, b):
    return lax.dot_general(a, b, (((1,), (1,)), ((), ())), preferred_element_type=F32)


def _row_call(body, row_ins, const_ins, outs, tm, name, pos_ins=(), lp=None, t_outs=()):
    n = row_ins[0].shape[0]
    assert n % tm == 0
    in_specs = [pl.BlockSpec((tm, a.shape[1]), lambda i: (i, 0)) for a in row_ins]
    if pos_ins or t_outs:
        assert lp % tm == 0
        nb = lp // tm
        in_specs += [pl.BlockSpec((tm, a.shape[1]), lambda i: (i % nb, 0)) for a in pos_ins]
    const_arrays = []
    for c in const_ins:
        arr, lead = c if isinstance(c, tuple) else (c, ())
        const_arrays.append(arr)
        in_specs.append(pl.BlockSpec((None,) * len(lead) + arr.shape[len(lead):],
                                     functools.partial(lambda lead_, i: lead_ + (0, 0), tuple(lead)),
                                     pipeline_mode=pl.Buffered(1)))
    out_specs = [pl.BlockSpec((tm, w), lambda i: (i, 0)) for (w, _) in outs]
    out_shape = [jax.ShapeDtypeStruct((n, w), dt) for (w, dt) in outs]
    out_specs += [pl.BlockSpec((None, None, w, tm), lambda i: (i // nb, i % nb, 0, 0)) for (w, _) in t_outs]
    out_shape += [jax.ShapeDtypeStruct((n // lp, nb, w, tm), dt) for (w, dt) in t_outs]
    res = pl.pallas_call(
        body, grid=(n // tm,), in_specs=in_specs, out_specs=out_specs, out_shape=out_shape, name=name,
        compiler_params=pltpu.CompilerParams(dimension_semantics=("parallel",), vmem_limit_bytes=VMEM_LIMIT),
    )(*row_ins, *pos_ins, *const_arrays)
    return res


def _ffn_rows(x, n_chunks, w1_ref, w3_ref, w2_ref, g_ref, b_ref):
    xb = x.astype(BF16)
    fc = w1_ref.shape[1] // n_chunks
    y = None
    for c in range(n_chunks):
        sl = slice(c * fc, (c + 1) * fc)
        gate = _dot(xb, w1_ref[:, sl])
        up = _dot(xb, w3_ref[:, sl])
        a = (_silu(gate) * up).astype(BF16)
        part = _dot(a, w2_ref[sl, :])
        y = part if y is None else y + part
    return _layer_norm(ALPHA * x + 0.5 * y, g_ref[...], b_ref[...])


def _ffn_body(n_chunks, h_ref, w1_ref, w3_ref, w2_ref, g_ref, b_ref, o_ref):
    o_ref[...] = _ffn_rows(h_ref[...], n_chunks, w1_ref, w3_ref, w2_ref, g_ref, b_ref)


def _mix_ffn_body(n_chunks, glu, y_ref, h_ref, wo_ref, bo_ref, gm_ref, bm_ref,
                  w1_ref, w3_ref, w2_ref, g_ref, b_ref, o_ref):
    m = _dot(y_ref[...], wo_ref[...])
    if glu:
        m = m + bo_ref[...]
        d = o_ref.shape[1]
        m = m[:, :d] * (1.0 / (1.0 + jnp.exp(-m[:, d:])))
    x = _layer_norm(ALPHA * h_ref[...] + m, gm_ref[...], bm_ref[...])
    o_ref[...] = _ffn_rows(x, n_chunks, w1_ref, w3_ref, w2_ref, g_ref, b_ref)


def _ffn_chunks(f, n_chunks):
    return n_chunks or (FFN_CHUNKS if f % (FFN_CHUNKS * LANES) == 0 else 1)


def _ffn(h2, w1, w3, w2, g, b, idx, tm=None, n_chunks=None, name="ffn"):
    n, d = h2.shape
    tm = tm or _pick_tile(n, FFN_ROW_TILES)
    (out,) = _row_call(functools.partial(_ffn_body, _ffn_chunks(w1.shape[-1], n_chunks)), [h2],
                       [(w1, idx), (w3, idx), (w2, idx), g.reshape(1, d), b.reshape(1, d)],
                       [(d, F32)], tm, name)
    return out


def _mix_ffn(y2, h2, w_out, b_out, g_mix, b_mix, w1, w3, w2, g, b, idx, tm=None, n_chunks=None, name="mix_ffn"):
    n, d = h2.shape
    tm = tm or _pick_tile(n, FFN_ROW_TILES)
    glu = b_out is not None
    wo = w_out.astype(BF16)
    bo = (b_out if glu else jnp.zeros((wo.shape[1],), F32)).reshape(1, wo.shape[1])
    (out,) = _row_call(functools.partial(_mix_ffn_body, _ffn_chunks(w1.shape[-1], n_chunks), glu), [y2, h2],
                       [wo, bo, g_mix.reshape(1, d), b_mix.reshape(1, d),
                        (w1, idx), (w3, idx), (w2, idx), g.reshape(1, d), b.reshape(1, d)],
                       [(d, F32)], tm, name)
    return out


def _ssd_in_body(h_ref, w_ref, xbc_ref, z_ref, dt_ref):
    r = _dot(h_ref[...].astype(BF16), w_ref[...])
    nx, nz = xbc_ref.shape[1], z_ref.shape[1]
    xbc_ref[...] = r[:, :nx]
    z_ref[...] = r[:, nx:nx + nz]
    dt_ref[...] = r[:, nx + nz:]


def _softplus(x):
    return jnp.maximum(x, 0.0) + jnp.log1p(jnp.exp(-jnp.abs(x)))


def _ssd_body(xbc_ref, dt_ref, z_ref, convw_ref, convb_ref, dtb_ref, alog_ref, dskip_ref, normg_ref,
              y_ref, buf_scr, state_scr, y_scr):
    t = SSD_CHUNK
    d_inner = SSD_HEADS * SSD_HEAD_DIM
    gn = SSD_GROUPS * SSD_STATE
    hp = SSD_HEAD_DIM
    r = SSD_HEADS // SSD_GROUPS

    @pl.when(pl.program_id(1) == 0)
    def _():
        buf_scr[0:8, :] = jnp.zeros((8, buf_scr.shape[1]), F32)
        state_scr[...] = jnp.zeros(state_scr.shape, F32)

    x = xbc_ref[...]
    buf_scr[8:8 + t, :] = x
    acc = convb_ref[...] + convw_ref[SSD_CONV - 1:SSD_CONV, :] * x
    for j in range(1, SSD_CONV):
        acc = acc + convw_ref[SSD_CONV - 1 - j:SSD_CONV - j, :] * buf_scr[8 - j:8 - j + t, :]
    buf_scr[0:8, :] = x[t - 8:t, :]
    xbc = _silu(acc)

    dt = _softplus(dt_ref[...] + dtb_ref[...])
    a = dt * (-jnp.exp(alog_ref[...]))
    rows = lax.broadcasted_iota(jnp.int32, (t, t), 0)
    cols = lax.broadcasted_iota(jnp.int32, (t, t), 1)
    tri = rows >= cols
    a_cum = jnp.dot(tri.astype(F32), a, precision=lax.Precision.HIGHEST, preferred_element_type=F32)
    a_last = a_cum[t - 1:t, :]
    a_cum_t = a_cum.T
    dt_t = dt.T
    w_end_t = (dt * jnp.exp(a_last - a_cum)).T
    chunk_dec = jnp.exp(a_last)

    for g in range(SSD_GROUPS):
        b_f = xbc[:, d_inner + g * SSD_STATE:d_inner + (g + 1) * SSD_STATE]
        c_g = xbc[:, d_inner + gn + g * SSD_STATE:d_inner + gn + (g + 1) * SSD_STATE].astype(BF16)
        b_g = b_f.astype(BF16)
        b_ft = b_f.T
        cb = _dot_nt(c_g, b_g)
        y_off_g = _dot(c_g, state_scr[:, g * r * hp:(g + 1) * r * hp].astype(BF16))
        for k in range(r):
            h = g * r + k
            col = jnp.broadcast_to(a_cum[:, h:h + 1], (t, t))
            decay = jnp.exp(jnp.where(tri, col - a_cum_t[h:h + 1, :], NEG_INF))
            xh = xbc[:, h * hp:(h + 1) * hp]
            xb = xh.astype(BF16)
            y_diag = _dot((cb * decay * dt_t[h:h + 1, :]).astype(BF16), xb)
            y_off = y_off_g[:, k * hp:(k + 1) * hp] * jnp.exp(col[:, :hp])
            y_scr[:, h * hp:(h + 1) * hp] = y_diag + y_off + dskip_ref[:, h * hp:(h + 1) * hp] * xh
            st = state_scr[:, h * hp:(h + 1) * hp]
            b_w = (b_ft * w_end_t[h:h + 1, :]).astype(BF16)
            state_scr[:, h * hp:(h + 1) * hp] = st * chunk_dec[:, h:h + 1] + _dot(b_w, xb)

    y = y_scr[...] * _silu(z_ref[...])
    gs = d_inner // SSD_GROUPS
    for g in range(SSD_GROUPS):
        sl = slice(g * gs, (g + 1) * gs)
        y_ref[:, sl] = _rms(y[:, sl], normg_ref[:, sl]).astype(y_ref.dtype)


def _ssd_mixer(h3, w_in, conv_w, conv_b, dt_bias, a_log, d_skip, norm_g):
    b, lp, d = h3.shape
    n = b * lp
    d_inner = SSD_HEADS * SSD_HEAD_DIM
    conv_ch = d_inner + 2 * SSD_GROUPS * SSD_STATE
    w_z, w_xbc, w_dt = w_in[:, :d_inner], w_in[:, d_inner:d_inner + conv_ch], w_in[:, d_inner + conv_ch:]
    w_dt = jnp.pad(w_dt, ((0, 0), (0, LANES - SSD_HEADS)))
    w_all = jnp.concatenate([w_xbc, w_z, w_dt], axis=1).astype(BF16)
    tm = _pick_tile(n, (512, 384, 256, 128))
    xbc, z, dt = _row_call(_ssd_in_body, [h3.reshape(n, d)], [w_all],
                           [(conv_ch, F32), (d_inner, F32), (LANES, F32)], tm, "ssd_in")
    t = SSD_CHUNK
    pad1 = lambda v: jnp.pad(v, (0, LANES - SSD_HEADS)).reshape(1, LANES)
    consts = [conv_w, conv_b.reshape(1, conv_ch), pad1(dt_bias), pad1(a_log),
              jnp.repeat(d_skip, SSD_HEAD_DIM).reshape(1, d_inner), norm_g.reshape(1, d_inner)]
    row_spec = lambda w: pl.BlockSpec((None, t, w), lambda i, c: (i, c, 0))
    y = pl.pallas_call(
        _ssd_body, grid=(b, lp // t),
        in_specs=[row_spec(conv_ch), row_spec(LANES), row_spec(d_inner)]
        + [pl.BlockSpec(c.shape, lambda i, c_: (0, 0)) for c in consts],
        out_specs=row_spec(d_inner),
        out_shape=jax.ShapeDtypeStruct((b, lp, d_inner), BF16),
        scratch_shapes=[pltpu.VMEM((8 + t, conv_ch), F32), pltpu.VMEM((SSD_STATE, d_inner), F32),
                        pltpu.VMEM((t, d_inner), F32)],
        name="ssd_scan",
        compiler_params=pltpu.CompilerParams(dimension_semantics=("parallel", "arbitrary"),
                                             vmem_limit_bytes=VMEM_LIMIT),
    )(xbc.reshape(b, lp, conv_ch), dt.reshape(b, lp, LANES), z.reshape(b, lp, d_inner), *consts)
    return y.reshape(n, d_inner)


def _qkv_body(n_q, scale, h_ref, w_ref, qk_ref, vt_ref):
    r = _dot(h_ref[...].astype(BF16), w_ref[...])
    n_qk = qk_ref.shape[1]
    qk_ref[:, :n_q] = (r[:, :n_q] * scale).astype(qk_ref.dtype)
    qk_ref[:, n_q:] = r[:, n_q:n_qk].astype(qk_ref.dtype)
    vt_ref[...] = r[:, n_qk:].T.astype(vt_ref.dtype)


def _causal_mask_pair(s, tq):
    rows = lax.broadcasted_iota(jnp.int32, s.shape, 0)
    cols = lax.broadcasted_iota(jnp.int32, s.shape, 1)
    cols = jnp.where(cols >= tq, cols - tq, cols)
    return jnp.where(rows <= cols, s, NEG_INF)


def _flash_causal(qi, tq, scores, vt_ref, m_scr, l_scr, acc_scr, s_scrs, p_scrs, mx_scrs):
    m_scr[...] = jnp.full(m_scr.shape, NEG_INF, F32)
    l_scr[...] = jnp.zeros(l_scr.shape, F32)
    acc_scr[...] = jnp.zeros(acc_scr.shape, F32)
    p_scrs[1][...] = jnp.zeros(p_scrs[1].shape, p_scrs[1].dtype)

    def softmax(s, s_max, pv):
        m_prev = m_scr[...]
        m_new = jnp.maximum(m_prev, s_max)
        alpha = jnp.exp2(m_prev - m_new)
        p = jnp.exp2(s - m_new)
        l_scr[...] = alpha * l_scr[...] + jnp.sum(p, axis=0, keepdims=True)
        m_scr[...] = m_new
        acc_scr[...] = alpha * (acc_scr[...] + pv)
        return p.astype(BF16)

    def produce(kj, slot):
        s = scores(kj)
        s_scrs[slot][...] = s
        mx_scrs[slot][...] = jnp.max(s, axis=0, keepdims=True)

    def step(kj, cur, nxt):
        pv = _dot(vt_ref[jnp.maximum(kj - 1, 0)], p_scrs[nxt][...])
        p_scrs[cur][...] = softmax(s_scrs[cur][...], mx_scrs[cur][...], pv)
        produce(kj + 1, nxt)

    produce(0, 0)

    def body(t, carry):
        step(2 * t, 0, 1)
        step(2 * t + 1, 1, 0)
        return carry

    lax.fori_loop(0, lax.shift_right_logical(qi, 1), body, 0)
    pl.when((qi & 1) == 1)(functools.partial(step, qi - 1, 0, 1))

    def last(cur, nxt):
        pv = _dot(vt_ref[jnp.maximum(qi - 1, 0)], p_scrs[nxt][...])
        s = _causal_mask_pair(s_scrs[cur][...], tq)
        p = softmax(s, jnp.max(s, axis=0, keepdims=True), pv)
        acc_scr[...] = acc_scr[...] + _dot(vt_ref[qi], p)

    for par in range(2):
        pl.when((qi & 1) == par)(functools.partial(last, par, 1 - par))


def _flash_scratch(e, tq):
    w = 2 * tq
    return [pltpu.VMEM((1, w), F32), pltpu.VMEM((1, w), F32), pltpu.VMEM((e, w), F32),
            pltpu.VMEM((tq, w), F32), pltpu.VMEM((tq, w), F32), pltpu.VMEM((tq, w), BF16), pltpu.VMEM((tq, w), BF16),
            pltpu.VMEM((1, w), F32), pltpu.VMEM((1, w), F32)]


def _t5_bucket(dist):
    max_exact = REL_BUCKETS // 2
    d = jnp.maximum(dist, 0)
    df = jnp.maximum(d, max_exact).astype(F32)
    large = max_exact + (jnp.log(df / max_exact) / math.log(REL_MAX_DIST / max_exact)
                         * (REL_BUCKETS - max_exact)).astype(jnp.int32)
    large = jnp.minimum(large, REL_BUCKETS - 1)
    return jnp.where(d < max_exact, d, large)


def _diff_attn_body(tq, lam_init, q_ref, k_ref, vt_ref, bias_ref, lam_ref, g_ref, o_ref,
                    m_scr, l_scr, acc_scr, s0_scr, s1_scr, p0_scr, p1_scr, mx0_scr, mx1_scr, qcat_scr):
    qi = pl.program_id(2)
    q = q_ref[...]
    lane = lax.broadcasted_iota(jnp.int32, q.shape, 1)
    zero = jnp.zeros_like(q)
    qcat_scr[0:tq, :] = jnp.where(lane < DIFF_HEAD_DIM, q, zero)
    qcat_scr[tq:2 * tq, :] = jnp.where(lane >= DIFF_HEAD_DIM, q, zero)

    def scores(kj):
        start = pl.multiple_of(kj * tq, tq)
        bias = bias_ref[jnp.minimum(qi - kj, 2)]
        return _dot_nt(k_ref[pl.ds(start, tq), :], qcat_scr[...]) + jnp.concatenate([bias, bias], axis=1)

    _flash_causal(qi, tq, scores, vt_ref, m_scr, l_scr, acc_scr, (s0_scr, s1_scr), (p0_scr, p1_scr), (mx0_scr, mx1_scr))
    acc = acc_scr[...] * (1.0 / l_scr[...])
    o_t = acc[:, :tq] - lam_ref[...] * acc[:, tq:]
    o_ref[...] = (_rms(o_t.T, g_ref[...]) * (1.0 - lam_init)).astype(o_ref.dtype)


def _diff_mixer(h3, w_qkv, lam_q1, lam_k1, lam_q2, lam_k2, subln_g, rel_bias, lam_init):
    b, lp, d = h3.shape
    n = b * lp
    hd = DIFF_HEADS * 2 * DIFF_HEAD_DIM
    tq = _pick_tile(lp, (384, 256, 128))
    assert tq >= REL_MAX_DIST
    nk = lp // tq
    qk, vt = _row_call(functools.partial(_qkv_body, hd, DIFF_HEAD_DIM ** -0.5 * LOG2E), [h3.reshape(n, d)],
                       [w_qkv.astype(BF16)], [(2 * hd, BF16)], tq, "diff_qkv", lp=lp, t_outs=[(hd, BF16)])
    qk = qk.reshape(b, lp, 2 * hd)
    m2 = 2 * tq
    kk = jnp.arange(m2)
    dist = jnp.where(kk < tq, kk, kk - m2)[None, :] + (jnp.arange(3) * tq)[:, None]
    tab = jnp.transpose(rel_bias[_t5_bucket(dist)].astype(F32) * LOG2E, (2, 0, 1))
    nh = DIFF_HEADS
    bias = jnp.tile(tab, (1, 1, tq))[..., :tq * (m2 - 1)].reshape(nh, 3, tq, m2 - 1)[..., :tq]
    lam = (jnp.exp(jnp.sum(lam_q1.astype(F32) * lam_k1.astype(F32)))
           - jnp.exp(jnp.sum(lam_q2.astype(F32) * lam_k2.astype(F32))) + lam_init)
    w = 2 * DIFF_HEAD_DIM
    o = pl.pallas_call(
        functools.partial(_diff_attn_body, tq, lam_init),
        grid=(b, nh, nk),
        in_specs=[pl.BlockSpec((None, tq, w), lambda bi, hi, qi: (bi, qi, hi)),
                  pl.BlockSpec((None, lp, w), lambda bi, hi, qi: (bi, 0, nh + hi)),
                  pl.BlockSpec((None, nk, DIFF_V_DIM, tq), lambda bi, hi, qi: (bi, 0, hi, 0)),
                  pl.BlockSpec((None, 3, tq, tq), lambda bi, hi, qi: (hi, 0, 0, 0)),
                  pl.BlockSpec((1, 1), lambda bi, hi, qi: (0, 0)),
                  pl.BlockSpec((1, DIFF_V_DIM), lambda bi, hi, qi: (0, 0))],
        out_specs=pl.BlockSpec((None, tq, DIFF_V_DIM), lambda bi, hi, qi: (bi, qi, hi)),
        out_shape=jax.ShapeDtypeStruct((b, lp, nh * DIFF_V_DIM), BF16),
        scratch_shapes=_flash_scratch(DIFF_V_DIM, tq) + [pltpu.VMEM((2 * tq, w), BF16)],
        name="diff_attn",
        compiler_params=pltpu.CompilerParams(dimension_semantics=("parallel", "parallel", "arbitrary"),
                                             vmem_limit_bytes=VMEM_LIMIT),
    )(qk, qk, vt, bias, lam.reshape(1, 1), subln_g.reshape(1, DIFF_V_DIM))
    return o.reshape(n, nh * DIFF_V_DIM)


def _gelu_tanh(x):
    return 0.5 * x * (1.0 + jnp.tanh(math.sqrt(2.0 / math.pi) * (x + 0.044715 * (x * x * x))))


def _s5_body(bp, x_ref, m_ref, bc_ref, ccre_ref, ccim_ref, are_ref, aim_ref, y_ref, sre_scr, sim_scr):
    rows = x_ref.shape[0]
    gb = m_ref.shape[0]
    kw = m_ref.shape[1]
    p = ccre_ref.shape[1]
    for g in range(gb):
        e = _dot(x_ref[:, g * kw:(g + 1) * kw], bc_ref[g])
        sre_scr[:, g * p:(g + 1) * p] = e[:, :p]
        sim_scr[:, g * p:(g + 1) * p] = e[:, p:]

    ar, ai = are_ref[...], aim_ref[...]
    sub = lax.broadcasted_iota(jnp.int32, ar.shape, 0)
    per_tile = S5_SUBLANES // bp

    def tile_step(r, carry):
        pr, pi = carry
        r0 = pl.multiple_of(r * S5_SUBLANES, S5_SUBLANES)
        er, ei = sre_scr[pl.ds(r0, S5_SUBLANES), :], sim_scr[pl.ds(r0, S5_SUBLANES), :]
        in_r, in_i = pr, pi
        for c in range(per_tile):
            nr = ar * pr - ai * pi + er
            ni = ar * pi + ai * pr + ei
            if per_tile > 1:
                nr = pltpu.roll(nr, bp, axis=0)
                ni = pltpu.roll(ni, bp, axis=0)
            if c + 1 < per_tile:
                sel = (sub >= (c + 1) * bp) & (sub < (c + 2) * bp)
                in_r, in_i = jnp.where(sel, nr, in_r), jnp.where(sel, ni, in_i)
                pr, pi = nr, ni
            else:
                pr, pi = nr, ni
        if per_tile > 1:
            pr = jnp.where(sub < bp, pr, pltpu.roll(pr, bp, axis=0))
            pi = jnp.where(sub < bp, pi, pltpu.roll(pi, bp, axis=0))
        sre_scr[pl.ds(r0, S5_SUBLANES), :] = in_r
        sim_scr[pl.ds(r0, S5_SUBLANES), :] = in_i
        return pr, pi

    zero = jnp.zeros(ar.shape, F32)
    lax.fori_loop(0, rows // S5_SUBLANES, tile_step, (zero, zero), unroll=S5_UNROLL)

    for g in range(gb):
        xg = x_ref[:, g * kw:(g + 1) * kw]
        y = (_dot(xg, m_ref[g]) + _dot(sre_scr[:, g * p:(g + 1) * p].astype(BF16), ccre_ref[g])
             + _dot(sim_scr[:, g * p:(g + 1) * p].astype(BF16), ccim_ref[g]))
        y_ref[:, g * kw:(g + 1) * kw] = _gelu_tanh(y).astype(y_ref.dtype)


def _s5_chunk_operators(lam_re, lam_im, log_step, b_re, b_im, c_re, c_im, d_skip):
    hi = lax.Precision.HIGHEST
    t_len = S5_CHUNK
    step = jnp.exp(log_step.astype(F32))[:, None]
    lr = jnp.minimum(lam_re.astype(F32), -1e-4)
    li = lam_im.astype(F32)
    mag = jnp.exp(lr * step)
    ar, ai = mag * jnp.cos(li * step), mag * jnp.sin(li * step)
    den = lr * lr + li * li
    cr = ((ar - 1.0) * lr + ai * li) / den
    ci = (ai * lr - (ar - 1.0) * li) / den
    br, bi = b_re.astype(F32), b_im.astype(F32)
    bbr = cr[..., None] * br - ci[..., None] * bi
    bbi = cr[..., None] * bi + ci[..., None] * br
    k = jnp.arange(t_len + 1, dtype=F32)[:, None, None]
    pmag = jnp.exp(k * (lr * step))
    pr, pi = pmag * jnp.cos(k * (li * step)), pmag * jnp.sin(k * (li * step))
    abr = pr[:t_len, ..., None] * bbr - pi[:t_len, ..., None] * bbi
    abi = pr[:t_len, ..., None] * bbi + pi[:t_len, ..., None] * bbr
    cre, cim = c_re.astype(F32), c_im.astype(F32)
    kern = (jnp.einsum('gop,kgpc->kgoc', cre, abr, precision=hi)
            - jnp.einsum('gop,kgpc->kgoc', cim, abi, precision=hi))
    kern = kern.at[0].add(jax.vmap(jnp.diag)(d_skip.astype(F32)))
    sig = jnp.arange(t_len)
    lag = sig[None, :] - sig[:, None]
    m = jnp.where((lag >= 0)[..., None, None, None], kern[jnp.maximum(lag, 0)], 0.0)
    g, o, c = kern.shape[1:]
    m = jnp.transpose(m, (2, 0, 4, 1, 3)).reshape(g, t_len * c, t_len * o)
    rev = t_len - 1 - sig
    p = bbr.shape[1]
    bc = jnp.concatenate([jnp.transpose(abr[rev], (1, 0, 3, 2)).reshape(g, t_len * c, p),
                          jnp.transpose(abi[rev], (1, 0, 3, 2)).reshape(g, t_len * c, p)], axis=2)
    car = cre[None] * pr[1:, :, None, :] - cim[None] * pi[1:, :, None, :]
    cai = cre[None] * pi[1:, :, None, :] + cim[None] * pr[1:, :, None, :]
    cc_re = jnp.transpose(car, (1, 3, 0, 2)).reshape(g, p, t_len * o)
    cc_im = -jnp.transpose(cai, (1, 3, 0, 2)).reshape(g, p, t_len * o)
    return (m.astype(BF16), bc.astype(BF16), cc_re.astype(BF16), cc_im.astype(BF16),
            pr[t_len].reshape(g * p), pi[t_len].reshape(g * p))


def _s5_mixer(h3, lam_re, lam_im, log_step, b_re, b_im, c_re, c_im, d_skip):
    b, lp, d = h3.shape
    t_len, g, c = S5_CHUNK, S5_GROUPS, S5_GROUP
    assert lp % t_len == 0 and b <= S5_SUBLANES
    bp = 4 if b <= 4 else S5_SUBLANES
    nj = lp // t_len
    m, bc, cc_re, cc_im, at_re, at_im = _s5_chunk_operators(lam_re, lam_im, log_step, b_re, b_im, c_re, c_im, d_skip)
    n_state = g * S5_STATE
    at_re = jnp.broadcast_to(at_re.reshape(1, n_state), (S5_SUBLANES, n_state))
    at_im = jnp.broadcast_to(at_im.reshape(1, n_state), (S5_SUBLANES, n_state))
    x = jnp.pad(h3.astype(BF16), ((0, bp - b), (0, 0), (0, 0))).reshape(bp, nj, t_len, g, c)
    x = jnp.transpose(x, (1, 0, 3, 2, 4)).reshape(nj * bp, g * t_len * c)
    rows = nj * bp
    gb = S5_GROUP_BLOCK
    kw = t_len * c
    y = pl.pallas_call(
        functools.partial(_s5_body, bp), grid=(g // gb,),
        in_specs=[pl.BlockSpec((rows, gb * kw), lambda i: (0, i)),
                  pl.BlockSpec((gb, kw, kw), lambda i: (i, 0, 0)),
                  pl.BlockSpec((gb, kw, 2 * S5_STATE), lambda i: (i, 0, 0)),
                  pl.BlockSpec((gb, S5_STATE, kw), lambda i: (i, 0, 0)),
                  pl.BlockSpec((gb, S5_STATE, kw), lambda i: (i, 0, 0)),
                  pl.BlockSpec((S5_SUBLANES, gb * S5_STATE), lambda i: (0, i)),
                  pl.BlockSpec((S5_SUBLANES, gb * S5_STATE), lambda i: (0, i))],
        out_specs=pl.BlockSpec((rows, gb * kw), lambda i: (0, i)),
        out_shape=jax.ShapeDtypeStruct((rows, g * kw), BF16),
        scratch_shapes=[pltpu.VMEM((rows, gb * S5_STATE), F32), pltpu.VMEM((rows, gb * S5_STATE), F32)],
        name="s5_chunks",
        compiler_params=pltpu.CompilerParams(dimension_semantics=("parallel",), vmem_limit_bytes=VMEM_LIMIT),
    )(x, m, bc, cc_re, cc_im, at_re, at_im)
    return jnp.transpose(y.reshape(nj, bp, g, t_len, c)[:, :b], (1, 0, 3, 2, 4)).reshape(b * lp, d)


def _rope_block(x, cm, s1, s2):
    return (x * cm + pltpu.roll(x, LANES - MLA_ROPE // 2, axis=1) * s1 + pltpu.roll(x, MLA_ROPE // 2, axis=1) * s2)


def _mla_in_body(h_ref, qcm_ref, qs1_ref, qs2_ref, kcm_ref, ks1_ref, ks2_ref,
                 win_ref, qg_ref, kvg_ref, wuq_ref, wuk_ref, wuv_ref, q_ref, k_ref, vt_ref):
    c = _dot(h_ref[...].astype(BF16), win_ref[...])
    c_q = _rms(c[:, :MLA_Q_RANK], qg_ref[...]).astype(BF16)
    c_kv = _rms(c[:, MLA_Q_RANK:MLA_Q_RANK + MLA_KV_RANK], kvg_ref[...]).astype(BF16)
    k_r = _rope_block(c[:, MLA_Q_RANK + MLA_KV_RANK:], kcm_ref[...], ks1_ref[...], ks2_ref[...])
    q = _dot(c_q, wuq_ref[...])
    k = _dot(c_kv, wuk_ref[...])
    qcm, qs1, qs2 = qcm_ref[...], qs1_ref[...], qs2_ref[...]
    for h in range(MLA_HEADS):
        sl = slice(h * LANES, (h + 1) * LANES)
        q_ref[:, sl] = _rope_block(q[:, sl], qcm, qs1, qs2).astype(q_ref.dtype)
        k_ref[:, sl] = (k[:, sl] + k_r).astype(k_ref.dtype)
    vt_ref[...] = _dot(c_kv, wuv_ref[...]).T.astype(vt_ref.dtype)


def _mla_attn_body(tq, q_ref, k_ref, vt_ref, o_ref,
                   m_scr, l_scr, acc_scr, s0_scr, s1_scr, p0_scr, p1_scr, mx0_scr, mx1_scr):
    qi = pl.program_id(2)

    def scores(kj):
        start = pl.multiple_of(kj * tq, tq)
        return jnp.concatenate(
            [_dot_nt(k_ref[pl.ds(start, tq), hh * LANES:(hh + 1) * LANES], q_ref[:, hh * LANES:(hh + 1) * LANES])
             for hh in range(2)], axis=1)

    _flash_causal(qi, tq, scores, vt_ref, m_scr, l_scr, acc_scr, (s0_scr, s1_scr), (p0_scr, p1_scr), (mx0_scr, mx1_scr))
    acc = acc_scr[...] * (1.0 / l_scr[...])
    row = lax.broadcasted_iota(jnp.int32, (2 * MLA_V, tq), 0)
    o_t = jnp.where(row < MLA_V, acc[:, :tq], acc[:, tq:])
    o_ref[...] = o_t.T.astype(o_ref.dtype)


def _mla_mixer(h3, w_in, q_norm_g, kv_norm_g, w_uq, w_ukv):
    b, lp, d = h3.shape
    n = b * lp
    nh = MLA_HEADS
    qk = MLA_NOPE + MLA_ROPE
    half = MLA_ROPE // 2
    w_kr = jnp.pad(w_in[:, MLA_Q_RANK + MLA_KV_RANK:], ((0, 0), (MLA_NOPE, LANES - qk)))
    w_in_all = jnp.concatenate([w_in[:, :MLA_Q_RANK + MLA_KV_RANK], w_kr], axis=1).astype(BF16)
    w_uq_p = jnp.pad(w_uq.reshape(MLA_Q_RANK, nh, qk), ((0, 0), (0, 0), (0, LANES - qk)))
    w_uq_p = w_uq_p.reshape(MLA_Q_RANK, nh * LANES).astype(BF16)
    w_ukv3 = w_ukv.reshape(MLA_KV_RANK, nh, MLA_NOPE + MLA_V)
    w_uk_p = jnp.pad(w_ukv3[:, :, :MLA_NOPE], ((0, 0), (0, 0), (0, LANES - MLA_NOPE)))
    w_uk_p = w_uk_p.reshape(MLA_KV_RANK, nh * LANES).astype(BF16)
    w_uv = w_ukv3[:, :, MLA_NOPE:].reshape(MLA_KV_RANK, nh * MLA_V).astype(BF16)
    pos = jnp.arange(lp, dtype=F32)
    inv_freq = ROPE_BASE ** (-jnp.arange(0, MLA_ROPE, 2, dtype=F32) / MLA_ROPE)
    ang = pos[:, None] * inv_freq[None, :]
    cos, sin = jnp.cos(ang), jnp.sin(ang)
    z = lambda w: jnp.zeros((lp, w), F32)
    cm = jnp.concatenate([jnp.ones((lp, MLA_NOPE), F32), cos, cos, z(LANES - qk)], axis=1)
    s1 = jnp.concatenate([z(MLA_NOPE), -sin, z(LANES - MLA_NOPE - half)], axis=1)
    s2 = jnp.concatenate([z(MLA_NOPE + half), sin, z(LANES - qk)], axis=1)
    scale = qk ** -0.5 * LOG2E
    tq = _pick_tile(lp, (384, 256, 128))
    nk = lp // tq
    q, k, vt = _row_call(
        _mla_in_body, [h3.reshape(n, d)],
        [w_in_all, q_norm_g.reshape(1, MLA_Q_RANK), kv_norm_g.reshape(1, MLA_KV_RANK), w_uq_p, w_uk_p, w_uv],
        [(nh * LANES, BF16), (nh * LANES, BF16)], tq, "mla_in",
        pos_ins=[cm * scale, s1 * scale, s2 * scale, cm, s1, s2], lp=lp, t_outs=[(nh * MLA_V, BF16)])
    q, k = q.reshape(b, lp, nh * LANES), k.reshape(b, lp, nh * LANES)
    o = pl.pallas_call(
        functools.partial(_mla_attn_body, tq),
        grid=(b, nh // 2, nk),
        in_specs=[pl.BlockSpec((None, tq, 2 * LANES), lambda bi, hi, qi: (bi, qi, hi)),
                  pl.BlockSpec((None, lp, 2 * LANES), lambda bi, hi, qi: (bi, 0, hi)),
                  pl.BlockSpec((None, nk, 2 * MLA_V, tq), lambda bi, hi, qi: (bi, 0, hi, 0))],
        out_specs=pl.BlockSpec((None, tq, 2 * MLA_V), lambda bi, hi, qi: (bi, qi, hi)),
        out_shape=jax.ShapeDtypeStruct((b, lp, nh * MLA_V), BF16),
        scratch_shapes=_flash_scratch(2 * MLA_V, tq),
        name="mla_attn",
        compiler_params=pltpu.CompilerParams(dimension_semantics=("parallel", "parallel", "arbitrary"),
                                             vmem_limit_bytes=VMEM_LIMIT),
    )(q, k, vt)
    return o.reshape(n, nh * MLA_V)


def kernel(x, meta, rel_bias, ln_g, ln_b, ffn_w1, ffn_w3, ffn_w2, ssd_w_in, ssd_conv_w, ssd_conv_b, ssd_dt_bias, ssd_a_log, ssd_d, ssd_norm_g, ssd_w_out, diff_w_qkv, diff_lam_q1, diff_lam_k1, diff_lam_q2, diff_lam_k2, diff_subln_g, diff_w_out, s5_lam_re, s5_lam_im, s5_log_step, s5_b_re, s5_b_im, s5_c_re, s5_c_im, s5_d, s5_w_glu, s5_b_glu, mla_w_in, mla_q_norm_g, mla_kv_norm_g, mla_w_uq, mla_w_ukv, mla_w_out):
    b, seq, d = x.shape
    l = seq + N_META
    lp = -(-l // SEQ_ALIGN) * SEQ_ALIGN
    n = b * lp
    h = jnp.concatenate([jnp.broadcast_to(meta[None].astype(x.dtype), (b, N_META, d)), x,
                         jnp.zeros((b, lp - l, d), x.dtype)], axis=1).reshape(n, d)
    w1b, w3b, w2b = ffn_w1.astype(BF16), ffn_w3.astype(BF16), ffn_w2.astype(BF16)
    for i in range(DEPTH):
        kind, j = i % 4, i // 4
        h = _ffn(h, w1b, w3b, w2b, ln_g[i, 0], ln_b[i, 0], (i, 0))
        h3 = h.reshape(b, lp, d)
        if kind == 0:
            y = _ssd_mixer(h3, ssd_w_in[j], ssd_conv_w[j], ssd_conv_b[j], ssd_dt_bias[j], ssd_a_log[j],
                           ssd_d[j], ssd_norm_g[j])
            w_out, b_out = ssd_w_out[j], None
        elif kind == 1:
            lam_init = 0.8 - 0.6 * math.exp(-0.3 * i)
            y = _diff_mixer(h3, diff_w_qkv[j], diff_lam_q1[j], diff_lam_k1[j], diff_lam_q2[j], diff_lam_k2[j],
                            diff_subln_g[j], rel_bias, lam_init)
            w_out, b_out = diff_w_out[j], None
        elif kind == 2:
            y = _s5_mixer(h3, s5_lam_re[j], s5_lam_im[j], s5_log_step[j], s5_b_re[j], s5_b_im[j], s5_c_re[j],
                          s5_c_im[j], s5_d[j])
            w_out, b_out = s5_w_glu[j], s5_b_glu[j]
        else:
            y = _mla_mixer(h3, mla_w_in[j], mla_q_norm_g[j], mla_kv_norm_g[j], mla_w_uq[j], mla_w_ukv[j])
            w_out, b_out = mla_w_out[j], None
        h = _mix_ffn(y, h, w_out, b_out, ln_g[i, 1], ln_b[i, 1], w1b, w3b, w2b, ln_g[i, 2], ln_b[i, 2], (i, 1))
    return h.reshape(b, lp, d)[:, N_META:l]
```

```python
import functools
import math

import jax
import jax.numpy as jnp
from jax import lax
from jax.experimental import pallas as pl
from jax.experimental.pallas import tpu as pltpu

F32 = jnp.float32
BF16 = jnp.bfloat16

N_META = 16
DEPTH = 4
ALPHA = (2.0 * DEPTH) ** 0.25
LN_EPS = 1e-5
RMS_EPS = 1e-6
NEG_INF = -1e30
LOG2E = math.log2(math.e)

SSD_HEAD_DIM = 64
SSD_HEADS = 32
SSD_GROUPS = 8
SSD_STATE = 128
SSD_CONV = 4
SSD_CHUNK = 128

DIFF_HEADS = 8
DIFF_HEAD_DIM = 64
DIFF_V_DIM = 128
REL_BUCKETS = 32
REL_MAX_DIST = 128

S5_GROUP = 16
S5_GROUPS = 64
S5_STATE = 64
S5_CHUNK = 16
S5_GROUP_BLOCK = 8
S5_SUBLANES = 8
S5_UNROLL = 4

MLA_HEADS = 16
MLA_Q_RANK = 384
MLA_KV_RANK = 256
MLA_NOPE = 64
MLA_ROPE = 32
MLA_V = 64
ROPE_BASE = 10000.0

FFN_ROW_TILES = (512, 384, 256, 128)
FFN_CHUNKS = 11

LANES = 128
SEQ_ALIGN = 128
VMEM_LIMIT = 56 * 1024 * 1024


def _pick_tile(n, candidates):
    for c in candidates:
        if n % c == 0:
            return c
    raise ValueError(f"no tile for {n}")


def _layer_norm(z, g, b):
    mu = jnp.mean(z, axis=-1, keepdims=True)
    d = z - mu
    var = jnp.mean(d * d, axis=-1, keepdims=True)
    return d * lax.rsqrt(var + LN_EPS) * g + b


def _rms(z, g):
    return z * lax.rsqrt(jnp.mean(z * z, axis=-1, keepdims=True) + RMS_EPS) * g


def _silu(x):
    return x * (1.0 / (1.0 + jnp.exp(-x)))


def _dot(a, b):
    return jnp.dot(a, b, preferred_element_type=F32)


def _dot_nt(a, b):
    return lax.dot_general(a, b, (((1,), (1,)), ((), ())), preferred_element_type=F32)


def _row_call(body, row_ins, const_ins, outs, tm, name, pos_ins=(), lp=None, t_outs=()):
    n = row_ins[0].shape[0]
    assert n % tm == 0
    in_specs = [pl.BlockSpec((tm, a.shape[1]), lambda i: (i, 0)) for a in row_ins]
    if pos_ins or t_outs:
        assert lp % tm == 0
        nb = lp // tm
        in_specs += [pl.BlockSpec((tm, a.shape[1]), lambda i: (i % nb, 0)) for a in pos_ins]
    const_arrays = []
    for c in const_ins:
        arr, lead = c if isinstance(c, tuple) else (c, ())
        const_arrays.append(arr)
        in_specs.append(pl.BlockSpec((None,) * len(lead) + arr.shape[len(lead):],
                                     functools.partial(lambda lead_, i: lead_ + (0, 0), tuple(lead)),
                                     pipeline_mode=pl.Buffered(1)))
    out_specs = [pl.BlockSpec((tm, w), lambda i: (i, 0)) for (w, _) in outs]
    out_shape = [jax.ShapeDtypeStruct((n, w), dt) for (w, dt) in outs]
    out_specs += [pl.BlockSpec((None, None, w, tm), lambda i: (i // nb, i % nb, 0, 0)) for (w, _) in t_outs]
    out_shape += [jax.ShapeDtypeStruct((n // lp, nb, w, tm), dt) for (w, dt) in t_outs]
    res = pl.pallas_call(
        body, grid=(n // tm,), in_specs=in_specs, out_specs=out_specs, out_shape=out_shape, name=name,
        compiler_params=pltpu.CompilerParams(dimension_semantics=("parallel",), vmem_limit_bytes=VMEM_LIMIT),
    )(*row_ins, *pos_ins, *const_arrays)
    return res


def _ffn_rows(x, n_chunks, w1_ref, w3_ref, w2_ref, g_ref, b_ref):
    xb = x.astype(BF16)
    fc = w1_ref.shape[1] // n_chunks
    y = None
    for c in range(n_chunks):
        sl = slice(c * fc, (c + 1) * fc)
        gate = _dot(xb, w1_ref[:, sl])
        up = _dot(xb, w3_ref[:, sl])
        a = (_silu(gate) * up).astype(BF16)
        part = _dot(a, w2_ref[sl, :])
        y = part if y is None else y + part
    return _layer_norm(ALPHA * x + 0.5 * y, g_ref[...], b_ref[...])


def _ffn_body(n_chunks, h_ref, w1_ref, w3_ref, w2_ref, g_ref, b_ref, o_ref):
    o_ref[...] = _ffn_rows(h_ref[...], n_chunks, w1_ref, w3_ref, w2_ref, g_ref, b_ref)


def _mix_ffn_body(n_chunks, glu, y_ref, h_ref, wo_ref, bo_ref, gm_ref, bm_ref,
                  w1_ref, w3_ref, w2_ref, g_ref, b_ref, o_ref):
    m = _dot(y_ref[...].astype(BF16), wo_ref[...])
    if glu:
        m = m + bo_ref[...]
        d = o_ref.shape[1]
        m = m[:, :d] * (1.0 / (1.0 + jnp.exp(-m[:, d:])))
    x = _layer_norm(ALPHA * h_ref[...] + m, gm_ref[...], bm_ref[...])
    o_ref[...] = _ffn_rows(x, n_chunks, w1_ref, w3_ref, w2_ref, g_ref, b_ref)


def _ffn_chunks(f, n_chunks):
    return n_chunks or (FFN_CHUNKS if f % (FFN_CHUNKS * LANES) == 0 else 1)


def _ffn(h2, w1, w3, w2, g, b, idx, tm=None, n_chunks=None, name="ffn"):
    n, d = h2.shape
    tm = tm or _pick_tile(n, FFN_ROW_TILES)
    (out,) = _row_call(functools.partial(_ffn_body, _ffn_chunks(w1.shape[-1], n_chunks)), [h2],
                       [(w1, idx), (w3, idx), (w2, idx), g.reshape(1, d), b.reshape(1, d)],
                       [(d, F32)], tm, name)
    return out


def _mix_ffn(y2, h2, w_out, b_out, g_mix, b_mix, w1, w3, w2, g, b, idx, tm=None, n_chunks=None, name="mix_ffn"):
    n, d = h2.shape
    tm = tm or _pick_tile(n, FFN_ROW_TILES)
    glu = b_out is not None
    wo = w_out.astype(BF16)
    bo = (b_out if glu else jnp.zeros((wo.shape[1],), F32)).reshape(1, wo.shape[1])
    (out,) = _row_call(functools.partial(_mix_ffn_body, _ffn_chunks(w1.shape[-1], n_chunks), glu), [y2, h2],
                       [wo, bo, g_mix.reshape(1, d), b_mix.reshape(1, d),
                        (w1, idx), (w3, idx), (w2, idx), g.reshape(1, d), b.reshape(1, d)],
                       [(d, F32)], tm, name)
    return out


def _ssd_in_body(h_ref, w_ref, xbc_ref, z_ref, dt_ref):
    r = _dot(h_ref[...].astype(BF16), w_ref[...])
    nx, nz = xbc_ref.shape[1], z_ref.shape[1]
    xbc_ref[...] = r[:, :nx]
    z_ref[...] = r[:, nx:nx + nz]
    dt_ref[...] = r[:, nx + nz:]


def _softplus(x):
    return jnp.maximum(x, 0.0) + jnp.log1p(jnp.exp(-jnp.abs(x)))


def _ssd_body(xbc_ref, dt_ref, z_ref, convw_ref, convb_ref, dtb_ref, alog_ref, dskip_ref, normg_ref,
              y_ref, buf_scr, state_scr, y_scr):
    t = SSD_CHUNK
    d_inner = SSD_HEADS * SSD_HEAD_DIM
    gn = SSD_GROUPS * SSD_STATE
    hp = SSD_HEAD_DIM
    r = SSD_HEADS // SSD_GROUPS

    @pl.when(pl.program_id(1) == 0)
    def _():
        buf_scr[0:8, :] = jnp.zeros((8, buf_scr.shape[1]), F32)
        state_scr[...] = jnp.zeros(state_scr.shape, F32)

    x = xbc_ref[...]
    buf_scr[8:8 + t, :] = x
    acc = convb_ref[...] + convw_ref[SSD_CONV - 1:SSD_CONV, :] * x
    for j in range(1, SSD_CONV):
        acc = acc + convw_ref[SSD_CONV - 1 - j:SSD_CONV - j, :] * buf_scr[8 - j:8 - j + t, :]
    buf_scr[0:8, :] = x[t - 8:t, :]
    xbc = _silu(acc)

    dt = _softplus(dt_ref[...] + dtb_ref[...])
    a = dt * (-jnp.exp(alog_ref[...]))
    rows = lax.broadcasted_iota(jnp.int32, (t, t), 0)
    cols = lax.broadcasted_iota(jnp.int32, (t, t), 1)
    tri = rows >= cols
    a_cum = jnp.dot(tri.astype(F32), a, precision=lax.Precision.HIGHEST, preferred_element_type=F32)
    a_last = a_cum[t - 1:t, :]
    a_cum_t = a_cum.T
    dt_t = dt.T
    w_end_t = (dt * jnp.exp(a_last - a_cum)).T
    chunk_dec = jnp.exp(a_last)

    for g in range(SSD_GROUPS):
        b_f = xbc[:, d_inner + g * SSD_STATE:d_inner + (g + 1) * SSD_STATE]
        c_g = xbc[:, d_inner + gn + g * SSD_STATE:d_inner + gn + (g + 1) * SSD_STATE].astype(BF16)
        b_g = b_f.astype(BF16)
        b_ft = b_f.T
        cb = _dot_nt(c_g, b_g)
        y_off_g = _dot(c_g, state_scr[:, g * r * hp:(g + 1) * r * hp].astype(BF16))
        for k in range(r):
            h = g * r + k
            col = jnp.broadcast_to(a_cum[:, h:h + 1], (t, t))
            decay = jnp.exp(jnp.where(tri, col - a_cum_t[h:h + 1, :], NEG_INF))
            xh = xbc[:, h * hp:(h + 1) * hp]
            xb = xh.astype(BF16)
            y_diag = _dot((cb * decay * dt_t[h:h + 1, :]).astype(BF16), xb)
            y_off = y_off_g[:, k * hp:(k + 1) * hp] * jnp.exp(col[:, :hp])
            y_scr[:, h * hp:(h + 1) * hp] = y_diag + y_off + dskip_ref[:, h * hp:(h + 1) * hp] * xh
            st = state_scr[:, h * hp:(h + 1) * hp]
            b_w = (b_ft * w_end_t[h:h + 1, :]).astype(BF16)
            state_scr[:, h * hp:(h + 1) * hp] = st * chunk_dec[:, h:h + 1] + _dot(b_w, xb)

    y = y_scr[...] * _silu(z_ref[...])
    gs = d_inner // SSD_GROUPS
    for g in range(SSD_GROUPS):
        sl = slice(g * gs, (g + 1) * gs)
        y_ref[:, sl] = _rms(y[:, sl], normg_ref[:, sl]).astype(y_ref.dtype)


def _ssd_mixer(h3, w_in, conv_w, conv_b, dt_bias, a_log, d_skip, norm_g):
    b, lp, d = h3.shape
    n = b * lp
    d_inner = SSD_HEADS * SSD_HEAD_DIM
    conv_ch = d_inner + 2 * SSD_GROUPS * SSD_STATE
    w_z, w_xbc, w_dt = w_in[:, :d_inner], w_in[:, d_inner:d_inner + conv_ch], w_in[:, d_inner + conv_ch:]
    w_dt = jnp.pad(w_dt, ((0, 0), (0, LANES - SSD_HEADS)))
    w_all = jnp.concatenate([w_xbc, w_z, w_dt], axis=1).astype(BF16)
    tm = _pick_tile(n, (512, 384, 256, 128))
    xbc, z, dt = _row_call(_ssd_in_body, [h3.reshape(n, d)], [w_all],
                           [(conv_ch, F32), (d_inner, F32), (LANES, F32)], tm, "ssd_in")
    t = SSD_CHUNK
    pad1 = lambda v: jnp.pad(v, (0, LANES - SSD_HEADS)).reshape(1, LANES)
    consts = [conv_w, conv_b.reshape(1, conv_ch), pad1(dt_bias), pad1(a_log),
              jnp.repeat(d_skip, SSD_HEAD_DIM).reshape(1, d_inner), norm_g.reshape(1, d_inner)]
    row_spec = lambda w: pl.BlockSpec((None, t, w), lambda i, c: (i, c, 0))
    y = pl.pallas_call(
        _ssd_body, grid=(b, lp // t),
        in_specs=[row_spec(conv_ch), row_spec(LANES), row_spec(d_inner)]
        + [pl.BlockSpec(c.shape, lambda i, c_: (0, 0)) for c in consts],
        out_specs=row_spec(d_inner),
        out_shape=jax.ShapeDtypeStruct((b, lp, d_inner), BF16),
        scratch_shapes=[pltpu.VMEM((8 + t, conv_ch), F32), pltpu.VMEM((SSD_STATE, d_inner), F32),
                        pltpu.VMEM((t, d_inner), F32)],
        name="ssd_scan",
        compiler_params=pltpu.CompilerParams(dimension_semantics=("parallel", "arbitrary"),
                                             vmem_limit_bytes=VMEM_LIMIT),
    )(xbc.reshape(b, lp, conv_ch), dt.reshape(b, lp, LANES), z.reshape(b, lp, d_inner), *consts)
    return y.reshape(n, d_inner)


def _qkv_body(n_q, scale, h_ref, w_ref, qk_ref, vt_ref):
    r = _dot(h_ref[...].astype(BF16), w_ref[...])
    n_qk = qk_ref.shape[1]
    qk_ref[:, :n_q] = (r[:, :n_q] * scale).astype(qk_ref.dtype)
    qk_ref[:, n_q:] = r[:, n_q:n_qk].astype(qk_ref.dtype)
    vt_ref[...] = r[:, n_qk:].T.astype(vt_ref.dtype)


def _causal_mask_pair(s, tq):
    rows = lax.broadcasted_iota(jnp.int32, s.shape, 0)
    cols = lax.broadcasted_iota(jnp.int32, s.shape, 1)
    cols = jnp.where(cols >= tq, cols - tq, cols)
    return jnp.where(rows <= cols, s, NEG_INF)


def _flash_causal(qi, tq, scores, vt_ref, m_scr, l_scr, acc_scr, s_scrs, p_scrs, mx_scrs):
    m_scr[...] = jnp.full(m_scr.shape, NEG_INF, F32)
    l_scr[...] = jnp.zeros(l_scr.shape, F32)
    acc_scr[...] = jnp.zeros(acc_scr.shape, F32)
    p_scrs[1][...] = jnp.zeros(p_scrs[1].shape, p_scrs[1].dtype)

    def softmax(s, s_max, pv):
        m_prev = m_scr[...]
        m_new = jnp.maximum(m_prev, s_max)
        alpha = jnp.exp2(m_prev - m_new)
        p = jnp.exp2(s - m_new)
        l_scr[...] = alpha * l_scr[...] + jnp.sum(p, axis=0, keepdims=True)
        m_scr[...] = m_new
        acc_scr[...] = alpha * (acc_scr[...] + pv)
        return p.astype(BF16)

    def produce(kj, slot):
        s = scores(kj)
        s_scrs[slot][...] = s
        mx_scrs[slot][...] = jnp.max(s, axis=0, keepdims=True)

    def step(kj, cur, nxt):
        pv = _dot(vt_ref[jnp.maximum(kj - 1, 0)], p_scrs[nxt][...])
        p_scrs[cur][...] = softmax(s_scrs[cur][...], mx_scrs[cur][...], pv)
        produce(kj + 1, nxt)

    produce(0, 0)

    def body(t, carry):
        step(2 * t, 0, 1)
        step(2 * t + 1, 1, 0)
        return carry

    lax.fori_loop(0, lax.shift_right_logical(qi, 1), body, 0)
    pl.when((qi & 1) == 1)(functools.partial(step, qi - 1, 0, 1))

    def last(cur, nxt):
        pv = _dot(vt_ref[jnp.maximum(qi - 1, 0)], p_scrs[nxt][...])
        s = _causal_mask_pair(s_scrs[cur][...], tq)
        p = softmax(s, jnp.max(s, axis=0, keepdims=True), pv)
        acc_scr[...] = acc_scr[...] + _dot(vt_ref[qi], p)

    for par in range(2):
        pl.when((qi & 1) == par)(functools.partial(last, par, 1 - par))


def _flash_scratch(e, tq):
    w = 2 * tq
    return [pltpu.VMEM((1, w), F32), pltpu.VMEM((1, w), F32), pltpu.VMEM((e, w), F32),
            pltpu.VMEM((tq, w), F32), pltpu.VMEM((tq, w), F32), pltpu.VMEM((tq, w), BF16), pltpu.VMEM((tq, w), BF16),
            pltpu.VMEM((1, w), F32), pltpu.VMEM((1, w), F32)]


def _t5_bucket(dist):
    max_exact = REL_BUCKETS // 2
    d = jnp.maximum(dist, 0)
    df = jnp.maximum(d, max_exact).astype(F32)
    large = max_exact + (jnp.log(df / max_exact) / math.log(REL_MAX_DIST / max_exact)
                         * (REL_BUCKETS - max_exact)).astype(jnp.int32)
    large = jnp.minimum(large, REL_BUCKETS - 1)
    return jnp.where(d < max_exact, d, large)


def _diff_attn_body(tq, lam_init, q_ref, k_ref, vt_ref, bias_ref, lam_ref, g_ref, o_ref,
                    m_scr, l_scr, acc_scr, s0_scr, s1_scr, p0_scr, p1_scr, mx0_scr, mx1_scr, qcat_scr):
    qi = pl.program_id(2)
    q = q_ref[...]
    lane = lax.broadcasted_iota(jnp.int32, q.shape, 1)
    zero = jnp.zeros_like(q)
    qcat_scr[0:tq, :] = jnp.where(lane < DIFF_HEAD_DIM, q, zero)
    qcat_scr[tq:2 * tq, :] = jnp.where(lane >= DIFF_HEAD_DIM, q, zero)

    def scores(kj):
        start = pl.multiple_of(kj * tq, tq)
        bias = bias_ref[jnp.minimum(qi - kj, 2)]
        return _dot_nt(k_ref[pl.ds(start, tq), :], qcat_scr[...]) + jnp.concatenate([bias, bias], axis=1)

    _flash_causal(qi, tq, scores, vt_ref, m_scr, l_scr, acc_scr, (s0_scr, s1_scr), (p0_scr, p1_scr), (mx0_scr, mx1_scr))
    acc = acc_scr[...] * (1.0 / l_scr[...])
    o_t = acc[:, :tq] - lam_ref[...] * acc[:, tq:]
    o_ref[...] = (_rms(o_t.T, g_ref[...]) * (1.0 - lam_init)).astype(o_ref.dtype)


def _diff_mixer(h3, w_qkv, lam_q1, lam_k1, lam_q2, lam_k2, subln_g, rel_bias, lam_init):
    b, lp, d = h3.shape
    n = b * lp
    hd = DIFF_HEADS * 2 * DIFF_HEAD_DIM
    tq = _pick_tile(lp, (384, 256, 128))
    assert tq >= REL_MAX_DIST
    nk = lp // tq
    qk, vt = _row_call(functools.partial(_qkv_body, hd, DIFF_HEAD_DIM ** -0.5 * LOG2E), [h3.reshape(n, d)],
                       [w_qkv.astype(BF16)], [(2 * hd, BF16)], tq, "diff_qkv", lp=lp, t_outs=[(hd, BF16)])
    qk = qk.reshape(b, lp, 2 * hd)
    m2 = 2 * tq
    kk = jnp.arange(m2)
    dist = jnp.where(kk < tq, kk, kk - m2)[None, :] + (jnp.arange(3) * tq)[:, None]
    tab = jnp.transpose(rel_bias[_t5_bucket(dist)].astype(F32) * LOG2E, (2, 0, 1))
    nh = DIFF_HEADS
    bias = jnp.tile(tab, (1, 1, tq))[..., :tq * (m2 - 1)].reshape(nh, 3, tq, m2 - 1)[..., :tq]
    lam = (jnp.exp(jnp.sum(lam_q1.astype(F32) * lam_k1.astype(F32)))
           - jnp.exp(jnp.sum(lam_q2.astype(F32) * lam_k2.astype(F32))) + lam_init)
    w = 2 * DIFF_HEAD_DIM
    o = pl.pallas_call(
        functools.partial(_diff_attn_body, tq, lam_init),
        grid=(b, nh, nk),
        in_specs=[pl.BlockSpec((None, tq, w), lambda bi, hi, qi: (bi, qi, hi)),
                  pl.BlockSpec((None, lp, w), lambda bi, hi, qi: (bi, 0, nh + hi)),
                  pl.BlockSpec((None, nk, DIFF_V_DIM, tq), lambda bi, hi, qi: (bi, 0, hi, 0)),
                  pl.BlockSpec((None, 3, tq, tq), lambda bi, hi, qi: (hi, 0, 0, 0)),
                  pl.BlockSpec((1, 1), lambda bi, hi, qi: (0, 0)),
                  pl.BlockSpec((1, DIFF_V_DIM), lambda bi, hi, qi: (0, 0))],
        out_specs=pl.BlockSpec((None, tq, DIFF_V_DIM), lambda bi, hi, qi: (bi, qi, hi)),
        out_shape=jax.ShapeDtypeStruct((b, lp, nh * DIFF_V_DIM), BF16),
        scratch_shapes=_flash_scratch(DIFF_V_DIM, tq) + [pltpu.VMEM((2 * tq, w), BF16)],
        name="diff_attn",
        compiler_params=pltpu.CompilerParams(dimension_semantics=("parallel", "parallel", "arbitrary"),
                                             vmem_limit_bytes=VMEM_LIMIT),
    )(qk, qk, vt, bias, lam.reshape(1, 1), subln_g.reshape(1, DIFF_V_DIM))
    return o.reshape(n, nh * DIFF_V_DIM)


def _gelu_tanh(x):
    return 0.5 * x * (1.0 + jnp.tanh(math.sqrt(2.0 / math.pi) * (x + 0.044715 * (x * x * x))))


def _s5_body(bp, x_ref, m_ref, bc_ref, ccre_ref, ccim_ref, are_ref, aim_ref, y_ref, sre_scr, sim_scr):
    rows = x_ref.shape[0]
    gb = m_ref.shape[0]
    kw = m_ref.shape[1]
    p = ccre_ref.shape[1]
    for g in range(gb):
        e = _dot(x_ref[:, g * kw:(g + 1) * kw], bc_ref[g])
        sre_scr[:, g * p:(g + 1) * p] = e[:, :p]
        sim_scr[:, g * p:(g + 1) * p] = e[:, p:]

    ar, ai = are_ref[...], aim_ref[...]
    sub = lax.broadcasted_iota(jnp.int32, ar.shape, 0)
    per_tile = S5_SUBLANES // bp

    def tile_step(r, carry):
        pr, pi = carry
        r0 = pl.multiple_of(r * S5_SUBLANES, S5_SUBLANES)
        er, ei = sre_scr[pl.ds(r0, S5_SUBLANES), :], sim_scr[pl.ds(r0, S5_SUBLANES), :]
        in_r, in_i = pr, pi
        for c in range(per_tile):
            nr = ar * pr - ai * pi + er
            ni = ar * pi + ai * pr + ei
            if per_tile > 1:
                nr = pltpu.roll(nr, bp, axis=0)
                ni = pltpu.roll(ni, bp, axis=0)
            if c + 1 < per_tile:
                sel = (sub >= (c + 1) * bp) & (sub < (c + 2) * bp)
                in_r, in_i = jnp.where(sel, nr, in_r), jnp.where(sel, ni, in_i)
                pr, pi = nr, ni
            else:
                pr, pi = nr, ni
        if per_tile > 1:
            pr = jnp.where(sub < bp, pr, pltpu.roll(pr, bp, axis=0))
            pi = jnp.where(sub < bp, pi, pltpu.roll(pi, bp, axis=0))
        sre_scr[pl.ds(r0, S5_SUBLANES), :] = in_r
        sim_scr[pl.ds(r0, S5_SUBLANES), :] = in_i
        return pr, pi

    zero = jnp.zeros(ar.shape, F32)
    lax.fori_loop(0, rows // S5_SUBLANES, tile_step, (zero, zero), unroll=S5_UNROLL)

    for g in range(gb):
        xg = x_ref[:, g * kw:(g + 1) * kw]
        y = (_dot(xg, m_ref[g]) + _dot(sre_scr[:, g * p:(g + 1) * p].astype(BF16), ccre_ref[g])
             + _dot(sim_scr[:, g * p:(g + 1) * p].astype(BF16), ccim_ref[g]))
        y_ref[:, g * kw:(g + 1) * kw] = _gelu_tanh(y).astype(y_ref.dtype)


def _s5_pack_body(n_batch, h_ref, perm_ref, x_ref):
    t_len = h_ref.shape[1]
    xcat = jnp.concatenate([h_ref[:, t, :].astype(BF16) for t in range(t_len)], axis=1)
    x = _dot(xcat, perm_ref[...]).astype(x_ref.dtype)
    x_ref[...] = jnp.where(pl.program_id(0) < n_batch, x, jnp.zeros_like(x))


def _s5_unpack_body(y_ref, perm_ref, o_ref):
    t_len = o_ref.shape[1]
    res = _dot(y_ref[...], perm_ref[...])
    for t in range(t_len):
        o_ref[:, t, :] = res[:, t * LANES:(t + 1) * LANES]


def _s5_chunk_operators(lam_re, lam_im, log_step, b_re, b_im, c_re, c_im, d_skip):
    hi = lax.Precision.HIGHEST
    t_len = S5_CHUNK
    step = jnp.exp(log_step.astype(F32))[:, None]
    lr = jnp.minimum(lam_re.astype(F32), -1e-4)
    li = lam_im.astype(F32)
    mag = jnp.exp(lr * step)
    ar, ai = mag * jnp.cos(li * step), mag * jnp.sin(li * step)
    den = lr * lr + li * li
    cr = ((ar - 1.0) * lr + ai * li) / den
    ci = (ai * lr - (ar - 1.0) * li) / den
    br, bi = b_re.astype(F32), b_im.astype(F32)
    bbr = cr[..., None] * br - ci[..., None] * bi
    bbi = cr[..., None] * bi + ci[..., None] * br
    k = jnp.arange(t_len + 1, dtype=F32)[:, None, None]
    pmag = jnp.exp(k * (lr * step))
    pr, pi = pmag * jnp.cos(k * (li * step)), pmag * jnp.sin(k * (li * step))
    abr = pr[:t_len, ..., None] * bbr - pi[:t_len, ..., None] * bbi
    abi = pr[:t_len, ..., None] * bbi + pi[:t_len, ..., None] * bbr
    cre, cim = c_re.astype(F32), c_im.astype(F32)
    kern = (jnp.einsum('gop,kgpc->kgoc', cre, abr, precision=hi)
            - jnp.einsum('gop,kgpc->kgoc', cim, abi, precision=hi))
    kern = kern.at[0].add(jax.vmap(jnp.diag)(d_skip.astype(F32)))
    sig = jnp.arange(t_len)
    lag = sig[None, :] - sig[:, None]
    m = jnp.where((lag >= 0)[..., None, None, None], kern[jnp.maximum(lag, 0)], 0.0)
    g, o, c = kern.shape[1:]
    m = jnp.transpose(m, (2, 0, 4, 1, 3)).reshape(g, t_len * c, t_len * o)
    rev = t_len - 1 - sig
    p = bbr.shape[1]
    bc = jnp.concatenate([jnp.transpose(abr[rev], (1, 0, 3, 2)).reshape(g, t_len * c, p),
                          jnp.transpose(abi[rev], (1, 0, 3, 2)).reshape(g, t_len * c, p)], axis=2)
    car = cre[None] * pr[1:, :, None, :] - cim[None] * pi[1:, :, None, :]
    cai = cre[None] * pi[1:, :, None, :] + cim[None] * pr[1:, :, None, :]
    cc_re = jnp.transpose(car, (1, 3, 0, 2)).reshape(g, p, t_len * o)
    cc_im = -jnp.transpose(cai, (1, 3, 0, 2)).reshape(g, p, t_len * o)
    return (m.astype(BF16), bc.astype(BF16), cc_re.astype(BF16), cc_im.astype(BF16),
            pr[t_len].reshape(g * p), pi[t_len].reshape(g * p))


def _s5_mixer(h3, lam_re, lam_im, log_step, b_re, b_im, c_re, c_im, d_skip):
    b, lp, d = h3.shape
    t_len, g, c = S5_CHUNK, S5_GROUPS, S5_GROUP
    assert lp % t_len == 0 and b <= S5_SUBLANES
    bp = 4 if b <= 4 else S5_SUBLANES
    nj = lp // t_len
    m, bc, cc_re, cc_im, at_re, at_im = _s5_chunk_operators(lam_re, lam_im, log_step, b_re, b_im, c_re, c_im, d_skip)
    n_state = g * S5_STATE
    at_re = jnp.broadcast_to(at_re.reshape(1, n_state), (S5_SUBLANES, n_state))
    at_im = jnp.broadcast_to(at_im.reshape(1, n_state), (S5_SUBLANES, n_state))
    rows = nj * bp
    gb = S5_GROUP_BLOCK
    kw = t_len * c
    assert gb * c == LANES
    n_blk = g // gb
    src = jnp.arange(gb * kw)
    tau, r, ch = src // (gb * c), (src // c) % gb, src % c
    perm = jax.nn.one_hot(r * kw + tau * c + ch, gb * kw, dtype=BF16)
    x = pl.pallas_call(
        functools.partial(_s5_pack_body, b),
        grid=(bp, n_blk),
        in_specs=[pl.BlockSpec((None, nj, t_len, LANES), lambda bi, mi: (jnp.minimum(bi, b - 1), 0, 0, mi)),
                  pl.BlockSpec(perm.shape, lambda bi, mi: (0, 0))],
        out_specs=pl.BlockSpec((nj, gb * kw), lambda bi, mi: (0, bi * n_blk + mi)),
        out_shape=jax.ShapeDtypeStruct((nj, bp * g * kw), BF16),
        name="s5_pack",
        compiler_params=pltpu.CompilerParams(dimension_semantics=("parallel", "parallel"),
                                             vmem_limit_bytes=VMEM_LIMIT),
    )(h3.reshape(b, nj, t_len, d), perm).reshape(rows, g * kw)
    y = pl.pallas_call(
        functools.partial(_s5_body, bp), grid=(g // gb,),
        in_specs=[pl.BlockSpec((rows, gb * kw), lambda i: (0, i)),
                  pl.BlockSpec((gb, kw, kw), lambda i: (i, 0, 0)),
                  pl.BlockSpec((gb, kw, 2 * S5_STATE), lambda i: (i, 0, 0)),
                  pl.BlockSpec((gb, S5_STATE, kw), lambda i: (i, 0, 0)),
                  pl.BlockSpec((gb, S5_STATE, kw), lambda i: (i, 0, 0)),
                  pl.BlockSpec((S5_SUBLANES, gb * S5_STATE), lambda i: (0, i)),
                  pl.BlockSpec((S5_SUBLANES, gb * S5_STATE), lambda i: (0, i))],
        out_specs=pl.BlockSpec((rows, gb * kw), lambda i: (0, i)),
        out_shape=jax.ShapeDtypeStruct((rows, g * kw), BF16),
        scratch_shapes=[pltpu.VMEM((rows, gb * S5_STATE), F32), pltpu.VMEM((rows, gb * S5_STATE), F32)],
        name="s5_chunks",
        compiler_params=pltpu.CompilerParams(dimension_semantics=("parallel",), vmem_limit_bytes=VMEM_LIMIT),
    )(x, m, bc, cc_re, cc_im, at_re, at_im)
    y = pl.pallas_call(
        _s5_unpack_body,
        grid=(b, n_blk),
        in_specs=[pl.BlockSpec((nj, gb * kw), lambda bi, mi: (0, bi * n_blk + mi)),
                  pl.BlockSpec(perm.shape, lambda bi, mi: (0, 0))],
        out_specs=pl.BlockSpec((None, nj, t_len, LANES), lambda bi, mi: (bi, 0, 0, mi)),
        out_shape=jax.ShapeDtypeStruct((b, nj, t_len, d), F32),
        name="s5_unpack",
        compiler_params=pltpu.CompilerParams(dimension_semantics=("parallel", "parallel"),
                                             vmem_limit_bytes=VMEM_LIMIT),
    )(y.reshape(nj, bp * g * kw), perm.T)
    return y.reshape(b * lp, d)


def _rope_block(x, cm, s1, s2):
    return (x * cm + pltpu.roll(x, LANES - MLA_ROPE // 2, axis=1) * s1 + pltpu.roll(x, MLA_ROPE // 2, axis=1) * s2)


def _mla_in_body(h_ref, qcm_ref, qs1_ref, qs2_ref, kcm_ref, ks1_ref, ks2_ref,
                 win_ref, qg_ref, kvg_ref, wuq_ref, wuk_ref, wuv_ref, q_ref, k_ref, vt_ref):
    c = _dot(h_ref[...].astype(BF16), win_ref[...])
    c_q = _rms(c[:, :MLA_Q_RANK], qg_ref[...]).astype(BF16)
    c_kv = _rms(c[:, MLA_Q_RANK:MLA_Q_RANK + MLA_KV_RANK], kvg_ref[...]).astype(BF16)
    k_r = _rope_block(c[:, MLA_Q_RANK + MLA_KV_RANK:], kcm_ref[...], ks1_ref[...], ks2_ref[...])
    q = _dot(c_q, wuq_ref[...])
    k = _dot(c_kv, wuk_ref[...])
    qcm, qs1, qs2 = qcm_ref[...], qs1_ref[...], qs2_ref[...]
    for h in range(MLA_HEADS):
        sl = slice(h * LANES, (h + 1) * LANES)
        q_ref[:, sl] = _rope_block(q[:, sl], qcm, qs1, qs2).astype(q_ref.dtype)
        k_ref[:, sl] = (k[:, sl] + k_r).astype(k_ref.dtype)
    vt_ref[...] = _dot(c_kv, wuv_ref[...]).T.astype(vt_ref.dtype)


def _mla_attn_body(tq, q_ref, k_ref, vt_ref, o_ref,
                   m_scr, l_scr, acc_scr, s0_scr, s1_scr, p0_scr, p1_scr, mx0_scr, mx1_scr):
    qi = pl.program_id(2)

    def scores(kj):
        start = pl.multiple_of(kj * tq, tq)
        return jnp.concatenate(
            [_dot_nt(k_ref[pl.ds(start, tq), hh * LANES:(hh + 1) * LANES], q_ref[:, hh * LANES:(hh + 1) * LANES])
             for hh in range(2)], axis=1)

    _flash_causal(qi, tq, scores, vt_ref, m_scr, l_scr, acc_scr, (s0_scr, s1_scr), (p0_scr, p1_scr), (mx0_scr, mx1_scr))
    acc = acc_scr[...] * (1.0 / l_scr[...])
    row = lax.broadcasted_iota(jnp.int32, (2 * MLA_V, tq), 0)
    o_t = jnp.where(row < MLA_V, acc[:, :tq], acc[:, tq:])
    o_ref[...] = o_t.T.astype(o_ref.dtype)


def _mla_mixer(h3, w_in, q_norm_g, kv_norm_g, w_uq, w_ukv):
    b, lp, d = h3.shape
    n = b * lp
    nh = MLA_HEADS
    qk = MLA_NOPE + MLA_ROPE
    half = MLA_ROPE // 2
    w_kr = jnp.pad(w_in[:, MLA_Q_RANK + MLA_KV_RANK:], ((0, 0), (MLA_NOPE, LANES - qk)))
    w_in_all = jnp.concatenate([w_in[:, :MLA_Q_RANK + MLA_KV_RANK], w_kr], axis=1).astype(BF16)
    w_uq_p = jnp.pad(w_uq.reshape(MLA_Q_RANK, nh, qk), ((0, 0), (0, 0), (0, LANES - qk)))
    w_uq_p = w_uq_p.reshape(MLA_Q_RANK, nh * LANES).astype(BF16)
    w_ukv3 = w_ukv.reshape(MLA_KV_RANK, nh, MLA_NOPE + MLA_V)
    w_uk_p = jnp.pad(w_ukv3[:, :, :MLA_NOPE], ((0, 0), (0, 0), (0, LANES - MLA_NOPE)))
    w_uk_p = w_uk_p.reshape(MLA_KV_RANK, nh * LANES).astype(BF16)
    w_uv = w_ukv3[:, :, MLA_NOPE:].reshape(MLA_KV_RANK, nh * MLA_V).astype(BF16)
    pos = jnp.arange(lp, dtype=F32)
    inv_freq = ROPE_BASE ** (-jnp.arange(0, MLA_ROPE, 2, dtype=F32) / MLA_ROPE)
    ang = pos[:, None] * inv_freq[None, :]
    cos, sin = jnp.cos(ang), jnp.sin(ang)
    z = lambda w: jnp.zeros((lp, w), F32)
    cm = jnp.concatenate([jnp.ones((lp, MLA_NOPE), F32), cos, cos, z(LANES - qk)], axis=1)
    s1 = jnp.concatenate([z(MLA_NOPE), -sin, z(LANES - MLA_NOPE - half)], axis=1)
    s2 = jnp.concatenate([z(MLA_NOPE + half), sin, z(LANES - qk)], axis=1)
    scale = qk ** -0.5 * LOG2E
    tq = _pick_tile(lp, (384, 256, 128))
    nk = lp // tq
    q, k, vt = _row_call(
        _mla_in_body, [h3.reshape(n, d)],
        [w_in_all, q_norm_g.reshape(1, MLA_Q_RANK), kv_norm_g.reshape(1, MLA_KV_RANK), w_uq_p, w_uk_p, w_uv],
        [(nh * LANES, BF16), (nh * LANES, BF16)], tq, "mla_in",
        pos_ins=[cm * scale, s1 * scale, s2 * scale, cm, s1, s2], lp=lp, t_outs=[(nh * MLA_V, BF16)])
    q, k = q.reshape(b, lp, nh * LANES), k.reshape(b, lp, nh * LANES)
    o = pl.pallas_call(
        functools.partial(_mla_attn_body, tq),
        grid=(b, nh // 2, nk),
        in_specs=[pl.BlockSpec((None, tq, 2 * LANES), lambda bi, hi, qi: (bi, qi, hi)),
                  pl.BlockSpec((None, lp, 2 * LANES), lambda bi, hi, qi: (bi, 0, hi)),
                  pl.BlockSpec((None, nk, 2 * MLA_V, tq), lambda bi, hi, qi: (bi, 0, hi, 0))],
        out_specs=pl.BlockSpec((None, tq, 2 * MLA_V), lambda bi, hi, qi: (bi, qi, hi)),
        out_shape=jax.ShapeDtypeStruct((b, lp, nh * MLA_V), BF16),
        scratch_shapes=_flash_scratch(2 * MLA_V, tq),
        name="mla_attn",
        compiler_params=pltpu.CompilerParams(dimension_semantics=("parallel", "parallel", "arbitrary"),
                                             vmem_limit_bytes=VMEM_LIMIT),
    )(q, k, vt)
    return o.reshape(n, nh * MLA_V)


def kernel(x, meta, rel_bias, ln_g, ln_b, ffn_w1, ffn_w3, ffn_w2, ssd_w_in, ssd_conv_w, ssd_conv_b, ssd_dt_bias, ssd_a_log, ssd_d, ssd_norm_g, ssd_w_out, diff_w_qkv, diff_lam_q1, diff_lam_k1, diff_lam_q2, diff_lam_k2, diff_subln_g, diff_w_out, s5_lam_re, s5_lam_im, s5_log_step, s5_b_re, s5_b_im, s5_c_re, s5_c_im, s5_d, s5_w_glu, s5_b_glu, mla_w_in, mla_q_norm_g, mla_kv_norm_g, mla_w_uq, mla_w_ukv, mla_w_out):
    b, seq, d = x.shape
    l = seq + N_META
    lp = -(-l // SEQ_ALIGN) * SEQ_ALIGN
    n = b * lp
    h = jnp.concatenate([jnp.broadcast_to(meta[None].astype(x.dtype), (b, N_META, d)), x,
                         jnp.zeros((b, lp - l, d), x.dtype)], axis=1).reshape(n, d)
    w1b, w3b, w2b = ffn_w1.astype(BF16), ffn_w3.astype(BF16), ffn_w2.astype(BF16)
    for i in range(DEPTH):
        kind, j = i % 4, i // 4
        h = _ffn(h, w1b, w3b, w2b, ln_g[i, 0], ln_b[i, 0], (i, 0))
        h3 = h.reshape(b, lp, d)
        if kind == 0:
            y = _ssd_mixer(h3, ssd_w_in[j], ssd_conv_w[j], ssd_conv_b[j], ssd_dt_bias[j], ssd_a_log[j],
                           ssd_d[j], ssd_norm_g[j])
            w_out, b_out = ssd_w_out[j], None
        elif kind == 1:
            lam_init = 0.8 - 0.6 * math.exp(-0.3 * i)
            y = _diff_mixer(h3, diff_w_qkv[j], diff_lam_q1[j], diff_lam_k1[j], diff_lam_q2[j], diff_lam_k2[j],
                            diff_subln_g[j], rel_bias, lam_init)
            w_out, b_out = diff_w_out[j], None
        elif kind == 2:
            y = _s5_mixer(h3, s5_lam_re[j], s5_lam_im[j], s5_log_step[j], s5_b_re[j], s5_b_im[j], s5_c_re[j],
                          s5_c_im[j], s5_d[j])
            w_out, b_out = s5_w_glu[j], s5_b_glu[j]
        else:
            y = _mla_mixer(h3, mla_w_in[j], mla_q_norm_g[j], mla_kv_norm_g[j], mla_w_uq[j], mla_w_ukv[j])
            w_out, b_out = mla_w_out[j], None
        h = _mix_ffn(y, h, w_out, b_out, ln_g[i, 1], ln_b[i, 1], w1b, w3b, w2b, ln_g[i, 2], ln_b[i, 2], (i, 1))
    return h.reshape(b, lp, d)[:, N_META:l]
```

```python
import functools
import math

import jax
import jax.numpy as jnp
from jax import lax
from jax.experimental import pallas as pl
from jax.experimental.pallas import tpu as pltpu

F32 = jnp.float32
BF16 = jnp.bfloat16

N_META = 16
DEPTH = 4
ALPHA = (2.0 * DEPTH) ** 0.25
LN_EPS = 1e-5
RMS_EPS = 1e-6
NEG_INF = -1e30
LOG2E = math.log2(math.e)
SOFTMAX_SUM_LIMIT = 2.0 ** 100
F32_FINITE_LIMIT = 3.0e38

SSD_HEAD_DIM = 64
SSD_HEADS = 32
SSD_GROUPS = 8
SSD_STATE = 128
SSD_CONV = 4
SSD_CHUNK = 128

DIFF_HEADS = 8
DIFF_HEAD_DIM = 64
DIFF_V_DIM = 128
REL_BUCKETS = 32
REL_MAX_DIST = 128

S5_GROUP = 16
S5_GROUPS = 64
S5_STATE = 64
S5_CHUNK = 16
S5_GROUP_BLOCK = 8
SUBLANES = 8
S5_SUBLANES = SUBLANES
S5_UNROLL = 4

MLA_HEADS = 16
MLA_Q_RANK = 384
MLA_KV_RANK = 256
MLA_NOPE = 64
MLA_ROPE = 32
MLA_V = 64
ROPE_BASE = 10000.0

FFN_ROW_TILES = (512, 384, 256, 128)
FFN_CHUNKS = 11

LANES = 128
SEQ_ALIGN = 128
VMEM_LIMIT = 56 * 1024 * 1024


def _pick_tile(n, candidates):
    for c in candidates:
        if n % c == 0:
            return c
    raise ValueError(f"no tile for {n}")


def _layer_norm(z, g, b):
    mu = jnp.mean(z, axis=-1, keepdims=True)
    d = z - mu
    var = jnp.mean(d * d, axis=-1, keepdims=True)
    return d * lax.rsqrt(var + LN_EPS) * g + b


def _rms(z, g):
    return z * lax.rsqrt(jnp.mean(z * z, axis=-1, keepdims=True) + RMS_EPS) * g


def _silu(x):
    return x * (1.0 / (1.0 + jnp.exp(-x)))


def _dot(a, b):
    return jnp.dot(a, b, preferred_element_type=F32)


def _dot_nt(a, b):
    return lax.dot_general(a, b, (((1,), (1,)), ((), ())), preferred_element_type=F32)


def _row_call(body, row_ins, const_ins, outs, tm, name, pos_ins=(), lp=None, t_outs=()):
    n = row_ins[0].shape[0]
    assert n % tm == 0
    in_specs = [pl.BlockSpec((tm, a.shape[1]), lambda i: (i, 0)) for a in row_ins]
    if pos_ins or t_outs:
        assert lp % tm == 0
        nb = lp // tm
        in_specs += [pl.BlockSpec((tm, a.shape[1]), lambda i: (i % nb, 0)) for a in pos_ins]
    const_arrays = []
    for c in const_ins:
        arr, lead = c if isinstance(c, tuple) else (c, ())
        const_arrays.append(arr)
        in_specs.append(pl.BlockSpec((None,) * len(lead) + arr.shape[len(lead):],
                                     functools.partial(lambda lead_, i: lead_ + (0, 0), tuple(lead)),
                                     pipeline_mode=pl.Buffered(1)))
    out_specs = [pl.BlockSpec((tm, w), lambda i: (i, 0)) for (w, _) in outs]
    out_shape = [jax.ShapeDtypeStruct((n, w), dt) for (w, dt) in outs]
    out_specs += [pl.BlockSpec((None, None, w, tm), lambda i: (i // nb, i % nb, 0, 0)) for (w, _) in t_outs]
    out_shape += [jax.ShapeDtypeStruct((n // lp, nb, w, tm), dt) for (w, dt) in t_outs]
    res = pl.pallas_call(
        body, grid=(n // tm,), in_specs=in_specs, out_specs=out_specs, out_shape=out_shape, name=name,
        compiler_params=pltpu.CompilerParams(dimension_semantics=("parallel",), vmem_limit_bytes=VMEM_LIMIT),
    )(*row_ins, *pos_ins, *const_arrays)
    return res


def _ffn_rows(x, n_chunks, w1_ref, w3_ref, w2_ref, g_ref, b_ref):
    xb = x.astype(BF16)
    fc = w1_ref.shape[1] // n_chunks
    y = None
    for c in range(n_chunks):
        sl = slice(c * fc, (c + 1) * fc)
        gate = _dot(xb, w1_ref[:, sl])
        up = _dot(xb, w3_ref[:, sl])
        a = (_silu(gate) * up).astype(BF16)
        part = _dot(a, w2_ref[sl, :])
        y = part if y is None else y + part
    return _layer_norm(ALPHA * x + 0.5 * y, g_ref[...], b_ref[...])


def _ffn_body(n_chunks, h_ref, w1_ref, w3_ref, w2_ref, g_ref, b_ref, o_ref):
    o_ref[...] = _ffn_rows(h_ref[...], n_chunks, w1_ref, w3_ref, w2_ref, g_ref, b_ref)


def _mix_ffn_body(n_chunks, glu, y_ref, h_ref, wo_ref, bo_ref, gm_ref, bm_ref,
                  w1_ref, w3_ref, w2_ref, g_ref, b_ref, o_ref):
    m = _dot(y_ref[...].astype(BF16), wo_ref[...])
    if glu:
        m = m + bo_ref[...]
        d = o_ref.shape[1]
        m = m[:, :d] * (1.0 / (1.0 + jnp.exp(-m[:, d:])))
    x = _layer_norm(ALPHA * h_ref[...] + m, gm_ref[...], bm_ref[...])
    o_ref[...] = _ffn_rows(x, n_chunks, w1_ref, w3_ref, w2_ref, g_ref, b_ref)


def _ffn_chunks(f, n_chunks):
    return n_chunks or (FFN_CHUNKS if f % (FFN_CHUNKS * LANES) == 0 else 1)


def _ffn(h2, w1, w3, w2, g, b, idx, tm=None, n_chunks=None, name="ffn"):
    n, d = h2.shape
    tm = tm or _pick_tile(n, FFN_ROW_TILES)
    (out,) = _row_call(functools.partial(_ffn_body, _ffn_chunks(w1.shape[-1], n_chunks)), [h2],
                       [(w1, idx), (w3, idx), (w2, idx), g.reshape(1, d), b.reshape(1, d)],
                       [(d, F32)], tm, name)
    return out


def _mix_ffn(y2, h2, w_out, b_out, g_mix, b_mix, w1, w3, w2, g, b, idx, tm=None, n_chunks=None, name="mix_ffn"):
    n, d = h2.shape
    tm = tm or _pick_tile(n, FFN_ROW_TILES)
    glu = b_out is not None
    wo = w_out.astype(BF16)
    bo = (b_out if glu else jnp.zeros((wo.shape[1],), F32)).reshape(1, wo.shape[1])
    (out,) = _row_call(functools.partial(_mix_ffn_body, _ffn_chunks(w1.shape[-1], n_chunks), glu), [y2, h2],
                       [wo, bo, g_mix.reshape(1, d), b_mix.reshape(1, d),
                        (w1, idx), (w3, idx), (w2, idx), g.reshape(1, d), b.reshape(1, d)],
                       [(d, F32)], tm, name)
    return out


def _ssd_in_body(h_ref, w_ref, xbc_ref, z_ref, dt_ref):
    r = _dot(h_ref[...].astype(BF16), w_ref[...])
    nx, nz = xbc_ref.shape[1], z_ref.shape[1]
    xbc_ref[...] = r[:, :nx]
    z_ref[...] = r[:, nx:nx + nz]
    dt_ref[...] = r[:, nx + nz:]


def _softplus(x):
    return jnp.maximum(x, 0.0) + jnp.log1p(jnp.exp(-jnp.abs(x)))


def _ssd_body(xbc_ref, dt_ref, z_ref, convw_ref, convb_ref, dtb_ref, alog_ref, dskip_ref, normg_ref,
              y_ref, buf_scr, state_scr, y_scr):
    t = SSD_CHUNK
    d_inner = SSD_HEADS * SSD_HEAD_DIM
    gn = SSD_GROUPS * SSD_STATE
    hp = SSD_HEAD_DIM
    r = SSD_HEADS // SSD_GROUPS

    @pl.when(pl.program_id(1) == 0)
    def _():
        buf_scr[0:SUBLANES, :] = jnp.zeros((SUBLANES, buf_scr.shape[1]), F32)
        state_scr[...] = jnp.zeros(state_scr.shape, F32)

    x = xbc_ref[...]
    buf_scr[SUBLANES:SUBLANES + t, :] = x
    acc = convb_ref[...] + convw_ref[SSD_CONV - 1:SSD_CONV, :] * x
    for j in range(1, SSD_CONV):
        acc = acc + convw_ref[SSD_CONV - 1 - j:SSD_CONV - j, :] * buf_scr[SUBLANES - j:SUBLANES - j + t, :]
    buf_scr[0:SUBLANES, :] = x[t - SUBLANES:t, :]
    xbc = _silu(acc)

    dt = _softplus(dt_ref[...] + dtb_ref[...])
    a = dt * (-jnp.exp(alog_ref[...]))
    rows = lax.broadcasted_iota(jnp.int32, (t, t), 0)
    cols = lax.broadcasted_iota(jnp.int32, (t, t), 1)
    tri = rows >= cols
    a_cum = jnp.dot(tri.astype(F32), a, precision=lax.Precision.HIGHEST, preferred_element_type=F32)
    a_last = a_cum[t - 1:t, :]
    a_cum_t = a_cum.T
    dt_t = dt.T
    w_end_t = (dt * jnp.exp(a_last - a_cum)).T
    chunk_dec = jnp.exp(a_last)

    for g in range(SSD_GROUPS):
        b_f = xbc[:, d_inner + g * SSD_STATE:d_inner + (g + 1) * SSD_STATE]
        c_g = xbc[:, d_inner + gn + g * SSD_STATE:d_inner + gn + (g + 1) * SSD_STATE].astype(BF16)
        b_g = b_f.astype(BF16)
        b_ft = b_f.T
        cb = _dot_nt(c_g, b_g)
        y_off_g = _dot(c_g, state_scr[:, g * r * hp:(g + 1) * r * hp].astype(BF16))
        for k in range(r):
            h = g * r + k
            col = jnp.broadcast_to(a_cum[:, h:h + 1], (t, t))
            decay = jnp.exp(jnp.where(tri, col - a_cum_t[h:h + 1, :], NEG_INF))
            xh = xbc[:, h * hp:(h + 1) * hp]
            xb = xh.astype(BF16)
            y_diag = _dot((cb * decay * dt_t[h:h + 1, :]).astype(BF16), xb)
            y_off = y_off_g[:, k * hp:(k + 1) * hp] * jnp.exp(col[:, :hp])
            y_scr[:, h * hp:(h + 1) * hp] = y_diag + y_off + dskip_ref[:, h * hp:(h + 1) * hp] * xh
            st = state_scr[:, h * hp:(h + 1) * hp]
            b_w = (b_ft * w_end_t[h:h + 1, :]).astype(BF16)
            state_scr[:, h * hp:(h + 1) * hp] = st * chunk_dec[:, h:h + 1] + _dot(b_w, xb)

    y = y_scr[...] * _silu(z_ref[...])
    gs = d_inner // SSD_GROUPS
    for g in range(SSD_GROUPS):
        sl = slice(g * gs, (g + 1) * gs)
        y_ref[:, sl] = _rms(y[:, sl], normg_ref[:, sl]).astype(y_ref.dtype)


def _ssd_mixer(h3, w_in, conv_w, conv_b, dt_bias, a_log, d_skip, norm_g):
    b, lp, d = h3.shape
    n = b * lp
    d_inner = SSD_HEADS * SSD_HEAD_DIM
    conv_ch = d_inner + 2 * SSD_GROUPS * SSD_STATE
    w_z, w_xbc, w_dt = w_in[:, :d_inner], w_in[:, d_inner:d_inner + conv_ch], w_in[:, d_inner + conv_ch:]
    w_dt = jnp.pad(w_dt, ((0, 0), (0, LANES - SSD_HEADS)))
    w_all = jnp.concatenate([w_xbc, w_z, w_dt], axis=1).astype(BF16)
    tm = _pick_tile(n, (512, 384, 256, 128))
    xbc, z, dt = _row_call(_ssd_in_body, [h3.reshape(n, d)], [w_all],
                           [(conv_ch, F32), (d_inner, F32), (LANES, F32)], tm, "ssd_in")
    t = SSD_CHUNK
    pad1 = lambda v: jnp.pad(v, (0, LANES - SSD_HEADS)).reshape(1, LANES)
    consts = [conv_w, conv_b.reshape(1, conv_ch), pad1(dt_bias), pad1(a_log),
              jnp.repeat(d_skip, SSD_HEAD_DIM).reshape(1, d_inner), norm_g.reshape(1, d_inner)]
    row_spec = lambda w: pl.BlockSpec((None, t, w), lambda i, c: (i, c, 0))
    y = pl.pallas_call(
        _ssd_body, grid=(b, lp // t),
        in_specs=[row_spec(conv_ch), row_spec(LANES), row_spec(d_inner)]
        + [pl.BlockSpec(c.shape, lambda i, c_: (0, 0)) for c in consts],
        out_specs=row_spec(d_inner),
        out_shape=jax.ShapeDtypeStruct((b, lp, d_inner), BF16),
        scratch_shapes=[pltpu.VMEM((SUBLANES + t, conv_ch), F32), pltpu.VMEM((SSD_STATE, d_inner), F32),
                        pltpu.VMEM((t, d_inner), F32)],
        name="ssd_scan",
        compiler_params=pltpu.CompilerParams(dimension_semantics=("parallel", "arbitrary"),
                                             vmem_limit_bytes=VMEM_LIMIT),
    )(xbc.reshape(b, lp, conv_ch), dt.reshape(b, lp, LANES), z.reshape(b, lp, d_inner), *consts)
    return y.reshape(n, d_inner)


def _qkv_body(n_q, scale, h_ref, w_ref, qk_ref, vt_ref):
    r = _dot(h_ref[...].astype(BF16), w_ref[...])
    n_qk = qk_ref.shape[1]
    qk_ref[:, :n_q] = (r[:, :n_q] * scale).astype(qk_ref.dtype)
    qk_ref[:, n_q:] = r[:, n_q:n_qk].astype(qk_ref.dtype)
    vt_ref[...] = r[:, n_qk:].T.astype(vt_ref.dtype)


def _causal_mask_pair(s, tq):
    rows = lax.broadcasted_iota(jnp.int32, s.shape, 0)
    cols = lax.broadcasted_iota(jnp.int32, s.shape, 1)
    cols = jnp.where(cols >= tq, cols - tq, cols)
    return jnp.where(rows <= cols, s, NEG_INF)


def _flash_causal(qi, tq, scores, vt_ref, m_scr, l_scr, acc_scr, s_scrs, p_scrs, mx_scrs):
    m_scr[...] = jnp.full(m_scr.shape, NEG_INF, F32)
    l_scr[...] = jnp.zeros(l_scr.shape, F32)
    acc_scr[...] = jnp.zeros(acc_scr.shape, F32)
    p_scrs[1][...] = jnp.zeros(p_scrs[1].shape, p_scrs[1].dtype)

    def softmax(s, s_max, pv):
        m_prev = m_scr[...]
        m_new = jnp.maximum(m_prev, s_max)
        alpha = jnp.exp2(m_prev - m_new)
        p = jnp.exp2(s - m_new)
        l_scr[...] = alpha * l_scr[...] + jnp.sum(p, axis=0, keepdims=True)
        m_scr[...] = m_new
        acc_scr[...] = alpha * (acc_scr[...] + pv)
        return p.astype(BF16)

    def produce(kj, slot):
        s = scores(kj)
        s_scrs[slot][...] = s
        mx_scrs[slot][...] = jnp.max(s, axis=0, keepdims=True)

    def step(kj, cur, nxt):
        pv = _dot(vt_ref[jnp.maximum(kj - 1, 0)], p_scrs[nxt][...])
        p_scrs[cur][...] = softmax(s_scrs[cur][...], mx_scrs[cur][...], pv)
        produce(kj + 1, nxt)

    produce(0, 0)

    def body(t, carry):
        step(2 * t, 0, 1)
        step(2 * t + 1, 1, 0)
        return carry

    lax.fori_loop(0, lax.shift_right_logical(qi, 1), body, 0)
    pl.when((qi & 1) == 1)(functools.partial(step, qi - 1, 0, 1))

    def last(cur, nxt):
        pv = _dot(vt_ref[jnp.maximum(qi - 1, 0)], p_scrs[nxt][...])
        s = _causal_mask_pair(s_scrs[cur][...], tq)
        p = softmax(s, jnp.max(s, axis=0, keepdims=True), pv)
        acc_scr[...] = acc_scr[...] + _dot(vt_ref[qi], p)

    for par in range(2):
        pl.when((qi & 1) == par)(functools.partial(last, par, 1 - par))


def _flash_fixed_ref(qi, tq, scores, vt_ref, m_scr, l_scr, acc_scr, p_scrs):
    s0 = scores(0)
    m_scr[...] = jnp.max(s0, axis=0, keepdims=True)
    p0 = jnp.exp2(s0 - m_scr[...])
    l_scr[...] = jnp.sum(p0, axis=0, keepdims=True)
    p_scrs[0][...] = p0.astype(BF16)
    acc_scr[...] = jnp.zeros(acc_scr.shape, F32)

    def step(kj, cur, prv):
        p = jnp.exp2(scores(kj) - m_scr[...])
        acc_scr[...] += _dot(vt_ref[kj - 1], p_scrs[prv][...])
        l_scr[...] += jnp.sum(p, axis=0, keepdims=True)
        p_scrs[cur][...] = p.astype(BF16)

    def body(t, carry):
        step(2 * t + 1, 1, 0)
        step(2 * t + 2, 0, 1)
        return carry

    n_mid = qi - 1
    lax.fori_loop(0, lax.shift_right_logical(n_mid, 1), body, 0)
    pl.when((n_mid & 1) == 1)(functools.partial(step, qi - 1, 1, 0))

    def last(cur, prv):
        p = jnp.exp2(_causal_mask_pair(scores(qi), tq) - m_scr[...])
        l_scr[...] += jnp.sum(p, axis=0, keepdims=True)
        acc_scr[...] += _dot(vt_ref[qi - 1], p_scrs[prv][...]) + _dot(vt_ref[qi], p.astype(BF16))

    for par in range(2):
        pl.when((qi & 1) == par)(functools.partial(last, par, 1 - par))


def _flash(qi, tq, scores, vt_ref, m_scr, l_scr, acc_scr, s_scrs, p_scrs, mx_scrs, exact_ref):
    exact_ref[0] = 1

    @pl.when(qi > 0)
    def _():
        _flash_fixed_ref(qi, tq, scores, vt_ref, m_scr, l_scr, acc_scr, p_scrs)
        ok = jnp.all(l_scr[...] < SOFTMAX_SUM_LIMIT) & jnp.all(jnp.abs(acc_scr[...]) < F32_FINITE_LIMIT)
        exact_ref[0] = jnp.where(ok, 0, 1)

    @pl.when(exact_ref[0] == 1)
    def _():
        _flash_causal(qi, tq, scores, vt_ref, m_scr, l_scr, acc_scr, s_scrs, p_scrs, mx_scrs)


def _flash_scratch(e, tq):
    w = 2 * tq
    return [pltpu.VMEM((1, w), F32), pltpu.VMEM((1, w), F32), pltpu.VMEM((e, w), F32),
            pltpu.VMEM((tq, w), F32), pltpu.VMEM((tq, w), F32), pltpu.VMEM((tq, w), BF16), pltpu.VMEM((tq, w), BF16),
            pltpu.VMEM((1, w), F32), pltpu.VMEM((1, w), F32), pltpu.SMEM((1,), jnp.int32)]


def _t5_bucket(dist):
    max_exact = REL_BUCKETS // 2
    d = jnp.maximum(dist, 0)
    df = jnp.maximum(d, max_exact).astype(F32)
    large = max_exact + (jnp.log(df / max_exact) / math.log(REL_MAX_DIST / max_exact)
                         * (REL_BUCKETS - max_exact)).astype(jnp.int32)
    large = jnp.minimum(large, REL_BUCKETS - 1)
    return jnp.where(d < max_exact, d, large)


def _diff_attn_body(tq, lam_init, q_ref, k_ref, vt_ref, bias_ref, lam_ref, g_ref, o_ref,
                    m_scr, l_scr, acc_scr, s0_scr, s1_scr, p0_scr, p1_scr, mx0_scr, mx1_scr, exact_ref, qcat_scr):
    qi = pl.program_id(2)
    q = q_ref[...]
    lane = lax.broadcasted_iota(jnp.int32, q.shape, 1)
    zero = jnp.zeros_like(q)
    qcat_scr[0:tq, :] = jnp.where(lane < DIFF_HEAD_DIM, q, zero)
    qcat_scr[tq:2 * tq, :] = jnp.where(lane >= DIFF_HEAD_DIM, q, zero)

    def scores(kj):
        start = pl.multiple_of(kj * tq, tq)
        bias = bias_ref[jnp.minimum(qi - kj, 2)]
        return _dot_nt(k_ref[pl.ds(start, tq), :], qcat_scr[...]) + jnp.concatenate([bias, bias], axis=1)

    _flash(qi, tq, scores, vt_ref, m_scr, l_scr, acc_scr, (s0_scr, s1_scr), (p0_scr, p1_scr), (mx0_scr, mx1_scr),
           exact_ref)
    acc = acc_scr[...] * (1.0 / l_scr[...])
    o_t = acc[:, :tq] - lam_ref[...] * acc[:, tq:]
    o_ref[...] = (_rms(o_t.T, g_ref[...]) * (1.0 - lam_init)).astype(o_ref.dtype)


def _diff_mixer(h3, w_qkv, lam_q1, lam_k1, lam_q2, lam_k2, subln_g, rel_bias, lam_init):
    b, lp, d = h3.shape
    n = b * lp
    hd = DIFF_HEADS * 2 * DIFF_HEAD_DIM
    tq = _pick_tile(lp, (384, 256, 128))
    assert tq >= REL_MAX_DIST
    nk = lp // tq
    qk, vt = _row_call(functools.partial(_qkv_body, hd, DIFF_HEAD_DIM ** -0.5 * LOG2E), [h3.reshape(n, d)],
                       [w_qkv.astype(BF16)], [(2 * hd, BF16)], tq, "diff_qkv", lp=lp, t_outs=[(hd, BF16)])
    qk = qk.reshape(b, lp, 2 * hd)
    m2 = 2 * tq
    kk = jnp.arange(m2)
    dist = jnp.where(kk < tq, kk, kk - m2)[None, :] + (jnp.arange(3) * tq)[:, None]
    tab = jnp.transpose(rel_bias[_t5_bucket(dist)].astype(F32) * LOG2E, (2, 0, 1))
    nh = DIFF_HEADS
    bias = jnp.tile(tab, (1, 1, tq))[..., :tq * (m2 - 1)].reshape(nh, 3, tq, m2 - 1)[..., :tq]
    lam = (jnp.exp(jnp.sum(lam_q1.astype(F32) * lam_k1.astype(F32)))
           - jnp.exp(jnp.sum(lam_q2.astype(F32) * lam_k2.astype(F32))) + lam_init)
    w = 2 * DIFF_HEAD_DIM
    o = pl.pallas_call(
        functools.partial(_diff_attn_body, tq, lam_init),
        grid=(b, nh, nk),
        in_specs=[pl.BlockSpec((None, tq, w), lambda bi, hi, qi: (bi, qi, hi)),
                  pl.BlockSpec((None, lp, w), lambda bi, hi, qi: (bi, 0, nh + hi)),
                  pl.BlockSpec((None, nk, DIFF_V_DIM, tq), lambda bi, hi, qi: (bi, 0, hi, 0)),
                  pl.BlockSpec((None, 3, tq, tq), lambda bi, hi, qi: (hi, 0, 0, 0)),
                  pl.BlockSpec((1, 1), lambda bi, hi, qi: (0, 0)),
                  pl.BlockSpec((1, DIFF_V_DIM), lambda bi, hi, qi: (0, 0))],
        out_specs=pl.BlockSpec((None, tq, DIFF_V_DIM), lambda bi, hi, qi: (bi, qi, hi)),
        out_shape=jax.ShapeDtypeStruct((b, lp, nh * DIFF_V_DIM), BF16),
        scratch_shapes=_flash_scratch(DIFF_V_DIM, tq) + [pltpu.VMEM((2 * tq, w), BF16)],
        name="diff_attn",
        compiler_params=pltpu.CompilerParams(dimension_semantics=("parallel", "parallel", "arbitrary"),
                                             vmem_limit_bytes=VMEM_LIMIT),
    )(qk, qk, vt, bias, lam.reshape(1, 1), subln_g.reshape(1, DIFF_V_DIM))
    return o.reshape(n, nh * DIFF_V_DIM)


def _gelu_tanh(x):
    return 0.5 * x * (1.0 + jnp.tanh(math.sqrt(2.0 / math.pi) * (x + 0.044715 * (x * x * x))))


def _s5_body(bp, x_ref, m_ref, bc_ref, ccre_ref, ccim_ref, are_ref, aim_ref, y_ref, sre_scr, sim_scr):
    rows = x_ref.shape[0]
    gb = m_ref.shape[0]
    kw = m_ref.shape[1]
    p = ccre_ref.shape[1]
    for g in range(gb):
        e = _dot(x_ref[:, g * kw:(g + 1) * kw], bc_ref[g])
        sre_scr[:, g * p:(g + 1) * p] = e[:, :p]
        sim_scr[:, g * p:(g + 1) * p] = e[:, p:]

    ar, ai = are_ref[...], aim_ref[...]
    sub = lax.broadcasted_iota(jnp.int32, ar.shape, 0)
    per_tile = S5_SUBLANES // bp

    def tile_step(r, carry):
        pr, pi = carry
        r0 = pl.multiple_of(r * S5_SUBLANES, S5_SUBLANES)
        er, ei = sre_scr[pl.ds(r0, S5_SUBLANES), :], sim_scr[pl.ds(r0, S5_SUBLANES), :]
        in_r, in_i = pr, pi
        for c in range(per_tile):
            nr = ar * pr - ai * pi + er
            ni = ar * pi + ai * pr + ei
            if per_tile > 1:
                nr = pltpu.roll(nr, bp, axis=0)
                ni = pltpu.roll(ni, bp, axis=0)
            if c + 1 < per_tile:
                sel = (sub >= (c + 1) * bp) & (sub < (c + 2) * bp)
                in_r, in_i = jnp.where(sel, nr, in_r), jnp.where(sel, ni, in_i)
                pr, pi = nr, ni
            else:
                pr, pi = nr, ni
        if per_tile > 1:
            pr = jnp.where(sub < bp, pr, pltpu.roll(pr, bp, axis=0))
            pi = jnp.where(sub < bp, pi, pltpu.roll(pi, bp, axis=0))
        sre_scr[pl.ds(r0, S5_SUBLANES), :] = in_r
        sim_scr[pl.ds(r0, S5_SUBLANES), :] = in_i
        return pr, pi

    zero = jnp.zeros(ar.shape, F32)
    lax.fori_loop(0, rows // S5_SUBLANES, tile_step, (zero, zero), unroll=S5_UNROLL)

    for g in range(gb):
        xg = x_ref[:, g * kw:(g + 1) * kw]
        y = (_dot(xg, m_ref[g]) + _dot(sre_scr[:, g * p:(g + 1) * p].astype(BF16), ccre_ref[g])
             + _dot(sim_scr[:, g * p:(g + 1) * p].astype(BF16), ccim_ref[g]))
        y_ref[:, g * kw:(g + 1) * kw] = _gelu_tanh(y).astype(y_ref.dtype)


def _s5_pack_body(n_batch, h_ref, perm_ref, x_ref):
    t_len = h_ref.shape[1]
    xcat = jnp.concatenate([h_ref[:, t, :].astype(BF16) for t in range(t_len)], axis=1)
    x = _dot(xcat, perm_ref[...]).astype(x_ref.dtype)
    x_ref[...] = jnp.where(pl.program_id(0) < n_batch, x, jnp.zeros_like(x))


def _s5_unpack_body(y_ref, perm_ref, o_ref):
    t_len = o_ref.shape[1]
    res = _dot(y_ref[...], perm_ref[...])
    for t in range(t_len):
        o_ref[:, t, :] = res[:, t * LANES:(t + 1) * LANES]


def _s5_chunk_operators(lam_re, lam_im, log_step, b_re, b_im, c_re, c_im, d_skip):
    hi = lax.Precision.HIGHEST
    t_len = S5_CHUNK
    step = jnp.exp(log_step.astype(F32))[:, None]
    lr = jnp.minimum(lam_re.astype(F32), -1e-4)
    li = lam_im.astype(F32)
    mag = jnp.exp(lr * step)
    ar, ai = mag * jnp.cos(li * step), mag * jnp.sin(li * step)
    den = lr * lr + li * li
    cr = ((ar - 1.0) * lr + ai * li) / den
    ci = (ai * lr - (ar - 1.0) * li) / den
    br, bi = b_re.astype(F32), b_im.astype(F32)
    bbr = cr[..., None] * br - ci[..., None] * bi
    bbi = cr[..., None] * bi + ci[..., None] * br
    k = jnp.arange(t_len + 1, dtype=F32)[:, None, None]
    pmag = jnp.exp(k * (lr * step))
    pr, pi = pmag * jnp.cos(k * (li * step)), pmag * jnp.sin(k * (li * step))
    abr = pr[:t_len, ..., None] * bbr - pi[:t_len, ..., None] * bbi
    abi = pr[:t_len, ..., None] * bbi + pi[:t_len, ..., None] * bbr
    cre, cim = c_re.astype(F32), c_im.astype(F32)
    kern = (jnp.einsum('gop,kgpc->kgoc', cre, abr, precision=hi)
            - jnp.einsum('gop,kgpc->kgoc', cim, abi, precision=hi))
    kern = kern.at[0].add(jax.vmap(jnp.diag)(d_skip.astype(F32)))
    sig = jnp.arange(t_len)
    lag = sig[None, :] - sig[:, None]
    m = jnp.where((lag >= 0)[..., None, None, None], kern[jnp.maximum(lag, 0)], 0.0)
    g, o, c = kern.shape[1:]
    m = jnp.transpose(m, (2, 0, 4, 1, 3)).reshape(g, t_len * c, t_len * o)
    rev = t_len - 1 - sig
    p = bbr.shape[1]
    bc = jnp.concatenate([jnp.transpose(abr[rev], (1, 0, 3, 2)).reshape(g, t_len * c, p),
                          jnp.transpose(abi[rev], (1, 0, 3, 2)).reshape(g, t_len * c, p)], axis=2)
    car = cre[None] * pr[1:, :, None, :] - cim[None] * pi[1:, :, None, :]
    cai = cre[None] * pi[1:, :, None, :] + cim[None] * pr[1:, :, None, :]
    cc_re = jnp.transpose(car, (1, 3, 0, 2)).reshape(g, p, t_len * o)
    cc_im = -jnp.transpose(cai, (1, 3, 0, 2)).reshape(g, p, t_len * o)
    return (m.astype(BF16), bc.astype(BF16), cc_re.astype(BF16), cc_im.astype(BF16),
            pr[t_len].reshape(g * p), pi[t_len].reshape(g * p))


def _s5_mixer(h3, lam_re, lam_im, log_step, b_re, b_im, c_re, c_im, d_skip):
    b, lp, d = h3.shape
    t_len, g, c = S5_CHUNK, S5_GROUPS, S5_GROUP
    assert lp % t_len == 0 and b <= S5_SUBLANES
    bp = 4 if b <= 4 else S5_SUBLANES
    nj = lp // t_len
    m, bc, cc_re, cc_im, at_re, at_im = _s5_chunk_operators(lam_re, lam_im, log_step, b_re, b_im, c_re, c_im, d_skip)
    n_state = g * S5_STATE
    at_re = jnp.broadcast_to(at_re.reshape(1, n_state), (S5_SUBLANES, n_state))
    at_im = jnp.broadcast_to(at_im.reshape(1, n_state), (S5_SUBLANES, n_state))
    rows = nj * bp
    gb = S5_GROUP_BLOCK
    kw = t_len * c
    assert gb * c == LANES
    n_blk = g // gb
    src = jnp.arange(gb * kw)
    tau, r, ch = src // (gb * c), (src // c) % gb, src % c
    perm = jax.nn.one_hot(r * kw + tau * c + ch, gb * kw, dtype=BF16)
    x = pl.pallas_call(
        functools.partial(_s5_pack_body, b),
        grid=(bp, n_blk),
        in_specs=[pl.BlockSpec((None, nj, t_len, LANES), lambda bi, mi: (jnp.minimum(bi, b - 1), 0, 0, mi)),
                  pl.BlockSpec(perm.shape, lambda bi, mi: (0, 0))],
        out_specs=pl.BlockSpec((nj, gb * kw), lambda bi, mi: (0, bi * n_blk + mi)),
        out_shape=jax.ShapeDtypeStruct((nj, bp * g * kw), BF16),
        name="s5_pack",
        compiler_params=pltpu.CompilerParams(dimension_semantics=("parallel", "parallel"),
                                             vmem_limit_bytes=VMEM_LIMIT),
    )(h3.reshape(b, nj, t_len, d), perm).reshape(rows, g * kw)
    y = pl.pallas_call(
        functools.partial(_s5_body, bp), grid=(g // gb,),
        in_specs=[pl.BlockSpec((rows, gb * kw), lambda i: (0, i)),
                  pl.BlockSpec((gb, kw, kw), lambda i: (i, 0, 0)),
                  pl.BlockSpec((gb, kw, 2 * S5_STATE), lambda i: (i, 0, 0)),
                  pl.BlockSpec((gb, S5_STATE, kw), lambda i: (i, 0, 0)),
                  pl.BlockSpec((gb, S5_STATE, kw), lambda i: (i, 0, 0)),
                  pl.BlockSpec((S5_SUBLANES, gb * S5_STATE), lambda i: (0, i)),
                  pl.BlockSpec((S5_SUBLANES, gb * S5_STATE), lambda i: (0, i))],
        out_specs=pl.BlockSpec((rows, gb * kw), lambda i: (0, i)),
        out_shape=jax.ShapeDtypeStruct((rows, g * kw), BF16),
        scratch_shapes=[pltpu.VMEM((rows, gb * S5_STATE), F32), pltpu.VMEM((rows, gb * S5_STATE), F32)],
        name="s5_chunks",
        compiler_params=pltpu.CompilerParams(dimension_semantics=("parallel",), vmem_limit_bytes=VMEM_LIMIT),
    )(x, m, bc, cc_re, cc_im, at_re, at_im)
    y = pl.pallas_call(
        _s5_unpack_body,
        grid=(b, n_blk),
        in_specs=[pl.BlockSpec((nj, gb * kw), lambda bi, mi: (0, bi * n_blk + mi)),
                  pl.BlockSpec(perm.shape, lambda bi, mi: (0, 0))],
        out_specs=pl.BlockSpec((None, nj, t_len, LANES), lambda bi, mi: (bi, 0, 0, mi)),
        out_shape=jax.ShapeDtypeStruct((b, nj, t_len, d), F32),
        name="s5_unpack",
        compiler_params=pltpu.CompilerParams(dimension_semantics=("parallel", "parallel"),
                                             vmem_limit_bytes=VMEM_LIMIT),
    )(y.reshape(nj, bp * g * kw), perm.T)
    return y.reshape(b * lp, d)


def _rope_block(x, cm, s1, s2):
    return (x * cm + pltpu.roll(x, LANES - MLA_ROPE // 2, axis=1) * s1 + pltpu.roll(x, MLA_ROPE // 2, axis=1) * s2)


def _mla_in_body(h_ref, qcm_ref, qs1_ref, qs2_ref, kcm_ref, ks1_ref, ks2_ref,
                 win_ref, qg_ref, kvg_ref, wuq_ref, wuk_ref, wuv_ref, q_ref, k_ref, vt_ref):
    c = _dot(h_ref[...].astype(BF16), win_ref[...])
    c_q = _rms(c[:, :MLA_Q_RANK], qg_ref[...]).astype(BF16)
    c_kv = _rms(c[:, MLA_Q_RANK:MLA_Q_RANK + MLA_KV_RANK], kvg_ref[...]).astype(BF16)
    k_r = _rope_block(c[:, MLA_Q_RANK + MLA_KV_RANK:], kcm_ref[...], ks1_ref[...], ks2_ref[...])
    q = _dot(c_q, wuq_ref[...])
    k = _dot(c_kv, wuk_ref[...])
    qcm, qs1, qs2 = qcm_ref[...], qs1_ref[...], qs2_ref[...]
    for h in range(MLA_HEADS):
        sl = slice(h * LANES, (h + 1) * LANES)
        q_ref[:, sl] = _rope_block(q[:, sl], qcm, qs1, qs2).astype(q_ref.dtype)
        k_ref[:, sl] = (k[:, sl] + k_r).astype(k_ref.dtype)
    vt_ref[...] = _dot(c_kv, wuv_ref[...]).T.astype(vt_ref.dtype)


def _mla_attn_body(tq, q_ref, k_ref, vt_ref, o_ref,
                   m_scr, l_scr, acc_scr, s0_scr, s1_scr, p0_scr, p1_scr, mx0_scr, mx1_scr, exact_ref):
    qi = pl.program_id(2)

    def scores(kj):
        start = pl.multiple_of(kj * tq, tq)
        return jnp.concatenate(
            [_dot_nt(k_ref[pl.ds(start, tq), hh * LANES:(hh + 1) * LANES], q_ref[:, hh * LANES:(hh + 1) * LANES])
             for hh in range(2)], axis=1)

    _flash(qi, tq, scores, vt_ref, m_scr, l_scr, acc_scr, (s0_scr, s1_scr), (p0_scr, p1_scr), (mx0_scr, mx1_scr),
           exact_ref)
    acc = acc_scr[...] * (1.0 / l_scr[...])
    row = lax.broadcasted_iota(jnp.int32, (2 * MLA_V, tq), 0)
    o_t = jnp.where(row < MLA_V, acc[:, :tq], acc[:, tq:])
    o_ref[...] = o_t.T.astype(o_ref.dtype)


def _mla_mixer(h3, w_in, q_norm_g, kv_norm_g, w_uq, w_ukv):
    b, lp, d = h3.shape
    n = b * lp
    nh = MLA_HEADS
    qk = MLA_NOPE + MLA_ROPE
    half = MLA_ROPE // 2
    w_kr = jnp.pad(w_in[:, MLA_Q_RANK + MLA_KV_RANK:], ((0, 0), (MLA_NOPE, LANES - qk)))
    w_in_all = jnp.concatenate([w_in[:, :MLA_Q_RANK + MLA_KV_RANK], w_kr], axis=1).astype(BF16)
    w_uq_p = jnp.pad(w_uq.reshape(MLA_Q_RANK, nh, qk), ((0, 0), (0, 0), (0, LANES - qk)))
    w_uq_p = w_uq_p.reshape(MLA_Q_RANK, nh * LANES).astype(BF16)
    w_ukv3 = w_ukv.reshape(MLA_KV_RANK, nh, MLA_NOPE + MLA_V)
    w_uk_p = jnp.pad(w_ukv3[:, :, :MLA_NOPE], ((0, 0), (0, 0), (0, LANES - MLA_NOPE)))
    w_uk_p = w_uk_p.reshape(MLA_KV_RANK, nh * LANES).astype(BF16)
    w_uv = w_ukv3[:, :, MLA_NOPE:].reshape(MLA_KV_RANK, nh * MLA_V).astype(BF16)
    pos = jnp.arange(lp, dtype=F32)
    inv_freq = ROPE_BASE ** (-jnp.arange(0, MLA_ROPE, 2, dtype=F32) / MLA_ROPE)
    ang = pos[:, None] * inv_freq[None, :]
    cos, sin = jnp.cos(ang), jnp.sin(ang)
    z = lambda w: jnp.zeros((lp, w), F32)
    cm = jnp.concatenate([jnp.ones((lp, MLA_NOPE), F32), cos, cos, z(LANES - qk)], axis=1)
    s1 = jnp.concatenate([z(MLA_NOPE), -sin, z(LANES - MLA_NOPE - half)], axis=1)
    s2 = jnp.concatenate([z(MLA_NOPE + half), sin, z(LANES - qk)], axis=1)
    scale = qk ** -0.5 * LOG2E
    tq = _pick_tile(lp, (384, 256, 128))
    nk = lp // tq
    q, k, vt = _row_call(
        _mla_in_body, [h3.reshape(n, d)],
        [w_in_all, q_norm_g.reshape(1, MLA_Q_RANK), kv_norm_g.reshape(1, MLA_KV_RANK), w_uq_p, w_uk_p, w_uv],
        [(nh * LANES, BF16), (nh * LANES, BF16)], tq, "mla_in",
        pos_ins=[cm * scale, s1 * scale, s2 * scale, cm, s1, s2], lp=lp, t_outs=[(nh * MLA_V, BF16)])
    q, k = q.reshape(b, lp, nh * LANES), k.reshape(b, lp, nh * LANES)
    o = pl.pallas_call(
        functools.partial(_mla_attn_body, tq),
        grid=(b, nh // 2, nk),
        in_specs=[pl.BlockSpec((None, tq, 2 * LANES), lambda bi, hi, qi: (bi, qi, hi)),
                  pl.BlockSpec((None, lp, 2 * LANES), lambda bi, hi, qi: (bi, 0, hi)),
                  pl.BlockSpec((None, nk, 2 * MLA_V, tq), lambda bi, hi, qi: (bi, 0, hi, 0))],
        out_specs=pl.BlockSpec((None, tq, 2 * MLA_V), lambda bi, hi, qi: (bi, qi, hi)),
        out_shape=jax.ShapeDtypeStruct((b, lp, nh * MLA_V), BF16),
        scratch_shapes=_flash_scratch(2 * MLA_V, tq),
        name="mla_attn",
        compiler_params=pltpu.CompilerParams(dimension_semantics=("parallel", "parallel", "arbitrary"),
                                             vmem_limit_bytes=VMEM_LIMIT),
    )(q, k, vt)
    return o.reshape(n, nh * MLA_V)


def kernel(x, meta, rel_bias, ln_g, ln_b, ffn_w1, ffn_w3, ffn_w2, ssd_w_in, ssd_conv_w, ssd_conv_b, ssd_dt_bias, ssd_a_log, ssd_d, ssd_norm_g, ssd_w_out, diff_w_qkv, diff_lam_q1, diff_lam_k1, diff_lam_q2, diff_lam_k2, diff_subln_g, diff_w_out, s5_lam_re, s5_lam_im, s5_log_step, s5_b_re, s5_b_im, s5_c_re, s5_c_im, s5_d, s5_w_glu, s5_b_glu, mla_w_in, mla_q_norm_g, mla_kv_norm_g, mla_w_uq, mla_w_ukv, mla_w_out):
    b, seq, d = x.shape
    l = seq + N_META
    lp = -(-l // SEQ_ALIGN) * SEQ_ALIGN
    n = b * lp
    h = jnp.concatenate([jnp.broadcast_to(meta[None].astype(x.dtype), (b, N_META, d)), x,
                         jnp.zeros((b, lp - l, d), x.dtype)], axis=1).reshape(n, d)
    w1b, w3b, w2b = ffn_w1.astype(BF16), ffn_w3.astype(BF16), ffn_w2.astype(BF16)
    for i in range(DEPTH):
        kind, j = i % 4, i // 4
        h = _ffn(h, w1b, w3b, w2b, ln_g[i, 0], ln_b[i, 0], (i, 0))
        h3 = h.reshape(b, lp, d)
        if kind == 0:
            y = _ssd_mixer(h3, ssd_w_in[j], ssd_conv_w[j], ssd_conv_b[j], ssd_dt_bias[j], ssd_a_log[j],
                           ssd_d[j], ssd_norm_g[j])
            w_out, b_out = ssd_w_out[j], None
        elif kind == 1:
            lam_init = 0.8 - 0.6 * math.exp(-0.3 * i)
            y = _diff_mixer(h3, diff_w_qkv[j], diff_lam_q1[j], diff_lam_k1[j], diff_lam_q2[j], diff_lam_k2[j],
                            diff_subln_g[j], rel_bias, lam_init)
            w_out, b_out = diff_w_out[j], None
        elif kind == 2:
            y = _s5_mixer(h3, s5_lam_re[j], s5_lam_im[j], s5_log_step[j], s5_b_re[j], s5_b_im[j], s5_c_re[j],
                          s5_c_im[j], s5_d[j])
            w_out, b_out = s5_w_glu[j], s5_b_glu[j]
        else:
            y = _mla_mixer(h3, mla_w_in[j], mla_q_norm_g[j], mla_kv_norm_g[j], mla_w_uq[j], mla_w_ukv[j])
            w_out, b_out = mla_w_out[j], None
        h = _mix_ffn(y, h, w_out, b_out, ln_g[i, 1], ln_b[i, 1], w1b, w3b, w2b, ln_g[i, 2], ln_b[i, 2], (i, 1))
    return h.reshape(b, lp, d)[:, N_META:l]
```

```python
import functools
import math

import jax
import jax.numpy as jnp
from jax import lax
from jax.experimental import pallas as pl
from jax.experimental.pallas import tpu as pltpu

F32 = jnp.float32
BF16 = jnp.bfloat16

N_META = 16
DEPTH = 4
ALPHA = (2.0 * DEPTH) ** 0.25
LN_EPS = 1e-5
RMS_EPS = 1e-6
NEG_INF = -1e30
LOG2E = math.log2(math.e)
SOFTMAX_SUM_LIMIT = 2.0 ** 100
F32_FINITE_LIMIT = 3.0e38

SSD_HEAD_DIM = 64
SSD_HEADS = 32
SSD_GROUPS = 8
SSD_STATE = 128
SSD_CONV = 4
SSD_CHUNK = 128

DIFF_HEADS = 8
DIFF_HEAD_DIM = 64
DIFF_V_DIM = 128
REL_BUCKETS = 32
REL_MAX_DIST = 128

S5_GROUP = 16
S5_GROUPS = 64
S5_STATE = 64
S5_CHUNK = 16
S5_GROUP_BLOCK = 8
SUBLANES = 8
S5_SUBLANES = SUBLANES
S5_UNROLL = 4

MLA_HEADS = 16
MLA_Q_RANK = 384
MLA_KV_RANK = 256
MLA_NOPE = 64
MLA_ROPE = 32
MLA_V = 64
ROPE_BASE = 10000.0

FFN_ROW_TILES = (512, 384, 256, 128)
FFN_CHUNKS = 11

LANES = 128
SEQ_ALIGN = 128
VMEM_LIMIT = 56 * 1024 * 1024


def _pick_tile(n, candidates):
    for c in candidates:
        if n % c == 0:
            return c
    raise ValueError(f"no tile for {n}")


def _layer_norm(z, g, b):
    mu = jnp.mean(z, axis=-1, keepdims=True)
    d = z - mu
    var = jnp.mean(d * d, axis=-1, keepdims=True)
    return d * lax.rsqrt(var + LN_EPS) * g + b


def _rms(z, g):
    return z * lax.rsqrt(jnp.mean(z * z, axis=-1, keepdims=True) + RMS_EPS) * g


def _silu(x):
    return x * (1.0 / (1.0 + jnp.exp(-x)))


def _dot(a, b):
    return jnp.dot(a, b, preferred_element_type=F32)


def _dot_nt(a, b):
    return lax.dot_general(a, b, (((1,), (1,)), ((), ())), preferred_element_type=F32)


def _row_call(body, row_ins, const_ins, outs, tm, name, pos_ins=(), lp=None, t_outs=()):
    n = row_ins[0].shape[0]
    assert n % tm == 0
    in_specs = [pl.BlockSpec((tm, a.shape[1]), lambda i: (i, 0)) for a in row_ins]
    if pos_ins or t_outs:
        assert lp % tm == 0
        nb = lp // tm
        in_specs += [pl.BlockSpec((tm, a.shape[1]), lambda i: (i % nb, 0)) for a in pos_ins]
    const_arrays = []
    for c in const_ins:
        arr, lead = c if isinstance(c, tuple) else (c, ())
        const_arrays.append(arr)
        in_specs.append(pl.BlockSpec((None,) * len(lead) + arr.shape[len(lead):],
                                     functools.partial(lambda lead_, i: lead_ + (0, 0), tuple(lead)),
                                     pipeline_mode=pl.Buffered(1)))
    out_specs = [pl.BlockSpec((tm, w), lambda i: (i, 0)) for (w, _) in outs]
    out_shape = [jax.ShapeDtypeStruct((n, w), dt) for (w, dt) in outs]
    out_specs += [pl.BlockSpec((None, None, w, tm), lambda i: (i // nb, i % nb, 0, 0)) for (w, _) in t_outs]
    out_shape += [jax.ShapeDtypeStruct((n // lp, nb, w, tm), dt) for (w, dt) in t_outs]
    res = pl.pallas_call(
        body, grid=(n // tm,), in_specs=in_specs, out_specs=out_specs, out_shape=out_shape, name=name,
        compiler_params=pltpu.CompilerParams(dimension_semantics=("parallel",), vmem_limit_bytes=VMEM_LIMIT),
    )(*row_ins, *pos_ins, *const_arrays)
    return res


def _ffn_rows(x, n_chunks, w1_ref, w3_ref, w2_ref, g_ref, b_ref):
    xb = x.astype(BF16)
    fc = w1_ref.shape[1] // n_chunks
    y = None
    for c in range(n_chunks):
        sl = slice(c * fc, (c + 1) * fc)
        gate = _dot(xb, w1_ref[:, sl])
        up = _dot(xb, w3_ref[:, sl])
        a = (_silu(gate) * up).astype(BF16)
        part = _dot(a, w2_ref[sl, :])
        y = part if y is None else y + part
    return _layer_norm(ALPHA * x + 0.5 * y, g_ref[...], b_ref[...])


def _ffn_body(n_chunks, h_ref, w1_ref, w3_ref, w2_ref, g_ref, b_ref, o_ref):
    o_ref[...] = _ffn_rows(h_ref[...], n_chunks, w1_ref, w3_ref, w2_ref, g_ref, b_ref)


def _mix_ffn_body(n_chunks, glu, y_ref, h_ref, wo_ref, bo_ref, gm_ref, bm_ref,
                  w1_ref, w3_ref, w2_ref, g_ref, b_ref, o_ref):
    m = _dot(y_ref[...].astype(BF16), wo_ref[...])
    if glu:
        m = m + bo_ref[...]
        d = o_ref.shape[1]
        m = m[:, :d] * (1.0 / (1.0 + jnp.exp(-m[:, d:])))
    x = _layer_norm(ALPHA * h_ref[...] + m, gm_ref[...], bm_ref[...])
    o_ref[...] = _ffn_rows(x, n_chunks, w1_ref, w3_ref, w2_ref, g_ref, b_ref)


def _ffn_chunks(f, n_chunks):
    return n_chunks or (FFN_CHUNKS if f % (FFN_CHUNKS * LANES) == 0 else 1)


def _ffn(h2, w1, w3, w2, g, b, idx, tm=None, n_chunks=None, name="ffn"):
    n, d = h2.shape
    tm = tm or _pick_tile(n, FFN_ROW_TILES)
    (out,) = _row_call(functools.partial(_ffn_body, _ffn_chunks(w1.shape[-1], n_chunks)), [h2],
                       [(w1, idx), (w3, idx), (w2, idx), g.reshape(1, d), b.reshape(1, d)],
                       [(d, F32)], tm, name)
    return out


def _mix_ffn(y2, h2, w_out, b_out, g_mix, b_mix, w1, w3, w2, g, b, idx, tm=None, n_chunks=None, name="mix_ffn"):
    n, d = h2.shape
    tm = tm or _pick_tile(n, FFN_ROW_TILES)
    glu = b_out is not None
    wo = w_out.astype(BF16)
    bo = (b_out if glu else jnp.zeros((wo.shape[1],), F32)).reshape(1, wo.shape[1])
    (out,) = _row_call(functools.partial(_mix_ffn_body, _ffn_chunks(w1.shape[-1], n_chunks), glu), [y2, h2],
                       [wo, bo, g_mix.reshape(1, d), b_mix.reshape(1, d),
                        (w1, idx), (w3, idx), (w2, idx), g.reshape(1, d), b.reshape(1, d)],
                       [(d, F32)], tm, name)
    return out


def _ssd_in_body(h_ref, w_ref, xbc_ref, z_ref, dt_ref):
    r = _dot(h_ref[...].astype(BF16), w_ref[...])
    nx, nz = xbc_ref.shape[1], z_ref.shape[1]
    xbc_ref[...] = r[:, :nx]
    z_ref[...] = r[:, nx:nx + nz]
    dt_ref[...] = r[:, nx + nz:]


def _softplus(x):
    return jnp.maximum(x, 0.0) + jnp.log1p(jnp.exp(-jnp.abs(x)))


def _ssd_body(xbc_ref, dt_ref, z_ref, convw_ref, convb_ref, dtb_ref, alog_ref, dskip_ref, normg_ref,
              y_ref, buf_scr, state_scr, y_scr):
    t = SSD_CHUNK
    d_inner = SSD_HEADS * SSD_HEAD_DIM
    gn = SSD_GROUPS * SSD_STATE
    hp = SSD_HEAD_DIM
    r = SSD_HEADS // SSD_GROUPS

    @pl.when(pl.program_id(1) == 0)
    def _():
        buf_scr[0:SUBLANES, :] = jnp.zeros((SUBLANES, buf_scr.shape[1]), F32)
        state_scr[...] = jnp.zeros(state_scr.shape, F32)

    x = xbc_ref[...]
    buf_scr[SUBLANES:SUBLANES + t, :] = x
    acc = convb_ref[...] + convw_ref[SSD_CONV - 1:SSD_CONV, :] * x
    for j in range(1, SSD_CONV):
        acc = acc + convw_ref[SSD_CONV - 1 - j:SSD_CONV - j, :] * buf_scr[SUBLANES - j:SUBLANES - j + t, :]
    buf_scr[0:SUBLANES, :] = x[t - SUBLANES:t, :]
    xbc = _silu(acc)

    dt = _softplus(dt_ref[...] + dtb_ref[...])
    a = dt * (-jnp.exp(alog_ref[...]))
    rows = lax.broadcasted_iota(jnp.int32, (t, t), 0)
    cols = lax.broadcasted_iota(jnp.int32, (t, t), 1)
    tri = rows >= cols
    a_cum = jnp.dot(tri.astype(F32), a, precision=lax.Precision.HIGHEST, preferred_element_type=F32)
    a_last = a_cum[t - 1:t, :]
    a_cum_t = a_cum.T
    dt_t = dt.T
    w_end_t = (dt * jnp.exp(a_last - a_cum)).T
    chunk_dec = jnp.exp(a_last)

    for g in range(SSD_GROUPS):
        b_f = xbc[:, d_inner + g * SSD_STATE:d_inner + (g + 1) * SSD_STATE]
        c_g = xbc[:, d_inner + gn + g * SSD_STATE:d_inner + gn + (g + 1) * SSD_STATE].astype(BF16)
        b_g = b_f.astype(BF16)
        b_ft = b_f.T
        cb = _dot_nt(c_g, b_g)
        y_off_g = _dot(c_g, state_scr[:, g * r * hp:(g + 1) * r * hp].astype(BF16))
        for k in range(r):
            h = g * r + k
            col = jnp.broadcast_to(a_cum[:, h:h + 1], (t, t))
            decay = jnp.exp(jnp.where(tri, col - a_cum_t[h:h + 1, :], NEG_INF))
            xh = xbc[:, h * hp:(h + 1) * hp]
            xb = xh.astype(BF16)
            y_diag = _dot((cb * decay * dt_t[h:h + 1, :]).astype(BF16), xb)
            y_off = y_off_g[:, k * hp:(k + 1) * hp] * jnp.exp(col[:, :hp])
            y_scr[:, h * hp:(h + 1) * hp] = y_diag + y_off + dskip_ref[:, h * hp:(h + 1) * hp] * xh
            st = state_scr[:, h * hp:(h + 1) * hp]
            b_w = (b_ft * w_end_t[h:h + 1, :]).astype(BF16)
            state_scr[:, h * hp:(h + 1) * hp] = st * chunk_dec[:, h:h + 1] + _dot(b_w, xb)

    y = y_scr[...] * _silu(z_ref[...])
    gs = d_inner // SSD_GROUPS
    for g in range(SSD_GROUPS):
        sl = slice(g * gs, (g + 1) * gs)
        y_ref[:, sl] = _rms(y[:, sl], normg_ref[:, sl]).astype(y_ref.dtype)


def _ssd_mixer(h3, w_in, conv_w, conv_b, dt_bias, a_log, d_skip, norm_g):
    b, lp, d = h3.shape
    n = b * lp
    d_inner = SSD_HEADS * SSD_HEAD_DIM
    conv_ch = d_inner + 2 * SSD_GROUPS * SSD_STATE
    w_z, w_xbc, w_dt = w_in[:, :d_inner], w_in[:, d_inner:d_inner + conv_ch], w_in[:, d_inner + conv_ch:]
    w_dt = jnp.pad(w_dt, ((0, 0), (0, LANES - SSD_HEADS)))
    w_all = jnp.concatenate([w_xbc, w_z, w_dt], axis=1).astype(BF16)
    tm = _pick_tile(n, (512, 384, 256, 128))
    xbc, z, dt = _row_call(_ssd_in_body, [h3.reshape(n, d)], [w_all],
                           [(conv_ch, F32), (d_inner, F32), (LANES, F32)], tm, "ssd_in")
    t = SSD_CHUNK
    pad1 = lambda v: jnp.pad(v, (0, LANES - SSD_HEADS)).reshape(1, LANES)
    consts = [conv_w, conv_b.reshape(1, conv_ch), pad1(dt_bias), pad1(a_log),
              jnp.repeat(d_skip, SSD_HEAD_DIM).reshape(1, d_inner), norm_g.reshape(1, d_inner)]
    row_spec = lambda w: pl.BlockSpec((None, t, w), lambda i, c: (i, c, 0))
    y = pl.pallas_call(
        _ssd_body, grid=(b, lp // t),
        in_specs=[row_spec(conv_ch), row_spec(LANES), row_spec(d_inner)]
        + [pl.BlockSpec(c.shape, lambda i, c_: (0, 0)) for c in consts],
        out_specs=row_spec(d_inner),
        out_shape=jax.ShapeDtypeStruct((b, lp, d_inner), BF16),
        scratch_shapes=[pltpu.VMEM((SUBLANES + t, conv_ch), F32), pltpu.VMEM((SSD_STATE, d_inner), F32),
                        pltpu.VMEM((t, d_inner), F32)],
        name="ssd_scan",
        compiler_params=pltpu.CompilerParams(dimension_semantics=("parallel", "arbitrary"),
                                             vmem_limit_bytes=VMEM_LIMIT),
    )(xbc.reshape(b, lp, conv_ch), dt.reshape(b, lp, LANES), z.reshape(b, lp, d_inner), *consts)
    return y.reshape(n, d_inner)


def _qkv_body(n_q, scale, h_ref, w_ref, qk_ref, vt_ref):
    r = _dot(h_ref[...].astype(BF16), w_ref[...])
    n_qk = qk_ref.shape[1]
    qk_ref[:, :n_q] = (r[:, :n_q] * scale).astype(qk_ref.dtype)
    qk_ref[:, n_q:] = r[:, n_q:n_qk].astype(qk_ref.dtype)
    vt_ref[...] = r[:, n_qk:].T.astype(vt_ref.dtype)


def _causal_mask_pair(s, tq):
    rows = lax.broadcasted_iota(jnp.int32, s.shape, 0)
    cols = lax.broadcasted_iota(jnp.int32, s.shape, 1)
    cols = jnp.where(cols >= tq, cols - tq, cols)
    return jnp.where(rows <= cols, s, NEG_INF)


def _flash_causal(qi, tq, scores, vt_ref, m_scr, l_scr, acc_scr, s_scrs, p_scrs, mx_scrs):
    m_scr[...] = jnp.full(m_scr.shape, NEG_INF, F32)
    l_scr[...] = jnp.zeros(l_scr.shape, F32)
    acc_scr[...] = jnp.zeros(acc_scr.shape, F32)
    p_scrs[1][...] = jnp.zeros(p_scrs[1].shape, p_scrs[1].dtype)

    def softmax(s, s_max, pv):
        m_prev = m_scr[...]
        m_new = jnp.maximum(m_prev, s_max)
        alpha = jnp.exp2(m_prev - m_new)
        p = jnp.exp2(s - m_new)
        l_scr[...] = alpha * l_scr[...] + jnp.sum(p, axis=0, keepdims=True)
        m_scr[...] = m_new
        acc_scr[...] = alpha * (acc_scr[...] + pv)
        return p.astype(BF16)

    def produce(kj, slot):
        s = scores(kj)
        s_scrs[slot][...] = s
        mx_scrs[slot][...] = jnp.max(s, axis=0, keepdims=True)

    def step(kj, cur, nxt):
        pv = _dot(vt_ref[jnp.maximum(kj - 1, 0)], p_scrs[nxt][...])
        p_scrs[cur][...] = softmax(s_scrs[cur][...], mx_scrs[cur][...], pv)
        produce(kj + 1, nxt)

    produce(0, 0)

    def body(t, carry):
        step(2 * t, 0, 1)
        step(2 * t + 1, 1, 0)
        return carry

    lax.fori_loop(0, lax.shift_right_logical(qi, 1), body, 0)
    pl.when((qi & 1) == 1)(functools.partial(step, qi - 1, 0, 1))

    def last(cur, nxt):
        pv = _dot(vt_ref[jnp.maximum(qi - 1, 0)], p_scrs[nxt][...])
        s = _causal_mask_pair(s_scrs[cur][...], tq)
        p = softmax(s, jnp.max(s, axis=0, keepdims=True), pv)
        acc_scr[...] = acc_scr[...] + _dot(vt_ref[qi], p)

    for par in range(2):
        pl.when((qi & 1) == par)(functools.partial(last, par, 1 - par))


def _flash_fixed_ref(qi, tq, scores, vt_ref, m_scr, l_scr, acc_scr, p_scrs):
    s0 = scores(0)
    m_scr[...] = jnp.max(s0, axis=0, keepdims=True)
    p0 = jnp.exp2(s0 - m_scr[...])
    l_scr[...] = jnp.sum(p0, axis=0, keepdims=True)
    p_scrs[0][...] = p0.astype(BF16)
    acc_scr[...] = jnp.zeros(acc_scr.shape, F32)

    def step(kj, cur, prv):
        p = jnp.exp2(scores(kj) - m_scr[...])
        acc_scr[...] += _dot(vt_ref[kj - 1], p_scrs[prv][...])
        l_scr[...] += jnp.sum(p, axis=0, keepdims=True)
        p_scrs[cur][...] = p.astype(BF16)

    def body(t, carry):
        step(2 * t + 1, 1, 0)
        step(2 * t + 2, 0, 1)
        return carry

    lax.fori_loop(0, lax.shift_right_logical(qi, 1), body, 0)
    pl.when((qi & 1) == 1)(functools.partial(step, qi, 1, 0))

    def tail(slot):
        acc_scr[...] += _dot(vt_ref[qi], p_scrs[slot][...])

    for par in range(2):
        pl.when((qi & 1) == par)(functools.partial(tail, par))


def _flash(qi, tq, scores, vt_ref, m_scr, l_scr, acc_scr, s_scrs, p_scrs, mx_scrs, exact_ref):
    _flash_fixed_ref(qi, tq, scores, vt_ref, m_scr, l_scr, acc_scr, p_scrs)
    ok = jnp.all(l_scr[...] < SOFTMAX_SUM_LIMIT) & jnp.all(jnp.abs(acc_scr[...]) < F32_FINITE_LIMIT)
    exact_ref[0] = jnp.where(ok, 0, 1)

    @pl.when(exact_ref[0] == 1)
    def _():
        _flash_causal(qi, tq, scores, vt_ref, m_scr, l_scr, acc_scr, s_scrs, p_scrs, mx_scrs)


def _flash_scratch(e, tq):
    w = 2 * tq
    return [pltpu.VMEM((1, w), F32), pltpu.VMEM((1, w), F32), pltpu.VMEM((e, w), F32),
            pltpu.VMEM((tq, w), F32), pltpu.VMEM((tq, w), F32), pltpu.VMEM((tq, w), BF16), pltpu.VMEM((tq, w), BF16),
            pltpu.VMEM((1, w), F32), pltpu.VMEM((1, w), F32), pltpu.SMEM((1,), jnp.int32)]


def _t5_bucket(dist):
    max_exact = REL_BUCKETS // 2
    d = jnp.maximum(dist, 0)
    df = jnp.maximum(d, max_exact).astype(F32)
    large = max_exact + (jnp.log(df / max_exact) / math.log(REL_MAX_DIST / max_exact)
                         * (REL_BUCKETS - max_exact)).astype(jnp.int32)
    large = jnp.minimum(large, REL_BUCKETS - 1)
    return jnp.where(d < max_exact, d, large)


def _diff_attn_body(tq, lam_init, q_ref, k_ref, vt_ref, bias_ref, lam_ref, g_ref, o_ref,
                    m_scr, l_scr, acc_scr, s0_scr, s1_scr, p0_scr, p1_scr, mx0_scr, mx1_scr, exact_ref, qcat_scr):
    qi = pl.program_id(2)
    q = q_ref[...]
    lane = lax.broadcasted_iota(jnp.int32, q.shape, 1)
    zero = jnp.zeros_like(q)
    qcat_scr[0:tq, :] = jnp.where(lane < DIFF_HEAD_DIM, q, zero)
    qcat_scr[tq:2 * tq, :] = jnp.where(lane >= DIFF_HEAD_DIM, q, zero)

    def scores(kj):
        start = pl.multiple_of(kj * tq, tq)
        bias = bias_ref[jnp.minimum(qi - kj, 2)]
        return _dot_nt(k_ref[pl.ds(start, tq), :], qcat_scr[...]) + jnp.concatenate([bias, bias], axis=1)

    _flash(qi, tq, scores, vt_ref, m_scr, l_scr, acc_scr, (s0_scr, s1_scr), (p0_scr, p1_scr), (mx0_scr, mx1_scr),
           exact_ref)
    acc = acc_scr[...] * (1.0 / l_scr[...])
    o_t = acc[:, :tq] - lam_ref[...] * acc[:, tq:]
    o_ref[...] = (_rms(o_t.T, g_ref[...]) * (1.0 - lam_init)).astype(o_ref.dtype)


def _diff_mixer(h3, w_qkv, lam_q1, lam_k1, lam_q2, lam_k2, subln_g, rel_bias, lam_init):
    b, lp, d = h3.shape
    n = b * lp
    hd = DIFF_HEADS * 2 * DIFF_HEAD_DIM
    tq = _pick_tile(lp, (384, 256, 128))
    assert tq >= REL_MAX_DIST
    nk = lp // tq
    qk, vt = _row_call(functools.partial(_qkv_body, hd, DIFF_HEAD_DIM ** -0.5 * LOG2E), [h3.reshape(n, d)],
                       [w_qkv.astype(BF16)], [(2 * hd, BF16)], tq, "diff_qkv", lp=lp, t_outs=[(hd, BF16)])
    qk = qk.reshape(b, lp, 2 * hd)
    m2 = 2 * tq
    kk = jnp.arange(m2)
    dist = jnp.where(kk < tq, kk, kk - m2)[None, :] + (jnp.arange(3) * tq)[:, None]
    tab = jnp.transpose(rel_bias[_t5_bucket(dist)].astype(F32) * LOG2E, (2, 0, 1))
    tab = jnp.where((jnp.arange(3)[:, None] == 0) & (kk[None, :] >= tq), NEG_INF, tab)
    nh = DIFF_HEADS
    bias = jnp.tile(tab, (1, 1, tq))[..., :tq * (m2 - 1)].reshape(nh, 3, tq, m2 - 1)[..., :tq]
    lam = (jnp.exp(jnp.sum(lam_q1.astype(F32) * lam_k1.astype(F32)))
           - jnp.exp(jnp.sum(lam_q2.astype(F32) * lam_k2.astype(F32))) + lam_init)
    w = 2 * DIFF_HEAD_DIM
    o = pl.pallas_call(
        functools.partial(_diff_attn_body, tq, lam_init),
        grid=(b, nh, nk),
        in_specs=[pl.BlockSpec((None, tq, w), lambda bi, hi, qi: (bi, qi, hi)),
                  pl.BlockSpec((None, lp, w), lambda bi, hi, qi: (bi, 0, nh + hi)),
                  pl.BlockSpec((None, nk, DIFF_V_DIM, tq), lambda bi, hi, qi: (bi, 0, hi, 0)),
                  pl.BlockSpec((None, 3, tq, tq), lambda bi, hi, qi: (hi, 0, 0, 0)),
                  pl.BlockSpec((1, 1), lambda bi, hi, qi: (0, 0)),
                  pl.BlockSpec((1, DIFF_V_DIM), lambda bi, hi, qi: (0, 0))],
        out_specs=pl.BlockSpec((None, tq, DIFF_V_DIM), lambda bi, hi, qi: (bi, qi, hi)),
        out_shape=jax.ShapeDtypeStruct((b, lp, nh * DIFF_V_DIM), BF16),
        scratch_shapes=_flash_scratch(DIFF_V_DIM, tq) + [pltpu.VMEM((2 * tq, w), BF16)],
        name="diff_attn",
        compiler_params=pltpu.CompilerParams(dimension_semantics=("parallel", "parallel", "arbitrary"),
                                             vmem_limit_bytes=VMEM_LIMIT),
    )(qk, qk, vt, bias, lam.reshape(1, 1), subln_g.reshape(1, DIFF_V_DIM))
    return o.reshape(n, nh * DIFF_V_DIM)


def _gelu_tanh(x):
    return 0.5 * x * (1.0 + jnp.tanh(math.sqrt(2.0 / math.pi) * (x + 0.044715 * (x * x * x))))


def _s5_body(bp, x_ref, m_ref, bc_ref, ccre_ref, ccim_ref, are_ref, aim_ref, y_ref, sre_scr, sim_scr):
    rows = x_ref.shape[0]
    gb = m_ref.shape[0]
    kw = m_ref.shape[1]
    p = ccre_ref.shape[1]
    for g in range(gb):
        e = _dot(x_ref[:, g * kw:(g + 1) * kw], bc_ref[g])
        sre_scr[:, g * p:(g + 1) * p] = e[:, :p]
        sim_scr[:, g * p:(g + 1) * p] = e[:, p:]

    ar, ai = are_ref[...], aim_ref[...]
    sub = lax.broadcasted_iota(jnp.int32, ar.shape, 0)
    per_tile = S5_SUBLANES // bp

    def tile_step(r, carry):
        pr, pi = carry
        r0 = pl.multiple_of(r * S5_SUBLANES, S5_SUBLANES)
        er, ei = sre_scr[pl.ds(r0, S5_SUBLANES), :], sim_scr[pl.ds(r0, S5_SUBLANES), :]
        in_r, in_i = pr, pi
        for c in range(per_tile):
            nr = ar * pr - ai * pi + er
            ni = ar * pi + ai * pr + ei
            if per_tile > 1:
                nr = pltpu.roll(nr, bp, axis=0)
                ni = pltpu.roll(ni, bp, axis=0)
            if c + 1 < per_tile:
                sel = (sub >= (c + 1) * bp) & (sub < (c + 2) * bp)
                in_r, in_i = jnp.where(sel, nr, in_r), jnp.where(sel, ni, in_i)
                pr, pi = nr, ni
            else:
                pr, pi = nr, ni
        if per_tile > 1:
            pr = jnp.where(sub < bp, pr, pltpu.roll(pr, bp, axis=0))
            pi = jnp.where(sub < bp, pi, pltpu.roll(pi, bp, axis=0))
        sre_scr[pl.ds(r0, S5_SUBLANES), :] = in_r
        sim_scr[pl.ds(r0, S5_SUBLANES), :] = in_i
        return pr, pi

    zero = jnp.zeros(ar.shape, F32)
    lax.fori_loop(0, rows // S5_SUBLANES, tile_step, (zero, zero), unroll=S5_UNROLL)

    for g in range(gb):
        xg = x_ref[:, g * kw:(g + 1) * kw]
        y = (_dot(xg, m_ref[g]) + _dot(sre_scr[:, g * p:(g + 1) * p].astype(BF16), ccre_ref[g])
             + _dot(sim_scr[:, g * p:(g + 1) * p].astype(BF16), ccim_ref[g]))
        y_ref[:, g * kw:(g + 1) * kw] = _gelu_tanh(y).astype(y_ref.dtype)


def _s5_pack_body(n_batch, h_ref, perm_ref, x_ref):
    t_len = h_ref.shape[1]
    xcat = jnp.concatenate([h_ref[:, t, :].astype(BF16) for t in range(t_len)], axis=1)
    x = _dot(xcat, perm_ref[...]).astype(x_ref.dtype)
    x_ref[...] = jnp.where(pl.program_id(0) < n_batch, x, jnp.zeros_like(x))


def _s5_unpack_body(y_ref, perm_ref, o_ref):
    t_len = o_ref.shape[1]
    res = _dot(y_ref[...], perm_ref[...])
    for t in range(t_len):
        o_ref[:, t, :] = res[:, t * LANES:(t + 1) * LANES]


def _s5_chunk_operators(lam_re, lam_im, log_step, b_re, b_im, c_re, c_im, d_skip):
    hi = lax.Precision.HIGHEST
    t_len = S5_CHUNK
    step = jnp.exp(log_step.astype(F32))[:, None]
    lr = jnp.minimum(lam_re.astype(F32), -1e-4)
    li = lam_im.astype(F32)
    mag = jnp.exp(lr * step)
    ar, ai = mag * jnp.cos(li * step), mag * jnp.sin(li * step)
    den = lr * lr + li * li
    cr = ((ar - 1.0) * lr + ai * li) / den
    ci = (ai * lr - (ar - 1.0) * li) / den
    br, bi = b_re.astype(F32), b_im.astype(F32)
    bbr = cr[..., None] * br - ci[..., None] * bi
    bbi = cr[..., None] * bi + ci[..., None] * br
    k = jnp.arange(t_len + 1, dtype=F32)[:, None, None]
    pmag = jnp.exp(k * (lr * step))
    pr, pi = pmag * jnp.cos(k * (li * step)), pmag * jnp.sin(k * (li * step))
    abr = pr[:t_len, ..., None] * bbr - pi[:t_len, ..., None] * bbi
    abi = pr[:t_len, ..., None] * bbi + pi[:t_len, ..., None] * bbr
    cre, cim = c_re.astype(F32), c_im.astype(F32)
    kern = (jnp.einsum('gop,kgpc->kgoc', cre, abr, precision=hi)
            - jnp.einsum('gop,kgpc->kgoc', cim, abi, precision=hi))
    kern = kern.at[0].add(jax.vmap(jnp.diag)(d_skip.astype(F32)))
    sig = jnp.arange(t_len)
    lag = sig[None, :] - sig[:, None]
    m = jnp.where((lag >= 0)[..., None, None, None], kern[jnp.maximum(lag, 0)], 0.0)
    g, o, c = kern.shape[1:]
    m = jnp.transpose(m, (2, 0, 4, 1, 3)).reshape(g, t_len * c, t_len * o)
    rev = t_len - 1 - sig
    p = bbr.shape[1]
    bc = jnp.concatenate([jnp.transpose(abr[rev], (1, 0, 3, 2)).reshape(g, t_len * c, p),
                          jnp.transpose(abi[rev], (1, 0, 3, 2)).reshape(g, t_len * c, p)], axis=2)
    car = cre[None] * pr[1:, :, None, :] - cim[None] * pi[1:, :, None, :]
    cai = cre[None] * pi[1:, :, None, :] + cim[None] * pr[1:, :, None, :]
    cc_re = jnp.transpose(car, (1, 3, 0, 2)).reshape(g, p, t_len * o)
    cc_im = -jnp.transpose(cai, (1, 3, 0, 2)).reshape(g, p, t_len * o)
    return (m.astype(BF16), bc.astype(BF16), cc_re.astype(BF16), cc_im.astype(BF16),
            pr[t_len].reshape(g * p), pi[t_len].reshape(g * p))


def _s5_mixer(h3, lam_re, lam_im, log_step, b_re, b_im, c_re, c_im, d_skip):
    b, lp, d = h3.shape
    t_len, g, c = S5_CHUNK, S5_GROUPS, S5_GROUP
    assert lp % t_len == 0 and b <= S5_SUBLANES
    bp = 4 if b <= 4 else S5_SUBLANES
    nj = lp // t_len
    m, bc, cc_re, cc_im, at_re, at_im = _s5_chunk_operators(lam_re, lam_im, log_step, b_re, b_im, c_re, c_im, d_skip)
    n_state = g * S5_STATE
    at_re = jnp.broadcast_to(at_re.reshape(1, n_state), (S5_SUBLANES, n_state))
    at_im = jnp.broadcast_to(at_im.reshape(1, n_state), (S5_SUBLANES, n_state))
    rows = nj * bp
    gb = S5_GROUP_BLOCK
    kw = t_len * c
    assert gb * c == LANES
    n_blk = g // gb
    src = jnp.arange(gb * kw)
    tau, r, ch = src // (gb * c), (src // c) % gb, src % c
    perm = jax.nn.one_hot(r * kw + tau * c + ch, gb * kw, dtype=BF16)
    x = pl.pallas_call(
        functools.partial(_s5_pack_body, b),
        grid=(bp, n_blk),
        in_specs=[pl.BlockSpec((None, nj, t_len, LANES), lambda bi, mi: (jnp.minimum(bi, b - 1), 0, 0, mi)),
                  pl.BlockSpec(perm.shape, lambda bi, mi: (0, 0))],
        out_specs=pl.BlockSpec((nj, gb * kw), lambda bi, mi: (0, bi * n_blk + mi)),
        out_shape=jax.ShapeDtypeStruct((nj, bp * g * kw), BF16),
        name="s5_pack",
        compiler_params=pltpu.CompilerParams(dimension_semantics=("parallel", "parallel"),
                                             vmem_limit_bytes=VMEM_LIMIT),
    )(h3.reshape(b, nj, t_len, d), perm).reshape(rows, g * kw)
    y = pl.pallas_call(
        functools.partial(_s5_body, bp), grid=(g // gb,),
        in_specs=[pl.BlockSpec((rows, gb * kw), lambda i: (0, i)),
                  pl.BlockSpec((gb, kw, kw), lambda i: (i, 0, 0)),
                  pl.BlockSpec((gb, kw, 2 * S5_STATE), lambda i: (i, 0, 0)),
                  pl.BlockSpec((gb, S5_STATE, kw), lambda i: (i, 0, 0)),
                  pl.BlockSpec((gb, S5_STATE, kw), lambda i: (i, 0, 0)),
                  pl.BlockSpec((S5_SUBLANES, gb * S5_STATE), lambda i: (0, i)),
                  pl.BlockSpec((S5_SUBLANES, gb * S5_STATE), lambda i: (0, i))],
        out_specs=pl.BlockSpec((rows, gb * kw), lambda i: (0, i)),
        out_shape=jax.ShapeDtypeStruct((rows, g * kw), BF16),
        scratch_shapes=[pltpu.VMEM((rows, gb * S5_STATE), F32), pltpu.VMEM((rows, gb * S5_STATE), F32)],
        name="s5_chunks",
        compiler_params=pltpu.CompilerParams(dimension_semantics=("parallel",), vmem_limit_bytes=VMEM_LIMIT),
    )(x, m, bc, cc_re, cc_im, at_re, at_im)
    y = pl.pallas_call(
        _s5_unpack_body,
        grid=(b, n_blk),
        in_specs=[pl.BlockSpec((nj, gb * kw), lambda bi, mi: (0, bi * n_blk + mi)),
                  pl.BlockSpec(perm.shape, lambda bi, mi: (0, 0))],
        out_specs=pl.BlockSpec((None, nj, t_len, LANES), lambda bi, mi: (bi, 0, 0, mi)),
        out_shape=jax.ShapeDtypeStruct((b, nj, t_len, d), F32),
        name="s5_unpack",
        compiler_params=pltpu.CompilerParams(dimension_semantics=("parallel", "parallel"),
                                             vmem_limit_bytes=VMEM_LIMIT),
    )(y.reshape(nj, bp * g * kw), perm.T)
    return y.reshape(b * lp, d)


def _rope_block(x, cm, s1, s2):
    return (x * cm + pltpu.roll(x, LANES - MLA_ROPE // 2, axis=1) * s1 + pltpu.roll(x, MLA_ROPE // 2, axis=1) * s2)


def _mla_in_body(h_ref, qcm_ref, qs1_ref, qs2_ref, kcm_ref, ks1_ref, ks2_ref,
                 win_ref, qg_ref, kvg_ref, wuq_ref, wuk_ref, wuv_ref, q_ref, k_ref, vt_ref):
    c = _dot(h_ref[...].astype(BF16), win_ref[...])
    c_q = _rms(c[:, :MLA_Q_RANK], qg_ref[...]).astype(BF16)
    c_kv = _rms(c[:, MLA_Q_RANK:MLA_Q_RANK + MLA_KV_RANK], kvg_ref[...]).astype(BF16)
    k_r = _rope_block(c[:, MLA_Q_RANK + MLA_KV_RANK:], kcm_ref[...], ks1_ref[...], ks2_ref[...])
    q = _dot(c_q, wuq_ref[...])
    k = _dot(c_kv, wuk_ref[...])
    qcm, qs1, qs2 = qcm_ref[...], qs1_ref[...], qs2_ref[...]
    for h in range(MLA_HEADS):
        sl = slice(h * LANES, (h + 1) * LANES)
        q_ref[:, sl] = _rope_block(q[:, sl], qcm, qs1, qs2).astype(q_ref.dtype)
        k_ref[:, sl] = (k[:, sl] + k_r).astype(k_ref.dtype)
    vt_ref[...] = _dot(c_kv, wuv_ref[...]).T.astype(vt_ref.dtype)


def _mla_attn_body(tq, q_ref, k_ref, vt_ref, mask_ref, o_ref,
                   m_scr, l_scr, acc_scr, s0_scr, s1_scr, p0_scr, p1_scr, mx0_scr, mx1_scr, exact_ref):
    qi = pl.program_id(2)

    def scores(kj):
        start = pl.multiple_of(kj * tq, tq)
        mask = mask_ref[jnp.minimum(qi - kj, 1)]
        return jnp.concatenate(
            [_dot_nt(k_ref[pl.ds(start, tq), hh * LANES:(hh + 1) * LANES], q_ref[:, hh * LANES:(hh + 1) * LANES])
             + mask for hh in range(2)], axis=1)

    _flash(qi, tq, scores, vt_ref, m_scr, l_scr, acc_scr, (s0_scr, s1_scr), (p0_scr, p1_scr), (mx0_scr, mx1_scr),
           exact_ref)
    acc = acc_scr[...] * (1.0 / l_scr[...])
    row = lax.broadcasted_iota(jnp.int32, (2 * MLA_V, tq), 0)
    o_t = jnp.where(row < MLA_V, acc[:, :tq], acc[:, tq:])
    o_ref[...] = o_t.T.astype(o_ref.dtype)


def _mla_mixer(h3, w_in, q_norm_g, kv_norm_g, w_uq, w_ukv):
    b, lp, d = h3.shape
    n = b * lp
    nh = MLA_HEADS
    qk = MLA_NOPE + MLA_ROPE
    half = MLA_ROPE // 2
    w_kr = jnp.pad(w_in[:, MLA_Q_RANK + MLA_KV_RANK:], ((0, 0), (MLA_NOPE, LANES - qk)))
    w_in_all = jnp.concatenate([w_in[:, :MLA_Q_RANK + MLA_KV_RANK], w_kr], axis=1).astype(BF16)
    w_uq_p = jnp.pad(w_uq.reshape(MLA_Q_RANK, nh, qk), ((0, 0), (0, 0), (0, LANES - qk)))
    w_uq_p = w_uq_p.reshape(MLA_Q_RANK, nh * LANES).astype(BF16)
    w_ukv3 = w_ukv.reshape(MLA_KV_RANK, nh, MLA_NOPE + MLA_V)
    w_uk_p = jnp.pad(w_ukv3[:, :, :MLA_NOPE], ((0, 0), (0, 0), (0, LANES - MLA_NOPE)))
    w_uk_p = w_uk_p.reshape(MLA_KV_RANK, nh * LANES).astype(BF16)
    w_uv = w_ukv3[:, :, MLA_NOPE:].reshape(MLA_KV_RANK, nh * MLA_V).astype(BF16)
    pos = jnp.arange(lp, dtype=F32)
    inv_freq = ROPE_BASE ** (-jnp.arange(0, MLA_ROPE, 2, dtype=F32) / MLA_ROPE)
    ang = pos[:, None] * inv_freq[None, :]
    cos, sin = jnp.cos(ang), jnp.sin(ang)
    z = lambda w: jnp.zeros((lp, w), F32)
    cm = jnp.concatenate([jnp.ones((lp, MLA_NOPE), F32), cos, cos, z(LANES - qk)], axis=1)
    s1 = jnp.concatenate([z(MLA_NOPE), -sin, z(LANES - MLA_NOPE - half)], axis=1)
    s2 = jnp.concatenate([z(MLA_NOPE + half), sin, z(LANES - qk)], axis=1)
    scale = qk ** -0.5 * LOG2E
    tq = _pick_tile(lp, (384, 256, 128))
    nk = lp // tq
    q, k, vt = _row_call(
        _mla_in_body, [h3.reshape(n, d)],
        [w_in_all, q_norm_g.reshape(1, MLA_Q_RANK), kv_norm_g.reshape(1, MLA_KV_RANK), w_uq_p, w_uk_p, w_uv],
        [(nh * LANES, BF16), (nh * LANES, BF16)], tq, "mla_in",
        pos_ins=[cm * scale, s1 * scale, s2 * scale, cm, s1, s2], lp=lp, t_outs=[(nh * MLA_V, BF16)])
    q, k = q.reshape(b, lp, nh * LANES), k.reshape(b, lp, nh * LANES)
    ti = jnp.arange(tq)
    mask = jnp.stack([jnp.where(ti[:, None] <= ti[None, :], 0.0, NEG_INF), jnp.zeros((tq, tq))]).astype(F32)
    o = pl.pallas_call(
        functools.partial(_mla_attn_body, tq),
        grid=(b, nh // 2, nk),
        in_specs=[pl.BlockSpec((None, tq, 2 * LANES), lambda bi, hi, qi: (bi, qi, hi)),
                  pl.BlockSpec((None, lp, 2 * LANES), lambda bi, hi, qi: (bi, 0, hi)),
                  pl.BlockSpec((None, nk, 2 * MLA_V, tq), lambda bi, hi, qi: (bi, 0, hi, 0)),
                  pl.BlockSpec((2, tq, tq), lambda bi, hi, qi: (0, 0, 0))],
        out_specs=pl.BlockSpec((None, tq, 2 * MLA_V), lambda bi, hi, qi: (bi, qi, hi)),
        out_shape=jax.ShapeDtypeStruct((b, lp, nh * MLA_V), BF16),
        scratch_shapes=_flash_scratch(2 * MLA_V, tq),
        name="mla_attn",
        compiler_params=pltpu.CompilerParams(dimension_semantics=("parallel", "parallel", "arbitrary"),
                                             vmem_limit_bytes=VMEM_LIMIT),
    )(q, k, vt, mask)
    return o.reshape(n, nh * MLA_V)


def kernel(x, meta, rel_bias, ln_g, ln_b, ffn_w1, ffn_w3, ffn_w2, ssd_w_in, ssd_conv_w, ssd_conv_b, ssd_dt_bias, ssd_a_log, ssd_d, ssd_norm_g, ssd_w_out, diff_w_qkv, diff_lam_q1, diff_lam_k1, diff_lam_q2, diff_lam_k2, diff_subln_g, diff_w_out, s5_lam_re, s5_lam_im, s5_log_step, s5_b_re, s5_b_im, s5_c_re, s5_c_im, s5_d, s5_w_glu, s5_b_glu, mla_w_in, mla_q_norm_g, mla_kv_norm_g, mla_w_uq, mla_w_ukv, mla_w_out):
    b, seq, d = x.shape
    l = seq + N_META
    lp = -(-l // SEQ_ALIGN) * SEQ_ALIGN
    n = b * lp
    h = jnp.concatenate([jnp.broadcast_to(meta[None].astype(x.dtype), (b, N_META, d)), x,
                         jnp.zeros((b, lp - l, d), x.dtype)], axis=1).reshape(n, d)
    w1b, w3b, w2b = ffn_w1.astype(BF16), ffn_w3.astype(BF16), ffn_w2.astype(BF16)
    for i in range(DEPTH):
        kind, j = i % 4, i // 4
        h = _ffn(h, w1b, w3b, w2b, ln_g[i, 0], ln_b[i, 0], (i, 0))
        h3 = h.reshape(b, lp, d)
        if kind == 0:
            y = _ssd_mixer(h3, ssd_w_in[j], ssd_conv_w[j], ssd_conv_b[j], ssd_dt_bias[j], ssd_a_log[j],
                           ssd_d[j], ssd_norm_g[j])
            w_out, b_out = ssd_w_out[j], None
        elif kind == 1:
            lam_init = 0.8 - 0.6 * math.exp(-0.3 * i)
            y = _diff_mixer(h3, diff_w_qkv[j], diff_lam_q1[j], diff_lam_k1[j], diff_lam_q2[j], diff_lam_k2[j],
                            diff_subln_g[j], rel_bias, lam_init)
            w_out, b_out = diff_w_out[j], None
        elif kind == 2:
            y = _s5_mixer(h3, s5_lam_re[j], s5_lam_im[j], s5_log_step[j], s5_b_re[j], s5_b_im[j], s5_c_re[j],
                          s5_c_im[j], s5_d[j])
            w_out, b_out = s5_w_glu[j], s5_b_glu[j]
        else:
            y = _mla_mixer(h3, mla_w_in[j], mla_q_norm_g[j], mla_kv_norm_g[j], mla_w_uq[j], mla_w_ukv[j])
            w_out, b_out = mla_w_out[j], None
        h = _mix_ffn(y, h, w_out, b_out, ln_g[i, 1], ln_b[i, 1], w1b, w3b, w2b, ln_g[i, 2], ln_b[i, 2], (i, 1))
    return h.reshape(b, lp, d)[:, N_META:l]
```

```python
import functools
import math

import jax
import jax.numpy as jnp
from jax import lax
from jax.experimental import pallas as pl
from jax.experimental.pallas import tpu as pltpu

F32 = jnp.float32
BF16 = jnp.bfloat16

N_META = 16
DEPTH = 4
ALPHA = (2.0 * DEPTH) ** 0.25
LN_EPS = 1e-5
RMS_EPS = 1e-6
NEG_INF = -1e30
LOG2E = math.log2(math.e)
SOFTMAX_SUM_LIMIT = 2.0 ** 100
F32_FINITE_LIMIT = 3.0e38

SSD_HEAD_DIM = 64
SSD_HEADS = 32
SSD_GROUPS = 8
SSD_STATE = 128
SSD_CONV = 4
SSD_CHUNK = 128

DIFF_HEADS = 8
DIFF_HEAD_DIM = 64
DIFF_V_DIM = 128
REL_BUCKETS = 32
REL_MAX_DIST = 128

S5_GROUP = 16
S5_GROUPS = 64
S5_STATE = 64
S5_CHUNK = 16
S5_GROUP_BLOCK = 8
SUBLANES = 8
S5_SUBLANES = SUBLANES
S5_UNROLL = 4

MLA_HEADS = 16
MLA_Q_RANK = 384
MLA_KV_RANK = 256
MLA_NOPE = 64
MLA_ROPE = 32
MLA_V = 64
ROPE_BASE = 10000.0

FFN_ROW_TILES = (512, 384, 256, 128)
FFN_CHUNKS = 11

LANES = 128
SEQ_ALIGN = 128
VMEM_LIMIT = 56 * 1024 * 1024


def _pick_tile(n, candidates):
    for c in candidates:
        if n % c == 0:
            return c
    raise ValueError(f"no tile for {n}")


def _layer_norm(z, g, b):
    mu = jnp.mean(z, axis=-1, keepdims=True)
    d = z - mu
    var = jnp.mean(d * d, axis=-1, keepdims=True)
    return d * lax.rsqrt(var + LN_EPS) * g + b


def _rms(z, g):
    return z * lax.rsqrt(jnp.mean(z * z, axis=-1, keepdims=True) + RMS_EPS) * g


def _silu(x):
    return x * (1.0 / (1.0 + jnp.exp(-x)))


def _dot(a, b):
    return jnp.dot(a, b, preferred_element_type=F32)


def _dot_nt(a, b):
    return lax.dot_general(a, b, (((1,), (1,)), ((), ())), preferred_element_type=F32)


def _row_call(body, row_ins, const_ins, outs, tm, name, pos_ins=(), lp=None, t_outs=()):
    n = row_ins[0].shape[0]
    assert n % tm == 0
    in_specs = [pl.BlockSpec((tm, a.shape[1]), lambda i: (i, 0)) for a in row_ins]
    if pos_ins or t_outs:
        assert lp % tm == 0
        nb = lp // tm
        in_specs += [pl.BlockSpec((tm, a.shape[1]), lambda i: (i % nb, 0)) for a in pos_ins]
    const_arrays = []
    for c in const_ins:
        arr, lead = c if isinstance(c, tuple) else (c, ())
        const_arrays.append(arr)
        in_specs.append(pl.BlockSpec((None,) * len(lead) + arr.shape[len(lead):],
                                     functools.partial(lambda lead_, i: lead_ + (0, 0), tuple(lead)),
                                     pipeline_mode=pl.Buffered(1)))
    out_specs = [pl.BlockSpec((tm, w), lambda i: (i, 0)) for (w, _) in outs]
    out_shape = [jax.ShapeDtypeStruct((n, w), dt) for (w, dt) in outs]
    out_specs += [pl.BlockSpec((None, None, w, tm), lambda i: (i // nb, i % nb, 0, 0)) for (w, _) in t_outs]
    out_shape += [jax.ShapeDtypeStruct((n // lp, nb, w, tm), dt) for (w, dt) in t_outs]
    res = pl.pallas_call(
        body, grid=(n // tm,), in_specs=in_specs, out_specs=out_specs, out_shape=out_shape, name=name,
        compiler_params=pltpu.CompilerParams(dimension_semantics=("parallel",), vmem_limit_bytes=VMEM_LIMIT),
    )(*row_ins, *pos_ins, *const_arrays)
    return res


def _ffn_rows(x, n_chunks, w1_ref, w3_ref, w2_ref, g_ref, b_ref):
    xb = x.astype(BF16)
    fc = w1_ref.shape[1] // n_chunks
    y = None
    for c in range(n_chunks):
        sl = slice(c * fc, (c + 1) * fc)
        gate = _dot(xb, w1_ref[:, sl])
        up = _dot(xb, w3_ref[:, sl])
        a = (_silu(gate) * up).astype(BF16)
        part = _dot(a, w2_ref[sl, :])
        y = part if y is None else y + part
    return _layer_norm(ALPHA * x + 0.5 * y, g_ref[...], b_ref[...])


def _ffn_body(n_chunks, h_ref, w1_ref, w3_ref, w2_ref, g_ref, b_ref, o_ref):
    o_ref[...] = _ffn_rows(h_ref[...], n_chunks, w1_ref, w3_ref, w2_ref, g_ref, b_ref)


def _mix_ffn_body(n_chunks, glu, y_ref, h_ref, wo_ref, bo_ref, gm_ref, bm_ref,
                  w1_ref, w3_ref, w2_ref, g_ref, b_ref, o_ref):
    m = _dot(y_ref[...].astype(BF16), wo_ref[...])
    if glu:
        m = m + bo_ref[...]
        d = o_ref.shape[1]
        m = m[:, :d] * (1.0 / (1.0 + jnp.exp(-m[:, d:])))
    x = _layer_norm(ALPHA * h_ref[...] + m, gm_ref[...], bm_ref[...])
    o_ref[...] = _ffn_rows(x, n_chunks, w1_ref, w3_ref, w2_ref, g_ref, b_ref)


def _ffn_chunks(f, n_chunks):
    return n_chunks or (FFN_CHUNKS if f % (FFN_CHUNKS * LANES) == 0 else 1)


def _ffn(h2, w1, w3, w2, g, b, idx, tm=None, n_chunks=None, name="ffn"):
    n, d = h2.shape
    tm = tm or _pick_tile(n, FFN_ROW_TILES)
    (out,) = _row_call(functools.partial(_ffn_body, _ffn_chunks(w1.shape[-1], n_chunks)), [h2],
                       [(w1, idx), (w3, idx), (w2, idx), g.reshape(1, d), b.reshape(1, d)],
                       [(d, F32)], tm, name)
    return out


def _mix_ffn(y2, h2, w_out, b_out, g_mix, b_mix, w1, w3, w2, g, b, idx, tm=None, n_chunks=None, name="mix_ffn"):
    n, d = h2.shape
    tm = tm or _pick_tile(n, FFN_ROW_TILES)
    glu = b_out is not None
    wo = w_out.astype(BF16)
    bo = (b_out if glu else jnp.zeros((wo.shape[1],), F32)).reshape(1, wo.shape[1])
    (out,) = _row_call(functools.partial(_mix_ffn_body, _ffn_chunks(w1.shape[-1], n_chunks), glu), [y2, h2],
                       [wo, bo, g_mix.reshape(1, d), b_mix.reshape(1, d),
                        (w1, idx), (w3, idx), (w2, idx), g.reshape(1, d), b.reshape(1, d)],
                       [(d, F32)], tm, name)
    return out


def _ssd_in_body(h_ref, w_ref, xbc_ref, z_ref, dt_ref):
    r = _dot(h_ref[...].astype(BF16), w_ref[...])
    nx, nz = xbc_ref.shape[1], z_ref.shape[1]
    xbc_ref[...] = r[:, :nx]
    z_ref[...] = r[:, nx:nx + nz]
    dt_ref[...] = r[:, nx + nz:]


def _softplus(x):
    return jnp.maximum(x, 0.0) + jnp.log1p(jnp.exp(-jnp.abs(x)))


def _ssd_body(xbc_ref, dt_ref, z_ref, convw_ref, convb_ref, dtb_ref, alog_ref, dskip_ref, normg_ref,
              y_ref, buf_scr, state_scr, y_scr):
    t = SSD_CHUNK
    d_inner = SSD_HEADS * SSD_HEAD_DIM
    gn = SSD_GROUPS * SSD_STATE
    hp = SSD_HEAD_DIM
    r = SSD_HEADS // SSD_GROUPS

    @pl.when(pl.program_id(1) == 0)
    def _():
        buf_scr[0:SUBLANES, :] = jnp.zeros((SUBLANES, buf_scr.shape[1]), F32)
        state_scr[...] = jnp.zeros(state_scr.shape, F32)

    x = xbc_ref[...]
    buf_scr[SUBLANES:SUBLANES + t, :] = x
    acc = convb_ref[...] + convw_ref[SSD_CONV - 1:SSD_CONV, :] * x
    for j in range(1, SSD_CONV):
        acc = acc + convw_ref[SSD_CONV - 1 - j:SSD_CONV - j, :] * buf_scr[SUBLANES - j:SUBLANES - j + t, :]
    buf_scr[0:SUBLANES, :] = x[t - SUBLANES:t, :]
    xbc = _silu(acc)

    dt = _softplus(dt_ref[...] + dtb_ref[...])
    a = dt * (-jnp.exp(alog_ref[...]))
    rows = lax.broadcasted_iota(jnp.int32, (t, t), 0)
    cols = lax.broadcasted_iota(jnp.int32, (t, t), 1)
    tri = rows >= cols
    a_cum = jnp.dot(tri.astype(F32), a, precision=lax.Precision.HIGHEST, preferred_element_type=F32)
    a_last = a_cum[t - 1:t, :]
    a_cum_t = a_cum.T
    dt_t = dt.T
    w_end_t = (dt * jnp.exp(a_last - a_cum)).T
    chunk_dec = jnp.exp(a_last)

    for g in range(SSD_GROUPS):
        b_f = xbc[:, d_inner + g * SSD_STATE:d_inner + (g + 1) * SSD_STATE]
        c_g = xbc[:, d_inner + gn + g * SSD_STATE:d_inner + gn + (g + 1) * SSD_STATE].astype(BF16)
        b_g = b_f.astype(BF16)
        b_ft = b_f.T
        cb = _dot_nt(c_g, b_g)
        y_off_g = _dot(c_g, state_scr[:, g * r * hp:(g + 1) * r * hp].astype(BF16))
        for k in range(r):
            h = g * r + k
            col = jnp.broadcast_to(a_cum[:, h:h + 1], (t, t))
            decay = jnp.exp(jnp.where(tri, col - a_cum_t[h:h + 1, :], NEG_INF))
            xh = xbc[:, h * hp:(h + 1) * hp]
            xb = xh.astype(BF16)
            y_diag = _dot((cb * decay * dt_t[h:h + 1, :]).astype(BF16), xb)
            y_off = y_off_g[:, k * hp:(k + 1) * hp] * jnp.exp(col[:, :hp])
            y_scr[:, h * hp:(h + 1) * hp] = y_diag + y_off + dskip_ref[:, h * hp:(h + 1) * hp] * xh
            st = state_scr[:, h * hp:(h + 1) * hp]
            b_w = (b_ft * w_end_t[h:h + 1, :]).astype(BF16)
            state_scr[:, h * hp:(h + 1) * hp] = st * chunk_dec[:, h:h + 1] + _dot(b_w, xb)

    y = y_scr[...] * _silu(z_ref[...])
    gs = d_inner // SSD_GROUPS
    for g in range(SSD_GROUPS):
        sl = slice(g * gs, (g + 1) * gs)
        y_ref[:, sl] = _rms(y[:, sl], normg_ref[:, sl]).astype(y_ref.dtype)


def _ssd_mixer(h3, w_in, conv_w, conv_b, dt_bias, a_log, d_skip, norm_g):
    b, lp, d = h3.shape
    n = b * lp
    d_inner = SSD_HEADS * SSD_HEAD_DIM
    conv_ch = d_inner + 2 * SSD_GROUPS * SSD_STATE
    w_z, w_xbc, w_dt = w_in[:, :d_inner], w_in[:, d_inner:d_inner + conv_ch], w_in[:, d_inner + conv_ch:]
    w_dt = jnp.pad(w_dt, ((0, 0), (0, LANES - SSD_HEADS)))
    w_all = jnp.concatenate([w_xbc, w_z, w_dt], axis=1).astype(BF16)
    tm = _pick_tile(n, (512, 384, 256, 128))
    xbc, z, dt = _row_call(_ssd_in_body, [h3.reshape(n, d)], [w_all],
                           [(conv_ch, F32), (d_inner, F32), (LANES, F32)], tm, "ssd_in")
    t = SSD_CHUNK
    pad1 = lambda v: jnp.pad(v, (0, LANES - SSD_HEADS)).reshape(1, LANES)
    consts = [conv_w, conv_b.reshape(1, conv_ch), pad1(dt_bias), pad1(a_log),
              jnp.repeat(d_skip, SSD_HEAD_DIM).reshape(1, d_inner), norm_g.reshape(1, d_inner)]
    row_spec = lambda w: pl.BlockSpec((None, t, w), lambda i, c: (i, c, 0))
    y = pl.pallas_call(
        _ssd_body, grid=(b, lp // t),
        in_specs=[row_spec(conv_ch), row_spec(LANES), row_spec(d_inner)]
        + [pl.BlockSpec(c.shape, lambda i, c_: (0, 0)) for c in consts],
        out_specs=row_spec(d_inner),
        out_shape=jax.ShapeDtypeStruct((b, lp, d_inner), BF16),
        scratch_shapes=[pltpu.VMEM((SUBLANES + t, conv_ch), F32), pltpu.VMEM((SSD_STATE, d_inner), F32),
                        pltpu.VMEM((t, d_inner), F32)],
        name="ssd_scan",
        compiler_params=pltpu.CompilerParams(dimension_semantics=("parallel", "arbitrary"),
                                             vmem_limit_bytes=VMEM_LIMIT),
    )(xbc.reshape(b, lp, conv_ch), dt.reshape(b, lp, LANES), z.reshape(b, lp, d_inner), *consts)
    return y.reshape(n, d_inner)


def _qkv_body(n_q, scale, h_ref, w_ref, qk_ref, vt_ref):
    r = _dot(h_ref[...].astype(BF16), w_ref[...])
    n_qk = qk_ref.shape[1]
    qk_ref[:, :n_q] = (r[:, :n_q] * scale).astype(qk_ref.dtype)
    qk_ref[:, n_q:] = r[:, n_q:n_qk].astype(qk_ref.dtype)
    vt_ref[...] = r[:, n_qk:].T.astype(vt_ref.dtype)


def _causal_mask_pair(s, tq):
    rows = lax.broadcasted_iota(jnp.int32, s.shape, 0)
    cols = lax.broadcasted_iota(jnp.int32, s.shape, 1)
    cols = jnp.where(cols >= tq, cols - tq, cols)
    return jnp.where(rows <= cols, s, NEG_INF)


def _flash_causal(qi, tq, scores, vt_ref, m_scr, l_scr, acc_scr, s_scrs, p_scrs, mx_scrs):
    m_scr[...] = jnp.full(m_scr.shape, NEG_INF, F32)
    l_scr[...] = jnp.zeros(l_scr.shape, F32)
    acc_scr[...] = jnp.zeros(acc_scr.shape, F32)
    p_scrs[1][...] = jnp.zeros(p_scrs[1].shape, p_scrs[1].dtype)

    def softmax(s, s_max, pv):
        m_prev = m_scr[...]
        m_new = jnp.maximum(m_prev, s_max)
        alpha = jnp.exp2(m_prev - m_new)
        p = jnp.exp2(s - m_new)
        l_scr[...] = alpha * l_scr[...] + jnp.sum(p, axis=0, keepdims=True)
        m_scr[...] = m_new
        acc_scr[...] = alpha * (acc_scr[...] + pv)
        return p.astype(BF16)

    def produce(kj, slot):
        s = scores(kj)
        s_scrs[slot][...] = s
        mx_scrs[slot][...] = jnp.max(s, axis=0, keepdims=True)

    def step(kj, cur, nxt):
        pv = _dot(vt_ref[jnp.maximum(kj - 1, 0)], p_scrs[nxt][...])
        p_scrs[cur][...] = softmax(s_scrs[cur][...], mx_scrs[cur][...], pv)
        produce(kj + 1, nxt)

    produce(0, 0)

    def body(t, carry):
        step(2 * t, 0, 1)
        step(2 * t + 1, 1, 0)
        return carry

    lax.fori_loop(0, lax.shift_right_logical(qi, 1), body, 0)
    pl.when((qi & 1) == 1)(functools.partial(step, qi - 1, 0, 1))

    def last(cur, nxt):
        pv = _dot(vt_ref[jnp.maximum(qi - 1, 0)], p_scrs[nxt][...])
        s = _causal_mask_pair(s_scrs[cur][...], tq)
        p = softmax(s, jnp.max(s, axis=0, keepdims=True), pv)
        acc_scr[...] = acc_scr[...] + _dot(vt_ref[qi], p)

    for par in range(2):
        pl.when((qi & 1) == par)(functools.partial(last, par, 1 - par))


def _flash_head(tq, nq, scores, vt_ref, finish, m_all, l_all, acc_all,
                m_scr, l_scr, acc_scr, s_scrs, p_scrs, mx_scrs, exact_ref):
    def init(qi, carry):
        m_all[qi] = jnp.max(scores(0, qi), axis=0, keepdims=True)
        l_all[qi] = jnp.zeros(l_all.shape[1:], F32)
        acc_all[qi] = jnp.zeros(acc_all.shape[1:], F32)
        return carry

    lax.fori_loop(0, nq, init, 0)

    def probs(kj, qi, slot):
        p = jnp.exp2(scores(kj, qi) - m_all[qi])
        l_all[qi] += jnp.sum(p, axis=0, keepdims=True)
        p_scrs[slot][...] = p.astype(BF16)

    def step(prev, cur, prv):
        kjp, qip = prev
        wrap = qip + 1 >= nq
        kj = jnp.where(wrap, kjp + 1, kjp)
        qi = jnp.where(wrap, kjp + 1, qip + 1)
        p = jnp.exp2(scores(kj, qi) - m_all[qi])
        acc_all[qip] += _dot(vt_ref[kjp], p_scrs[prv][...])
        l_all[qi] += jnp.sum(p, axis=0, keepdims=True)
        p_scrs[cur][...] = p.astype(BF16)
        return kj, qi

    n_steps = nq * (nq + 1) // 2 - 1
    probs(0, 0, 0)

    def body(t, prev):
        return step(step(prev, 1, 0), 0, 1)

    prev = lax.fori_loop(0, n_steps // 2, body, (jnp.int32(0), jnp.int32(0)))
    if n_steps % 2:
        prev = step(prev, 1, 0)
    acc_all[prev[1]] += _dot(vt_ref[prev[0]], p_scrs[n_steps % 2][...])

    ok = jnp.all(l_all[...] < SOFTMAX_SUM_LIMIT) & jnp.all(jnp.abs(acc_all[...]) < F32_FINITE_LIMIT)
    exact_ref[0] = jnp.where(ok, 0, 1)

    @pl.when(exact_ref[0] == 0)
    def _():
        def fin(qi, carry):
            finish(qi, l_all[qi], acc_all[qi])
            return carry

        lax.fori_loop(0, nq, fin, 0)

    @pl.when(exact_ref[0] == 1)
    def _():
        def redo(qi, carry):
            _flash_causal(qi, tq, lambda kj: scores(kj, qi), vt_ref, m_scr, l_scr, acc_scr, s_scrs, p_scrs, mx_scrs)
            finish(qi, l_scr[...], acc_scr[...])
            return carry

        lax.fori_loop(0, nq, redo, 0)


def _flash_scratch(e, tq, nq):
    w = 2 * tq
    return [pltpu.VMEM((nq, 1, w), F32), pltpu.VMEM((nq, 1, w), F32), pltpu.VMEM((nq, e, w), F32),
            pltpu.VMEM((1, w), F32), pltpu.VMEM((1, w), F32), pltpu.VMEM((e, w), F32),
            pltpu.VMEM((tq, w), F32), pltpu.VMEM((tq, w), F32), pltpu.VMEM((tq, w), BF16), pltpu.VMEM((tq, w), BF16),
            pltpu.VMEM((1, w), F32), pltpu.VMEM((1, w), F32), pltpu.SMEM((1,), jnp.int32)]


def _t5_bucket(dist):
    max_exact = REL_BUCKETS // 2
    d = jnp.maximum(dist, 0)
    df = jnp.maximum(d, max_exact).astype(F32)
    large = max_exact + (jnp.log(df / max_exact) / math.log(REL_MAX_DIST / max_exact)
                         * (REL_BUCKETS - max_exact)).astype(jnp.int32)
    large = jnp.minimum(large, REL_BUCKETS - 1)
    return jnp.where(d < max_exact, d, large)


def _diff_attn_body(tq, nq, lam_init, q_ref, k_ref, vt_ref, bias_ref, lam_ref, g_ref, o_ref,
                    m_all, l_all, acc_all, m_scr, l_scr, acc_scr, s0_scr, s1_scr, p0_scr, p1_scr, mx0_scr, mx1_scr,
                    exact_ref, qcat_all):
    lane = lax.broadcasted_iota(jnp.int32, (tq, q_ref.shape[1]), 1)

    def prep(qi, carry):
        q = q_ref[pl.ds(pl.multiple_of(qi * tq, tq), tq), :]
        zero = jnp.zeros_like(q)
        qcat_all[qi, 0:tq, :] = jnp.where(lane < DIFF_HEAD_DIM, q, zero)
        qcat_all[qi, tq:2 * tq, :] = jnp.where(lane >= DIFF_HEAD_DIM, q, zero)
        return carry

    lax.fori_loop(0, nq, prep, 0)

    def scores(kj, qi):
        start = pl.multiple_of(kj * tq, tq)
        bias = bias_ref[jnp.minimum(qi - kj, 2)]
        return _dot_nt(k_ref[pl.ds(start, tq), :], qcat_all[qi]) + jnp.concatenate([bias, bias], axis=1)

    def finish(qi, l, acc):
        a = acc * (1.0 / l)
        o_t = a[:, :tq] - lam_ref[...] * a[:, tq:]
        o_ref[pl.ds(pl.multiple_of(qi * tq, tq), tq), :] = (
            _rms(o_t.T, g_ref[...]) * (1.0 - lam_init)).astype(o_ref.dtype)

    _flash_head(tq, nq, scores, vt_ref, finish, m_all, l_all, acc_all, m_scr, l_scr, acc_scr,
                (s0_scr, s1_scr), (p0_scr, p1_scr), (mx0_scr, mx1_scr), exact_ref)


def _diff_mixer(h3, w_qkv, lam_q1, lam_k1, lam_q2, lam_k2, subln_g, rel_bias, lam_init):
    b, lp, d = h3.shape
    n = b * lp
    hd = DIFF_HEADS * 2 * DIFF_HEAD_DIM
    tq = _pick_tile(lp, (384, 256, 128))
    assert tq >= REL_MAX_DIST
    nk = lp // tq
    qk, vt = _row_call(functools.partial(_qkv_body, hd, DIFF_HEAD_DIM ** -0.5 * LOG2E), [h3.reshape(n, d)],
                       [w_qkv.astype(BF16)], [(2 * hd, BF16)], tq, "diff_qkv", lp=lp, t_outs=[(hd, BF16)])
    qk = qk.reshape(b, lp, 2 * hd)
    m2 = 2 * tq
    kk = jnp.arange(m2)
    dist = jnp.where(kk < tq, kk, kk - m2)[None, :] + (jnp.arange(3) * tq)[:, None]
    tab = jnp.transpose(rel_bias[_t5_bucket(dist)].astype(F32) * LOG2E, (2, 0, 1))
    tab = jnp.where((jnp.arange(3)[:, None] == 0) & (kk[None, :] >= tq), NEG_INF, tab)
    nh = DIFF_HEADS
    bias = jnp.tile(tab, (1, 1, tq))[..., :tq * (m2 - 1)].reshape(nh, 3, tq, m2 - 1)[..., :tq]
    lam = (jnp.exp(jnp.sum(lam_q1.astype(F32) * lam_k1.astype(F32)))
           - jnp.exp(jnp.sum(lam_q2.astype(F32) * lam_k2.astype(F32))) + lam_init)
    w = 2 * DIFF_HEAD_DIM
    o = pl.pallas_call(
        functools.partial(_diff_attn_body, tq, nk, lam_init),
        grid=(b, nh),
        in_specs=[pl.BlockSpec((None, lp, w), lambda bi, hi: (bi, 0, hi)),
                  pl.BlockSpec((None, lp, w), lambda bi, hi: (bi, 0, nh + hi)),
                  pl.BlockSpec((None, nk, DIFF_V_DIM, tq), lambda bi, hi: (bi, 0, hi, 0)),
                  pl.BlockSpec((None, 3, tq, tq), lambda bi, hi: (hi, 0, 0, 0)),
                  pl.BlockSpec((1, 1), lambda bi, hi: (0, 0)),
                  pl.BlockSpec((1, DIFF_V_DIM), lambda bi, hi: (0, 0))],
        out_specs=pl.BlockSpec((None, lp, DIFF_V_DIM), lambda bi, hi: (bi, 0, hi)),
        out_shape=jax.ShapeDtypeStruct((b, lp, nh * DIFF_V_DIM), BF16),
        scratch_shapes=_flash_scratch(DIFF_V_DIM, tq, nk) + [pltpu.VMEM((nk, 2 * tq, w), BF16)],
        name="diff_attn",
        compiler_params=pltpu.CompilerParams(dimension_semantics=("parallel", "parallel"),
                                             vmem_limit_bytes=VMEM_LIMIT),
    )(qk, qk, vt, bias, lam.reshape(1, 1), subln_g.reshape(1, DIFF_V_DIM))
    return o.reshape(n, nh * DIFF_V_DIM)


def _gelu_tanh(x):
    return 0.5 * x * (1.0 + jnp.tanh(math.sqrt(2.0 / math.pi) * (x + 0.044715 * (x * x * x))))


def _s5_body(bp, x_ref, m_ref, bc_ref, ccre_ref, ccim_ref, are_ref, aim_ref, y_ref, sre_scr, sim_scr):
    rows = x_ref.shape[0]
    gb = m_ref.shape[0]
    kw = m_ref.shape[1]
    p = ccre_ref.shape[1]
    for g in range(gb):
        e = _dot(x_ref[:, g * kw:(g + 1) * kw], bc_ref[g])
        sre_scr[:, g * p:(g + 1) * p] = e[:, :p]
        sim_scr[:, g * p:(g + 1) * p] = e[:, p:]

    ar, ai = are_ref[...], aim_ref[...]
    sub = lax.broadcasted_iota(jnp.int32, ar.shape, 0)
    per_tile = S5_SUBLANES // bp

    def tile_step(r, carry):
        pr, pi = carry
        r0 = pl.multiple_of(r * S5_SUBLANES, S5_SUBLANES)
        er, ei = sre_scr[pl.ds(r0, S5_SUBLANES), :], sim_scr[pl.ds(r0, S5_SUBLANES), :]
        in_r, in_i = pr, pi
        for c in range(per_tile):
            nr = ar * pr - ai * pi + er
            ni = ar * pi + ai * pr + ei
            if per_tile > 1:
                nr = pltpu.roll(nr, bp, axis=0)
                ni = pltpu.roll(ni, bp, axis=0)
            if c + 1 < per_tile:
                sel = (sub >= (c + 1) * bp) & (sub < (c + 2) * bp)
                in_r, in_i = jnp.where(sel, nr, in_r), jnp.where(sel, ni, in_i)
                pr, pi = nr, ni
            else:
                pr, pi = nr, ni
        if per_tile > 1:
            pr = jnp.where(sub < bp, pr, pltpu.roll(pr, bp, axis=0))
            pi = jnp.where(sub < bp, pi, pltpu.roll(pi, bp, axis=0))
        sre_scr[pl.ds(r0, S5_SUBLANES), :] = in_r
        sim_scr[pl.ds(r0, S5_SUBLANES), :] = in_i
        return pr, pi

    zero = jnp.zeros(ar.shape, F32)
    lax.fori_loop(0, rows // S5_SUBLANES, tile_step, (zero, zero), unroll=S5_UNROLL)

    for g in range(gb):
        xg = x_ref[:, g * kw:(g + 1) * kw]
        y = (_dot(xg, m_ref[g]) + _dot(sre_scr[:, g * p:(g + 1) * p].astype(BF16), ccre_ref[g])
             + _dot(sim_scr[:, g * p:(g + 1) * p].astype(BF16), ccim_ref[g]))
        y_ref[:, g * kw:(g + 1) * kw] = _gelu_tanh(y).astype(y_ref.dtype)


def _s5_pack_body(n_batch, h_ref, perm_ref, x_ref):
    t_len = h_ref.shape[1]
    xcat = jnp.concatenate([h_ref[:, t, :].astype(BF16) for t in range(t_len)], axis=1)
    x = _dot(xcat, perm_ref[...]).astype(x_ref.dtype)
    x_ref[...] = jnp.where(pl.program_id(0) < n_batch, x, jnp.zeros_like(x))


def _s5_unpack_body(y_ref, perm_ref, o_ref):
    t_len = o_ref.shape[1]
    res = _dot(y_ref[...], perm_ref[...])
    for t in range(t_len):
        o_ref[:, t, :] = res[:, t * LANES:(t + 1) * LANES]


def _s5_chunk_operators(lam_re, lam_im, log_step, b_re, b_im, c_re, c_im, d_skip):
    hi = lax.Precision.HIGHEST
    t_len = S5_CHUNK
    step = jnp.exp(log_step.astype(F32))[:, None]
    lr = jnp.minimum(lam_re.astype(F32), -1e-4)
    li = lam_im.astype(F32)
    mag = jnp.exp(lr * step)
    ar, ai = mag * jnp.cos(li * step), mag * jnp.sin(li * step)
    den = lr * lr + li * li
    cr = ((ar - 1.0) * lr + ai * li) / den
    ci = (ai * lr - (ar - 1.0) * li) / den
    br, bi = b_re.astype(F32), b_im.astype(F32)
    bbr = cr[..., None] * br - ci[..., None] * bi
    bbi = cr[..., None] * bi + ci[..., None] * br
    k = jnp.arange(t_len + 1, dtype=F32)[:, None, None]
    pmag = jnp.exp(k * (lr * step))
    pr, pi = pmag * jnp.cos(k * (li * step)), pmag * jnp.sin(k * (li * step))
    abr = pr[:t_len, ..., None] * bbr - pi[:t_len, ..., None] * bbi
    abi = pr[:t_len, ..., None] * bbi + pi[:t_len, ..., None] * bbr
    cre, cim = c_re.astype(F32), c_im.astype(F32)
    kern = (jnp.einsum('gop,kgpc->kgoc', cre, abr, precision=hi)
            - jnp.einsum('gop,kgpc->kgoc', cim, abi, precision=hi))
    kern = kern.at[0].add(jax.vmap(jnp.diag)(d_skip.astype(F32)))
    sig = jnp.arange(t_len)
    lag = sig[None, :] - sig[:, None]
    m = jnp.where((lag >= 0)[..., None, None, None], kern[jnp.maximum(lag, 0)], 0.0)
    g, o, c = kern.shape[1:]
    m = jnp.transpose(m, (2, 0, 4, 1, 3)).reshape(g, t_len * c, t_len * o)
    rev = t_len - 1 - sig
    p = bbr.shape[1]
    bc = jnp.concatenate([jnp.transpose(abr[rev], (1, 0, 3, 2)).reshape(g, t_len * c, p),
                          jnp.transpose(abi[rev], (1, 0, 3, 2)).reshape(g, t_len * c, p)], axis=2)
    car = cre[None] * pr[1:, :, None, :] - cim[None] * pi[1:, :, None, :]
    cai = cre[None] * pi[1:, :, None, :] + cim[None] * pr[1:, :, None, :]
    cc_re = jnp.transpose(car, (1, 3, 0, 2)).reshape(g, p, t_len * o)
    cc_im = -jnp.transpose(cai, (1, 3, 0, 2)).reshape(g, p, t_len * o)
    return (m.astype(BF16), bc.astype(BF16), cc_re.astype(BF16), cc_im.astype(BF16),
            pr[t_len].reshape(g * p), pi[t_len].reshape(g * p))


def _s5_mixer(h3, lam_re, lam_im, log_step, b_re, b_im, c_re, c_im, d_skip):
    b, lp, d = h3.shape
    t_len, g, c = S5_CHUNK, S5_GROUPS, S5_GROUP
    assert lp % t_len == 0 and b <= S5_SUBLANES
    bp = 4 if b <= 4 else S5_SUBLANES
    nj = lp // t_len
    m, bc, cc_re, cc_im, at_re, at_im = _s5_chunk_operators(lam_re, lam_im, log_step, b_re, b_im, c_re, c_im, d_skip)
    n_state = g * S5_STATE
    at_re = jnp.broadcast_to(at_re.reshape(1, n_state), (S5_SUBLANES, n_state))
    at_im = jnp.broadcast_to(at_im.reshape(1, n_state), (S5_SUBLANES, n_state))
    rows = nj * bp
    gb = S5_GROUP_BLOCK
    kw = t_len * c
    assert gb * c == LANES
    n_blk = g // gb
    src = jnp.arange(gb * kw)
    tau, r, ch = src // (gb * c), (src // c) % gb, src % c
    perm = jax.nn.one_hot(r * kw + tau * c + ch, gb * kw, dtype=BF16)
    x = pl.pallas_call(
        functools.partial(_s5_pack_body, b),
        grid=(bp, n_blk),
        in_specs=[pl.BlockSpec((None, nj, t_len, LANES), lambda bi, mi: (jnp.minimum(bi, b - 1), 0, 0, mi)),
                  pl.BlockSpec(perm.shape, lambda bi, mi: (0, 0))],
        out_specs=pl.BlockSpec((nj, gb * kw), lambda bi, mi: (0, bi * n_blk + mi)),
        out_shape=jax.ShapeDtypeStruct((nj, bp * g * kw), BF16),
        name="s5_pack",
        compiler_params=pltpu.CompilerParams(dimension_semantics=("parallel", "parallel"),
                                             vmem_limit_bytes=VMEM_LIMIT),
    )(h3.reshape(b, nj, t_len, d), perm).reshape(rows, g * kw)
    y = pl.pallas_call(
        functools.partial(_s5_body, bp), grid=(g // gb,),
        in_specs=[pl.BlockSpec((rows, gb * kw), lambda i: (0, i)),
                  pl.BlockSpec((gb, kw, kw), lambda i: (i, 0, 0)),
                  pl.BlockSpec((gb, kw, 2 * S5_STATE), lambda i: (i, 0, 0)),
                  pl.BlockSpec((gb, S5_STATE, kw), lambda i: (i, 0, 0)),
                  pl.BlockSpec((gb, S5_STATE, kw), lambda i: (i, 0, 0)),
                  pl.BlockSpec((S5_SUBLANES, gb * S5_STATE), lambda i: (0, i)),
                  pl.BlockSpec((S5_SUBLANES, gb * S5_STATE), lambda i: (0, i))],
        out_specs=pl.BlockSpec((rows, gb * kw), lambda i: (0, i)),
        out_shape=jax.ShapeDtypeStruct((rows, g * kw), BF16),
        scratch_shapes=[pltpu.VMEM((rows, gb * S5_STATE), F32), pltpu.VMEM((rows, gb * S5_STATE), F32)],
        name="s5_chunks",
        compiler_params=pltpu.CompilerParams(dimension_semantics=("parallel",), vmem_limit_bytes=VMEM_LIMIT),
    )(x, m, bc, cc_re, cc_im, at_re, at_im)
    y = pl.pallas_call(
        _s5_unpack_body,
        grid=(b, n_blk),
        in_specs=[pl.BlockSpec((nj, gb * kw), lambda bi, mi: (0, bi * n_blk + mi)),
                  pl.BlockSpec(perm.shape, lambda bi, mi: (0, 0))],
        out_specs=pl.BlockSpec((None, nj, t_len, LANES), lambda bi, mi: (bi, 0, 0, mi)),
        out_shape=jax.ShapeDtypeStruct((b, nj, t_len, d), F32),
        name="s5_unpack",
        compiler_params=pltpu.CompilerParams(dimension_semantics=("parallel", "parallel"),
                                             vmem_limit_bytes=VMEM_LIMIT),
    )(y.reshape(nj, bp * g * kw), perm.T)
    return y.reshape(b * lp, d)


def _rope_block(x, cm, s1, s2):
    return (x * cm + pltpu.roll(x, LANES - MLA_ROPE // 2, axis=1) * s1 + pltpu.roll(x, MLA_ROPE // 2, axis=1) * s2)


def _mla_in_body(h_ref, qcm_ref, qs1_ref, qs2_ref, kcm_ref, ks1_ref, ks2_ref,
                 win_ref, qg_ref, kvg_ref, wuq_ref, wuk_ref, wuv_ref, q_ref, k_ref, vt_ref):
    c = _dot(h_ref[...].astype(BF16), win_ref[...])
    c_q = _rms(c[:, :MLA_Q_RANK], qg_ref[...]).astype(BF16)
    c_kv = _rms(c[:, MLA_Q_RANK:MLA_Q_RANK + MLA_KV_RANK], kvg_ref[...]).astype(BF16)
    k_r = _rope_block(c[:, MLA_Q_RANK + MLA_KV_RANK:], kcm_ref[...], ks1_ref[...], ks2_ref[...])
    q = _dot(c_q, wuq_ref[...])
    k = _dot(c_kv, wuk_ref[...])
    qcm, qs1, qs2 = qcm_ref[...], qs1_ref[...], qs2_ref[...]
    for h in range(MLA_HEADS):
        sl = slice(h * LANES, (h + 1) * LANES)
        q_ref[:, sl] = _rope_block(q[:, sl], qcm, qs1, qs2).astype(q_ref.dtype)
        k_ref[:, sl] = (k[:, sl] + k_r).astype(k_ref.dtype)
    vt_ref[...] = _dot(c_kv, wuv_ref[...]).T.astype(vt_ref.dtype)


def _mla_attn_body(tq, nq, q_ref, k_ref, vt_ref, mask_ref, o_ref,
                   m_all, l_all, acc_all, m_scr, l_scr, acc_scr, s0_scr, s1_scr, p0_scr, p1_scr, mx0_scr, mx1_scr,
                   exact_ref):
    def scores(kj, qi):
        k0 = pl.multiple_of(kj * tq, tq)
        q0 = pl.multiple_of(qi * tq, tq)
        mask = mask_ref[jnp.minimum(qi - kj, 1)]
        return jnp.concatenate(
            [_dot_nt(k_ref[pl.ds(k0, tq), hh * LANES:(hh + 1) * LANES],
                     q_ref[pl.ds(q0, tq), hh * LANES:(hh + 1) * LANES]) + mask for hh in range(2)],
            axis=1)

    row = lax.broadcasted_iota(jnp.int32, (2 * MLA_V, tq), 0)

    def finish(qi, l, acc):
        a = acc * (1.0 / l)
        o_t = jnp.where(row < MLA_V, a[:, :tq], a[:, tq:])
        o_ref[pl.ds(pl.multiple_of(qi * tq, tq), tq), :] = o_t.T.astype(o_ref.dtype)

    _flash_head(tq, nq, scores, vt_ref, finish, m_all, l_all, acc_all, m_scr, l_scr, acc_scr,
                (s0_scr, s1_scr), (p0_scr, p1_scr), (mx0_scr, mx1_scr), exact_ref)


def _mla_mixer(h3, w_in, q_norm_g, kv_norm_g, w_uq, w_ukv):
    b, lp, d = h3.shape
    n = b * lp
    nh = MLA_HEADS
    qk = MLA_NOPE + MLA_ROPE
    half = MLA_ROPE // 2
    w_kr = jnp.pad(w_in[:, MLA_Q_RANK + MLA_KV_RANK:], ((0, 0), (MLA_NOPE, LANES - qk)))
    w_in_all = jnp.concatenate([w_in[:, :MLA_Q_RANK + MLA_KV_RANK], w_kr], axis=1).astype(BF16)
    w_uq_p = jnp.pad(w_uq.reshape(MLA_Q_RANK, nh, qk), ((0, 0), (0, 0), (0, LANES - qk)))
    w_uq_p = w_uq_p.reshape(MLA_Q_RANK, nh * LANES).astype(BF16)
    w_ukv3 = w_ukv.reshape(MLA_KV_RANK, nh, MLA_NOPE + MLA_V)
    w_uk_p = jnp.pad(w_ukv3[:, :, :MLA_NOPE], ((0, 0), (0, 0), (0, LANES - MLA_NOPE)))
    w_uk_p = w_uk_p.reshape(MLA_KV_RANK, nh * LANES).astype(BF16)
    w_uv = w_ukv3[:, :, MLA_NOPE:].reshape(MLA_KV_RANK, nh * MLA_V).astype(BF16)
    pos = jnp.arange(lp, dtype=F32)
    inv_freq = ROPE_BASE ** (-jnp.arange(0, MLA_ROPE, 2, dtype=F32) / MLA_ROPE)
    ang = pos[:, None] * inv_freq[None, :]
    cos, sin = jnp.cos(ang), jnp.sin(ang)
    z = lambda w: jnp.zeros((lp, w), F32)
    cm = jnp.concatenate([jnp.ones((lp, MLA_NOPE), F32), cos, cos, z(LANES - qk)], axis=1)
    s1 = jnp.concatenate([z(MLA_NOPE), -sin, z(LANES - MLA_NOPE - half)], axis=1)
    s2 = jnp.concatenate([z(MLA_NOPE + half), sin, z(LANES - qk)], axis=1)
    scale = qk ** -0.5 * LOG2E
    tq = _pick_tile(lp, (384, 256, 128))
    nk = lp // tq
    q, k, vt = _row_call(
        _mla_in_body, [h3.reshape(n, d)],
        [w_in_all, q_norm_g.reshape(1, MLA_Q_RANK), kv_norm_g.reshape(1, MLA_KV_RANK), w_uq_p, w_uk_p, w_uv],
        [(nh * LANES, BF16), (nh * LANES, BF16)], tq, "mla_in",
        pos_ins=[cm * scale, s1 * scale, s2 * scale, cm, s1, s2], lp=lp, t_outs=[(nh * MLA_V, BF16)])
    q, k = q.reshape(b, lp, nh * LANES), k.reshape(b, lp, nh * LANES)
    ti = jnp.arange(tq)
    mask = jnp.stack([jnp.where(ti[:, None] <= ti[None, :], 0.0, NEG_INF), jnp.zeros((tq, tq))]).astype(F32)
    o = pl.pallas_call(
        functools.partial(_mla_attn_body, tq, nk),
        grid=(b, nh // 2),
        in_specs=[pl.BlockSpec((None, lp, 2 * LANES), lambda bi, hi: (bi, 0, hi)),
                  pl.BlockSpec((None, lp, 2 * LANES), lambda bi, hi: (bi, 0, hi)),
                  pl.BlockSpec((None, nk, 2 * MLA_V, tq), lambda bi, hi: (bi, 0, hi, 0)),
                  pl.BlockSpec((2, tq, tq), lambda bi, hi: (0, 0, 0))],
        out_specs=pl.BlockSpec((None, lp, 2 * MLA_V), lambda bi, hi: (bi, 0, hi)),
        out_shape=jax.ShapeDtypeStruct((b, lp, nh * MLA_V), BF16),
        scratch_shapes=_flash_scratch(2 * MLA_V, tq, nk),
        name="mla_attn",
        compiler_params=pltpu.CompilerParams(dimension_semantics=("parallel", "parallel"),
                                             vmem_limit_bytes=VMEM_LIMIT),
    )(q, k, vt, mask)
    return o.reshape(n, nh * MLA_V)


def kernel(x, meta, rel_bias, ln_g, ln_b, ffn_w1, ffn_w3, ffn_w2, ssd_w_in, ssd_conv_w, ssd_conv_b, ssd_dt_bias, ssd_a_log, ssd_d, ssd_norm_g, ssd_w_out, diff_w_qkv, diff_lam_q1, diff_lam_k1, diff_lam_q2, diff_lam_k2, diff_subln_g, diff_w_out, s5_lam_re, s5_lam_im, s5_log_step, s5_b_re, s5_b_im, s5_c_re, s5_c_im, s5_d, s5_w_glu, s5_b_glu, mla_w_in, mla_q_norm_g, mla_kv_norm_g, mla_w_uq, mla_w_ukv, mla_w_out):
    b, seq, d = x.shape
    l = seq + N_META
    lp = -(-l // SEQ_ALIGN) * SEQ_ALIGN
    n = b * lp
    h = jnp.concatenate([jnp.broadcast_to(meta[None].astype(x.dtype), (b, N_META, d)), x,
                         jnp.zeros((b, lp - l, d), x.dtype)], axis=1).reshape(n, d)
    w1b, w3b, w2b = ffn_w1.astype(BF16), ffn_w3.astype(BF16), ffn_w2.astype(BF16)
    for i in range(DEPTH):
        kind, j = i % 4, i // 4
        h = _ffn(h, w1b, w3b, w2b, ln_g[i, 0], ln_b[i, 0], (i, 0))
        h3 = h.reshape(b, lp, d)
        if kind == 0:
            y = _ssd_mixer(h3, ssd_w_in[j], ssd_conv_w[j], ssd_conv_b[j], ssd_dt_bias[j], ssd_a_log[j],
                           ssd_d[j], ssd_norm_g[j])
            w_out, b_out = ssd_w_out[j], None
        elif kind == 1:
            lam_init = 0.8 - 0.6 * math.exp(-0.3 * i)
            y = _diff_mixer(h3, diff_w_qkv[j], diff_lam_q1[j], diff_lam_k1[j], diff_lam_q2[j], diff_lam_k2[j],
                            diff_subln_g[j], rel_bias, lam_init)
            w_out, b_out = diff_w_out[j], None
        elif kind == 2:
            y = _s5_mixer(h3, s5_lam_re[j], s5_lam_im[j], s5_log_step[j], s5_b_re[j], s5_b_im[j], s5_c_re[j],
                          s5_c_im[j], s5_d[j])
            w_out, b_out = s5_w_glu[j], s5_b_glu[j]
        else:
            y = _mla_mixer(h3, mla_w_in[j], mla_q_norm_g[j], mla_kv_norm_g[j], mla_w_uq[j], mla_w_ukv[j])
            w_out, b_out = mla_w_out[j], None
        h = _mix_ffn(y, h, w_out, b_out, ln_g[i, 1], ln_b[i, 1], w1b, w3b, w2b, ln_g[i, 2], ln_b[i, 2], (i, 1))
    return h.reshape(b, lp, d)[:, N_META:l]
```

```python
import functools
import math

import jax
import jax.numpy as jnp
from jax import lax
from jax.experimental import pallas as pl
from jax.experimental.pallas import tpu as pltpu

F32 = jnp.float32
BF16 = jnp.bfloat16

N_META = 16
DEPTH = 4
ALPHA = (2.0 * DEPTH) ** 0.25
LN_EPS = 1e-5
RMS_EPS = 1e-6
NEG_INF = -1e30
LOG2E = math.log2(math.e)
ATTN_REF_KEYS = 16
SOFTMAX_SUM_LIMIT = 2.0 ** 100
F32_FINITE_LIMIT = 3.0e38

SSD_HEAD_DIM = 64
SSD_HEADS = 32
SSD_GROUPS = 8
SSD_STATE = 128
SSD_CONV = 4
SSD_CHUNK = 128

DIFF_HEADS = 8
DIFF_HEAD_DIM = 64
DIFF_V_DIM = 128
REL_BUCKETS = 32
REL_MAX_DIST = 128

S5_GROUP = 16
S5_GROUPS = 64
S5_STATE = 64
S5_CHUNK = 16
S5_GROUP_BLOCK = 8
SUBLANES = 8
S5_SUBLANES = SUBLANES
S5_UNROLL = 4

MLA_HEADS = 16
MLA_Q_RANK = 384
MLA_KV_RANK = 256
MLA_NOPE = 64
MLA_ROPE = 32
MLA_V = 64
ROPE_BASE = 10000.0

FFN_ROW_TILES = (512, 384, 256, 128)
FFN_CHUNKS = 11

LANES = 128
SEQ_ALIGN = 128
VMEM_LIMIT = 56 * 1024 * 1024


def _pick_tile(n, candidates):
    for c in candidates:
        if n % c == 0:
            return c
    raise ValueError(f"no tile for {n}")


def _layer_norm(z, g, b):
    mu = jnp.mean(z, axis=-1, keepdims=True)
    d = z - mu
    var = jnp.mean(d * d, axis=-1, keepdims=True)
    return d * lax.rsqrt(var + LN_EPS) * g + b


def _rms(z, g):
    return z * lax.rsqrt(jnp.mean(z * z, axis=-1, keepdims=True) + RMS_EPS) * g


def _silu(x):
    return x * (1.0 / (1.0 + jnp.exp(-x)))


def _dot(a, b):
    return jnp.dot(a, b, preferred_element_type=F32)


def _dot_nt(a, b):
    return lax.dot_general(a, b, (((1,), (1,)), ((), ())), preferred_element_type=F32)


def _row_call(body, row_ins, const_ins, outs, tm, name, pos_ins=(), lp=None, t_outs=()):
    n = row_ins[0].shape[0]
    assert n % tm == 0
    in_specs = [pl.BlockSpec((tm, a.shape[1]), lambda i: (i, 0)) for a in row_ins]
    if pos_ins or t_outs:
        assert lp % tm == 0
        nb = lp // tm
        in_specs += [pl.BlockSpec((tm, a.shape[1]), lambda i: (i % nb, 0)) for a in pos_ins]
    const_arrays = []
    for c in const_ins:
        arr, lead = c if isinstance(c, tuple) else (c, ())
        const_arrays.append(arr)
        in_specs.append(pl.BlockSpec((None,) * len(lead) + arr.shape[len(lead):],
                                     functools.partial(lambda lead_, i: lead_ + (0, 0), tuple(lead)),
                                     pipeline_mode=pl.Buffered(1)))
    out_specs = [pl.BlockSpec((tm, w), lambda i: (i, 0)) for (w, _) in outs]
    out_shape = [jax.ShapeDtypeStruct((n, w), dt) for (w, dt) in outs]
    out_specs += [pl.BlockSpec((None, None, w, tm), lambda i: (i // nb, i % nb, 0, 0)) for (w, _) in t_outs]
    out_shape += [jax.ShapeDtypeStruct((n // lp, nb, w, tm), dt) for (w, dt) in t_outs]
    res = pl.pallas_call(
        body, grid=(n // tm,), in_specs=in_specs, out_specs=out_specs, out_shape=out_shape, name=name,
        compiler_params=pltpu.CompilerParams(dimension_semantics=("parallel",), vmem_limit_bytes=VMEM_LIMIT),
    )(*row_ins, *pos_ins, *const_arrays)
    return res


def _ffn_rows(x, n_chunks, w1_ref, w3_ref, w2_ref, g_ref, b_ref):
    xb = x.astype(BF16)
    fc = w1_ref.shape[1] // n_chunks
    y = None
    for c in range(n_chunks):
        sl = slice(c * fc, (c + 1) * fc)
        gate = _dot(xb, w1_ref[:, sl])
        up = _dot(xb, w3_ref[:, sl])
        a = (_silu(gate) * up).astype(BF16)
        part = _dot(a, w2_ref[sl, :])
        y = part if y is None else y + part
    return _layer_norm(ALPHA * x + 0.5 * y, g_ref[...], b_ref[...])


def _ffn_body(n_chunks, h_ref, w1_ref, w3_ref, w2_ref, g_ref, b_ref, o_ref):
    o_ref[...] = _ffn_rows(h_ref[...], n_chunks, w1_ref, w3_ref, w2_ref, g_ref, b_ref)


def _mix_ffn_body(n_chunks, glu, y_ref, h_ref, wo_ref, bo_ref, gm_ref, bm_ref,
                  w1_ref, w3_ref, w2_ref, g_ref, b_ref, o_ref):
    m = _dot(y_ref[...].astype(BF16), wo_ref[...])
    if glu:
        m = m + bo_ref[...]
        d = o_ref.shape[1]
        m = m[:, :d] * (1.0 / (1.0 + jnp.exp(-m[:, d:])))
    x = _layer_norm(ALPHA * h_ref[...] + m, gm_ref[...], bm_ref[...])
    o_ref[...] = _ffn_rows(x, n_chunks, w1_ref, w3_ref, w2_ref, g_ref, b_ref)


def _ffn_chunks(f, n_chunks):
    return n_chunks or (FFN_CHUNKS if f % (FFN_CHUNKS * LANES) == 0 else 1)


def _ffn(h2, w1, w3, w2, g, b, idx, tm=None, n_chunks=None, name="ffn"):
    n, d = h2.shape
    tm = tm or _pick_tile(n, FFN_ROW_TILES)
    (out,) = _row_call(functools.partial(_ffn_body, _ffn_chunks(w1.shape[-1], n_chunks)), [h2],
                       [(w1, idx), (w3, idx), (w2, idx), g.reshape(1, d), b.reshape(1, d)],
                       [(d, F32)], tm, name)
    return out


def _mix_ffn(y2, h2, w_out, b_out, g_mix, b_mix, w1, w3, w2, g, b, idx, tm=None, n_chunks=None, name="mix_ffn"):
    n, d = h2.shape
    tm = tm or _pick_tile(n, FFN_ROW_TILES)
    glu = b_out is not None
    wo = w_out.astype(BF16)
    bo = (b_out if glu else jnp.zeros((wo.shape[1],), F32)).reshape(1, wo.shape[1])
    (out,) = _row_call(functools.partial(_mix_ffn_body, _ffn_chunks(w1.shape[-1], n_chunks), glu), [y2, h2],
                       [wo, bo, g_mix.reshape(1, d), b_mix.reshape(1, d),
                        (w1, idx), (w3, idx), (w2, idx), g.reshape(1, d), b.reshape(1, d)],
                       [(d, F32)], tm, name)
    return out


def _ssd_in_body(h_ref, w_ref, xbc_ref, z_ref, dt_ref):
    r = _dot(h_ref[...].astype(BF16), w_ref[...])
    nx, nz = xbc_ref.shape[1], z_ref.shape[1]
    xbc_ref[...] = r[:, :nx]
    z_ref[...] = r[:, nx:nx + nz]
    dt_ref[...] = r[:, nx + nz:]


def _softplus(x):
    return jnp.maximum(x, 0.0) + jnp.log1p(jnp.exp(-jnp.abs(x)))


def _ssd_body(xbc_ref, dt_ref, z_ref, convw_ref, convb_ref, dtb_ref, alog_ref, dskip_ref, normg_ref,
              y_ref, buf_scr, state_scr, y_scr):
    t = SSD_CHUNK
    d_inner = SSD_HEADS * SSD_HEAD_DIM
    gn = SSD_GROUPS * SSD_STATE
    hp = SSD_HEAD_DIM
    r = SSD_HEADS // SSD_GROUPS

    @pl.when(pl.program_id(1) == 0)
    def _():
        buf_scr[0:SUBLANES, :] = jnp.zeros((SUBLANES, buf_scr.shape[1]), F32)
        state_scr[...] = jnp.zeros(state_scr.shape, F32)

    x = xbc_ref[...]
    buf_scr[SUBLANES:SUBLANES + t, :] = x
    acc = convb_ref[...] + convw_ref[SSD_CONV - 1:SSD_CONV, :] * x
    for j in range(1, SSD_CONV):
        acc = acc + convw_ref[SSD_CONV - 1 - j:SSD_CONV - j, :] * buf_scr[SUBLANES - j:SUBLANES - j + t, :]
    buf_scr[0:SUBLANES, :] = x[t - SUBLANES:t, :]
    xbc = _silu(acc)

    dt = _softplus(dt_ref[...] + dtb_ref[...])
    a = dt * (-jnp.exp(alog_ref[...]))
    rows = lax.broadcasted_iota(jnp.int32, (t, t), 0)
    cols = lax.broadcasted_iota(jnp.int32, (t, t), 1)
    tri = rows >= cols
    a_cum = jnp.dot(tri.astype(F32), a, precision=lax.Precision.HIGHEST, preferred_element_type=F32)
    a_last = a_cum[t - 1:t, :]
    a_cum_t = a_cum.T
    dt_t = dt.T
    w_end_t = (dt * jnp.exp(a_last - a_cum)).T
    chunk_dec = jnp.exp(a_last)

    for g in range(SSD_GROUPS):
        b_f = xbc[:, d_inner + g * SSD_STATE:d_inner + (g + 1) * SSD_STATE]
        c_g = xbc[:, d_inner + gn + g * SSD_STATE:d_inner + gn + (g + 1) * SSD_STATE].astype(BF16)
        b_g = b_f.astype(BF16)
        b_ft = b_f.T
        cb = _dot_nt(c_g, b_g)
        y_off_g = _dot(c_g, state_scr[:, g * r * hp:(g + 1) * r * hp].astype(BF16))
        for k in range(r):
            h = g * r + k
            col = jnp.broadcast_to(a_cum[:, h:h + 1], (t, t))
            decay = jnp.exp(jnp.where(tri, col - a_cum_t[h:h + 1, :], NEG_INF))
            xh = xbc[:, h * hp:(h + 1) * hp]
            xb = xh.astype(BF16)
            y_diag = _dot((cb * decay * dt_t[h:h + 1, :]).astype(BF16), xb)
            y_off = y_off_g[:, k * hp:(k + 1) * hp] * jnp.exp(col[:, :hp])
            y_scr[:, h * hp:(h + 1) * hp] = y_diag + y_off + dskip_ref[:, h * hp:(h + 1) * hp] * xh
            st = state_scr[:, h * hp:(h + 1) * hp]
            b_w = (b_ft * w_end_t[h:h + 1, :]).astype(BF16)
            state_scr[:, h * hp:(h + 1) * hp] = st * chunk_dec[:, h:h + 1] + _dot(b_w, xb)

    y = y_scr[...] * _silu(z_ref[...])
    gs = d_inner // SSD_GROUPS
    for g in range(SSD_GROUPS):
        sl = slice(g * gs, (g + 1) * gs)
        y_ref[:, sl] = _rms(y[:, sl], normg_ref[:, sl]).astype(y_ref.dtype)


def _ssd_mixer(h3, w_in, conv_w, conv_b, dt_bias, a_log, d_skip, norm_g):
    b, lp, d = h3.shape
    n = b * lp
    d_inner = SSD_HEADS * SSD_HEAD_DIM
    conv_ch = d_inner + 2 * SSD_GROUPS * SSD_STATE
    w_z, w_xbc, w_dt = w_in[:, :d_inner], w_in[:, d_inner:d_inner + conv_ch], w_in[:, d_inner + conv_ch:]
    w_dt = jnp.pad(w_dt, ((0, 0), (0, LANES - SSD_HEADS)))
    w_all = jnp.concatenate([w_xbc, w_z, w_dt], axis=1).astype(BF16)
    tm = _pick_tile(n, (512, 384, 256, 128))
    xbc, z, dt = _row_call(_ssd_in_body, [h3.reshape(n, d)], [w_all],
                           [(conv_ch, F32), (d_inner, F32), (LANES, F32)], tm, "ssd_in")
    t = SSD_CHUNK
    pad1 = lambda v: jnp.pad(v, (0, LANES - SSD_HEADS)).reshape(1, LANES)
    consts = [conv_w, conv_b.reshape(1, conv_ch), pad1(dt_bias), pad1(a_log),
              jnp.repeat(d_skip, SSD_HEAD_DIM).reshape(1, d_inner), norm_g.reshape(1, d_inner)]
    row_spec = lambda w: pl.BlockSpec((None, t, w), lambda i, c: (i, c, 0))
    y = pl.pallas_call(
        _ssd_body, grid=(b, lp // t),
        in_specs=[row_spec(conv_ch), row_spec(LANES), row_spec(d_inner)]
        + [pl.BlockSpec(c.shape, lambda i, c_: (0, 0)) for c in consts],
        out_specs=row_spec(d_inner),
        out_shape=jax.ShapeDtypeStruct((b, lp, d_inner), BF16),
        scratch_shapes=[pltpu.VMEM((SUBLANES + t, conv_ch), F32), pltpu.VMEM((SSD_STATE, d_inner), F32),
                        pltpu.VMEM((t, d_inner), F32)],
        name="ssd_scan",
        compiler_params=pltpu.CompilerParams(dimension_semantics=("parallel", "arbitrary"),
                                             vmem_limit_bytes=VMEM_LIMIT),
    )(xbc.reshape(b, lp, conv_ch), dt.reshape(b, lp, LANES), z.reshape(b, lp, d_inner), *consts)
    return y.reshape(n, d_inner)


def _qkv_body(n_q, scale, h_ref, w_ref, qk_ref, vt_ref):
    r = _dot(h_ref[...].astype(BF16), w_ref[...])
    n_qk = qk_ref.shape[1]
    qk_ref[:, :n_q] = (r[:, :n_q] * scale).astype(qk_ref.dtype)
    qk_ref[:, n_q:] = r[:, n_q:n_qk].astype(qk_ref.dtype)
    vt_ref[...] = r[:, n_qk:].T.astype(vt_ref.dtype)


def _causal_mask_pair(s, tq):
    rows = lax.broadcasted_iota(jnp.int32, s.shape, 0)
    cols = lax.broadcasted_iota(jnp.int32, s.shape, 1)
    cols = jnp.where(cols >= tq, cols - tq, cols)
    return jnp.where(rows <= cols, s, NEG_INF)


def _flash_causal(qi, tq, scores, vt_ref, m_scr, l_scr, acc_scr, s_scrs, p_scrs, mx_scrs):
    m_scr[...] = jnp.full(m_scr.shape, NEG_INF, F32)
    l_scr[...] = jnp.zeros(l_scr.shape, F32)
    acc_scr[...] = jnp.zeros(acc_scr.shape, F32)
    p_scrs[1][...] = jnp.zeros(p_scrs[1].shape, p_scrs[1].dtype)

    def softmax(s, s_max, pv):
        m_prev = m_scr[...]
        m_new = jnp.maximum(m_prev, s_max)
        alpha = jnp.exp2(m_prev - m_new)
        p = jnp.exp2(s - m_new)
        l_scr[...] = alpha * l_scr[...] + jnp.sum(p, axis=0, keepdims=True)
        m_scr[...] = m_new
        acc_scr[...] = alpha * (acc_scr[...] + pv)
        return p.astype(BF16)

    def produce(kj, slot):
        s = scores(kj)
        s_scrs[slot][...] = s
        mx_scrs[slot][...] = jnp.max(s, axis=0, keepdims=True)

    def step(kj, cur, nxt):
        pv = _dot(vt_ref[jnp.maximum(kj - 1, 0)], p_scrs[nxt][...])
        p_scrs[cur][...] = softmax(s_scrs[cur][...], mx_scrs[cur][...], pv)
        produce(kj + 1, nxt)

    produce(0, 0)

    def body(t, carry):
        step(2 * t, 0, 1)
        step(2 * t + 1, 1, 0)
        return carry

    lax.fori_loop(0, lax.shift_right_logical(qi, 1), body, 0)
    pl.when((qi & 1) == 1)(functools.partial(step, qi - 1, 0, 1))

    def last(cur, nxt):
        pv = _dot(vt_ref[jnp.maximum(qi - 1, 0)], p_scrs[nxt][...])
        s = _causal_mask_pair(s_scrs[cur][...], tq)
        p = softmax(s, jnp.max(s, axis=0, keepdims=True), pv)
        acc_scr[...] = acc_scr[...] + _dot(vt_ref[qi], p)

    for par in range(2):
        pl.when((qi & 1) == par)(functools.partial(last, par, 1 - par))


def _flash_head(tq, nq, scores, vt_ref, finish, m_all, l_all, acc_all,
                m_scr, l_scr, acc_scr, s_scrs, p_scrs, mx_scrs, exact_ref):
    def init(qi, carry):
        m_all[qi] = jnp.max(scores(0, qi, ATTN_REF_KEYS), axis=0, keepdims=True)
        l_all[qi] = jnp.zeros(l_all.shape[1:], F32)
        acc_all[qi] = jnp.zeros(acc_all.shape[1:], F32)
        return carry

    lax.fori_loop(0, nq, init, 0)

    def probs(kj, qi, slot):
        p = jnp.exp2(scores(kj, qi) - m_all[qi])
        l_all[qi] += jnp.sum(p, axis=0, keepdims=True)
        p_scrs[slot][...] = p.astype(BF16)

    def step(prev, cur, prv):
        kjp, qip = prev
        wrap = qip + 1 >= nq
        kj = jnp.where(wrap, kjp + 1, kjp)
        qi = jnp.where(wrap, kjp + 1, qip + 1)
        p = jnp.exp2(scores(kj, qi) - m_all[qi])
        acc_all[qip] += _dot(vt_ref[kjp], p_scrs[prv][...])
        l_all[qi] += jnp.sum(p, axis=0, keepdims=True)
        p_scrs[cur][...] = p.astype(BF16)
        return kj, qi

    n_steps = nq * (nq + 1) // 2 - 1
    probs(0, 0, 0)

    def body(t, prev):
        return step(step(prev, 1, 0), 0, 1)

    prev = lax.fori_loop(0, n_steps // 2, body, (jnp.int32(0), jnp.int32(0)))
    if n_steps % 2:
        prev = step(prev, 1, 0)
    acc_all[prev[1]] += _dot(vt_ref[prev[0]], p_scrs[n_steps % 2][...])

    ok = jnp.all(l_all[...] < SOFTMAX_SUM_LIMIT) & jnp.all(jnp.abs(acc_all[...]) < F32_FINITE_LIMIT)
    exact_ref[0] = jnp.where(ok, 0, 1)

    @pl.when(exact_ref[0] == 0)
    def _():
        def fin(qi, carry):
            finish(qi, l_all[qi], acc_all[qi])
            return carry

        lax.fori_loop(0, nq, fin, 0)

    @pl.when(exact_ref[0] == 1)
    def _():
        def redo(qi, carry):
            _flash_causal(qi, tq, lambda kj: scores(kj, qi), vt_ref, m_scr, l_scr, acc_scr, s_scrs, p_scrs, mx_scrs)
            finish(qi, l_scr[...], acc_scr[...])
            return carry

        lax.fori_loop(0, nq, redo, 0)


def _flash_scratch(e, tq, nq):
    w = 2 * tq
    return [pltpu.VMEM((nq, 1, w), F32), pltpu.VMEM((nq, 1, w), F32), pltpu.VMEM((nq, e, w), F32),
            pltpu.VMEM((1, w), F32), pltpu.VMEM((1, w), F32), pltpu.VMEM((e, w), F32),
            pltpu.VMEM((tq, w), F32), pltpu.VMEM((tq, w), F32), pltpu.VMEM((tq, w), BF16), pltpu.VMEM((tq, w), BF16),
            pltpu.VMEM((1, w), F32), pltpu.VMEM((1, w), F32), pltpu.SMEM((1,), jnp.int32)]


def _t5_bucket(dist):
    max_exact = REL_BUCKETS // 2
    d = jnp.maximum(dist, 0)
    df = jnp.maximum(d, max_exact).astype(F32)
    large = max_exact + (jnp.log(df / max_exact) / math.log(REL_MAX_DIST / max_exact)
                         * (REL_BUCKETS - max_exact)).astype(jnp.int32)
    large = jnp.minimum(large, REL_BUCKETS - 1)
    return jnp.where(d < max_exact, d, large)


def _diff_attn_body(tq, nq, lam_init, q_ref, k_ref, vt_ref, bias_ref, lam_ref, g_ref, o_ref,
                    m_all, l_all, acc_all, m_scr, l_scr, acc_scr, s0_scr, s1_scr, p0_scr, p1_scr, mx0_scr, mx1_scr,
                    exact_ref, qcat_all):
    lane = lax.broadcasted_iota(jnp.int32, (tq, q_ref.shape[1]), 1)

    def prep(qi, carry):
        q = q_ref[pl.ds(pl.multiple_of(qi * tq, tq), tq), :]
        zero = jnp.zeros_like(q)
        qcat_all[qi, 0:tq, :] = jnp.where(lane < DIFF_HEAD_DIM, q, zero)
        qcat_all[qi, tq:2 * tq, :] = jnp.where(lane >= DIFF_HEAD_DIM, q, zero)
        return carry

    lax.fori_loop(0, nq, prep, 0)

    def scores(kj, qi, rows=tq):
        start = pl.multiple_of(kj * tq, tq)
        bias = bias_ref[jnp.minimum(qi - kj, 2), 0:rows, :]
        return _dot_nt(k_ref[pl.ds(start, rows), :], qcat_all[qi]) + jnp.concatenate([bias, bias], axis=1)

    def finish(qi, l, acc):
        a = acc * (1.0 / l)
        o_t = a[:, :tq] - lam_ref[...] * a[:, tq:]
        o_ref[pl.ds(pl.multiple_of(qi * tq, tq), tq), :] = (
            _rms(o_t.T, g_ref[...]) * (1.0 - lam_init)).astype(o_ref.dtype)

    _flash_head(tq, nq, scores, vt_ref, finish, m_all, l_all, acc_all, m_scr, l_scr, acc_scr,
                (s0_scr, s1_scr), (p0_scr, p1_scr), (mx0_scr, mx1_scr), exact_ref)


def _diff_mixer(h3, w_qkv, lam_q1, lam_k1, lam_q2, lam_k2, subln_g, rel_bias, lam_init):
    b, lp, d = h3.shape
    n = b * lp
    hd = DIFF_HEADS * 2 * DIFF_HEAD_DIM
    tq = _pick_tile(lp, (384, 256, 128))
    assert tq >= REL_MAX_DIST
    nk = lp // tq
    qk, vt = _row_call(functools.partial(_qkv_body, hd, DIFF_HEAD_DIM ** -0.5 * LOG2E), [h3.reshape(n, d)],
                       [w_qkv.astype(BF16)], [(2 * hd, BF16)], tq, "diff_qkv", lp=lp, t_outs=[(hd, BF16)])
    qk = qk.reshape(b, lp, 2 * hd)
    m2 = 2 * tq
    kk = jnp.arange(m2)
    dist = jnp.where(kk < tq, kk, kk - m2)[None, :] + (jnp.arange(3) * tq)[:, None]
    tab = jnp.transpose(rel_bias[_t5_bucket(dist)].astype(F32) * LOG2E, (2, 0, 1))
    tab = jnp.where((jnp.arange(3)[:, None] == 0) & (kk[None, :] >= tq), NEG_INF, tab)
    nh = DIFF_HEADS
    bias = jnp.tile(tab, (1, 1, tq))[..., :tq * (m2 - 1)].reshape(nh, 3, tq, m2 - 1)[..., :tq]
    lam = (jnp.exp(jnp.sum(lam_q1.astype(F32) * lam_k1.astype(F32)))
           - jnp.exp(jnp.sum(lam_q2.astype(F32) * lam_k2.astype(F32))) + lam_init)
    w = 2 * DIFF_HEAD_DIM
    o = pl.pallas_call(
        functools.partial(_diff_attn_body, tq, nk, lam_init),
        grid=(b, nh),
        in_specs=[pl.BlockSpec((None, lp, w), lambda bi, hi: (bi, 0, hi)),
                  pl.BlockSpec((None, lp, w), lambda bi, hi: (bi, 0, nh + hi)),
                  pl.BlockSpec((None, nk, DIFF_V_DIM, tq), lambda bi, hi: (bi, 0, hi, 0)),
                  pl.BlockSpec((None, 3, tq, tq), lambda bi, hi: (hi, 0, 0, 0)),
                  pl.BlockSpec((1, 1), lambda bi, hi: (0, 0)),
                  pl.BlockSpec((1, DIFF_V_DIM), lambda bi, hi: (0, 0))],
        out_specs=pl.BlockSpec((None, lp, DIFF_V_DIM), lambda bi, hi: (bi, 0, hi)),
        out_shape=jax.ShapeDtypeStruct((b, lp, nh * DIFF_V_DIM), BF16),
        scratch_shapes=_flash_scratch(DIFF_V_DIM, tq, nk) + [pltpu.VMEM((nk, 2 * tq, w), BF16)],
        name="diff_attn",
        compiler_params=pltpu.CompilerParams(dimension_semantics=("parallel", "parallel"),
                                             vmem_limit_bytes=VMEM_LIMIT),
    )(qk, qk, vt, bias, lam.reshape(1, 1), subln_g.reshape(1, DIFF_V_DIM))
    return o.reshape(n, nh * DIFF_V_DIM)


def _gelu_tanh(x):
    return 0.5 * x * (1.0 + jnp.tanh(math.sqrt(2.0 / math.pi) * (x + 0.044715 * (x * x * x))))


def _s5_body(bp, x_ref, m_ref, bc_ref, ccre_ref, ccim_ref, are_ref, aim_ref, y_ref, sre_scr, sim_scr):
    rows = x_ref.shape[0]
    gb = m_ref.shape[0]
    kw = m_ref.shape[1]
    p = ccre_ref.shape[1]
    for g in range(gb):
        e = _dot(x_ref[:, g * kw:(g + 1) * kw], bc_ref[g])
        sre_scr[:, g * p:(g + 1) * p] = e[:, :p]
        sim_scr[:, g * p:(g + 1) * p] = e[:, p:]

    ar, ai = are_ref[...], aim_ref[...]
    sub = lax.broadcasted_iota(jnp.int32, ar.shape, 0)
    per_tile = S5_SUBLANES // bp

    def tile_step(r, carry):
        pr, pi = carry
        r0 = pl.multiple_of(r * S5_SUBLANES, S5_SUBLANES)
        er, ei = sre_scr[pl.ds(r0, S5_SUBLANES), :], sim_scr[pl.ds(r0, S5_SUBLANES), :]
        in_r, in_i = pr, pi
        for c in range(per_tile):
            nr = ar * pr - ai * pi + er
            ni = ar * pi + ai * pr + ei
            if per_tile > 1:
                nr = pltpu.roll(nr, bp, axis=0)
                ni = pltpu.roll(ni, bp, axis=0)
            if c + 1 < per_tile:
                sel = (sub >= (c + 1) * bp) & (sub < (c + 2) * bp)
                in_r, in_i = jnp.where(sel, nr, in_r), jnp.where(sel, ni, in_i)
                pr, pi = nr, ni
            else:
                pr, pi = nr, ni
        if per_tile > 1:
            pr = jnp.where(sub < bp, pr, pltpu.roll(pr, bp, axis=0))
            pi = jnp.where(sub < bp, pi, pltpu.roll(pi, bp, axis=0))
        sre_scr[pl.ds(r0, S5_SUBLANES), :] = in_r
        sim_scr[pl.ds(r0, S5_SUBLANES), :] = in_i
        return pr, pi

    zero = jnp.zeros(ar.shape, F32)
    lax.fori_loop(0, rows // S5_SUBLANES, tile_step, (zero, zero), unroll=S5_UNROLL)

    for g in range(gb):
        xg = x_ref[:, g * kw:(g + 1) * kw]
        y = (_dot(xg, m_ref[g]) + _dot(sre_scr[:, g * p:(g + 1) * p].astype(BF16), ccre_ref[g])
             + _dot(sim_scr[:, g * p:(g + 1) * p].astype(BF16), ccim_ref[g]))
        y_ref[:, g * kw:(g + 1) * kw] = _gelu_tanh(y).astype(y_ref.dtype)


def _s5_pack_body(n_batch, h_ref, perm_ref, x_ref):
    t_len = h_ref.shape[1]
    xcat = jnp.concatenate([h_ref[:, t, :].astype(BF16) for t in range(t_len)], axis=1)
    x = _dot(xcat, perm_ref[...]).astype(x_ref.dtype)
    x_ref[...] = jnp.where(pl.program_id(0) < n_batch, x, jnp.zeros_like(x))


def _s5_unpack_body(y_ref, perm_ref, o_ref):
    t_len = o_ref.shape[1]
    res = _dot(y_ref[...], perm_ref[...])
    for t in range(t_len):
        o_ref[:, t, :] = res[:, t * LANES:(t + 1) * LANES]


def _s5_chunk_operators(lam_re, lam_im, log_step, b_re, b_im, c_re, c_im, d_skip):
    hi = lax.Precision.HIGHEST
    t_len = S5_CHUNK
    step = jnp.exp(log_step.astype(F32))[:, None]
    lr = jnp.minimum(lam_re.astype(F32), -1e-4)
    li = lam_im.astype(F32)
    mag = jnp.exp(lr * step)
    ar, ai = mag * jnp.cos(li * step), mag * jnp.sin(li * step)
    den = lr * lr + li * li
    cr = ((ar - 1.0) * lr + ai * li) / den
    ci = (ai * lr - (ar - 1.0) * li) / den
    br, bi = b_re.astype(F32), b_im.astype(F32)
    bbr = cr[..., None] * br - ci[..., None] * bi
    bbi = cr[..., None] * bi + ci[..., None] * br
    k = jnp.arange(t_len + 1, dtype=F32)[:, None, None]
    pmag = jnp.exp(k * (lr * step))
    pr, pi = pmag * jnp.cos(k * (li * step)), pmag * jnp.sin(k * (li * step))
    abr = pr[:t_len, ..., None] * bbr - pi[:t_len, ..., None] * bbi
    abi = pr[:t_len, ..., None] * bbi + pi[:t_len, ..., None] * bbr
    cre, cim = c_re.astype(F32), c_im.astype(F32)
    kern = (jnp.einsum('gop,kgpc->kgoc', cre, abr, precision=hi)
            - jnp.einsum('gop,kgpc->kgoc', cim, abi, precision=hi))
    kern = kern.at[0].add(jax.vmap(jnp.diag)(d_skip.astype(F32)))
    sig = jnp.arange(t_len)
    lag = sig[None, :] - sig[:, None]
    m = jnp.where((lag >= 0)[..., None, None, None], kern[jnp.maximum(lag, 0)], 0.0)
    g, o, c = kern.shape[1:]
    m = jnp.transpose(m, (2, 0, 4, 1, 3)).reshape(g, t_len * c, t_len * o)
    rev = t_len - 1 - sig
    p = bbr.shape[1]
    bc = jnp.concatenate([jnp.transpose(abr[rev], (1, 0, 3, 2)).reshape(g, t_len * c, p),
                          jnp.transpose(abi[rev], (1, 0, 3, 2)).reshape(g, t_len * c, p)], axis=2)
    car = cre[None] * pr[1:, :, None, :] - cim[None] * pi[1:, :, None, :]
    cai = cre[None] * pi[1:, :, None, :] + cim[None] * pr[1:, :, None, :]
    cc_re = jnp.transpose(car, (1, 3, 0, 2)).reshape(g, p, t_len * o)
    cc_im = -jnp.transpose(cai, (1, 3, 0, 2)).reshape(g, p, t_len * o)
    return (m.astype(BF16), bc.astype(BF16), cc_re.astype(BF16), cc_im.astype(BF16),
            pr[t_len].reshape(g * p), pi[t_len].reshape(g * p))


def _s5_mixer(h3, lam_re, lam_im, log_step, b_re, b_im, c_re, c_im, d_skip):
    b, lp, d = h3.shape
    t_len, g, c = S5_CHUNK, S5_GROUPS, S5_GROUP
    assert lp % t_len == 0 and b <= S5_SUBLANES
    bp = 4 if b <= 4 else S5_SUBLANES
    nj = lp // t_len
    m, bc, cc_re, cc_im, at_re, at_im = _s5_chunk_operators(lam_re, lam_im, log_step, b_re, b_im, c_re, c_im, d_skip)
    n_state = g * S5_STATE
    at_re = jnp.broadcast_to(at_re.reshape(1, n_state), (S5_SUBLANES, n_state))
    at_im = jnp.broadcast_to(at_im.reshape(1, n_state), (S5_SUBLANES, n_state))
    rows = nj * bp
    gb = S5_GROUP_BLOCK
    kw = t_len * c
    assert gb * c == LANES
    n_blk = g // gb
    src = jnp.arange(gb * kw)
    tau, r, ch = src // (gb * c), (src // c) % gb, src % c
    perm = jax.nn.one_hot(r * kw + tau * c + ch, gb * kw, dtype=BF16)
    x = pl.pallas_call(
        functools.partial(_s5_pack_body, b),
        grid=(bp, n_blk),
        in_specs=[pl.BlockSpec((None, nj, t_len, LANES), lambda bi, mi: (jnp.minimum(bi, b - 1), 0, 0, mi)),
                  pl.BlockSpec(perm.shape, lambda bi, mi: (0, 0))],
        out_specs=pl.BlockSpec((nj, gb * kw), lambda bi, mi: (0, bi * n_blk + mi)),
        out_shape=jax.ShapeDtypeStruct((nj, bp * g * kw), BF16),
        name="s5_pack",
        compiler_params=pltpu.CompilerParams(dimension_semantics=("parallel", "parallel"),
                                             vmem_limit_bytes=VMEM_LIMIT),
    )(h3.reshape(b, nj, t_len, d), perm).reshape(rows, g * kw)
    y = pl.pallas_call(
        functools.partial(_s5_body, bp), grid=(g // gb,),
        in_specs=[pl.BlockSpec((rows, gb * kw), lambda i: (0, i)),
                  pl.BlockSpec((gb, kw, kw), lambda i: (i, 0, 0)),
                  pl.BlockSpec((gb, kw, 2 * S5_STATE), lambda i: (i, 0, 0)),
                  pl.BlockSpec((gb, S5_STATE, kw), lambda i: (i, 0, 0)),
                  pl.BlockSpec((gb, S5_STATE, kw), lambda i: (i, 0, 0)),
                  pl.BlockSpec((S5_SUBLANES, gb * S5_STATE), lambda i: (0, i)),
                  pl.BlockSpec((S5_SUBLANES, gb * S5_STATE), lambda i: (0, i))],
        out_specs=pl.BlockSpec((rows, gb * kw), lambda i: (0, i)),
        out_shape=jax.ShapeDtypeStruct((rows, g * kw), BF16),
        scratch_shapes=[pltpu.VMEM((rows, gb * S5_STATE), F32), pltpu.VMEM((rows, gb * S5_STATE), F32)],
        name="s5_chunks",
        compiler_params=pltpu.CompilerParams(dimension_semantics=("parallel",), vmem_limit_bytes=VMEM_LIMIT),
    )(x, m, bc, cc_re, cc_im, at_re, at_im)
    y = pl.pallas_call(
        _s5_unpack_body,
        grid=(b, n_blk),
        in_specs=[pl.BlockSpec((nj, gb * kw), lambda bi, mi: (0, bi * n_blk + mi)),
                  pl.BlockSpec(perm.shape, lambda bi, mi: (0, 0))],
        out_specs=pl.BlockSpec((None, nj, t_len, LANES), lambda bi, mi: (bi, 0, 0, mi)),
        out_shape=jax.ShapeDtypeStruct((b, nj, t_len, d), F32),
        name="s5_unpack",
        compiler_params=pltpu.CompilerParams(dimension_semantics=("parallel", "parallel"),
                                             vmem_limit_bytes=VMEM_LIMIT),
    )(y.reshape(nj, bp * g * kw), perm.T)
    return y.reshape(b * lp, d)


def _rope_block(x, cm, s1, s2):
    return (x * cm + pltpu.roll(x, LANES - MLA_ROPE // 2, axis=1) * s1 + pltpu.roll(x, MLA_ROPE // 2, axis=1) * s2)


def _mla_in_body(h_ref, qcm_ref, qs1_ref, qs2_ref, kcm_ref, ks1_ref, ks2_ref,
                 win_ref, qg_ref, kvg_ref, wuq_ref, wuk_ref, wuv_ref, q_ref, k_ref, vt_ref):
    c = _dot(h_ref[...].astype(BF16), win_ref[...])
    c_q = _rms(c[:, :MLA_Q_RANK], qg_ref[...]).astype(BF16)
    c_kv = _rms(c[:, MLA_Q_RANK:MLA_Q_RANK + MLA_KV_RANK], kvg_ref[...]).astype(BF16)
    k_r = _rope_block(c[:, MLA_Q_RANK + MLA_KV_RANK:], kcm_ref[...], ks1_ref[...], ks2_ref[...])
    q = _dot(c_q, wuq_ref[...])
    k = _dot(c_kv, wuk_ref[...])
    qcm, qs1, qs2 = qcm_ref[...], qs1_ref[...], qs2_ref[...]
    for h in range(MLA_HEADS):
        sl = slice(h * LANES, (h + 1) * LANES)
        q_ref[:, sl] = _rope_block(q[:, sl], qcm, qs1, qs2).astype(q_ref.dtype)
        k_ref[:, sl] = (k[:, sl] + k_r).astype(k_ref.dtype)
    vt_ref[...] = _dot(c_kv, wuv_ref[...]).T.astype(vt_ref.dtype)


def _mla_attn_body(tq, nq, q_ref, k_ref, vt_ref, mask_ref, o_ref,
                   m_all, l_all, acc_all, m_scr, l_scr, acc_scr, s0_scr, s1_scr, p0_scr, p1_scr, mx0_scr, mx1_scr,
                   exact_ref):
    def scores(kj, qi, rows=tq):
        k0 = pl.multiple_of(kj * tq, tq)
        q0 = pl.multiple_of(qi * tq, tq)
        mask = mask_ref[jnp.minimum(qi - kj, 1), 0:rows, :]
        return jnp.concatenate(
            [_dot_nt(k_ref[pl.ds(k0, rows), hh * LANES:(hh + 1) * LANES],
                     q_ref[pl.ds(q0, tq), hh * LANES:(hh + 1) * LANES]) + mask for hh in range(2)],
            axis=1)

    row = lax.broadcasted_iota(jnp.int32, (2 * MLA_V, tq), 0)

    def finish(qi, l, acc):
        a = acc * (1.0 / l)
        o_t = jnp.where(row < MLA_V, a[:, :tq], a[:, tq:])
        o_ref[pl.ds(pl.multiple_of(qi * tq, tq), tq), :] = o_t.T.astype(o_ref.dtype)

    _flash_head(tq, nq, scores, vt_ref, finish, m_all, l_all, acc_all, m_scr, l_scr, acc_scr,
                (s0_scr, s1_scr), (p0_scr, p1_scr), (mx0_scr, mx1_scr), exact_ref)


def _mla_mixer(h3, w_in, q_norm_g, kv_norm_g, w_uq, w_ukv):
    b, lp, d = h3.shape
    n = b * lp
    nh = MLA_HEADS
    qk = MLA_NOPE + MLA_ROPE
    half = MLA_ROPE // 2
    w_kr = jnp.pad(w_in[:, MLA_Q_RANK + MLA_KV_RANK:], ((0, 0), (MLA_NOPE, LANES - qk)))
    w_in_all = jnp.concatenate([w_in[:, :MLA_Q_RANK + MLA_KV_RANK], w_kr], axis=1).astype(BF16)
    w_uq_p = jnp.pad(w_uq.reshape(MLA_Q_RANK, nh, qk), ((0, 0), (0, 0), (0, LANES - qk)))
    w_uq_p = w_uq_p.reshape(MLA_Q_RANK, nh * LANES).astype(BF16)
    w_ukv3 = w_ukv.reshape(MLA_KV_RANK, nh, MLA_NOPE + MLA_V)
    w_uk_p = jnp.pad(w_ukv3[:, :, :MLA_NOPE], ((0, 0), (0, 0), (0, LANES - MLA_NOPE)))
    w_uk_p = w_uk_p.reshape(MLA_KV_RANK, nh * LANES).astype(BF16)
    w_uv = w_ukv3[:, :, MLA_NOPE:].reshape(MLA_KV_RANK, nh * MLA_V).astype(BF16)
    pos = jnp.arange(lp, dtype=F32)
    inv_freq = ROPE_BASE ** (-jnp.arange(0, MLA_ROPE, 2, dtype=F32) / MLA_ROPE)
    ang = pos[:, None] * inv_freq[None, :]
    cos, sin = jnp.cos(ang), jnp.sin(ang)
    z = lambda w: jnp.zeros((lp, w), F32)
    cm = jnp.concatenate([jnp.ones((lp, MLA_NOPE), F32), cos, cos, z(LANES - qk)], axis=1)
    s1 = jnp.concatenate([z(MLA_NOPE), -sin, z(LANES - MLA_NOPE - half)], axis=1)
    s2 = jnp.concatenate([z(MLA_NOPE + half), sin, z(LANES - qk)], axis=1)
    scale = qk ** -0.5 * LOG2E
    tq = _pick_tile(lp, (384, 256, 128))
    nk = lp // tq
    q, k, vt = _row_call(
        _mla_in_body, [h3.reshape(n, d)],
        [w_in_all, q_norm_g.reshape(1, MLA_Q_RANK), kv_norm_g.reshape(1, MLA_KV_RANK), w_uq_p, w_uk_p, w_uv],
        [(nh * LANES, BF16), (nh * LANES, BF16)], tq, "mla_in",
        pos_ins=[cm * scale, s1 * scale, s2 * scale, cm, s1, s2], lp=lp, t_outs=[(nh * MLA_V, BF16)])
    q, k = q.reshape(b, lp, nh * LANES), k.reshape(b, lp, nh * LANES)
    ti = jnp.arange(tq)
    mask = jnp.stack([jnp.where(ti[:, None] <= ti[None, :], 0.0, NEG_INF), jnp.zeros((tq, tq))]).astype(F32)
    o = pl.pallas_call(
        functools.partial(_mla_attn_body, tq, nk),
        grid=(b, nh // 2),
        in_specs=[pl.BlockSpec((None, lp, 2 * LANES), lambda bi, hi: (bi, 0, hi)),
                  pl.BlockSpec((None, lp, 2 * LANES), lambda bi, hi: (bi, 0, hi)),
                  pl.BlockSpec((None, nk, 2 * MLA_V, tq), lambda bi, hi: (bi, 0, hi, 0)),
                  pl.BlockSpec((2, tq, tq), lambda bi, hi: (0, 0, 0))],
        out_specs=pl.BlockSpec((None, lp, 2 * MLA_V), lambda bi, hi: (bi, 0, hi)),
        out_shape=jax.ShapeDtypeStruct((b, lp, nh * MLA_V), BF16),
        scratch_shapes=_flash_scratch(2 * MLA_V, tq, nk),
        name="mla_attn",
        compiler_params=pltpu.CompilerParams(dimension_semantics=("parallel", "parallel"),
                                             vmem_limit_bytes=VMEM_LIMIT),
    )(q, k, vt, mask)
    return o.reshape(n, nh * MLA_V)


def kernel(x, meta, rel_bias, ln_g, ln_b, ffn_w1, ffn_w3, ffn_w2, ssd_w_in, ssd_conv_w, ssd_conv_b, ssd_dt_bias, ssd_a_log, ssd_d, ssd_norm_g, ssd_w_out, diff_w_qkv, diff_lam_q1, diff_lam_k1, diff_lam_q2, diff_lam_k2, diff_subln_g, diff_w_out, s5_lam_re, s5_lam_im, s5_log_step, s5_b_re, s5_b_im, s5_c_re, s5_c_im, s5_d, s5_w_glu, s5_b_glu, mla_w_in, mla_q_norm_g, mla_kv_norm_g, mla_w_uq, mla_w_ukv, mla_w_out):
    b, seq, d = x.shape
    l = seq + N_META
    lp = -(-l // SEQ_ALIGN) * SEQ_ALIGN
    n = b * lp
    h = jnp.concatenate([jnp.broadcast_to(meta[None].astype(x.dtype), (b, N_META, d)), x,
                         jnp.zeros((b, lp - l, d), x.dtype)], axis=1).reshape(n, d)
    w1b, w3b, w2b = ffn_w1.astype(BF16), ffn_w3.astype(BF16), ffn_w2.astype(BF16)
    for i in range(DEPTH):
        kind, j = i % 4, i // 4
        h = _ffn(h, w1b, w3b, w2b, ln_g[i, 0], ln_b[i, 0], (i, 0))
        h3 = h.reshape(b, lp, d)
        if kind == 0:
            y = _ssd_mixer(h3, ssd_w_in[j], ssd_conv_w[j], ssd_conv_b[j], ssd_dt_bias[j], ssd_a_log[j],
                           ssd_d[j], ssd_norm_g[j])
            w_out, b_out = ssd_w_out[j], None
        elif kind == 1:
            lam_init = 0.8 - 0.6 * math.exp(-0.3 * i)
            y = _diff_mixer(h3, diff_w_qkv[j], diff_lam_q1[j], diff_lam_k1[j], diff_lam_q2[j], diff_lam_k2[j],
                            diff_subln_g[j], rel_bias, lam_init)
            w_out, b_out = diff_w_out[j], None
        elif kind == 2:
            y = _s5_mixer(h3, s5_lam_re[j], s5_lam_im[j], s5_log_step[j], s5_b_re[j], s5_b_im[j], s5_c_re[j],
                          s5_c_im[j], s5_d[j])
            w_out, b_out = s5_w_glu[j], s5_b_glu[j]
        else:
            y = _mla_mixer(h3, mla_w_in[j], mla_q_norm_g[j], mla_kv_norm_g[j], mla_w_uq[j], mla_w_ukv[j])
            w_out, b_out = mla_w_out[j], None
        h = _mix_ffn(y, h, w_out, b_out, ln_g[i, 1], ln_b[i, 1], w1b, w3b, w2b, ln_g[i, 2], ln_b[i, 2], (i, 1))
    return h.reshape(b, lp, d)[:, N_META:l]
```

```python
import functools
import math

import jax
import jax.numpy as jnp
from jax import lax
from jax.experimental import pallas as pl
from jax.experimental.pallas import tpu as pltpu

F32 = jnp.float32
BF16 = jnp.bfloat16

N_META = 16
DEPTH = 4
ALPHA = (2.0 * DEPTH) ** 0.25
LN_EPS = 1e-5
RMS_EPS = 1e-6
NEG_INF = -1e30
LOG2E = math.log2(math.e)
ATTN_REF_KEYS = 16
SOFTMAX_SUM_LIMIT = 2.0 ** 100
F32_FINITE_LIMIT = 3.0e38

SSD_HEAD_DIM = 64
SSD_HEADS = 32
SSD_GROUPS = 8
SSD_STATE = 128
SSD_CONV = 4
SSD_CHUNK = 128

DIFF_HEADS = 8
DIFF_HEAD_DIM = 64
DIFF_V_DIM = 128
REL_BUCKETS = 32
REL_MAX_DIST = 128

S5_GROUP = 16
S5_GROUPS = 64
S5_STATE = 64
S5_CHUNK = 16
S5_GROUP_BLOCK = 8
SUBLANES = 8
S5_SUBLANES = SUBLANES
S5_UNROLL = 4

MLA_HEADS = 16
MLA_Q_RANK = 384
MLA_KV_RANK = 256
MLA_NOPE = 64
MLA_ROPE = 32
MLA_V = 64
ROPE_BASE = 10000.0

FFN_ROW_TILES = (512, 384, 256, 128)
FFN_CHUNKS = 11

LANES = 128
SEQ_ALIGN = 128
VMEM_LIMIT = 56 * 1024 * 1024


def _pick_tile(n, candidates):
    for c in candidates:
        if n % c == 0:
            return c
    raise ValueError(f"no tile for {n}")


def _layer_norm(z, g, b):
    mu = jnp.mean(z, axis=-1, keepdims=True)
    d = z - mu
    var = jnp.mean(d * d, axis=-1, keepdims=True)
    return d * lax.rsqrt(var + LN_EPS) * g + b


def _rms(z, g):
    return z * lax.rsqrt(jnp.mean(z * z, axis=-1, keepdims=True) + RMS_EPS) * g


def _silu(x):
    return x * (1.0 / (1.0 + jnp.exp(-x)))


def _dot(a, b):
    return jnp.dot(a, b, preferred_element_type=F32)


def _dot_nt(a, b):
    return lax.dot_general(a, b, (((1,), (1,)), ((), ())), preferred_element_type=F32)


def _row_call(body, row_ins, const_ins, outs, tm, name, pos_ins=(), lp=None, t_outs=()):
    n = row_ins[0].shape[0]
    assert n % tm == 0
    in_specs = [pl.BlockSpec((tm, a.shape[1]), lambda i: (i, 0)) for a in row_ins]
    if pos_ins or t_outs:
        assert lp % tm == 0
        nb = lp // tm
        in_specs += [pl.BlockSpec((tm, a.shape[1]), lambda i: (i % nb, 0)) for a in pos_ins]
    const_arrays = []
    for c in const_ins:
        arr, lead = c if isinstance(c, tuple) else (c, ())
        const_arrays.append(arr)
        in_specs.append(pl.BlockSpec((None,) * len(lead) + arr.shape[len(lead):],
                                     functools.partial(lambda lead_, i: lead_ + (0, 0), tuple(lead)),
                                     pipeline_mode=pl.Buffered(1)))
    out_specs = [pl.BlockSpec((tm, w), lambda i: (i, 0)) for (w, _) in outs]
    out_shape = [jax.ShapeDtypeStruct((n, w), dt) for (w, dt) in outs]
    out_specs += [pl.BlockSpec((None, None, w, tm), lambda i: (i // nb, i % nb, 0, 0)) for (w, _) in t_outs]
    out_shape += [jax.ShapeDtypeStruct((n // lp, nb, w, tm), dt) for (w, dt) in t_outs]
    res = pl.pallas_call(
        body, grid=(n // tm,), in_specs=in_specs, out_specs=out_specs, out_shape=out_shape, name=name,
        compiler_params=pltpu.CompilerParams(dimension_semantics=("parallel",), vmem_limit_bytes=VMEM_LIMIT),
    )(*row_ins, *pos_ins, *const_arrays)
    return res


def _ffn_rows(x, n_chunks, w1_ref, w3_ref, w2_ref, g_ref, b_ref):
    xb = x.astype(BF16)
    fc = w1_ref.shape[1] // n_chunks
    y = None
    for c in range(n_chunks):
        sl = slice(c * fc, (c + 1) * fc)
        gate = _dot(xb, w1_ref[:, sl])
        up = _dot(xb, w3_ref[:, sl])
        a = (_silu(gate) * up).astype(BF16)
        part = _dot(a, w2_ref[sl, :])
        y = part if y is None else y + part
    return _layer_norm(ALPHA * x + 0.5 * y, g_ref[...], b_ref[...])


def _ffn_body(n_chunks, h_ref, w1_ref, w3_ref, w2_ref, g_ref, b_ref, o_ref):
    o_ref[...] = _ffn_rows(h_ref[...], n_chunks, w1_ref, w3_ref, w2_ref, g_ref, b_ref)


def _mix_ffn_body(n_chunks, glu, y_ref, h_ref, wo_ref, bo_ref, gm_ref, bm_ref,
                  w1_ref, w3_ref, w2_ref, g_ref, b_ref, o_ref):
    m = _dot(y_ref[...].astype(BF16), wo_ref[...])
    if glu:
        m = m + bo_ref[...]
        d = o_ref.shape[1]
        m = m[:, :d] * (1.0 / (1.0 + jnp.exp(-m[:, d:])))
    x = _layer_norm(ALPHA * h_ref[...] + m, gm_ref[...], bm_ref[...])
    o_ref[...] = _ffn_rows(x, n_chunks, w1_ref, w3_ref, w2_ref, g_ref, b_ref)


def _ffn_chunks(f, n_chunks):
    return n_chunks or (FFN_CHUNKS if f % (FFN_CHUNKS * LANES) == 0 else 1)


def _ffn(h2, w1, w3, w2, g, b, idx, tm=None, n_chunks=None, name="ffn"):
    n, d = h2.shape
    tm = tm or _pick_tile(n, FFN_ROW_TILES)
    (out,) = _row_call(functools.partial(_ffn_body, _ffn_chunks(w1.shape[-1], n_chunks)), [h2],
                       [(w1, idx), (w3, idx), (w2, idx), g.reshape(1, d), b.reshape(1, d)],
                       [(d, F32)], tm, name)
    return out


def _mix_ffn(y2, h2, w_out, b_out, g_mix, b_mix, w1, w3, w2, g, b, idx, tm=None, n_chunks=None, name="mix_ffn"):
    n, d = h2.shape
    tm = tm or _pick_tile(n, FFN_ROW_TILES)
    glu = b_out is not None
    wo = w_out.astype(BF16)
    bo = (b_out if glu else jnp.zeros((wo.shape[1],), F32)).reshape(1, wo.shape[1])
    (out,) = _row_call(functools.partial(_mix_ffn_body, _ffn_chunks(w1.shape[-1], n_chunks), glu), [y2, h2],
                       [wo, bo, g_mix.reshape(1, d), b_mix.reshape(1, d),
                        (w1, idx), (w3, idx), (w2, idx), g.reshape(1, d), b.reshape(1, d)],
                       [(d, F32)], tm, name)
    return out


def _ssd_in_body(h_ref, w_ref, xbc_ref, z_ref, dt_ref):
    r = _dot(h_ref[...].astype(BF16), w_ref[...])
    nx, nz = xbc_ref.shape[1], z_ref.shape[1]
    xbc_ref[...] = r[:, :nx]
    z_ref[...] = r[:, nx:nx + nz]
    dt_ref[...] = r[:, nx + nz:]


def _softplus(x):
    return jnp.maximum(x, 0.0) + jnp.log1p(jnp.exp(-jnp.abs(x)))


def _ssd_body(xbc_ref, dt_ref, z_ref, convw_ref, convb_ref, dtb_ref, alog_ref, dskip_ref, normg_ref,
              y_ref, buf_scr, state_scr, y_scr):
    t = SSD_CHUNK
    d_inner = SSD_HEADS * SSD_HEAD_DIM
    gn = SSD_GROUPS * SSD_STATE
    hp = SSD_HEAD_DIM
    r = SSD_HEADS // SSD_GROUPS

    @pl.when(pl.program_id(1) == 0)
    def _():
        buf_scr[0:SUBLANES, :] = jnp.zeros((SUBLANES, buf_scr.shape[1]), F32)
        state_scr[...] = jnp.zeros(state_scr.shape, F32)

    x = xbc_ref[...]
    buf_scr[SUBLANES:SUBLANES + t, :] = x
    acc = convb_ref[...] + convw_ref[SSD_CONV - 1:SSD_CONV, :] * x
    for j in range(1, SSD_CONV):
        acc = acc + convw_ref[SSD_CONV - 1 - j:SSD_CONV - j, :] * buf_scr[SUBLANES - j:SUBLANES - j + t, :]
    buf_scr[0:SUBLANES, :] = x[t - SUBLANES:t, :]
    xbc = _silu(acc)

    dt = _softplus(dt_ref[...] + dtb_ref[...])
    a = dt * (-jnp.exp(alog_ref[...]))
    rows = lax.broadcasted_iota(jnp.int32, (t, t), 0)
    cols = lax.broadcasted_iota(jnp.int32, (t, t), 1)
    tri = rows >= cols
    a_cum = jnp.dot(tri.astype(F32), a, precision=lax.Precision.HIGHEST, preferred_element_type=F32)
    a_last = a_cum[t - 1:t, :]
    a_cum_t = a_cum.T
    dt_t = dt.T
    w_end_t = (dt * jnp.exp(a_last - a_cum)).T
    chunk_dec = jnp.exp(a_last)

    for g in range(SSD_GROUPS):
        b_f = xbc[:, d_inner + g * SSD_STATE:d_inner + (g + 1) * SSD_STATE]
        c_g = xbc[:, d_inner + gn + g * SSD_STATE:d_inner + gn + (g + 1) * SSD_STATE].astype(BF16)
        b_g = b_f.astype(BF16)
        b_ft = b_f.T
        cb = _dot_nt(c_g, b_g)
        y_off_g = _dot(c_g, state_scr[:, g * r * hp:(g + 1) * r * hp].astype(BF16))
        for k in range(r):
            h = g * r + k
            col = jnp.broadcast_to(a_cum[:, h:h + 1], (t, t))
            decay = jnp.exp(jnp.where(tri, col - a_cum_t[h:h + 1, :], NEG_INF))
            xh = xbc[:, h * hp:(h + 1) * hp]
            xb = xh.astype(BF16)
            y_diag = _dot((cb * decay * dt_t[h:h + 1, :]).astype(BF16), xb)
            y_off = y_off_g[:, k * hp:(k + 1) * hp] * jnp.exp(col[:, :hp])
            y_scr[:, h * hp:(h + 1) * hp] = y_diag + y_off + dskip_ref[:, h * hp:(h + 1) * hp] * xh
            st = state_scr[:, h * hp:(h + 1) * hp]
            b_w = (b_ft * w_end_t[h:h + 1, :]).astype(BF16)
            state_scr[:, h * hp:(h + 1) * hp] = st * chunk_dec[:, h:h + 1] + _dot(b_w, xb)

    y = y_scr[...] * _silu(z_ref[...])
    gs = d_inner // SSD_GROUPS
    for g in range(SSD_GROUPS):
        sl = slice(g * gs, (g + 1) * gs)
        y_ref[:, sl] = _rms(y[:, sl], normg_ref[:, sl]).astype(y_ref.dtype)


def _ssd_mixer(h3, w_in, conv_w, conv_b, dt_bias, a_log, d_skip, norm_g):
    b, lp, d = h3.shape
    n = b * lp
    d_inner = SSD_HEADS * SSD_HEAD_DIM
    conv_ch = d_inner + 2 * SSD_GROUPS * SSD_STATE
    w_z, w_xbc, w_dt = w_in[:, :d_inner], w_in[:, d_inner:d_inner + conv_ch], w_in[:, d_inner + conv_ch:]
    w_dt = jnp.pad(w_dt, ((0, 0), (0, LANES - SSD_HEADS)))
    w_all = jnp.concatenate([w_xbc, w_z, w_dt], axis=1).astype(BF16)
    tm = _pick_tile(n, (512, 384, 256, 128))
    xbc, z, dt = _row_call(_ssd_in_body, [h3.reshape(n, d)], [w_all],
                           [(conv_ch, F32), (d_inner, F32), (LANES, F32)], tm, "ssd_in")
    t = SSD_CHUNK
    pad1 = lambda v: jnp.pad(v, (0, LANES - SSD_HEADS)).reshape(1, LANES)
    consts = [conv_w, conv_b.reshape(1, conv_ch), pad1(dt_bias), pad1(a_log),
              jnp.repeat(d_skip, SSD_HEAD_DIM).reshape(1, d_inner), norm_g.reshape(1, d_inner)]
    row_spec = lambda w: pl.BlockSpec((None, t, w), lambda i, c: (i, c, 0))
    y = pl.pallas_call(
        _ssd_body, grid=(b, lp // t),
        in_specs=[row_spec(conv_ch), row_spec(LANES), row_spec(d_inner)]
        + [pl.BlockSpec(c.shape, lambda i, c_: (0, 0)) for c in consts],
        out_specs=row_spec(d_inner),
        out_shape=jax.ShapeDtypeStruct((b, lp, d_inner), BF16),
        scratch_shapes=[pltpu.VMEM((SUBLANES + t, conv_ch), F32), pltpu.VMEM((SSD_STATE, d_inner), F32),
                        pltpu.VMEM((t, d_inner), F32)],
        name="ssd_scan",
        compiler_params=pltpu.CompilerParams(dimension_semantics=("parallel", "arbitrary"),
                                             vmem_limit_bytes=VMEM_LIMIT),
    )(xbc.reshape(b, lp, conv_ch), dt.reshape(b, lp, LANES), z.reshape(b, lp, d_inner), *consts)
    return y.reshape(n, d_inner)


def _qkv_body(n_q, scale, h_ref, w_ref, qk_ref, vt_ref):
    r = _dot(h_ref[...].astype(BF16), w_ref[...])
    n_qk = qk_ref.shape[1]
    qk_ref[:, :n_q] = (r[:, :n_q] * scale).astype(qk_ref.dtype)
    qk_ref[:, n_q:] = r[:, n_q:n_qk].astype(qk_ref.dtype)
    vt_ref[...] = r[:, n_qk:].T.astype(vt_ref.dtype)


def _causal_mask_pair(s, tq):
    rows = lax.broadcasted_iota(jnp.int32, s.shape, 0)
    cols = lax.broadcasted_iota(jnp.int32, s.shape, 1)
    cols = jnp.where(cols >= tq, cols - tq, cols)
    return jnp.where(rows <= cols, s, NEG_INF)


def _flash_causal(qi, tq, scores, vt_ref, m_scr, l_scr, acc_scr, s_scrs, p_scrs, mx_scrs):
    m_scr[...] = jnp.full(m_scr.shape, NEG_INF, F32)
    l_scr[...] = jnp.zeros(l_scr.shape, F32)
    acc_scr[...] = jnp.zeros(acc_scr.shape, F32)
    p_scrs[1][...] = jnp.zeros(p_scrs[1].shape, p_scrs[1].dtype)

    def softmax(s, s_max, pv):
        m_prev = m_scr[...]
        m_new = jnp.maximum(m_prev, s_max)
        alpha = jnp.exp2(m_prev - m_new)
        p = jnp.exp2(s - m_new)
        l_scr[...] = alpha * l_scr[...] + jnp.sum(p, axis=0, keepdims=True)
        m_scr[...] = m_new
        acc_scr[...] = alpha * (acc_scr[...] + pv)
        return p.astype(BF16)

    def produce(kj, slot):
        s = scores(kj)
        s_scrs[slot][...] = s
        mx_scrs[slot][...] = jnp.max(s, axis=0, keepdims=True)

    def step(kj, cur, nxt):
        pv = _dot(vt_ref[jnp.maximum(kj - 1, 0)], p_scrs[nxt][...])
        p_scrs[cur][...] = softmax(s_scrs[cur][...], mx_scrs[cur][...], pv)
        produce(kj + 1, nxt)

    produce(0, 0)

    def body(t, carry):
        step(2 * t, 0, 1)
        step(2 * t + 1, 1, 0)
        return carry

    lax.fori_loop(0, lax.shift_right_logical(qi, 1), body, 0)
    pl.when((qi & 1) == 1)(functools.partial(step, qi - 1, 0, 1))

    def last(cur, nxt):
        pv = _dot(vt_ref[jnp.maximum(qi - 1, 0)], p_scrs[nxt][...])
        s = _causal_mask_pair(s_scrs[cur][...], tq)
        p = softmax(s, jnp.max(s, axis=0, keepdims=True), pv)
        acc_scr[...] = acc_scr[...] + _dot(vt_ref[qi], p)

    for par in range(2):
        pl.when((qi & 1) == par)(functools.partial(last, par, 1 - par))


def _flash_head(tq, nq, scores, vt_ref, finish, m_all, l_all, acc_all,
                m_scr, l_scr, acc_scr, s_scrs, p_scrs, mx_scrs, exact_ref):
    def init(qi, carry):
        m_all[qi] = jnp.max(scores(0, qi, ATTN_REF_KEYS), axis=0, keepdims=True)
        l_all[qi] = jnp.zeros(l_all.shape[1:], F32)
        acc_all[qi] = jnp.zeros(acc_all.shape[1:], F32)
        return carry

    lax.fori_loop(0, nq, init, 0)

    def probs(kj, qi, slot):
        p = jnp.exp2(scores(kj, qi) - m_all[qi])
        l_all[qi] += jnp.sum(p, axis=0, keepdims=True)
        p_scrs[slot][...] = p.astype(BF16)

    def step(prev, cur, prv):
        kjp, qip = prev
        wrap = qip + 1 >= nq
        kj = jnp.where(wrap, kjp + 1, kjp)
        qi = jnp.where(wrap, kjp + 1, qip + 1)
        p = jnp.exp2(scores(kj, qi) - m_all[qi])
        acc_all[qip] += _dot(vt_ref[kjp], p_scrs[prv][...])
        l_all[qi] += jnp.sum(p, axis=0, keepdims=True)
        p_scrs[cur][...] = p.astype(BF16)
        return kj, qi

    n_steps = nq * (nq + 1) // 2 - 1
    probs(0, 0, 0)

    def body(t, prev):
        return step(step(prev, 1, 0), 0, 1)

    prev = lax.fori_loop(0, n_steps // 2, body, (jnp.int32(0), jnp.int32(0)))
    if n_steps % 2:
        prev = step(prev, 1, 0)
    acc_all[prev[1]] += _dot(vt_ref[prev[0]], p_scrs[n_steps % 2][...])

    ok = jnp.all(l_all[...] < SOFTMAX_SUM_LIMIT) & jnp.all(jnp.abs(acc_all[...]) < F32_FINITE_LIMIT)
    exact_ref[0] = jnp.where(ok, 0, 1)

    @pl.when(exact_ref[0] == 0)
    def _():
        def fin(qi, carry):
            finish(qi, l_all[qi], acc_all[qi])
            return carry

        lax.fori_loop(0, nq, fin, 0)

    @pl.when(exact_ref[0] == 1)
    def _():
        def redo(qi, carry):
            _flash_causal(qi, tq, lambda kj: scores(kj, qi), vt_ref, m_scr, l_scr, acc_scr, s_scrs, p_scrs, mx_scrs)
            finish(qi, l_scr[...], acc_scr[...])
            return carry

        lax.fori_loop(0, nq, redo, 0)


def _flash_scratch(e, tq, nq):
    w = 2 * tq
    return [pltpu.VMEM((nq, 1, w), F32), pltpu.VMEM((nq, 1, w), F32), pltpu.VMEM((nq, e, w), F32),
            pltpu.VMEM((1, w), F32), pltpu.VMEM((1, w), F32), pltpu.VMEM((e, w), F32),
            pltpu.VMEM((tq, w), F32), pltpu.VMEM((tq, w), F32), pltpu.VMEM((tq, w), BF16), pltpu.VMEM((tq, w), BF16),
            pltpu.VMEM((1, w), F32), pltpu.VMEM((1, w), F32), pltpu.SMEM((1,), jnp.int32)]


def _t5_bucket(dist):
    max_exact = REL_BUCKETS // 2
    d = jnp.maximum(dist, 0)
    df = jnp.maximum(d, max_exact).astype(F32)
    large = max_exact + (jnp.log(df / max_exact) / math.log(REL_MAX_DIST / max_exact)
                         * (REL_BUCKETS - max_exact)).astype(jnp.int32)
    large = jnp.minimum(large, REL_BUCKETS - 1)
    return jnp.where(d < max_exact, d, large)


def _diff_attn_body(tq, nq, lam_init, q_ref, k_ref, vt_ref, bias_ref, lam_ref, g_ref, o_ref,
                    m_all, l_all, acc_all, m_scr, l_scr, acc_scr, s0_scr, s1_scr, p0_scr, p1_scr, mx0_scr, mx1_scr,
                    exact_ref, qcat_all):
    lane = lax.broadcasted_iota(jnp.int32, (tq, q_ref.shape[1]), 1)

    def prep(qi, carry):
        q = q_ref[pl.ds(pl.multiple_of(qi * tq, tq), tq), :]
        zero = jnp.zeros_like(q)
        qcat_all[qi, 0:tq, :] = jnp.where(lane < DIFF_HEAD_DIM, q, zero)
        qcat_all[qi, tq:2 * tq, :] = jnp.where(lane >= DIFF_HEAD_DIM, q, zero)
        return carry

    lax.fori_loop(0, nq, prep, 0)

    def scores(kj, qi, rows=tq):
        start = pl.multiple_of(kj * tq, tq)
        bias = bias_ref[jnp.minimum(qi - kj, 2), 0:rows, :]
        return _dot_nt(k_ref[pl.ds(start, rows), :], qcat_all[qi]) + jnp.concatenate([bias, bias], axis=1)

    def finish(qi, l, acc):
        a = acc * (1.0 / l)
        o_t = a[:, :tq] - lam_ref[...] * a[:, tq:]
        o_ref[pl.ds(pl.multiple_of(qi * tq, tq), tq), :] = (
            _rms(o_t.T, g_ref[...]) * (1.0 - lam_init)).astype(o_ref.dtype)

    _flash_head(tq, nq, scores, vt_ref, finish, m_all, l_all, acc_all, m_scr, l_scr, acc_scr,
                (s0_scr, s1_scr), (p0_scr, p1_scr), (mx0_scr, mx1_scr), exact_ref)


def _diff_mixer(h3, w_qkv, lam_q1, lam_k1, lam_q2, lam_k2, subln_g, rel_bias, lam_init):
    b, lp, d = h3.shape
    n = b * lp
    hd = DIFF_HEADS * 2 * DIFF_HEAD_DIM
    tq = _pick_tile(lp, (384, 256, 128))
    assert tq >= REL_MAX_DIST
    nk = lp // tq
    qk, vt = _row_call(functools.partial(_qkv_body, hd, DIFF_HEAD_DIM ** -0.5 * LOG2E), [h3.reshape(n, d)],
                       [w_qkv.astype(BF16)], [(2 * hd, BF16)], tq, "diff_qkv", lp=lp, t_outs=[(hd, BF16)])
    qk = qk.reshape(b, lp, 2 * hd)
    m2 = 2 * tq
    kk = jnp.arange(m2)
    dist = jnp.where(kk < tq, kk, kk - m2)[None, :] + (jnp.arange(3) * tq)[:, None]
    tab = jnp.transpose(rel_bias[_t5_bucket(dist)].astype(F32) * LOG2E, (2, 0, 1))
    tab = jnp.where((jnp.arange(3)[:, None] == 0) & (kk[None, :] >= tq), NEG_INF, tab)
    nh = DIFF_HEADS
    bias = jnp.tile(tab, (1, 1, tq))[..., :tq * (m2 - 1)].reshape(nh, 3, tq, m2 - 1)[..., :tq]
    lam = (jnp.exp(jnp.sum(lam_q1.astype(F32) * lam_k1.astype(F32)))
           - jnp.exp(jnp.sum(lam_q2.astype(F32) * lam_k2.astype(F32))) + lam_init)
    w = 2 * DIFF_HEAD_DIM
    o = pl.pallas_call(
        functools.partial(_diff_attn_body, tq, nk, lam_init),
        grid=(b, nh),
        in_specs=[pl.BlockSpec((None, lp, w), lambda bi, hi: (bi, 0, hi)),
                  pl.BlockSpec((None, lp, w), lambda bi, hi: (bi, 0, nh + hi)),
                  pl.BlockSpec((None, nk, DIFF_V_DIM, tq), lambda bi, hi: (bi, 0, hi, 0)),
                  pl.BlockSpec((None, 3, tq, tq), lambda bi, hi: (hi, 0, 0, 0)),
                  pl.BlockSpec((1, 1), lambda bi, hi: (0, 0)),
                  pl.BlockSpec((1, DIFF_V_DIM), lambda bi, hi: (0, 0))],
        out_specs=pl.BlockSpec((None, lp, DIFF_V_DIM), lambda bi, hi: (bi, 0, hi)),
        out_shape=jax.ShapeDtypeStruct((b, lp, nh * DIFF_V_DIM), BF16),
        scratch_shapes=_flash_scratch(DIFF_V_DIM, tq, nk) + [pltpu.VMEM((nk, 2 * tq, w), BF16)],
        name="diff_attn",
        compiler_params=pltpu.CompilerParams(dimension_semantics=("parallel", "parallel"),
                                             vmem_limit_bytes=VMEM_LIMIT),
    )(qk, qk, vt, bias, lam.reshape(1, 1), subln_g.reshape(1, DIFF_V_DIM))
    return o.reshape(n, nh * DIFF_V_DIM)


def _gelu_tanh(x):
    return 0.5 * x * (1.0 + jnp.tanh(math.sqrt(2.0 / math.pi) * (x + 0.044715 * (x * x * x))))


def _s5_body(bp, x_ref, m_ref, bc_ref, ccre_ref, ccim_ref, are_ref, aim_ref, y_ref, sre_scr, sim_scr):
    rows = x_ref.shape[0]
    gb = m_ref.shape[0]
    kw = m_ref.shape[1]
    p = ccre_ref.shape[1]
    for g in range(gb):
        e = _dot(x_ref[:, g * kw:(g + 1) * kw], bc_ref[g])
        sre_scr[:, g * p:(g + 1) * p] = e[:, :p]
        sim_scr[:, g * p:(g + 1) * p] = e[:, p:]

    ar, ai = are_ref[...], aim_ref[...]
    sub = lax.broadcasted_iota(jnp.int32, ar.shape, 0)
    per_tile = S5_SUBLANES // bp

    def tile_step(r, carry):
        pr, pi = carry
        r0 = pl.multiple_of(r * S5_SUBLANES, S5_SUBLANES)
        er, ei = sre_scr[pl.ds(r0, S5_SUBLANES), :], sim_scr[pl.ds(r0, S5_SUBLANES), :]
        in_r, in_i = pr, pi
        for c in range(per_tile):
            nr = ar * pr - ai * pi + er
            ni = ar * pi + ai * pr + ei
            if per_tile > 1:
                nr = pltpu.roll(nr, bp, axis=0)
                ni = pltpu.roll(ni, bp, axis=0)
            if c + 1 < per_tile:
                sel = (sub >= (c + 1) * bp) & (sub < (c + 2) * bp)
                in_r, in_i = jnp.where(sel, nr, in_r), jnp.where(sel, ni, in_i)
                pr, pi = nr, ni
            else:
                pr, pi = nr, ni
        if per_tile > 1:
            pr = jnp.where(sub < bp, pr, pltpu.roll(pr, bp, axis=0))
            pi = jnp.where(sub < bp, pi, pltpu.roll(pi, bp, axis=0))
        sre_scr[pl.ds(r0, S5_SUBLANES), :] = in_r
        sim_scr[pl.ds(r0, S5_SUBLANES), :] = in_i
        return pr, pi

    zero = jnp.zeros(ar.shape, F32)
    lax.fori_loop(0, rows // S5_SUBLANES, tile_step, (zero, zero), unroll=S5_UNROLL)

    for g in range(gb):
        xg = x_ref[:, g * kw:(g + 1) * kw]
        y = (_dot(xg, m_ref[g]) + _dot(sre_scr[:, g * p:(g + 1) * p].astype(BF16), ccre_ref[g])
             + _dot(sim_scr[:, g * p:(g + 1) * p].astype(BF16), ccim_ref[g]))
        y_ref[:, g * kw:(g + 1) * kw] = _gelu_tanh(y).astype(y_ref.dtype)


def _s5_pack_body(n_batch, h_ref, perm_ref, x_ref):
    nb, nj, t_len, _ = h_ref.shape
    w = perm_ref.shape[1]
    xcat = jnp.concatenate(
        [jnp.concatenate([h_ref[bi, :, t, :].astype(BF16) for t in range(t_len)], axis=1) for bi in range(nb)], axis=0)
    x = _dot(xcat, perm_ref[...]).astype(x_ref.dtype)
    for bi in range(x_ref.shape[1] // w):
        x_ref[:, bi * w:(bi + 1) * w] = x[bi * nj:(bi + 1) * nj, :] if bi < n_batch else jnp.zeros((nj, w), x_ref.dtype)


def _s5_unpack_body(y_ref, perm_ref, o_ref):
    nb, nj, t_len, _ = o_ref.shape
    w = perm_ref.shape[0]
    ycat = jnp.concatenate([y_ref[:, bi * w:(bi + 1) * w] for bi in range(nb)], axis=0)
    res = _dot(ycat, perm_ref[...])
    for bi in range(nb):
        for t in range(t_len):
            o_ref[bi, :, t, :] = res[bi * nj:(bi + 1) * nj, t * LANES:(t + 1) * LANES]


def _s5_chunk_operators(lam_re, lam_im, log_step, b_re, b_im, c_re, c_im, d_skip):
    hi = lax.Precision.HIGHEST
    t_len = S5_CHUNK
    step = jnp.exp(log_step.astype(F32))[:, None]
    lr = jnp.minimum(lam_re.astype(F32), -1e-4)
    li = lam_im.astype(F32)
    mag = jnp.exp(lr * step)
    ar, ai = mag * jnp.cos(li * step), mag * jnp.sin(li * step)
    den = lr * lr + li * li
    cr = ((ar - 1.0) * lr + ai * li) / den
    ci = (ai * lr - (ar - 1.0) * li) / den
    br, bi = b_re.astype(F32), b_im.astype(F32)
    bbr = cr[..., None] * br - ci[..., None] * bi
    bbi = cr[..., None] * bi + ci[..., None] * br
    k = jnp.arange(t_len + 1, dtype=F32)[:, None, None]
    pmag = jnp.exp(k * (lr * step))
    pr, pi = pmag * jnp.cos(k * (li * step)), pmag * jnp.sin(k * (li * step))
    abr = pr[:t_len, ..., None] * bbr - pi[:t_len, ..., None] * bbi
    abi = pr[:t_len, ..., None] * bbi + pi[:t_len, ..., None] * bbr
    cre, cim = c_re.astype(F32), c_im.astype(F32)
    kern = (jnp.einsum('gop,kgpc->kgoc', cre, abr, precision=hi)
            - jnp.einsum('gop,kgpc->kgoc', cim, abi, precision=hi))
    kern = kern.at[0].add(jax.vmap(jnp.diag)(d_skip.astype(F32)))
    sig = jnp.arange(t_len)
    lag = sig[None, :] - sig[:, None]
    m = jnp.where((lag >= 0)[..., None, None, None], kern[jnp.maximum(lag, 0)], 0.0)
    g, o, c = kern.shape[1:]
    m = jnp.transpose(m, (2, 0, 4, 1, 3)).reshape(g, t_len * c, t_len * o)
    rev = t_len - 1 - sig
    p = bbr.shape[1]
    bc = jnp.concatenate([jnp.transpose(abr[rev], (1, 0, 3, 2)).reshape(g, t_len * c, p),
                          jnp.transpose(abi[rev], (1, 0, 3, 2)).reshape(g, t_len * c, p)], axis=2)
    car = cre[None] * pr[1:, :, None, :] - cim[None] * pi[1:, :, None, :]
    cai = cre[None] * pi[1:, :, None, :] + cim[None] * pr[1:, :, None, :]
    cc_re = jnp.transpose(car, (1, 3, 0, 2)).reshape(g, p, t_len * o)
    cc_im = -jnp.transpose(cai, (1, 3, 0, 2)).reshape(g, p, t_len * o)
    return (m.astype(BF16), bc.astype(BF16), cc_re.astype(BF16), cc_im.astype(BF16),
            pr[t_len].reshape(g * p), pi[t_len].reshape(g * p))


def _s5_mixer(h3, lam_re, lam_im, log_step, b_re, b_im, c_re, c_im, d_skip):
    b, lp, d = h3.shape
    t_len, g, c = S5_CHUNK, S5_GROUPS, S5_GROUP
    assert lp % t_len == 0 and b <= S5_SUBLANES
    bp = 4 if b <= 4 else S5_SUBLANES
    nj = lp // t_len
    m, bc, cc_re, cc_im, at_re, at_im = _s5_chunk_operators(lam_re, lam_im, log_step, b_re, b_im, c_re, c_im, d_skip)
    n_state = g * S5_STATE
    at_re = jnp.broadcast_to(at_re.reshape(1, n_state), (S5_SUBLANES, n_state))
    at_im = jnp.broadcast_to(at_im.reshape(1, n_state), (S5_SUBLANES, n_state))
    rows = nj * bp
    gb = S5_GROUP_BLOCK
    kw = t_len * c
    assert gb * c == LANES
    n_blk = g // gb
    src = jnp.arange(gb * kw)
    tau, r, ch = src // (gb * c), (src // c) % gb, src % c
    perm = jax.nn.one_hot(r * kw + tau * c + ch, gb * kw, dtype=BF16)
    x = pl.pallas_call(
        functools.partial(_s5_pack_body, b),
        grid=(n_blk,),
        in_specs=[pl.BlockSpec((b, nj, t_len, LANES), lambda mi: (0, 0, 0, mi)),
                  pl.BlockSpec(perm.shape, lambda mi: (0, 0), pipeline_mode=pl.Buffered(1))],
        out_specs=pl.BlockSpec((nj, bp * gb * kw), lambda mi: (0, mi)),
        out_shape=jax.ShapeDtypeStruct((nj, n_blk * bp * gb * kw), BF16),
        name="s5_pack",
        compiler_params=pltpu.CompilerParams(dimension_semantics=("parallel",), vmem_limit_bytes=VMEM_LIMIT),
    )(h3.reshape(b, nj, t_len, d), perm)
    x = jnp.transpose(x.reshape(nj, n_blk, bp, gb * kw), (0, 2, 1, 3)).reshape(rows, g * kw)
    y = pl.pallas_call(
        functools.partial(_s5_body, bp), grid=(g // gb,),
        in_specs=[pl.BlockSpec((rows, gb * kw), lambda i: (0, i)),
                  pl.BlockSpec((gb, kw, kw), lambda i: (i, 0, 0)),
                  pl.BlockSpec((gb, kw, 2 * S5_STATE), lambda i: (i, 0, 0)),
                  pl.BlockSpec((gb, S5_STATE, kw), lambda i: (i, 0, 0)),
                  pl.BlockSpec((gb, S5_STATE, kw), lambda i: (i, 0, 0)),
                  pl.BlockSpec((S5_SUBLANES, gb * S5_STATE), lambda i: (0, i)),
                  pl.BlockSpec((S5_SUBLANES, gb * S5_STATE), lambda i: (0, i))],
        out_specs=pl.BlockSpec((rows, gb * kw), lambda i: (0, i)),
        out_shape=jax.ShapeDtypeStruct((rows, g * kw), BF16),
        scratch_shapes=[pltpu.VMEM((rows, gb * S5_STATE), F32), pltpu.VMEM((rows, gb * S5_STATE), F32)],
        name="s5_chunks",
        compiler_params=pltpu.CompilerParams(dimension_semantics=("parallel",), vmem_limit_bytes=VMEM_LIMIT),
    )(x, m, bc, cc_re, cc_im, at_re, at_im)
    y = pl.pallas_call(
        _s5_unpack_body,
        grid=(n_blk,),
        in_specs=[pl.BlockSpec((nj, b * gb * kw), lambda mi: (0, mi)),
                  pl.BlockSpec(perm.shape, lambda mi: (0, 0), pipeline_mode=pl.Buffered(1))],
        out_specs=pl.BlockSpec((b, nj, t_len, LANES), lambda mi: (0, 0, 0, mi)),
        out_shape=jax.ShapeDtypeStruct((b, nj, t_len, d), F32),
        name="s5_unpack",
        compiler_params=pltpu.CompilerParams(dimension_semantics=("parallel",), vmem_limit_bytes=VMEM_LIMIT),
    )(jnp.transpose(y.reshape(nj, bp, n_blk, gb * kw)[:, :b], (0, 2, 1, 3)).reshape(nj, n_blk * b * gb * kw), perm.T)
    return y.reshape(b * lp, d)


def _rope_block(x, cm, s1, s2):
    return (x * cm + pltpu.roll(x, LANES - MLA_ROPE // 2, axis=1) * s1 + pltpu.roll(x, MLA_ROPE // 2, axis=1) * s2)


def _mla_in_body(h_ref, qcm_ref, qs1_ref, qs2_ref, kcm_ref, ks1_ref, ks2_ref,
                 win_ref, qg_ref, kvg_ref, wuq_ref, wuk_ref, wuv_ref, q_ref, k_ref, vt_ref):
    c = _dot(h_ref[...].astype(BF16), win_ref[...])
    c_q = _rms(c[:, :MLA_Q_RANK], qg_ref[...]).astype(BF16)
    c_kv = _rms(c[:, MLA_Q_RANK:MLA_Q_RANK + MLA_KV_RANK], kvg_ref[...]).astype(BF16)
    k_r = _rope_block(c[:, MLA_Q_RANK + MLA_KV_RANK:], kcm_ref[...], ks1_ref[...], ks2_ref[...])
    q = _dot(c_q, wuq_ref[...])
    k = _dot(c_kv, wuk_ref[...])
    qcm, qs1, qs2 = qcm_ref[...], qs1_ref[...], qs2_ref[...]
    for h in range(MLA_HEADS):
        sl = slice(h * LANES, (h + 1) * LANES)
        q_ref[:, sl] = _rope_block(q[:, sl], qcm, qs1, qs2).astype(q_ref.dtype)
        k_ref[:, sl] = (k[:, sl] + k_r).astype(k_ref.dtype)
    vt_ref[...] = _dot(c_kv, wuv_ref[...]).T.astype(vt_ref.dtype)


def _mla_attn_body(tq, nq, q_ref, k_ref, vt_ref, mask_ref, o_ref,
                   m_all, l_all, acc_all, m_scr, l_scr, acc_scr, s0_scr, s1_scr, p0_scr, p1_scr, mx0_scr, mx1_scr,
                   exact_ref):
    def scores(kj, qi, rows=tq):
        k0 = pl.multiple_of(kj * tq, tq)
        q0 = pl.multiple_of(qi * tq, tq)
        mask = mask_ref[jnp.minimum(qi - kj, 1), 0:rows, :]
        return jnp.concatenate(
            [_dot_nt(k_ref[pl.ds(k0, rows), hh * LANES:(hh + 1) * LANES],
                     q_ref[pl.ds(q0, tq), hh * LANES:(hh + 1) * LANES]) + mask for hh in range(2)],
            axis=1)

    row = lax.broadcasted_iota(jnp.int32, (2 * MLA_V, tq), 0)

    def finish(qi, l, acc):
        a = acc * (1.0 / l)
        o_t = jnp.where(row < MLA_V, a[:, :tq], a[:, tq:])
        o_ref[pl.ds(pl.multiple_of(qi * tq, tq), tq), :] = o_t.T.astype(o_ref.dtype)

    _flash_head(tq, nq, scores, vt_ref, finish, m_all, l_all, acc_all, m_scr, l_scr, acc_scr,
                (s0_scr, s1_scr), (p0_scr, p1_scr), (mx0_scr, mx1_scr), exact_ref)


def _mla_mixer(h3, w_in, q_norm_g, kv_norm_g, w_uq, w_ukv):
    b, lp, d = h3.shape
    n = b * lp
    nh = MLA_HEADS
    qk = MLA_NOPE + MLA_ROPE
    half = MLA_ROPE // 2
    w_kr = jnp.pad(w_in[:, MLA_Q_RANK + MLA_KV_RANK:], ((0, 0), (MLA_NOPE, LANES - qk)))
    w_in_all = jnp.concatenate([w_in[:, :MLA_Q_RANK + MLA_KV_RANK], w_kr], axis=1).astype(BF16)
    w_uq_p = jnp.pad(w_uq.reshape(MLA_Q_RANK, nh, qk), ((0, 0), (0, 0), (0, LANES - qk)))
    w_uq_p = w_uq_p.reshape(MLA_Q_RANK, nh * LANES).astype(BF16)
    w_ukv3 = w_ukv.reshape(MLA_KV_RANK, nh, MLA_NOPE + MLA_V)
    w_uk_p = jnp.pad(w_ukv3[:, :, :MLA_NOPE], ((0, 0), (0, 0), (0, LANES - MLA_NOPE)))
    w_uk_p = w_uk_p.reshape(MLA_KV_RANK, nh * LANES).astype(BF16)
    w_uv = w_ukv3[:, :, MLA_NOPE:].reshape(MLA_KV_RANK, nh * MLA_V).astype(BF16)
    pos = jnp.arange(lp, dtype=F32)
    inv_freq = ROPE_BASE ** (-jnp.arange(0, MLA_ROPE, 2, dtype=F32) / MLA_ROPE)
    ang = pos[:, None] * inv_freq[None, :]
    cos, sin = jnp.cos(ang), jnp.sin(ang)
    z = lambda w: jnp.zeros((lp, w), F32)
    cm = jnp.concatenate([jnp.ones((lp, MLA_NOPE), F32), cos, cos, z(LANES - qk)], axis=1)
    s1 = jnp.concatenate([z(MLA_NOPE), -sin, z(LANES - MLA_NOPE - half)], axis=1)
    s2 = jnp.concatenate([z(MLA_NOPE + half), sin, z(LANES - qk)], axis=1)
    scale = qk ** -0.5 * LOG2E
    tq = _pick_tile(lp, (384, 256, 128))
    nk = lp // tq
    q, k, vt = _row_call(
        _mla_in_body, [h3.reshape(n, d)],
        [w_in_all, q_norm_g.reshape(1, MLA_Q_RANK), kv_norm_g.reshape(1, MLA_KV_RANK), w_uq_p, w_uk_p, w_uv],
        [(nh * LANES, BF16), (nh * LANES, BF16)], tq, "mla_in",
        pos_ins=[cm * scale, s1 * scale, s2 * scale, cm, s1, s2], lp=lp, t_outs=[(nh * MLA_V, BF16)])
    q, k = q.reshape(b, lp, nh * LANES), k.reshape(b, lp, nh * LANES)
    ti = jnp.arange(tq)
    mask = jnp.stack([jnp.where(ti[:, None] <= ti[None, :], 0.0, NEG_INF), jnp.zeros((tq, tq))]).astype(F32)
    o = pl.pallas_call(
        functools.partial(_mla_attn_body, tq, nk),
        grid=(b, nh // 2),
        in_specs=[pl.BlockSpec((None, lp, 2 * LANES), lambda bi, hi: (bi, 0, hi)),
                  pl.BlockSpec((None, lp, 2 * LANES), lambda bi, hi: (bi, 0, hi)),
                  pl.BlockSpec((None, nk, 2 * MLA_V, tq), lambda bi, hi: (bi, 0, hi, 0)),
                  pl.BlockSpec((2, tq, tq), lambda bi, hi: (0, 0, 0))],
        out_specs=pl.BlockSpec((None, lp, 2 * MLA_V), lambda bi, hi: (bi, 0, hi)),
        out_shape=jax.ShapeDtypeStruct((b, lp, nh * MLA_V), BF16),
        scratch_shapes=_flash_scratch(2 * MLA_V, tq, nk),
        name="mla_attn",
        compiler_params=pltpu.CompilerParams(dimension_semantics=("parallel", "parallel"),
                                             vmem_limit_bytes=VMEM_LIMIT),
    )(q, k, vt, mask)
    return o.reshape(n, nh * MLA_V)


def kernel(x, meta, rel_bias, ln_g, ln_b, ffn_w1, ffn_w3, ffn_w2, ssd_w_in, ssd_conv_w, ssd_conv_b, ssd_dt_bias, ssd_a_log, ssd_d, ssd_norm_g, ssd_w_out, diff_w_qkv, diff_lam_q1, diff_lam_k1, diff_lam_q2, diff_lam_k2, diff_subln_g, diff_w_out, s5_lam_re, s5_lam_im, s5_log_step, s5_b_re, s5_b_im, s5_c_re, s5_c_im, s5_d, s5_w_glu, s5_b_glu, mla_w_in, mla_q_norm_g, mla_kv_norm_g, mla_w_uq, mla_w_ukv, mla_w_out):
    b, seq, d = x.shape
    l = seq + N_META
    lp = -(-l // SEQ_ALIGN) * SEQ_ALIGN
    n = b * lp
    h = jnp.concatenate([jnp.broadcast_to(meta[None].astype(x.dtype), (b, N_META, d)), x,
                         jnp.zeros((b, lp - l, d), x.dtype)], axis=1).reshape(n, d)
    w1b, w3b, w2b = ffn_w1.astype(BF16), ffn_w3.astype(BF16), ffn_w2.astype(BF16)
    for i in range(DEPTH):
        kind, j = i % 4, i // 4
        h = _ffn(h, w1b, w3b, w2b, ln_g[i, 0], ln_b[i, 0], (i, 0))
        h3 = h.reshape(b, lp, d)
        if kind == 0:
            y = _ssd_mixer(h3, ssd_w_in[j], ssd_conv_w[j], ssd_conv_b[j], ssd_dt_bias[j], ssd_a_log[j],
                           ssd_d[j], ssd_norm_g[j])
            w_out, b_out = ssd_w_out[j], None
        elif kind == 1:
            lam_init = 0.8 - 0.6 * math.exp(-0.3 * i)
            y = _diff_mixer(h3, diff_w_qkv[j], diff_lam_q1[j], diff_lam_k1[j], diff_lam_q2[j], diff_lam_k2[j],
                            diff_subln_g[j], rel_bias, lam_init)
            w_out, b_out = diff_w_out[j], None
        elif kind == 2:
            y = _s5_mixer(h3, s5_lam_re[j], s5_lam_im[j], s5_log_step[j], s5_b_re[j], s5_b_im[j], s5_c_re[j],
                          s5_c_im[j], s5_d[j])
            w_out, b_out = s5_w_glu[j], s5_b_glu[j]
        else:
            y = _mla_mixer(h3, mla_w_in[j], mla_q_norm_g[j], mla_kv_norm_g[j], mla_w_uq[j], mla_w_ukv[j])
            w_out, b_out = mla_w_out[j], None
        h = _mix_ffn(y, h, w_out, b_out, ln_g[i, 1], ln_b[i, 1], w1b, w3b, w2b, ln_g[i, 2], ln_b[i, 2], (i, 1))
    return h.reshape(b, lp, d)[:, N_META:l]
```

```python
import functools
import math

import jax
import jax.numpy as jnp
from jax import lax
from jax.experimental import pallas as pl
from jax.experimental.pallas import tpu as pltpu

F32 = jnp.float32
BF16 = jnp.bfloat16

N_META = 16
DEPTH = 4
ALPHA = (2.0 * DEPTH) ** 0.25
LN_EPS = 1e-5
RMS_EPS = 1e-6
NEG_INF = -1e30
LOG2E = math.log2(math.e)
ATTN_REF_KEYS = 16
ATTN_EDGE_UNROLL = 4
SOFTMAX_SUM_LIMIT = 2.0 ** 100
F32_FINITE_LIMIT = 3.0e38

SSD_HEAD_DIM = 64
SSD_HEADS = 32
SSD_GROUPS = 8
SSD_STATE = 128
SSD_CONV = 4
SSD_CHUNK = 128

DIFF_HEADS = 8
DIFF_HEAD_DIM = 64
DIFF_V_DIM = 128
REL_BUCKETS = 32
REL_MAX_DIST = 128

S5_GROUP = 16
S5_GROUPS = 64
S5_STATE = 64
S5_CHUNK = 16
S5_GROUP_BLOCK = 8
SUBLANES = 8
S5_SUBLANES = SUBLANES
S5_UNROLL = 4

MLA_HEADS = 16
MLA_Q_RANK = 384
MLA_KV_RANK = 256
MLA_NOPE = 64
MLA_ROPE = 32
MLA_V = 64
ROPE_BASE = 10000.0

FFN_ROW_TILES = (512, 384, 256, 128)
FFN_CHUNKS = 11

LANES = 128
SEQ_ALIGN = 128
VMEM_LIMIT = 56 * 1024 * 1024


def _pick_tile(n, candidates):
    for c in candidates:
        if n % c == 0:
            return c
    raise ValueError(f"no tile for {n}")


def _layer_norm(z, g, b):
    mu = jnp.mean(z, axis=-1, keepdims=True)
    d = z - mu
    var = jnp.mean(d * d, axis=-1, keepdims=True)
    return d * lax.rsqrt(var + LN_EPS) * g + b


def _rms(z, g):
    return z * lax.rsqrt(jnp.mean(z * z, axis=-1, keepdims=True) + RMS_EPS) * g


def _silu(x):
    return x * (1.0 / (1.0 + jnp.exp(-x)))


def _dot(a, b):
    return jnp.dot(a, b, preferred_element_type=F32)


def _dot_nt(a, b):
    return lax.dot_general(a, b, (((1,), (1,)), ((), ())), preferred_element_type=F32)


def _row_call(body, row_ins, const_ins, outs, tm, name, pos_ins=(), lp=None, t_outs=()):
    n = row_ins[0].shape[0]
    assert n % tm == 0
    in_specs = [pl.BlockSpec((tm, a.shape[1]), lambda i: (i, 0)) for a in row_ins]
    if pos_ins or t_outs:
        assert lp % tm == 0
        nb = lp // tm
        in_specs += [pl.BlockSpec((tm, a.shape[1]), lambda i: (i % nb, 0)) for a in pos_ins]
    const_arrays = []
    for c in const_ins:
        arr, lead = c if isinstance(c, tuple) else (c, ())
        const_arrays.append(arr)
        in_specs.append(pl.BlockSpec((None,) * len(lead) + arr.shape[len(lead):],
                                     functools.partial(lambda lead_, i: lead_ + (0, 0), tuple(lead)),
                                     pipeline_mode=pl.Buffered(1)))
    out_specs = [pl.BlockSpec((tm, w), lambda i: (i, 0)) for (w, _) in outs]
    out_shape = [jax.ShapeDtypeStruct((n, w), dt) for (w, dt) in outs]
    out_specs += [pl.BlockSpec((None, None, w, tm), lambda i: (i // nb, i % nb, 0, 0)) for (w, _) in t_outs]
    out_shape += [jax.ShapeDtypeStruct((n // lp, nb, w, tm), dt) for (w, dt) in t_outs]
    res = pl.pallas_call(
        body, grid=(n // tm,), in_specs=in_specs, out_specs=out_specs, out_shape=out_shape, name=name,
        compiler_params=pltpu.CompilerParams(dimension_semantics=("parallel",), vmem_limit_bytes=VMEM_LIMIT),
    )(*row_ins, *pos_ins, *const_arrays)
    return res


def _ffn_rows(x, n_chunks, w1_ref, w3_ref, w2_ref, g_ref, b_ref):
    xb = x.astype(BF16)
    fc = w1_ref.shape[1] // n_chunks
    y = None
    for c in range(n_chunks):
        sl = slice(c * fc, (c + 1) * fc)
        gate = _dot(xb, w1_ref[:, sl])
        up = _dot(xb, w3_ref[:, sl])
        a = (_silu(gate) * up).astype(BF16)
        part = _dot(a, w2_ref[sl, :])
        y = part if y is None else y + part
    return _layer_norm(ALPHA * x + 0.5 * y, g_ref[...], b_ref[...])


def _ffn_body(n_chunks, h_ref, w1_ref, w3_ref, w2_ref, g_ref, b_ref, o_ref):
    o_ref[...] = _ffn_rows(h_ref[...], n_chunks, w1_ref, w3_ref, w2_ref, g_ref, b_ref)


def _mix_ffn_body(n_chunks, glu, y_ref, h_ref, wo_ref, bo_ref, gm_ref, bm_ref,
                  w1_ref, w3_ref, w2_ref, g_ref, b_ref, o_ref):
    m = _dot(y_ref[...].astype(BF16), wo_ref[...])
    if glu:
        m = m + bo_ref[...]
        d = o_ref.shape[1]
        m = m[:, :d] * (1.0 / (1.0 + jnp.exp(-m[:, d:])))
    x = _layer_norm(ALPHA * h_ref[...] + m, gm_ref[...], bm_ref[...])
    o_ref[...] = _ffn_rows(x, n_chunks, w1_ref, w3_ref, w2_ref, g_ref, b_ref)


def _ffn_chunks(f, n_chunks):
    return n_chunks or (FFN_CHUNKS if f % (FFN_CHUNKS * LANES) == 0 else 1)


def _ffn(h2, w1, w3, w2, g, b, idx, tm=None, n_chunks=None, name="ffn"):
    n, d = h2.shape
    tm = tm or _pick_tile(n, FFN_ROW_TILES)
    (out,) = _row_call(functools.partial(_ffn_body, _ffn_chunks(w1.shape[-1], n_chunks)), [h2],
                       [(w1, idx), (w3, idx), (w2, idx), g.reshape(1, d), b.reshape(1, d)],
                       [(d, F32)], tm, name)
    return out


def _mix_ffn(y2, h2, w_out, b_out, g_mix, b_mix, w1, w3, w2, g, b, idx, tm=None, n_chunks=None, name="mix_ffn"):
    n, d = h2.shape
    tm = tm or _pick_tile(n, FFN_ROW_TILES)
    glu = b_out is not None
    wo = w_out.astype(BF16)
    bo = (b_out if glu else jnp.zeros((wo.shape[1],), F32)).reshape(1, wo.shape[1])
    (out,) = _row_call(functools.partial(_mix_ffn_body, _ffn_chunks(w1.shape[-1], n_chunks), glu), [y2, h2],
                       [wo, bo, g_mix.reshape(1, d), b_mix.reshape(1, d),
                        (w1, idx), (w3, idx), (w2, idx), g.reshape(1, d), b.reshape(1, d)],
                       [(d, F32)], tm, name)
    return out


def _ssd_in_body(h_ref, w_ref, xbc_ref, z_ref, dt_ref):
    r = _dot(h_ref[...].astype(BF16), w_ref[...])
    nx, nz = xbc_ref.shape[1], z_ref.shape[1]
    xbc_ref[...] = r[:, :nx]
    z_ref[...] = r[:, nx:nx + nz]
    dt_ref[...] = r[:, nx + nz:]


def _softplus(x):
    return jnp.maximum(x, 0.0) + jnp.log1p(jnp.exp(-jnp.abs(x)))


def _ssd_body(xbc_ref, dt_ref, z_ref, convw_ref, convb_ref, dtb_ref, alog_ref, dskip_ref, normg_ref,
              y_ref, buf_scr, state_scr, y_scr):
    t = SSD_CHUNK
    d_inner = SSD_HEADS * SSD_HEAD_DIM
    gn = SSD_GROUPS * SSD_STATE
    hp = SSD_HEAD_DIM
    r = SSD_HEADS // SSD_GROUPS

    @pl.when(pl.program_id(1) == 0)
    def _():
        buf_scr[0:SUBLANES, :] = jnp.zeros((SUBLANES, buf_scr.shape[1]), F32)
        state_scr[...] = jnp.zeros(state_scr.shape, F32)

    x = xbc_ref[...]
    buf_scr[SUBLANES:SUBLANES + t, :] = x
    acc = convb_ref[...] + convw_ref[SSD_CONV - 1:SSD_CONV, :] * x
    for j in range(1, SSD_CONV):
        acc = acc + convw_ref[SSD_CONV - 1 - j:SSD_CONV - j, :] * buf_scr[SUBLANES - j:SUBLANES - j + t, :]
    buf_scr[0:SUBLANES, :] = x[t - SUBLANES:t, :]
    xbc = _silu(acc)

    dt = _softplus(dt_ref[...] + dtb_ref[...])
    a = dt * (-jnp.exp(alog_ref[...]))
    rows = lax.broadcasted_iota(jnp.int32, (t, t), 0)
    cols = lax.broadcasted_iota(jnp.int32, (t, t), 1)
    tri = rows >= cols
    a_cum = jnp.dot(tri.astype(F32), a, precision=lax.Precision.HIGHEST, preferred_element_type=F32)
    a_last = a_cum[t - 1:t, :]
    a_cum_t = a_cum.T
    dt_t = dt.T
    w_end_t = (dt * jnp.exp(a_last - a_cum)).T
    chunk_dec = jnp.exp(a_last)

    for g in range(SSD_GROUPS):
        b_f = xbc[:, d_inner + g * SSD_STATE:d_inner + (g + 1) * SSD_STATE]
        c_g = xbc[:, d_inner + gn + g * SSD_STATE:d_inner + gn + (g + 1) * SSD_STATE].astype(BF16)
        b_g = b_f.astype(BF16)
        b_ft = b_f.T
        cb = _dot_nt(c_g, b_g)
        y_off_g = _dot(c_g, state_scr[:, g * r * hp:(g + 1) * r * hp].astype(BF16))
        for k in range(r):
            h = g * r + k
            col = jnp.broadcast_to(a_cum[:, h:h + 1], (t, t))
            decay = jnp.exp(jnp.where(tri, col - a_cum_t[h:h + 1, :], NEG_INF))
            xh = xbc[:, h * hp:(h + 1) * hp]
            xb = xh.astype(BF16)
            y_diag = _dot((cb * decay * dt_t[h:h + 1, :]).astype(BF16), xb)
            y_off = y_off_g[:, k * hp:(k + 1) * hp] * jnp.exp(col[:, :hp])
            y_scr[:, h * hp:(h + 1) * hp] = y_diag + y_off + dskip_ref[:, h * hp:(h + 1) * hp] * xh
            st = state_scr[:, h * hp:(h + 1) * hp]
            b_w = (b_ft * w_end_t[h:h + 1, :]).astype(BF16)
            state_scr[:, h * hp:(h + 1) * hp] = st * chunk_dec[:, h:h + 1] + _dot(b_w, xb)

    y = y_scr[...] * _silu(z_ref[...])
    gs = d_inner // SSD_GROUPS
    for g in range(SSD_GROUPS):
        sl = slice(g * gs, (g + 1) * gs)
        y_ref[:, sl] = _rms(y[:, sl], normg_ref[:, sl]).astype(y_ref.dtype)


def _ssd_mixer(h3, w_in, conv_w, conv_b, dt_bias, a_log, d_skip, norm_g):
    b, lp, d = h3.shape
    n = b * lp
    d_inner = SSD_HEADS * SSD_HEAD_DIM
    conv_ch = d_inner + 2 * SSD_GROUPS * SSD_STATE
    w_z, w_xbc, w_dt = w_in[:, :d_inner], w_in[:, d_inner:d_inner + conv_ch], w_in[:, d_inner + conv_ch:]
    w_dt = jnp.pad(w_dt, ((0, 0), (0, LANES - SSD_HEADS)))
    w_all = jnp.concatenate([w_xbc, w_z, w_dt], axis=1).astype(BF16)
    tm = _pick_tile(n, (512, 384, 256, 128))
    xbc, z, dt = _row_call(_ssd_in_body, [h3.reshape(n, d)], [w_all],
                           [(conv_ch, F32), (d_inner, F32), (LANES, F32)], tm, "ssd_in")
    t = SSD_CHUNK
    pad1 = lambda v: jnp.pad(v, (0, LANES - SSD_HEADS)).reshape(1, LANES)
    consts = [conv_w, conv_b.reshape(1, conv_ch), pad1(dt_bias), pad1(a_log),
              jnp.repeat(d_skip, SSD_HEAD_DIM).reshape(1, d_inner), norm_g.reshape(1, d_inner)]
    row_spec = lambda w: pl.BlockSpec((None, t, w), lambda i, c: (i, c, 0))
    y = pl.pallas_call(
        _ssd_body, grid=(b, lp // t),
        in_specs=[row_spec(conv_ch), row_spec(LANES), row_spec(d_inner)]
        + [pl.BlockSpec(c.shape, lambda i, c_: (0, 0)) for c in consts],
        out_specs=row_spec(d_inner),
        out_shape=jax.ShapeDtypeStruct((b, lp, d_inner), BF16),
        scratch_shapes=[pltpu.VMEM((SUBLANES + t, conv_ch), F32), pltpu.VMEM((SSD_STATE, d_inner), F32),
                        pltpu.VMEM((t, d_inner), F32)],
        name="ssd_scan",
        compiler_params=pltpu.CompilerParams(dimension_semantics=("parallel", "arbitrary"),
                                             vmem_limit_bytes=VMEM_LIMIT),
    )(xbc.reshape(b, lp, conv_ch), dt.reshape(b, lp, LANES), z.reshape(b, lp, d_inner), *consts)
    return y.reshape(n, d_inner)


def _qkv_body(n_q, scale, h_ref, w_ref, qk_ref, vt_ref):
    r = _dot(h_ref[...].astype(BF16), w_ref[...])
    n_qk = qk_ref.shape[1]
    qk_ref[:, :n_q] = (r[:, :n_q] * scale).astype(qk_ref.dtype)
    qk_ref[:, n_q:] = r[:, n_q:n_qk].astype(qk_ref.dtype)
    vt_ref[...] = r[:, n_qk:].T.astype(vt_ref.dtype)


def _causal_mask_pair(s, tq):
    rows = lax.broadcasted_iota(jnp.int32, s.shape, 0)
    cols = lax.broadcasted_iota(jnp.int32, s.shape, 1)
    cols = jnp.where(cols >= tq, cols - tq, cols)
    return jnp.where(rows <= cols, s, NEG_INF)


def _flash_causal(qi, tq, scores, vt_ref, m_scr, l_scr, acc_scr, s_scrs, p_scrs, mx_scrs):
    m_scr[...] = jnp.full(m_scr.shape, NEG_INF, F32)
    l_scr[...] = jnp.zeros(l_scr.shape, F32)
    acc_scr[...] = jnp.zeros(acc_scr.shape, F32)
    p_scrs[1][...] = jnp.zeros(p_scrs[1].shape, p_scrs[1].dtype)

    def softmax(s, s_max, pv):
        m_prev = m_scr[...]
        m_new = jnp.maximum(m_prev, s_max)
        alpha = jnp.exp2(m_prev - m_new)
        p = jnp.exp2(s - m_new)
        l_scr[...] = alpha * l_scr[...] + jnp.sum(p, axis=0, keepdims=True)
        m_scr[...] = m_new
        acc_scr[...] = alpha * (acc_scr[...] + pv)
        return p.astype(BF16)

    def produce(kj, slot):
        s = scores(kj)
        s_scrs[slot][...] = s
        mx_scrs[slot][...] = jnp.max(s, axis=0, keepdims=True)

    def step(kj, cur, nxt):
        pv = _dot(vt_ref[jnp.maximum(kj - 1, 0)], p_scrs[nxt][...])
        p_scrs[cur][...] = softmax(s_scrs[cur][...], mx_scrs[cur][...], pv)
        produce(kj + 1, nxt)

    produce(0, 0)

    def body(t, carry):
        step(2 * t, 0, 1)
        step(2 * t + 1, 1, 0)
        return carry

    lax.fori_loop(0, lax.shift_right_logical(qi, 1), body, 0)
    pl.when((qi & 1) == 1)(functools.partial(step, qi - 1, 0, 1))

    def last(cur, nxt):
        pv = _dot(vt_ref[jnp.maximum(qi - 1, 0)], p_scrs[nxt][...])
        s = _causal_mask_pair(s_scrs[cur][...], tq)
        p = softmax(s, jnp.max(s, axis=0, keepdims=True), pv)
        acc_scr[...] = acc_scr[...] + _dot(vt_ref[qi], p)

    for par in range(2):
        pl.when((qi & 1) == par)(functools.partial(last, par, 1 - par))


def _flash_head(tq, nq, scores, vt_ref, finish, m_all, l_all, acc_all,
                m_scr, l_scr, acc_scr, s_scrs, p_scrs, mx_scrs, exact_ref):
    def init(qi, carry):
        m_all[qi] = jnp.max(scores(0, qi, ATTN_REF_KEYS), axis=0, keepdims=True)
        l_all[qi] = jnp.zeros(l_all.shape[1:], F32)
        acc_all[qi] = jnp.zeros(acc_all.shape[1:], F32)
        return carry

    lax.fori_loop(0, nq, init, 0, unroll=ATTN_EDGE_UNROLL)

    def probs(kj, qi, slot):
        p = jnp.exp2(scores(kj, qi) - m_all[qi])
        l_all[qi] += jnp.sum(p, axis=0, keepdims=True)
        p_scrs[slot][...] = p.astype(BF16)

    def step(prev, cur, prv):
        kjp, qip = prev
        wrap = qip + 1 >= nq
        kj = jnp.where(wrap, kjp + 1, kjp)
        qi = jnp.where(wrap, kjp + 1, qip + 1)
        p = jnp.exp2(scores(kj, qi) - m_all[qi])
        acc_all[qip] += _dot(vt_ref[kjp], p_scrs[prv][...])
        l_all[qi] += jnp.sum(p, axis=0, keepdims=True)
        p_scrs[cur][...] = p.astype(BF16)
        return kj, qi

    n_steps = nq * (nq + 1) // 2 - 1
    probs(0, 0, 0)

    def body(t, prev):
        return step(step(prev, 1, 0), 0, 1)

    prev = lax.fori_loop(0, n_steps // 2, body, (jnp.int32(0), jnp.int32(0)))
    if n_steps % 2:
        prev = step(prev, 1, 0)
    acc_all[prev[1]] += _dot(vt_ref[prev[0]], p_scrs[n_steps % 2][...])

    ok = jnp.all(l_all[...] < SOFTMAX_SUM_LIMIT) & jnp.all(jnp.abs(acc_all[...]) < F32_FINITE_LIMIT)
    exact_ref[0] = jnp.where(ok, 0, 1)

    @pl.when(exact_ref[0] == 0)
    def _():
        def fin(qi, carry):
            finish(qi, l_all[qi], acc_all[qi])
            return carry

        lax.fori_loop(0, nq, fin, 0, unroll=ATTN_EDGE_UNROLL)

    @pl.when(exact_ref[0] == 1)
    def _():
        def redo(qi, carry):
            _flash_causal(qi, tq, lambda kj: scores(kj, qi), vt_ref, m_scr, l_scr, acc_scr, s_scrs, p_scrs, mx_scrs)
            finish(qi, l_scr[...], acc_scr[...])
            return carry

        lax.fori_loop(0, nq, redo, 0)


def _flash_scratch(e, tq, nq):
    w = 2 * tq
    return [pltpu.VMEM((nq, 1, w), F32), pltpu.VMEM((nq, 1, w), F32), pltpu.VMEM((nq, e, w), F32),
            pltpu.VMEM((1, w), F32), pltpu.VMEM((1, w), F32), pltpu.VMEM((e, w), F32),
            pltpu.VMEM((tq, w), F32), pltpu.VMEM((tq, w), F32), pltpu.VMEM((tq, w), BF16), pltpu.VMEM((tq, w), BF16),
            pltpu.VMEM((1, w), F32), pltpu.VMEM((1, w), F32), pltpu.SMEM((1,), jnp.int32)]


def _t5_bucket(dist):
    max_exact = REL_BUCKETS // 2
    d = jnp.maximum(dist, 0)
    df = jnp.maximum(d, max_exact).astype(F32)
    large = max_exact + (jnp.log(df / max_exact) / math.log(REL_MAX_DIST / max_exact)
                         * (REL_BUCKETS - max_exact)).astype(jnp.int32)
    large = jnp.minimum(large, REL_BUCKETS - 1)
    return jnp.where(d < max_exact, d, large)


def _diff_attn_body(tq, nq, lam_init, q_ref, k_ref, vt_ref, bias_ref, lam_ref, g_ref, o_ref,
                    m_all, l_all, acc_all, m_scr, l_scr, acc_scr, s0_scr, s1_scr, p0_scr, p1_scr, mx0_scr, mx1_scr,
                    exact_ref, qcat_all):
    lane = lax.broadcasted_iota(jnp.int32, (tq, q_ref.shape[1]), 1)

    def prep(qi, carry):
        q = q_ref[pl.ds(pl.multiple_of(qi * tq, tq), tq), :]
        zero = jnp.zeros_like(q)
        qcat_all[qi, 0:tq, :] = jnp.where(lane < DIFF_HEAD_DIM, q, zero)
        qcat_all[qi, tq:2 * tq, :] = jnp.where(lane >= DIFF_HEAD_DIM, q, zero)
        return carry

    lax.fori_loop(0, nq, prep, 0)

    def scores(kj, qi, rows=tq):
        start = pl.multiple_of(kj * tq, tq)
        bias = bias_ref[jnp.minimum(qi - kj, 2), 0:rows, :]
        return _dot_nt(k_ref[pl.ds(start, rows), :], qcat_all[qi]) + jnp.concatenate([bias, bias], axis=1)

    def finish(qi, l, acc):
        a = acc * (1.0 / l)
        o_t = a[:, :tq] - lam_ref[...] * a[:, tq:]
        o_ref[pl.ds(pl.multiple_of(qi * tq, tq), tq), :] = (
            _rms(o_t.T, g_ref[...]) * (1.0 - lam_init)).astype(o_ref.dtype)

    _flash_head(tq, nq, scores, vt_ref, finish, m_all, l_all, acc_all, m_scr, l_scr, acc_scr,
                (s0_scr, s1_scr), (p0_scr, p1_scr), (mx0_scr, mx1_scr), exact_ref)


def _diff_mixer(h3, w_qkv, lam_q1, lam_k1, lam_q2, lam_k2, subln_g, rel_bias, lam_init):
    b, lp, d = h3.shape
    n = b * lp
    hd = DIFF_HEADS * 2 * DIFF_HEAD_DIM
    tq = _pick_tile(lp, (384, 256, 128))
    assert tq >= REL_MAX_DIST
    nk = lp // tq
    qk, vt = _row_call(functools.partial(_qkv_body, hd, DIFF_HEAD_DIM ** -0.5 * LOG2E), [h3.reshape(n, d)],
                       [w_qkv.astype(BF16)], [(2 * hd, BF16)], tq, "diff_qkv", lp=lp, t_outs=[(hd, BF16)])
    qk = qk.reshape(b, lp, 2 * hd)
    m2 = 2 * tq
    kk = jnp.arange(m2)
    dist = jnp.where(kk < tq, kk, kk - m2)[None, :] + (jnp.arange(3) * tq)[:, None]
    tab = jnp.transpose(rel_bias[_t5_bucket(dist)].astype(F32) * LOG2E, (2, 0, 1))
    tab = jnp.where((jnp.arange(3)[:, None] == 0) & (kk[None, :] >= tq), NEG_INF, tab)
    nh = DIFF_HEADS
    bias = jnp.tile(tab, (1, 1, tq))[..., :tq * (m2 - 1)].reshape(nh, 3, tq, m2 - 1)[..., :tq]
    lam = (jnp.exp(jnp.sum(lam_q1.astype(F32) * lam_k1.astype(F32)))
           - jnp.exp(jnp.sum(lam_q2.astype(F32) * lam_k2.astype(F32))) + lam_init)
    w = 2 * DIFF_HEAD_DIM
    o = pl.pallas_call(
        functools.partial(_diff_attn_body, tq, nk, lam_init),
        grid=(b, nh),
        in_specs=[pl.BlockSpec((None, lp, w), lambda bi, hi: (bi, 0, hi)),
                  pl.BlockSpec((None, lp, w), lambda bi, hi: (bi, 0, nh + hi)),
                  pl.BlockSpec((None, nk, DIFF_V_DIM, tq), lambda bi, hi: (bi, 0, hi, 0)),
                  pl.BlockSpec((None, 3, tq, tq), lambda bi, hi: (hi, 0, 0, 0)),
                  pl.BlockSpec((1, 1), lambda bi, hi: (0, 0)),
                  pl.BlockSpec((1, DIFF_V_DIM), lambda bi, hi: (0, 0))],
        out_specs=pl.BlockSpec((None, lp, DIFF_V_DIM), lambda bi, hi: (bi, 0, hi)),
        out_shape=jax.ShapeDtypeStruct((b, lp, nh * DIFF_V_DIM), BF16),
        scratch_shapes=_flash_scratch(DIFF_V_DIM, tq, nk) + [pltpu.VMEM((nk, 2 * tq, w), BF16)],
        name="diff_attn",
        compiler_params=pltpu.CompilerParams(dimension_semantics=("parallel", "parallel"),
                                             vmem_limit_bytes=VMEM_LIMIT),
    )(qk, qk, vt, bias, lam.reshape(1, 1), subln_g.reshape(1, DIFF_V_DIM))
    return o.reshape(n, nh * DIFF_V_DIM)


def _gelu_tanh(x):
    return 0.5 * x * (1.0 + jnp.tanh(math.sqrt(2.0 / math.pi) * (x + 0.044715 * (x * x * x))))


def _s5_body(bp, x_ref, m_ref, bc_ref, ccre_ref, ccim_ref, are_ref, aim_ref, y_ref, sre_scr, sim_scr):
    rows = x_ref.shape[0]
    gb = m_ref.shape[0]
    kw = m_ref.shape[1]
    p = ccre_ref.shape[1]
    for g in range(gb):
        e = _dot(x_ref[:, g * kw:(g + 1) * kw], bc_ref[g])
        sre_scr[:, g * p:(g + 1) * p] = e[:, :p]
        sim_scr[:, g * p:(g + 1) * p] = e[:, p:]

    ar, ai = are_ref[...], aim_ref[...]
    sub = lax.broadcasted_iota(jnp.int32, ar.shape, 0)
    per_tile = S5_SUBLANES // bp

    def tile_step(r, carry):
        pr, pi = carry
        r0 = pl.multiple_of(r * S5_SUBLANES, S5_SUBLANES)
        er, ei = sre_scr[pl.ds(r0, S5_SUBLANES), :], sim_scr[pl.ds(r0, S5_SUBLANES), :]
        in_r, in_i = pr, pi
        for c in range(per_tile):
            nr = ar * pr - ai * pi + er
            ni = ar * pi + ai * pr + ei
            if per_tile > 1:
                nr = pltpu.roll(nr, bp, axis=0)
                ni = pltpu.roll(ni, bp, axis=0)
            if c + 1 < per_tile:
                sel = (sub >= (c + 1) * bp) & (sub < (c + 2) * bp)
                in_r, in_i = jnp.where(sel, nr, in_r), jnp.where(sel, ni, in_i)
                pr, pi = nr, ni
            else:
                pr, pi = nr, ni
        if per_tile > 1:
            pr = jnp.where(sub < bp, pr, pltpu.roll(pr, bp, axis=0))
            pi = jnp.where(sub < bp, pi, pltpu.roll(pi, bp, axis=0))
        sre_scr[pl.ds(r0, S5_SUBLANES), :] = in_r
        sim_scr[pl.ds(r0, S5_SUBLANES), :] = in_i
        return pr, pi

    zero = jnp.zeros(ar.shape, F32)
    lax.fori_loop(0, rows // S5_SUBLANES, tile_step, (zero, zero), unroll=S5_UNROLL)

    for g in range(gb):
        xg = x_ref[:, g * kw:(g + 1) * kw]
        y = (_dot(xg, m_ref[g]) + _dot(sre_scr[:, g * p:(g + 1) * p].astype(BF16), ccre_ref[g])
             + _dot(sim_scr[:, g * p:(g + 1) * p].astype(BF16), ccim_ref[g]))
        y_ref[:, g * kw:(g + 1) * kw] = _gelu_tanh(y).astype(y_ref.dtype)


def _s5_pack_body(n_batch, h_ref, perm_ref, x_ref):
    t_len = h_ref.shape[1]
    xcat = jnp.concatenate([h_ref[:, t, :].astype(BF16) for t in range(t_len)], axis=1)
    x = _dot(xcat, perm_ref[...]).astype(x_ref.dtype)
    x_ref[...] = jnp.where(pl.program_id(0) < n_batch, x, jnp.zeros_like(x))


def _s5_unpack_body(y_ref, perm_ref, o_ref):
    t_len = o_ref.shape[1]
    res = _dot(y_ref[...], perm_ref[...])
    for t in range(t_len):
        o_ref[:, t, :] = res[:, t * LANES:(t + 1) * LANES]


def _s5_chunk_operators(lam_re, lam_im, log_step, b_re, b_im, c_re, c_im, d_skip):
    hi = lax.Precision.HIGHEST
    t_len = S5_CHUNK
    step = jnp.exp(log_step.astype(F32))[:, None]
    lr = jnp.minimum(lam_re.astype(F32), -1e-4)
    li = lam_im.astype(F32)
    mag = jnp.exp(lr * step)
    ar, ai = mag * jnp.cos(li * step), mag * jnp.sin(li * step)
    den = lr * lr + li * li
    cr = ((ar - 1.0) * lr + ai * li) / den
    ci = (ai * lr - (ar - 1.0) * li) / den
    br, bi = b_re.astype(F32), b_im.astype(F32)
    bbr = cr[..., None] * br - ci[..., None] * bi
    bbi = cr[..., None] * bi + ci[..., None] * br
    k = jnp.arange(t_len + 1, dtype=F32)[:, None, None]
    pmag = jnp.exp(k * (lr * step))
    pr, pi = pmag * jnp.cos(k * (li * step)), pmag * jnp.sin(k * (li * step))
    abr = pr[:t_len, ..., None] * bbr - pi[:t_len, ..., None] * bbi
    abi = pr[:t_len, ..., None] * bbi + pi[:t_len, ..., None] * bbr
    cre, cim = c_re.astype(F32), c_im.astype(F32)
    kern = (jnp.einsum('gop,kgpc->kgoc', cre, abr, precision=hi)
            - jnp.einsum('gop,kgpc->kgoc', cim, abi, precision=hi))
    kern = kern.at[0].add(jax.vmap(jnp.diag)(d_skip.astype(F32)))
    sig = jnp.arange(t_len)
    lag = sig[None, :] - sig[:, None]
    m = jnp.where((lag >= 0)[..., None, None, None], kern[jnp.maximum(lag, 0)], 0.0)
    g, o, c = kern.shape[1:]
    m = jnp.transpose(m, (2, 0, 4, 1, 3)).reshape(g, t_len * c, t_len * o)
    rev = t_len - 1 - sig
    p = bbr.shape[1]
    bc = jnp.concatenate([jnp.transpose(abr[rev], (1, 0, 3, 2)).reshape(g, t_len * c, p),
                          jnp.transpose(abi[rev], (1, 0, 3, 2)).reshape(g, t_len * c, p)], axis=2)
    car = cre[None] * pr[1:, :, None, :] - cim[None] * pi[1:, :, None, :]
    cai = cre[None] * pi[1:, :, None, :] + cim[None] * pr[1:, :, None, :]
    cc_re = jnp.transpose(car, (1, 3, 0, 2)).reshape(g, p, t_len * o)
    cc_im = -jnp.transpose(cai, (1, 3, 0, 2)).reshape(g, p, t_len * o)
    return (m.astype(BF16), bc.astype(BF16), cc_re.astype(BF16), cc_im.astype(BF16),
            pr[t_len].reshape(g * p), pi[t_len].reshape(g * p))


def _s5_mixer(h3, lam_re, lam_im, log_step, b_re, b_im, c_re, c_im, d_skip):
    b, lp, d = h3.shape
    t_len, g, c = S5_CHUNK, S5_GROUPS, S5_GROUP
    assert lp % t_len == 0 and b <= S5_SUBLANES
    bp = 4 if b <= 4 else S5_SUBLANES
    nj = lp // t_len
    m, bc, cc_re, cc_im, at_re, at_im = _s5_chunk_operators(lam_re, lam_im, log_step, b_re, b_im, c_re, c_im, d_skip)
    n_state = g * S5_STATE
    at_re = jnp.broadcast_to(at_re.reshape(1, n_state), (S5_SUBLANES, n_state))
    at_im = jnp.broadcast_to(at_im.reshape(1, n_state), (S5_SUBLANES, n_state))
    rows = nj * bp
    gb = S5_GROUP_BLOCK
    kw = t_len * c
    assert gb * c == LANES
    n_blk = g // gb
    src = jnp.arange(gb * kw)
    tau, r, ch = src // (gb * c), (src // c) % gb, src % c
    perm = jax.nn.one_hot(r * kw + tau * c + ch, gb * kw, dtype=BF16)
    x = pl.pallas_call(
        functools.partial(_s5_pack_body, b),
        grid=(bp, n_blk),
        in_specs=[pl.BlockSpec((None, nj, t_len, LANES), lambda bi, mi: (jnp.minimum(bi, b - 1), 0, 0, mi)),
                  pl.BlockSpec(perm.shape, lambda bi, mi: (0, 0))],
        out_specs=pl.BlockSpec((nj, gb * kw), lambda bi, mi: (0, bi * n_blk + mi)),
        out_shape=jax.ShapeDtypeStruct((nj, bp * g * kw), BF16),
        name="s5_pack",
        compiler_params=pltpu.CompilerParams(dimension_semantics=("parallel", "parallel"),
                                             vmem_limit_bytes=VMEM_LIMIT),
    )(h3.reshape(b, nj, t_len, d), perm).reshape(rows, g * kw)
    y = pl.pallas_call(
        functools.partial(_s5_body, bp), grid=(g // gb,),
        in_specs=[pl.BlockSpec((rows, gb * kw), lambda i: (0, i)),
                  pl.BlockSpec((gb, kw, kw), lambda i: (i, 0, 0)),
                  pl.BlockSpec((gb, kw, 2 * S5_STATE), lambda i: (i, 0, 0)),
                  pl.BlockSpec((gb, S5_STATE, kw), lambda i: (i, 0, 0)),
                  pl.BlockSpec((gb, S5_STATE, kw), lambda i: (i, 0, 0)),
                  pl.BlockSpec((S5_SUBLANES, gb * S5_STATE), lambda i: (0, i)),
                  pl.BlockSpec((S5_SUBLANES, gb * S5_STATE), lambda i: (0, i))],
        out_specs=pl.BlockSpec((rows, gb * kw), lambda i: (0, i)),
        out_shape=jax.ShapeDtypeStruct((rows, g * kw), BF16),
        scratch_shapes=[pltpu.VMEM((rows, gb * S5_STATE), F32), pltpu.VMEM((rows, gb * S5_STATE), F32)],
        name="s5_chunks",
        compiler_params=pltpu.CompilerParams(dimension_semantics=("parallel",), vmem_limit_bytes=VMEM_LIMIT),
    )(x, m, bc, cc_re, cc_im, at_re, at_im)
    y = pl.pallas_call(
        _s5_unpack_body,
        grid=(b, n_blk),
        in_specs=[pl.BlockSpec((nj, gb * kw), lambda bi, mi: (0, bi * n_blk + mi)),
                  pl.BlockSpec(perm.shape, lambda bi, mi: (0, 0))],
        out_specs=pl.BlockSpec((None, nj, t_len, LANES), lambda bi, mi: (bi, 0, 0, mi)),
        out_shape=jax.ShapeDtypeStruct((b, nj, t_len, d), F32),
        name="s5_unpack",
        compiler_params=pltpu.CompilerParams(dimension_semantics=("parallel", "parallel"),
                                             vmem_limit_bytes=VMEM_LIMIT),
    )(y.reshape(nj, bp * g * kw), perm.T)
    return y.reshape(b * lp, d)


def _rope_block(x, cm, s1, s2):
    return (x * cm + pltpu.roll(x, LANES - MLA_ROPE // 2, axis=1) * s1 + pltpu.roll(x, MLA_ROPE // 2, axis=1) * s2)


def _mla_in_body(h_ref, qcm_ref, qs1_ref, qs2_ref, kcm_ref, ks1_ref, ks2_ref,
                 win_ref, qg_ref, kvg_ref, wuq_ref, wuk_ref, wuv_ref, q_ref, k_ref, vt_ref):
    c = _dot(h_ref[...].astype(BF16), win_ref[...])
    c_q = _rms(c[:, :MLA_Q_RANK], qg_ref[...]).astype(BF16)
    c_kv = _rms(c[:, MLA_Q_RANK:MLA_Q_RANK + MLA_KV_RANK], kvg_ref[...]).astype(BF16)
    k_r = _rope_block(c[:, MLA_Q_RANK + MLA_KV_RANK:], kcm_ref[...], ks1_ref[...], ks2_ref[...])
    q = _dot(c_q, wuq_ref[...])
    k = _dot(c_kv, wuk_ref[...])
    qcm, qs1, qs2 = qcm_ref[...], qs1_ref[...], qs2_ref[...]
    for h in range(MLA_HEADS):
        sl = slice(h * LANES, (h + 1) * LANES)
        q_ref[:, sl] = _rope_block(q[:, sl], qcm, qs1, qs2).astype(q_ref.dtype)
        k_ref[:, sl] = (k[:, sl] + k_r).astype(k_ref.dtype)
    vt_ref[...] = _dot(c_kv, wuv_ref[...]).T.astype(vt_ref.dtype)


def _mla_attn_body(tq, nq, q_ref, k_ref, vt_ref, mask_ref, o_ref,
                   m_all, l_all, acc_all, m_scr, l_scr, acc_scr, s0_scr, s1_scr, p0_scr, p1_scr, mx0_scr, mx1_scr,
                   exact_ref):
    def scores(kj, qi, rows=tq):
        k0 = pl.multiple_of(kj * tq, tq)
        q0 = pl.multiple_of(qi * tq, tq)
        mask = mask_ref[jnp.minimum(qi - kj, 1), 0:rows, :]
        return jnp.concatenate(
            [_dot_nt(k_ref[pl.ds(k0, rows), hh * LANES:(hh + 1) * LANES],
                     q_ref[pl.ds(q0, tq), hh * LANES:(hh + 1) * LANES]) + mask for hh in range(2)],
            axis=1)

    row = lax.broadcasted_iota(jnp.int32, (2 * MLA_V, tq), 0)

    def finish(qi, l, acc):
        a = acc * (1.0 / l)
        o_t = jnp.where(row < MLA_V, a[:, :tq], a[:, tq:])
        o_ref[pl.ds(pl.multiple_of(qi * tq, tq), tq), :] = o_t.T.astype(o_ref.dtype)

    _flash_head(tq, nq, scores, vt_ref, finish, m_all, l_all, acc_all, m_scr, l_scr, acc_scr,
                (s0_scr, s1_scr), (p0_scr, p1_scr), (mx0_scr, mx1_scr), exact_ref)


def _mla_mixer(h3, w_in, q_norm_g, kv_norm_g, w_uq, w_ukv):
    b, lp, d = h3.shape
    n = b * lp
    nh = MLA_HEADS
    qk = MLA_NOPE + MLA_ROPE
    half = MLA_ROPE // 2
    w_kr = jnp.pad(w_in[:, MLA_Q_RANK + MLA_KV_RANK:], ((0, 0), (MLA_NOPE, LANES - qk)))
    w_in_all = jnp.concatenate([w_in[:, :MLA_Q_RANK + MLA_KV_RANK], w_kr], axis=1).astype(BF16)
    w_uq_p = jnp.pad(w_uq.reshape(MLA_Q_RANK, nh, qk), ((0, 0), (0, 0), (0, LANES - qk)))
    w_uq_p = w_uq_p.reshape(MLA_Q_RANK, nh * LANES).astype(BF16)
    w_ukv3 = w_ukv.reshape(MLA_KV_RANK, nh, MLA_NOPE + MLA_V)
    w_uk_p = jnp.pad(w_ukv3[:, :, :MLA_NOPE], ((0, 0), (0, 0), (0, LANES - MLA_NOPE)))
    w_uk_p = w_uk_p.reshape(MLA_KV_RANK, nh * LANES).astype(BF16)
    w_uv = w_ukv3[:, :, MLA_NOPE:].reshape(MLA_KV_RANK, nh * MLA_V).astype(BF16)
    pos = jnp.arange(lp, dtype=F32)
    inv_freq = ROPE_BASE ** (-jnp.arange(0, MLA_ROPE, 2, dtype=F32) / MLA_ROPE)
    ang = pos[:, None] * inv_freq[None, :]
    cos, sin = jnp.cos(ang), jnp.sin(ang)
    z = lambda w: jnp.zeros((lp, w), F32)
    cm = jnp.concatenate([jnp.ones((lp, MLA_NOPE), F32), cos, cos, z(LANES - qk)], axis=1)
    s1 = jnp.concatenate([z(MLA_NOPE), -sin, z(LANES - MLA_NOPE - half)], axis=1)
    s2 = jnp.concatenate([z(MLA_NOPE + half), sin, z(LANES - qk)], axis=1)
    scale = qk ** -0.5 * LOG2E
    tq = _pick_tile(lp, (384, 256, 128))
    nk = lp // tq
    q, k, vt = _row_call(
        _mla_in_body, [h3.reshape(n, d)],
        [w_in_all, q_norm_g.reshape(1, MLA_Q_RANK), kv_norm_g.reshape(1, MLA_KV_RANK), w_uq_p, w_uk_p, w_uv],
        [(nh * LANES, BF16), (nh * LANES, BF16)], tq, "mla_in",
        pos_ins=[cm * scale, s1 * scale, s2 * scale, cm, s1, s2], lp=lp, t_outs=[(nh * MLA_V, BF16)])
    q, k = q.reshape(b, lp, nh * LANES), k.reshape(b, lp, nh * LANES)
    ti = jnp.arange(tq)
    mask = jnp.stack([jnp.where(ti[:, None] <= ti[None, :], 0.0, NEG_INF), jnp.zeros((tq, tq))]).astype(F32)
    o = pl.pallas_call(
        functools.partial(_mla_attn_body, tq, nk),
        grid=(b, nh // 2),
        in_specs=[pl.BlockSpec((None, lp, 2 * LANES), lambda bi, hi: (bi, 0, hi)),
                  pl.BlockSpec((None, lp, 2 * LANES), lambda bi, hi: (bi, 0, hi)),
                  pl.BlockSpec((None, nk, 2 * MLA_V, tq), lambda bi, hi: (bi, 0, hi, 0)),
                  pl.BlockSpec((2, tq, tq), lambda bi, hi: (0, 0, 0))],
        out_specs=pl.BlockSpec((None, lp, 2 * MLA_V), lambda bi, hi: (bi, 0, hi)),
        out_shape=jax.ShapeDtypeStruct((b, lp, nh * MLA_V), BF16),
        scratch_shapes=_flash_scratch(2 * MLA_V, tq, nk),
        name="mla_attn",
        compiler_params=pltpu.CompilerParams(dimension_semantics=("parallel", "parallel"),
                                             vmem_limit_bytes=VMEM_LIMIT),
    )(q, k, vt, mask)
    return o.reshape(n, nh * MLA_V)


def kernel(x, meta, rel_bias, ln_g, ln_b, ffn_w1, ffn_w3, ffn_w2, ssd_w_in, ssd_conv_w, ssd_conv_b, ssd_dt_bias, ssd_a_log, ssd_d, ssd_norm_g, ssd_w_out, diff_w_qkv, diff_lam_q1, diff_lam_k1, diff_lam_q2, diff_lam_k2, diff_subln_g, diff_w_out, s5_lam_re, s5_lam_im, s5_log_step, s5_b_re, s5_b_im, s5_c_re, s5_c_im, s5_d, s5_w_glu, s5_b_glu, mla_w_in, mla_q_norm_g, mla_kv_norm_g, mla_w_uq, mla_w_ukv, mla_w_out):
    b, seq, d = x.shape
    l = seq + N_META
    lp = -(-l // SEQ_ALIGN) * SEQ_ALIGN
    n = b * lp
    h = jnp.concatenate([jnp.broadcast_to(meta[None].astype(x.dtype), (b, N_META, d)), x,
                         jnp.zeros((b, lp - l, d), x.dtype)], axis=1).reshape(n, d)
    w1b, w3b, w2b = ffn_w1.astype(BF16), ffn_w3.astype(BF16), ffn_w2.astype(BF16)
    for i in range(DEPTH):
        kind, j = i % 4, i // 4
        h = _ffn(h, w1b, w3b, w2b, ln_g[i, 0], ln_b[i, 0], (i, 0))
        h3 = h.reshape(b, lp, d)
        if kind == 0:
            y = _ssd_mixer(h3, ssd_w_in[j], ssd_conv_w[j], ssd_conv_b[j], ssd_dt_bias[j], ssd_a_log[j],
                           ssd_d[j], ssd_norm_g[j])
            w_out, b_out = ssd_w_out[j], None
        elif kind == 1:
            lam_init = 0.8 - 0.6 * math.exp(-0.3 * i)
            y = _diff_mixer(h3, diff_w_qkv[j], diff_lam_q1[j], diff_lam_k1[j], diff_lam_q2[j], diff_lam_k2[j],
                            diff_subln_g[j], rel_bias, lam_init)
            w_out, b_out = diff_w_out[j], None
        elif kind == 2:
            y = _s5_mixer(h3, s5_lam_re[j], s5_lam_im[j], s5_log_step[j], s5_b_re[j], s5_b_im[j], s5_c_re[j],
                          s5_c_im[j], s5_d[j])
            w_out, b_out = s5_w_glu[j], s5_b_glu[j]
        else:
            y = _mla_mixer(h3, mla_w_in[j], mla_q_norm_g[j], mla_kv_norm_g[j], mla_w_uq[j], mla_w_ukv[j])
            w_out, b_out = mla_w_out[j], None
        h = _mix_ffn(y, h, w_out, b_out, ln_g[i, 1], ln_b[i, 1], w1b, w3b, w2b, ln_g[i, 2], ln_b[i, 2], (i, 1))
    return h.reshape(b, lp, d)[:, N_META:l]
```

```python
import functools
import math

import jax
import jax.numpy as jnp
from jax import lax
from jax.experimental import pallas as pl
from jax.experimental.pallas import tpu as pltpu

F32 = jnp.float32
BF16 = jnp.bfloat16

N_META = 16
DEPTH = 4
ALPHA = (2.0 * DEPTH) ** 0.25
LN_EPS = 1e-5
RMS_EPS = 1e-6
NEG_INF = -1e30
LOG2E = math.log2(math.e)
ATTN_REF_KEYS = 16
ATTN_PAIR_UNROLL = 4
ATTN_EDGE_UNROLL = 4
SOFTMAX_SUM_LIMIT = 2.0 ** 100
F32_FINITE_LIMIT = 3.0e38

SSD_HEAD_DIM = 64
SSD_HEADS = 32
SSD_GROUPS = 8
SSD_STATE = 128
SSD_CONV = 4
SSD_CHUNK = 128

DIFF_HEADS = 8
DIFF_HEAD_DIM = 64
DIFF_V_DIM = 128
REL_BUCKETS = 32
REL_MAX_DIST = 128

S5_GROUP = 16
S5_GROUPS = 64
S5_STATE = 64
S5_CHUNK = 16
S5_GROUP_BLOCK = 8
SUBLANES = 8
S5_SUBLANES = SUBLANES
S5_UNROLL = 4

MLA_HEADS = 16
MLA_Q_RANK = 384
MLA_KV_RANK = 256
MLA_NOPE = 64
MLA_ROPE = 32
MLA_V = 64
ROPE_BASE = 10000.0

FFN_ROW_TILES = (512, 384, 256, 128)
FFN_CHUNKS = 11

LANES = 128
SEQ_ALIGN = 128
VMEM_LIMIT = 56 * 1024 * 1024


def _pick_tile(n, candidates):
    for c in candidates:
        if n % c == 0:
            return c
    raise ValueError(f"no tile for {n}")


def _layer_norm(z, g, b):
    mu = jnp.mean(z, axis=-1, keepdims=True)
    d = z - mu
    var = jnp.mean(d * d, axis=-1, keepdims=True)
    return d * lax.rsqrt(var + LN_EPS) * g + b


def _rms(z, g):
    return z * lax.rsqrt(jnp.mean(z * z, axis=-1, keepdims=True) + RMS_EPS) * g


def _silu(x):
    return x * (1.0 / (1.0 + jnp.exp(-x)))


def _dot(a, b):
    return jnp.dot(a, b, preferred_element_type=F32)


def _dot_nt(a, b):
    return lax.dot_general(a, b, (((1,), (1,)), ((), ())), preferred_element_type=F32)


def _row_call(body, row_ins, const_ins, outs, tm, name, pos_ins=(), lp=None, t_outs=()):
    n = row_ins[0].shape[0]
    assert n % tm == 0
    in_specs = [pl.BlockSpec((tm, a.shape[1]), lambda i: (i, 0)) for a in row_ins]
    if pos_ins or t_outs:
        assert lp % tm == 0
        nb = lp // tm
        in_specs += [pl.BlockSpec((tm, a.shape[1]), lambda i: (i % nb, 0)) for a in pos_ins]
    const_arrays = []
    for c in const_ins:
        arr, lead = c if isinstance(c, tuple) else (c, ())
        const_arrays.append(arr)
        in_specs.append(pl.BlockSpec((None,) * len(lead) + arr.shape[len(lead):],
                                     functools.partial(lambda lead_, i: lead_ + (0, 0), tuple(lead)),
                                     pipeline_mode=pl.Buffered(1)))
    out_specs = [pl.BlockSpec((tm, w), lambda i: (i, 0)) for (w, _) in outs]
    out_shape = [jax.ShapeDtypeStruct((n, w), dt) for (w, dt) in outs]
    out_specs += [pl.BlockSpec((None, None, w, tm), lambda i: (i // nb, i % nb, 0, 0)) for (w, _) in t_outs]
    out_shape += [jax.ShapeDtypeStruct((n // lp, nb, w, tm), dt) for (w, dt) in t_outs]
    res = pl.pallas_call(
        body, grid=(n // tm,), in_specs=in_specs, out_specs=out_specs, out_shape=out_shape, name=name,
        compiler_params=pltpu.CompilerParams(dimension_semantics=("parallel",), vmem_limit_bytes=VMEM_LIMIT),
    )(*row_ins, *pos_ins, *const_arrays)
    return res


def _ffn_rows(x, n_chunks, w1_ref, w3_ref, w2_ref, g_ref, b_ref):
    xb = x.astype(BF16)
    fc = w1_ref.shape[1] // n_chunks
    y = None
    for c in range(n_chunks):
        sl = slice(c * fc, (c + 1) * fc)
        gate = _dot(xb, w1_ref[:, sl])
        up = _dot(xb, w3_ref[:, sl])
        a = (_silu(gate) * up).astype(BF16)
        part = _dot(a, w2_ref[sl, :])
        y = part if y is None else y + part
    return _layer_norm(ALPHA * x + 0.5 * y, g_ref[...], b_ref[...])


def _ffn_body(n_chunks, h_ref, w1_ref, w3_ref, w2_ref, g_ref, b_ref, o_ref):
    o_ref[...] = _ffn_rows(h_ref[...], n_chunks, w1_ref, w3_ref, w2_ref, g_ref, b_ref)


def _mix_ffn_body(n_chunks, glu, y_ref, h_ref, wo_ref, bo_ref, gm_ref, bm_ref,
                  w1_ref, w3_ref, w2_ref, g_ref, b_ref, o_ref):
    m = _dot(y_ref[...].astype(BF16), wo_ref[...])
    if glu:
        m = m + bo_ref[...]
        d = o_ref.shape[1]
        m = m[:, :d] * (1.0 / (1.0 + jnp.exp(-m[:, d:])))
    x = _layer_norm(ALPHA * h_ref[...] + m, gm_ref[...], bm_ref[...])
    o_ref[...] = _ffn_rows(x, n_chunks, w1_ref, w3_ref, w2_ref, g_ref, b_ref)


def _ffn_chunks(f, n_chunks):
    return n_chunks or (FFN_CHUNKS if f % (FFN_CHUNKS * LANES) == 0 else 1)


def _ffn(h2, w1, w3, w2, g, b, idx, tm=None, n_chunks=None, name="ffn"):
    n, d = h2.shape
    tm = tm or _pick_tile(n, FFN_ROW_TILES)
    (out,) = _row_call(functools.partial(_ffn_body, _ffn_chunks(w1.shape[-1], n_chunks)), [h2],
                       [(w1, idx), (w3, idx), (w2, idx), g.reshape(1, d), b.reshape(1, d)],
                       [(d, F32)], tm, name)
    return out


def _mix_ffn(y2, h2, w_out, b_out, g_mix, b_mix, w1, w3, w2, g, b, idx, tm=None, n_chunks=None, name="mix_ffn"):
    n, d = h2.shape
    tm = tm or _pick_tile(n, FFN_ROW_TILES)
    glu = b_out is not None
    wo = w_out.astype(BF16)
    bo = (b_out if glu else jnp.zeros((wo.shape[1],), F32)).reshape(1, wo.shape[1])
    (out,) = _row_call(functools.partial(_mix_ffn_body, _ffn_chunks(w1.shape[-1], n_chunks), glu), [y2, h2],
                       [wo, bo, g_mix.reshape(1, d), b_mix.reshape(1, d),
                        (w1, idx), (w3, idx), (w2, idx), g.reshape(1, d), b.reshape(1, d)],
                       [(d, F32)], tm, name)
    return out


def _ssd_in_body(h_ref, w_ref, xbc_ref, z_ref, dt_ref):
    r = _dot(h_ref[...].astype(BF16), w_ref[...])
    nx, nz = xbc_ref.shape[1], z_ref.shape[1]
    xbc_ref[...] = r[:, :nx]
    z_ref[...] = r[:, nx:nx + nz]
    dt_ref[...] = r[:, nx + nz:]


def _softplus(x):
    return jnp.maximum(x, 0.0) + jnp.log1p(jnp.exp(-jnp.abs(x)))


def _ssd_body(xbc_ref, dt_ref, z_ref, convw_ref, convb_ref, dtb_ref, alog_ref, dskip_ref, normg_ref,
              y_ref, buf_scr, state_scr, y_scr):
    t = SSD_CHUNK
    d_inner = SSD_HEADS * SSD_HEAD_DIM
    gn = SSD_GROUPS * SSD_STATE
    hp = SSD_HEAD_DIM
    r = SSD_HEADS // SSD_GROUPS

    @pl.when(pl.program_id(1) == 0)
    def _():
        buf_scr[0:SUBLANES, :] = jnp.zeros((SUBLANES, buf_scr.shape[1]), F32)
        state_scr[...] = jnp.zeros(state_scr.shape, F32)

    x = xbc_ref[...]
    buf_scr[SUBLANES:SUBLANES + t, :] = x
    acc = convb_ref[...] + convw_ref[SSD_CONV - 1:SSD_CONV, :] * x
    for j in range(1, SSD_CONV):
        acc = acc + convw_ref[SSD_CONV - 1 - j:SSD_CONV - j, :] * buf_scr[SUBLANES - j:SUBLANES - j + t, :]
    buf_scr[0:SUBLANES, :] = x[t - SUBLANES:t, :]
    xbc = _silu(acc)

    dt = _softplus(dt_ref[...] + dtb_ref[...])
    a = dt * (-jnp.exp(alog_ref[...]))
    rows = lax.broadcasted_iota(jnp.int32, (t, t), 0)
    cols = lax.broadcasted_iota(jnp.int32, (t, t), 1)
    tri = rows >= cols
    a_cum = jnp.dot(tri.astype(F32), a, precision=lax.Precision.HIGHEST, preferred_element_type=F32)
    a_last = a_cum[t - 1:t, :]
    a_cum_t = a_cum.T
    dt_t = dt.T
    w_end_t = (dt * jnp.exp(a_last - a_cum)).T
    chunk_dec = jnp.exp(a_last)

    for g in range(SSD_GROUPS):
        b_f = xbc[:, d_inner + g * SSD_STATE:d_inner + (g + 1) * SSD_STATE]
        c_g = xbc[:, d_inner + gn + g * SSD_STATE:d_inner + gn + (g + 1) * SSD_STATE].astype(BF16)
        b_g = b_f.astype(BF16)
        b_ft = b_f.T
        cb = _dot_nt(c_g, b_g)
        y_off_g = _dot(c_g, state_scr[:, g * r * hp:(g + 1) * r * hp].astype(BF16))
        for k in range(r):
            h = g * r + k
            col = jnp.broadcast_to(a_cum[:, h:h + 1], (t, t))
            decay = jnp.exp(jnp.where(tri, col - a_cum_t[h:h + 1, :], NEG_INF))
            xh = xbc[:, h * hp:(h + 1) * hp]
            xb = xh.astype(BF16)
            y_diag = _dot((cb * decay * dt_t[h:h + 1, :]).astype(BF16), xb)
            y_off = y_off_g[:, k * hp:(k + 1) * hp] * jnp.exp(col[:, :hp])
            y_scr[:, h * hp:(h + 1) * hp] = y_diag + y_off + dskip_ref[:, h * hp:(h + 1) * hp] * xh
            st = state_scr[:, h * hp:(h + 1) * hp]
            b_w = (b_ft * w_end_t[h:h + 1, :]).astype(BF16)
            state_scr[:, h * hp:(h + 1) * hp] = st * chunk_dec[:, h:h + 1] + _dot(b_w, xb)

    y = y_scr[...] * _silu(z_ref[...])
    gs = d_inner // SSD_GROUPS
    for g in range(SSD_GROUPS):
        sl = slice(g * gs, (g + 1) * gs)
        y_ref[:, sl] = _rms(y[:, sl], normg_ref[:, sl]).astype(y_ref.dtype)


def _ssd_mixer(h3, w_in, conv_w, conv_b, dt_bias, a_log, d_skip, norm_g):
    b, lp, d = h3.shape
    n = b * lp
    d_inner = SSD_HEADS * SSD_HEAD_DIM
    conv_ch = d_inner + 2 * SSD_GROUPS * SSD_STATE
    w_z, w_xbc, w_dt = w_in[:, :d_inner], w_in[:, d_inner:d_inner + conv_ch], w_in[:, d_inner + conv_ch:]
    w_dt = jnp.pad(w_dt, ((0, 0), (0, LANES - SSD_HEADS)))
    w_all = jnp.concatenate([w_xbc, w_z, w_dt], axis=1).astype(BF16)
    tm = _pick_tile(n, (512, 384, 256, 128))
    xbc, z, dt = _row_call(_ssd_in_body, [h3.reshape(n, d)], [w_all],
                           [(conv_ch, F32), (d_inner, F32), (LANES, F32)], tm, "ssd_in")
    t = SSD_CHUNK
    pad1 = lambda v: jnp.pad(v, (0, LANES - SSD_HEADS)).reshape(1, LANES)
    consts = [conv_w, conv_b.reshape(1, conv_ch), pad1(dt_bias), pad1(a_log),
              jnp.repeat(d_skip, SSD_HEAD_DIM).reshape(1, d_inner), norm_g.reshape(1, d_inner)]
    row_spec = lambda w: pl.BlockSpec((None, t, w), lambda i, c: (i, c, 0))
    y = pl.pallas_call(
        _ssd_body, grid=(b, lp // t),
        in_specs=[row_spec(conv_ch), row_spec(LANES), row_spec(d_inner)]
        + [pl.BlockSpec(c.shape, lambda i, c_: (0, 0)) for c in consts],
        out_specs=row_spec(d_inner),
        out_shape=jax.ShapeDtypeStruct((b, lp, d_inner), BF16),
        scratch_shapes=[pltpu.VMEM((SUBLANES + t, conv_ch), F32), pltpu.VMEM((SSD_STATE, d_inner), F32),
                        pltpu.VMEM((t, d_inner), F32)],
        name="ssd_scan",
        compiler_params=pltpu.CompilerParams(dimension_semantics=("parallel", "arbitrary"),
                                             vmem_limit_bytes=VMEM_LIMIT),
    )(xbc.reshape(b, lp, conv_ch), dt.reshape(b, lp, LANES), z.reshape(b, lp, d_inner), *consts)
    return y.reshape(n, d_inner)


def _qkv_body(n_q, scale, h_ref, w_ref, qk_ref, vt_ref):
    r = _dot(h_ref[...].astype(BF16), w_ref[...])
    n_qk = qk_ref.shape[1]
    qk_ref[:, :n_q] = (r[:, :n_q] * scale).astype(qk_ref.dtype)
    qk_ref[:, n_q:] = r[:, n_q:n_qk].astype(qk_ref.dtype)
    vt_ref[...] = r[:, n_qk:].T.astype(vt_ref.dtype)


def _causal_mask_pair(s, tq):
    rows = lax.broadcasted_iota(jnp.int32, s.shape, 0)
    cols = lax.broadcasted_iota(jnp.int32, s.shape, 1)
    cols = jnp.where(cols >= tq, cols - tq, cols)
    return jnp.where(rows <= cols, s, NEG_INF)


def _flash_causal(qi, tq, scores, vt_ref, m_scr, l_scr, acc_scr, s_scrs, p_scrs, mx_scrs):
    m_scr[...] = jnp.full(m_scr.shape, NEG_INF, F32)
    l_scr[...] = jnp.zeros(l_scr.shape, F32)
    acc_scr[...] = jnp.zeros(acc_scr.shape, F32)
    p_scrs[1][...] = jnp.zeros(p_scrs[1].shape, p_scrs[1].dtype)

    def softmax(s, s_max, pv):
        m_prev = m_scr[...]
        m_new = jnp.maximum(m_prev, s_max)
        alpha = jnp.exp2(m_prev - m_new)
        p = jnp.exp2(s - m_new)
        l_scr[...] = alpha * l_scr[...] + jnp.sum(p, axis=0, keepdims=True)
        m_scr[...] = m_new
        acc_scr[...] = alpha * (acc_scr[...] + pv)
        return p.astype(BF16)

    def produce(kj, slot):
        s = scores(kj)
        s_scrs[slot][...] = s
        mx_scrs[slot][...] = jnp.max(s, axis=0, keepdims=True)

    def step(kj, cur, nxt):
        pv = _dot(vt_ref[jnp.maximum(kj - 1, 0)], p_scrs[nxt][...])
        p_scrs[cur][...] = softmax(s_scrs[cur][...], mx_scrs[cur][...], pv)
        produce(kj + 1, nxt)

    produce(0, 0)

    def body(t, carry):
        step(2 * t, 0, 1)
        step(2 * t + 1, 1, 0)
        return carry

    lax.fori_loop(0, lax.shift_right_logical(qi, 1), body, 0)
    pl.when((qi & 1) == 1)(functools.partial(step, qi - 1, 0, 1))

    def last(cur, nxt):
        pv = _dot(vt_ref[jnp.maximum(qi - 1, 0)], p_scrs[nxt][...])
        s = _causal_mask_pair(s_scrs[cur][...], tq)
        p = softmax(s, jnp.max(s, axis=0, keepdims=True), pv)
        acc_scr[...] = acc_scr[...] + _dot(vt_ref[qi], p)

    for par in range(2):
        pl.when((qi & 1) == par)(functools.partial(last, par, 1 - par))


def _flash_head(tq, nq, scores, vt_ref, finish, m_all, l_all, acc_all,
                m_scr, l_scr, acc_scr, s_scrs, p_scrs, mx_scrs, exact_ref):
    def init(qi, carry):
        m_all[qi] = jnp.max(scores(0, qi, ATTN_REF_KEYS), axis=0, keepdims=True)
        l_all[qi] = jnp.zeros(l_all.shape[1:], F32)
        acc_all[qi] = jnp.zeros(acc_all.shape[1:], F32)
        return carry

    lax.fori_loop(0, nq, init, 0, unroll=ATTN_EDGE_UNROLL)

    def probs(kj, qi, slot):
        p = jnp.exp2(scores(kj, qi) - m_all[qi])
        l_all[qi] += jnp.sum(p, axis=0, keepdims=True)
        p_scrs[slot][...] = p.astype(BF16)

    def step(prev, cur, prv):
        kjp, qip = prev
        wrap = qip + 1 >= nq
        kj = jnp.where(wrap, kjp + 1, kjp)
        qi = jnp.where(wrap, kjp + 1, qip + 1)
        p = jnp.exp2(scores(kj, qi) - m_all[qi])
        acc_all[qip] += _dot(vt_ref[kjp], p_scrs[prv][...])
        l_all[qi] += jnp.sum(p, axis=0, keepdims=True)
        p_scrs[cur][...] = p.astype(BF16)
        return kj, qi

    n_steps = nq * (nq + 1) // 2 - 1
    probs(0, 0, 0)

    def body(t, prev):
        return step(step(prev, 1, 0), 0, 1)

    prev = lax.fori_loop(0, n_steps // 2, body, (jnp.int32(0), jnp.int32(0)), unroll=ATTN_PAIR_UNROLL)
    if n_steps % 2:
        prev = step(prev, 1, 0)
    acc_all[prev[1]] += _dot(vt_ref[prev[0]], p_scrs[n_steps % 2][...])

    ok = jnp.all(l_all[...] < SOFTMAX_SUM_LIMIT) & jnp.all(jnp.abs(acc_all[...]) < F32_FINITE_LIMIT)
    exact_ref[0] = jnp.where(ok, 0, 1)

    @pl.when(exact_ref[0] == 0)
    def _():
        def fin(qi, carry):
            finish(qi, l_all[qi], acc_all[qi])
            return carry

        lax.fori_loop(0, nq, fin, 0, unroll=ATTN_EDGE_UNROLL)

    @pl.when(exact_ref[0] == 1)
    def _():
        def redo(qi, carry):
            _flash_causal(qi, tq, lambda kj: scores(kj, qi), vt_ref, m_scr, l_scr, acc_scr, s_scrs, p_scrs, mx_scrs)
            finish(qi, l_scr[...], acc_scr[...])
            return carry

        lax.fori_loop(0, nq, redo, 0)


def _flash_scratch(e, tq, nq):
    w = 2 * tq
    return [pltpu.VMEM((nq, 1, w), F32), pltpu.VMEM((nq, 1, w), F32), pltpu.VMEM((nq, e, w), F32),
            pltpu.VMEM((1, w), F32), pltpu.VMEM((1, w), F32), pltpu.VMEM((e, w), F32),
            pltpu.VMEM((tq, w), F32), pltpu.VMEM((tq, w), F32), pltpu.VMEM((tq, w), BF16), pltpu.VMEM((tq, w), BF16),
            pltpu.VMEM((1, w), F32), pltpu.VMEM((1, w), F32), pltpu.SMEM((1,), jnp.int32)]


def _t5_bucket(dist):
    max_exact = REL_BUCKETS // 2
    d = jnp.maximum(dist, 0)
    df = jnp.maximum(d, max_exact).astype(F32)
    large = max_exact + (jnp.log(df / max_exact) / math.log(REL_MAX_DIST / max_exact)
                         * (REL_BUCKETS - max_exact)).astype(jnp.int32)
    large = jnp.minimum(large, REL_BUCKETS - 1)
    return jnp.where(d < max_exact, d, large)


def _diff_attn_body(tq, nq, lam_init, q_ref, k_ref, vt_ref, bias_ref, lam_ref, g_ref, o_ref,
                    m_all, l_all, acc_all, m_scr, l_scr, acc_scr, s0_scr, s1_scr, p0_scr, p1_scr, mx0_scr, mx1_scr,
                    exact_ref, qcat_all):
    lane = lax.broadcasted_iota(jnp.int32, (tq, q_ref.shape[1]), 1)

    def prep(qi, carry):
        q = q_ref[pl.ds(pl.multiple_of(qi * tq, tq), tq), :]
        zero = jnp.zeros_like(q)
        qcat_all[qi, 0:tq, :] = jnp.where(lane < DIFF_HEAD_DIM, q, zero)
        qcat_all[qi, tq:2 * tq, :] = jnp.where(lane >= DIFF_HEAD_DIM, q, zero)
        return carry

    lax.fori_loop(0, nq, prep, 0)

    def scores(kj, qi, rows=tq):
        start = pl.multiple_of(kj * tq, tq)
        bias = bias_ref[jnp.minimum(qi - kj, 2), 0:rows, :]
        return _dot_nt(k_ref[pl.ds(start, rows), :], qcat_all[qi]) + jnp.concatenate([bias, bias], axis=1)

    def finish(qi, l, acc):
        a = acc * (1.0 / l)
        o_t = a[:, :tq] - lam_ref[...] * a[:, tq:]
        o_ref[pl.ds(pl.multiple_of(qi * tq, tq), tq), :] = (
            _rms(o_t.T, g_ref[...]) * (1.0 - lam_init)).astype(o_ref.dtype)

    _flash_head(tq, nq, scores, vt_ref, finish, m_all, l_all, acc_all, m_scr, l_scr, acc_scr,
                (s0_scr, s1_scr), (p0_scr, p1_scr), (mx0_scr, mx1_scr), exact_ref)


def _diff_mixer(h3, w_qkv, lam_q1, lam_k1, lam_q2, lam_k2, subln_g, rel_bias, lam_init):
    b, lp, d = h3.shape
    n = b * lp
    hd = DIFF_HEADS * 2 * DIFF_HEAD_DIM
    tq = _pick_tile(lp, (384, 256, 128))
    assert tq >= REL_MAX_DIST
    nk = lp // tq
    qk, vt = _row_call(functools.partial(_qkv_body, hd, DIFF_HEAD_DIM ** -0.5 * LOG2E), [h3.reshape(n, d)],
                       [w_qkv.astype(BF16)], [(2 * hd, BF16)], tq, "diff_qkv", lp=lp, t_outs=[(hd, BF16)])
    qk = qk.reshape(b, lp, 2 * hd)
    m2 = 2 * tq
    kk = jnp.arange(m2)
    dist = jnp.where(kk < tq, kk, kk - m2)[None, :] + (jnp.arange(3) * tq)[:, None]
    tab = jnp.transpose(rel_bias[_t5_bucket(dist)].astype(F32) * LOG2E, (2, 0, 1))
    tab = jnp.where((jnp.arange(3)[:, None] == 0) & (kk[None, :] >= tq), NEG_INF, tab)
    nh = DIFF_HEADS
    bias = jnp.tile(tab, (1, 1, tq))[..., :tq * (m2 - 1)].reshape(nh, 3, tq, m2 - 1)[..., :tq]
    lam = (jnp.exp(jnp.sum(lam_q1.astype(F32) * lam_k1.astype(F32)))
           - jnp.exp(jnp.sum(lam_q2.astype(F32) * lam_k2.astype(F32))) + lam_init)
    w = 2 * DIFF_HEAD_DIM
    o = pl.pallas_call(
        functools.partial(_diff_attn_body, tq, nk, lam_init),
        grid=(b, nh),
        in_specs=[pl.BlockSpec((None, lp, w), lambda bi, hi: (bi, 0, hi)),
                  pl.BlockSpec((None, lp, w), lambda bi, hi: (bi, 0, nh + hi)),
                  pl.BlockSpec((None, nk, DIFF_V_DIM, tq), lambda bi, hi: (bi, 0, hi, 0)),
                  pl.BlockSpec((None, 3, tq, tq), lambda bi, hi: (hi, 0, 0, 0)),
                  pl.BlockSpec((1, 1), lambda bi, hi: (0, 0)),
                  pl.BlockSpec((1, DIFF_V_DIM), lambda bi, hi: (0, 0))],
        out_specs=pl.BlockSpec((None, lp, DIFF_V_DIM), lambda bi, hi: (bi, 0, hi)),
        out_shape=jax.ShapeDtypeStruct((b, lp, nh * DIFF_V_DIM), BF16),
        scratch_shapes=_flash_scratch(DIFF_V_DIM, tq, nk) + [pltpu.VMEM((nk, 2 * tq, w), BF16)],
        name="diff_attn",
        compiler_params=pltpu.CompilerParams(dimension_semantics=("parallel", "parallel"),
                                             vmem_limit_bytes=VMEM_LIMIT),
    )(qk, qk, vt, bias, lam.reshape(1, 1), subln_g.reshape(1, DIFF_V_DIM))
    return o.reshape(n, nh * DIFF_V_DIM)


def _gelu_tanh(x):
    return 0.5 * x * (1.0 + jnp.tanh(math.sqrt(2.0 / math.pi) * (x + 0.044715 * (x * x * x))))


def _s5_body(bp, x_ref, m_ref, bc_ref, ccre_ref, ccim_ref, are_ref, aim_ref, y_ref, sre_scr, sim_scr):
    rows = x_ref.shape[0]
    gb = m_ref.shape[0]
    kw = m_ref.shape[1]
    p = ccre_ref.shape[1]
    for g in range(gb):
        e = _dot(x_ref[:, g * kw:(g + 1) * kw], bc_ref[g])
        sre_scr[:, g * p:(g + 1) * p] = e[:, :p]
        sim_scr[:, g * p:(g + 1) * p] = e[:, p:]

    ar, ai = are_ref[...], aim_ref[...]
    sub = lax.broadcasted_iota(jnp.int32, ar.shape, 0)
    per_tile = S5_SUBLANES // bp

    def tile_step(r, carry):
        pr, pi = carry
        r0 = pl.multiple_of(r * S5_SUBLANES, S5_SUBLANES)
        er, ei = sre_scr[pl.ds(r0, S5_SUBLANES), :], sim_scr[pl.ds(r0, S5_SUBLANES), :]
        in_r, in_i = pr, pi
        for c in range(per_tile):
            nr = ar * pr - ai * pi + er
            ni = ar * pi + ai * pr + ei
            if per_tile > 1:
                nr = pltpu.roll(nr, bp, axis=0)
                ni = pltpu.roll(ni, bp, axis=0)
            if c + 1 < per_tile:
                sel = (sub >= (c + 1) * bp) & (sub < (c + 2) * bp)
                in_r, in_i = jnp.where(sel, nr, in_r), jnp.where(sel, ni, in_i)
                pr, pi = nr, ni
            else:
                pr, pi = nr, ni
        if per_tile > 1:
            pr = jnp.where(sub < bp, pr, pltpu.roll(pr, bp, axis=0))
            pi = jnp.where(sub < bp, pi, pltpu.roll(pi, bp, axis=0))
        sre_scr[pl.ds(r0, S5_SUBLANES), :] = in_r
        sim_scr[pl.ds(r0, S5_SUBLANES), :] = in_i
        return pr, pi

    zero = jnp.zeros(ar.shape, F32)
    lax.fori_loop(0, rows // S5_SUBLANES, tile_step, (zero, zero), unroll=S5_UNROLL)

    for g in range(gb):
        xg = x_ref[:, g * kw:(g + 1) * kw]
        y = (_dot(xg, m_ref[g]) + _dot(sre_scr[:, g * p:(g + 1) * p].astype(BF16), ccre_ref[g])
             + _dot(sim_scr[:, g * p:(g + 1) * p].astype(BF16), ccim_ref[g]))
        y_ref[:, g * kw:(g + 1) * kw] = _gelu_tanh(y).astype(y_ref.dtype)


def _s5_pack_body(n_batch, h_ref, perm_ref, x_ref):
    t_len = h_ref.shape[1]
    xcat = jnp.concatenate([h_ref[:, t, :].astype(BF16) for t in range(t_len)], axis=1)
    x = _dot(xcat, perm_ref[...]).astype(x_ref.dtype)
    x_ref[...] = jnp.where(pl.program_id(0) < n_batch, x, jnp.zeros_like(x))


def _s5_unpack_body(y_ref, perm_ref, o_ref):
    t_len = o_ref.shape[1]
    res = _dot(y_ref[...], perm_ref[...])
    for t in range(t_len):
        o_ref[:, t, :] = res[:, t * LANES:(t + 1) * LANES]


def _s5_chunk_operators(lam_re, lam_im, log_step, b_re, b_im, c_re, c_im, d_skip):
    hi = lax.Precision.HIGHEST
    t_len = S5_CHUNK
    step = jnp.exp(log_step.astype(F32))[:, None]
    lr = jnp.minimum(lam_re.astype(F32), -1e-4)
    li = lam_im.astype(F32)
    mag = jnp.exp(lr * step)
    ar, ai = mag * jnp.cos(li * step), mag * jnp.sin(li * step)
    den = lr * lr + li * li
    cr = ((ar - 1.0) * lr + ai * li) / den
    ci = (ai * lr - (ar - 1.0) * li) / den
    br, bi = b_re.astype(F32), b_im.astype(F32)
    bbr = cr[..., None] * br - ci[..., None] * bi
    bbi = cr[..., None] * bi + ci[..., None] * br
    k = jnp.arange(t_len + 1, dtype=F32)[:, None, None]
    pmag = jnp.exp(k * (lr * step))
    pr, pi = pmag * jnp.cos(k * (li * step)), pmag * jnp.sin(k * (li * step))
    abr = pr[:t_len, ..., None] * bbr - pi[:t_len, ..., None] * bbi
    abi = pr[:t_len, ..., None] * bbi + pi[:t_len, ..., None] * bbr
    cre, cim = c_re.astype(F32), c_im.astype(F32)
    kern = (jnp.einsum('gop,kgpc->kgoc', cre, abr, precision=hi)
            - jnp.einsum('gop,kgpc->kgoc', cim, abi, precision=hi))
    kern = kern.at[0].add(jax.vmap(jnp.diag)(d_skip.astype(F32)))
    sig = jnp.arange(t_len)
    lag = sig[None, :] - sig[:, None]
    m = jnp.where((lag >= 0)[..., None, None, None], kern[jnp.maximum(lag, 0)], 0.0)
    g, o, c = kern.shape[1:]
    m = jnp.transpose(m, (2, 0, 4, 1, 3)).reshape(g, t_len * c, t_len * o)
    rev = t_len - 1 - sig
    p = bbr.shape[1]
    bc = jnp.concatenate([jnp.transpose(abr[rev], (1, 0, 3, 2)).reshape(g, t_len * c, p),
                          jnp.transpose(abi[rev], (1, 0, 3, 2)).reshape(g, t_len * c, p)], axis=2)
    car = cre[None] * pr[1:, :, None, :] - cim[None] * pi[1:, :, None, :]
    cai = cre[None] * pi[1:, :, None, :] + cim[None] * pr[1:, :, None, :]
    cc_re = jnp.transpose(car, (1, 3, 0, 2)).reshape(g, p, t_len * o)
    cc_im = -jnp.transpose(cai, (1, 3, 0, 2)).reshape(g, p, t_len * o)
    return (m.astype(BF16), bc.astype(BF16), cc_re.astype(BF16), cc_im.astype(BF16),
            pr[t_len].reshape(g * p), pi[t_len].reshape(g * p))


def _s5_mixer(h3, lam_re, lam_im, log_step, b_re, b_im, c_re, c_im, d_skip):
    b, lp, d = h3.shape
    t_len, g, c = S5_CHUNK, S5_GROUPS, S5_GROUP
    assert lp % t_len == 0 and b <= S5_SUBLANES
    bp = 4 if b <= 4 else S5_SUBLANES
    nj = lp // t_len
    m, bc, cc_re, cc_im, at_re, at_im = _s5_chunk_operators(lam_re, lam_im, log_step, b_re, b_im, c_re, c_im, d_skip)
    n_state = g * S5_STATE
    at_re = jnp.broadcast_to(at_re.reshape(1, n_state), (S5_SUBLANES, n_state))
    at_im = jnp.broadcast_to(at_im.reshape(1, n_state), (S5_SUBLANES, n_state))
    rows = nj * bp
    gb = S5_GROUP_BLOCK
    kw = t_len * c
    assert gb * c == LANES
    n_blk = g // gb
    src = jnp.arange(gb * kw)
    tau, r, ch = src // (gb * c), (src // c) % gb, src % c
    perm = jax.nn.one_hot(r * kw + tau * c + ch, gb * kw, dtype=BF16)
    x = pl.pallas_call(
        functools.partial(_s5_pack_body, b),
        grid=(bp, n_blk),
        in_specs=[pl.BlockSpec((None, nj, t_len, LANES), lambda bi, mi: (jnp.minimum(bi, b - 1), 0, 0, mi)),
                  pl.BlockSpec(perm.shape, lambda bi, mi: (0, 0))],
        out_specs=pl.BlockSpec((nj, gb * kw), lambda bi, mi: (0, bi * n_blk + mi)),
        out_shape=jax.ShapeDtypeStruct((nj, bp * g * kw), BF16),
        name="s5_pack",
        compiler_params=pltpu.CompilerParams(dimension_semantics=("parallel", "parallel"),
                                             vmem_limit_bytes=VMEM_LIMIT),
    )(h3.reshape(b, nj, t_len, d), perm).reshape(rows, g * kw)
    y = pl.pallas_call(
        functools.partial(_s5_body, bp), grid=(g // gb,),
        in_specs=[pl.BlockSpec((rows, gb * kw), lambda i: (0, i)),
                  pl.BlockSpec((gb, kw, kw), lambda i: (i, 0, 0)),
                  pl.BlockSpec((gb, kw, 2 * S5_STATE), lambda i: (i, 0, 0)),
                  pl.BlockSpec((gb, S5_STATE, kw), lambda i: (i, 0, 0)),
                  pl.BlockSpec((gb, S5_STATE, kw), lambda i: (i, 0, 0)),
                  pl.BlockSpec((S5_SUBLANES, gb * S5_STATE), lambda i: (0, i)),
                  pl.BlockSpec((S5_SUBLANES, gb * S5_STATE), lambda i: (0, i))],
        out_specs=pl.BlockSpec((rows, gb * kw), lambda i: (0, i)),
        out_shape=jax.ShapeDtypeStruct((rows, g * kw), BF16),
        scratch_shapes=[pltpu.VMEM((rows, gb * S5_STATE), F32), pltpu.VMEM((rows, gb * S5_STATE), F32)],
        name="s5_chunks",
        compiler_params=pltpu.CompilerParams(dimension_semantics=("parallel",), vmem_limit_bytes=VMEM_LIMIT),
    )(x, m, bc, cc_re, cc_im, at_re, at_im)
    y = pl.pallas_call(
        _s5_unpack_body,
        grid=(b, n_blk),
        in_specs=[pl.BlockSpec((nj, gb * kw), lambda bi, mi: (0, bi * n_blk + mi)),
                  pl.BlockSpec(perm.shape, lambda bi, mi: (0, 0))],
        out_specs=pl.BlockSpec((None, nj, t_len, LANES), lambda bi, mi: (bi, 0, 0, mi)),
        out_shape=jax.ShapeDtypeStruct((b, nj, t_len, d), F32),
        name="s5_unpack",
        compiler_params=pltpu.CompilerParams(dimension_semantics=("parallel", "parallel"),
                                             vmem_limit_bytes=VMEM_LIMIT),
    )(y.reshape(nj, bp * g * kw), perm.T)
    return y.reshape(b * lp, d)


def _rope_block(x, cm, s1, s2):
    return (x * cm + pltpu.roll(x, LANES - MLA_ROPE // 2, axis=1) * s1 + pltpu.roll(x, MLA_ROPE // 2, axis=1) * s2)


def _mla_in_body(h_ref, qcm_ref, qs1_ref, qs2_ref, kcm_ref, ks1_ref, ks2_ref,
                 win_ref, qg_ref, kvg_ref, wuq_ref, wuk_ref, wuv_ref, q_ref, k_ref, vt_ref):
    c = _dot(h_ref[...].astype(BF16), win_ref[...])
    c_q = _rms(c[:, :MLA_Q_RANK], qg_ref[...]).astype(BF16)
    c_kv = _rms(c[:, MLA_Q_RANK:MLA_Q_RANK + MLA_KV_RANK], kvg_ref[...]).astype(BF16)
    k_r = _rope_block(c[:, MLA_Q_RANK + MLA_KV_RANK:], kcm_ref[...], ks1_ref[...], ks2_ref[...])
    q = _dot(c_q, wuq_ref[...])
    k = _dot(c_kv, wuk_ref[...])
    qcm, qs1, qs2 = qcm_ref[...], qs1_ref[...], qs2_ref[...]
    for h in range(MLA_HEADS):
        sl = slice(h * LANES, (h + 1) * LANES)
        q_ref[:, sl] = _rope_block(q[:, sl], qcm, qs1, qs2).astype(q_ref.dtype)
        k_ref[:, sl] = (k[:, sl] + k_r).astype(k_ref.dtype)
    vt_ref[...] = _dot(c_kv, wuv_ref[...]).T.astype(vt_ref.dtype)


def _mla_attn_body(tq, nq, q_ref, k_ref, vt_ref, mask_ref, o_ref,
                   m_all, l_all, acc_all, m_scr, l_scr, acc_scr, s0_scr, s1_scr, p0_scr, p1_scr, mx0_scr, mx1_scr,
                   exact_ref):
    def scores(kj, qi, rows=tq):
        k0 = pl.multiple_of(kj * tq, tq)
        q0 = pl.multiple_of(qi * tq, tq)
        mask = mask_ref[jnp.minimum(qi - kj, 1), 0:rows, :]
        return jnp.concatenate(
            [_dot_nt(k_ref[pl.ds(k0, rows), hh * LANES:(hh + 1) * LANES],
                     q_ref[pl.ds(q0, tq), hh * LANES:(hh + 1) * LANES]) + mask for hh in range(2)],
            axis=1)

    row = lax.broadcasted_iota(jnp.int32, (2 * MLA_V, tq), 0)

    def finish(qi, l, acc):
        a = acc * (1.0 / l)
        o_t = jnp.where(row < MLA_V, a[:, :tq], a[:, tq:])
        o_ref[pl.ds(pl.multiple_of(qi * tq, tq), tq), :] = o_t.T.astype(o_ref.dtype)

    _flash_head(tq, nq, scores, vt_ref, finish, m_all, l_all, acc_all, m_scr, l_scr, acc_scr,
                (s0_scr, s1_scr), (p0_scr, p1_scr), (mx0_scr, mx1_scr), exact_ref)


def _mla_mixer(h3, w_in, q_norm_g, kv_norm_g, w_uq, w_ukv):
    b, lp, d = h3.shape
    n = b * lp
    nh = MLA_HEADS
    qk = MLA_NOPE + MLA_ROPE
    half = MLA_ROPE // 2
    w_kr = jnp.pad(w_in[:, MLA_Q_RANK + MLA_KV_RANK:], ((0, 0), (MLA_NOPE, LANES - qk)))
    w_in_all = jnp.concatenate([w_in[:, :MLA_Q_RANK + MLA_KV_RANK], w_kr], axis=1).astype(BF16)
    w_uq_p = jnp.pad(w_uq.reshape(MLA_Q_RANK, nh, qk), ((0, 0), (0, 0), (0, LANES - qk)))
    w_uq_p = w_uq_p.reshape(MLA_Q_RANK, nh * LANES).astype(BF16)
    w_ukv3 = w_ukv.reshape(MLA_KV_RANK, nh, MLA_NOPE + MLA_V)
    w_uk_p = jnp.pad(w_ukv3[:, :, :MLA_NOPE], ((0, 0), (0, 0), (0, LANES - MLA_NOPE)))
    w_uk_p = w_uk_p.reshape(MLA_KV_RANK, nh * LANES).astype(BF16)
    w_uv = w_ukv3[:, :, MLA_NOPE:].reshape(MLA_KV_RANK, nh * MLA_V).astype(BF16)
    pos = jnp.arange(lp, dtype=F32)
    inv_freq = ROPE_BASE ** (-jnp.arange(0, MLA_ROPE, 2, dtype=F32) / MLA_ROPE)
    ang = pos[:, None] * inv_freq[None, :]
    cos, sin = jnp.cos(ang), jnp.sin(ang)
    z = lambda w: jnp.zeros((lp, w), F32)
    cm = jnp.concatenate([jnp.ones((lp, MLA_NOPE), F32), cos, cos, z(LANES - qk)], axis=1)
    s1 = jnp.concatenate([z(MLA_NOPE), -sin, z(LANES - MLA_NOPE - half)], axis=1)
    s2 = jnp.concatenate([z(MLA_NOPE + half), sin, z(LANES - qk)], axis=1)
    scale = qk ** -0.5 * LOG2E
    tq = _pick_tile(lp, (384, 256, 128))
    nk = lp // tq
    q, k, vt = _row_call(
        _mla_in_body, [h3.reshape(n, d)],
        [w_in_all, q_norm_g.reshape(1, MLA_Q_RANK), kv_norm_g.reshape(1, MLA_KV_RANK), w_uq_p, w_uk_p, w_uv],
        [(nh * LANES, BF16), (nh * LANES, BF16)], tq, "mla_in",
        pos_ins=[cm * scale, s1 * scale, s2 * scale, cm, s1, s2], lp=lp, t_outs=[(nh * MLA_V, BF16)])
    q, k = q.reshape(b, lp, nh * LANES), k.reshape(b, lp, nh * LANES)
    ti = jnp.arange(tq)
    mask = jnp.stack([jnp.where(ti[:, None] <= ti[None, :], 0.0, NEG_INF), jnp.zeros((tq, tq))]).astype(F32)
    o = pl.pallas_call(
        functools.partial(_mla_attn_body, tq, nk),
        grid=(b, nh // 2),
        in_specs=[pl.BlockSpec((None, lp, 2 * LANES), lambda bi, hi: (bi, 0, hi)),
                  pl.BlockSpec((None, lp, 2 * LANES), lambda bi, hi: (bi, 0, hi)),
                  pl.BlockSpec((None, nk, 2 * MLA_V, tq), lambda bi, hi: (bi, 0, hi, 0)),
                  pl.BlockSpec((2, tq, tq), lambda bi, hi: (0, 0, 0))],
        out_specs=pl.BlockSpec((None, lp, 2 * MLA_V), lambda bi, hi: (bi, 0, hi)),
        out_shape=jax.ShapeDtypeStruct((b, lp, nh * MLA_V), BF16),
        scratch_shapes=_flash_scratch(2 * MLA_V, tq, nk),
        name="mla_attn",
        compiler_params=pltpu.CompilerParams(dimension_semantics=("parallel", "parallel"),
                                             vmem_limit_bytes=VMEM_LIMIT),
    )(q, k, vt, mask)
    return o.reshape(n, nh * MLA_V)


def kernel(x, meta, rel_bias, ln_g, ln_b, ffn_w1, ffn_w3, ffn_w2, ssd_w_in, ssd_conv_w, ssd_conv_b, ssd_dt_bias, ssd_a_log, ssd_d, ssd_norm_g, ssd_w_out, diff_w_qkv, diff_lam_q1, diff_lam_k1, diff_lam_q2, diff_lam_k2, diff_subln_g, diff_w_out, s5_lam_re, s5_lam_im, s5_log_step, s5_b_re, s5_b_im, s5_c_re, s5_c_im, s5_d, s5_w_glu, s5_b_glu, mla_w_in, mla_q_norm_g, mla_kv_norm_g, mla_w_uq, mla_w_ukv, mla_w_out):
    b, seq, d = x.shape
    l = seq + N_META
    lp = -(-l // SEQ_ALIGN) * SEQ_ALIGN
    n = b * lp
    h = jnp.concatenate([jnp.broadcast_to(meta[None].astype(x.dtype), (b, N_META, d)), x,
                         jnp.zeros((b, lp - l, d), x.dtype)], axis=1).reshape(n, d)
    w1b, w3b, w2b = ffn_w1.astype(BF16), ffn_w3.astype(BF16), ffn_w2.astype(BF16)
    for i in range(DEPTH):
        kind, j = i % 4, i // 4
        h = _ffn(h, w1b, w3b, w2b, ln_g[i, 0], ln_b[i, 0], (i, 0))
        h3 = h.reshape(b, lp, d)
        if kind == 0:
            y = _ssd_mixer(h3, ssd_w_in[j], ssd_conv_w[j], ssd_conv_b[j], ssd_dt_bias[j], ssd_a_log[j],
                           ssd_d[j], ssd_norm_g[j])
            w_out, b_out = ssd_w_out[j], None
        elif kind == 1:
            lam_init = 0.8 - 0.6 * math.exp(-0.3 * i)
            y = _diff_mixer(h3, diff_w_qkv[j], diff_lam_q1[j], diff_lam_k1[j], diff_lam_q2[j], diff_lam_k2[j],
                            diff_subln_g[j], rel_bias, lam_init)
            w_out, b_out = diff_w_out[j], None
        elif kind == 2:
            y = _s5_mixer(h3, s5_lam_re[j], s5_lam_im[j], s5_log_step[j], s5_b_re[j], s5_b_im[j], s5_c_re[j],
                          s5_c_im[j], s5_d[j])
            w_out, b_out = s5_w_glu[j], s5_b_glu[j]
        else:
            y = _mla_mixer(h3, mla_w_in[j], mla_q_norm_g[j], mla_kv_norm_g[j], mla_w_uq[j], mla_w_ukv[j])
            w_out, b_out = mla_w_out[j], None
        h = _mix_ffn(y, h, w_out, b_out, ln_g[i, 1], ln_b[i, 1], w1b, w3b, w2b, ln_g[i, 2], ln_b[i, 2], (i, 1))
    return h.reshape(b, lp, d)[:, N_META:l]
```

```python
import functools
import math

import jax
import jax.numpy as jnp
from jax import lax
from jax.experimental import pallas as pl
from jax.experimental.pallas import tpu as pltpu

F32 = jnp.float32
BF16 = jnp.bfloat16

N_META = 16
DEPTH = 4
ALPHA = (2.0 * DEPTH) ** 0.25
LN_EPS = 1e-5
RMS_EPS = 1e-6
NEG_INF = -1e30
LOG2E = math.log2(math.e)
ATTN_REF_KEYS = 16
ATTN_PAIR_UNROLL = 8
ATTN_EDGE_UNROLL = 4
SOFTMAX_SUM_LIMIT = 2.0 ** 100
F32_FINITE_LIMIT = 3.0e38

SSD_HEAD_DIM = 64
SSD_HEADS = 32
SSD_GROUPS = 8
SSD_STATE = 128
SSD_CONV = 4
SSD_CHUNK = 128

DIFF_HEADS = 8
DIFF_HEAD_DIM = 64
DIFF_V_DIM = 128
REL_BUCKETS = 32
REL_MAX_DIST = 128

S5_GROUP = 16
S5_GROUPS = 64
S5_STATE = 64
S5_CHUNK = 16
S5_GROUP_BLOCK = 8
SUBLANES = 8
S5_SUBLANES = SUBLANES
S5_UNROLL = 4

MLA_HEADS = 16
MLA_Q_RANK = 384
MLA_KV_RANK = 256
MLA_NOPE = 64
MLA_ROPE = 32
MLA_V = 64
ROPE_BASE = 10000.0

FFN_ROW_TILES = (512, 384, 256, 128)
FFN_CHUNKS = 11

LANES = 128
SEQ_ALIGN = 128
VMEM_LIMIT = 56 * 1024 * 1024


def _pick_tile(n, candidates):
    for c in candidates:
        if n % c == 0:
            return c
    raise ValueError(f"no tile for {n}")


def _layer_norm(z, g, b):
    mu = jnp.mean(z, axis=-1, keepdims=True)
    d = z - mu
    var = jnp.mean(d * d, axis=-1, keepdims=True)
    return d * lax.rsqrt(var + LN_EPS) * g + b


def _rms(z, g):
    return z * lax.rsqrt(jnp.mean(z * z, axis=-1, keepdims=True) + RMS_EPS) * g


def _silu(x):
    return x * (1.0 / (1.0 + jnp.exp(-x)))


def _dot(a, b):
    return jnp.dot(a, b, preferred_element_type=F32)


def _dot_nt(a, b):
    return lax.dot_general(a, b, (((1,), (1,)), ((), ())), preferred_element_type=F32)


def _row_call(body, row_ins, const_ins, outs, tm, name, pos_ins=(), lp=None, t_outs=()):
    n = row_ins[0].shape[0]
    assert n % tm == 0
    in_specs = [pl.BlockSpec((tm, a.shape[1]), lambda i: (i, 0)) for a in row_ins]
    if pos_ins or t_outs:
        assert lp % tm == 0
        nb = lp // tm
        in_specs += [pl.BlockSpec((tm, a.shape[1]), lambda i: (i % nb, 0)) for a in pos_ins]
    const_arrays = []
    for c in const_ins:
        arr, lead = c if isinstance(c, tuple) else (c, ())
        const_arrays.append(arr)
        in_specs.append(pl.BlockSpec((None,) * len(lead) + arr.shape[len(lead):],
                                     functools.partial(lambda lead_, i: lead_ + (0, 0), tuple(lead)),
                                     pipeline_mode=pl.Buffered(1)))
    out_specs = [pl.BlockSpec((tm, w), lambda i: (i, 0)) for (w, _) in outs]
    out_shape = [jax.ShapeDtypeStruct((n, w), dt) for (w, dt) in outs]
    out_specs += [pl.BlockSpec((None, None, w, tm), lambda i: (i // nb, i % nb, 0, 0)) for (w, _) in t_outs]
    out_shape += [jax.ShapeDtypeStruct((n // lp, nb, w, tm), dt) for (w, dt) in t_outs]
    res = pl.pallas_call(
        body, grid=(n // tm,), in_specs=in_specs, out_specs=out_specs, out_shape=out_shape, name=name,
        compiler_params=pltpu.CompilerParams(dimension_semantics=("parallel",), vmem_limit_bytes=VMEM_LIMIT),
    )(*row_ins, *pos_ins, *const_arrays)
    return res


def _ffn_rows(x, n_chunks, w1_ref, w3_ref, w2_ref, g_ref, b_ref):
    xb = x.astype(BF16)
    fc = w1_ref.shape[1] // n_chunks
    y = None
    for c in range(n_chunks):
        sl = slice(c * fc, (c + 1) * fc)
        gate = _dot(xb, w1_ref[:, sl])
        up = _dot(xb, w3_ref[:, sl])
        a = (_silu(gate) * up).astype(BF16)
        part = _dot(a, w2_ref[sl, :])
        y = part if y is None else y + part
    return _layer_norm(ALPHA * x + 0.5 * y, g_ref[...], b_ref[...])


def _ffn_body(n_chunks, h_ref, w1_ref, w3_ref, w2_ref, g_ref, b_ref, o_ref):
    o_ref[...] = _ffn_rows(h_ref[...], n_chunks, w1_ref, w3_ref, w2_ref, g_ref, b_ref)


def _mix_ffn_body(n_chunks, glu, y_ref, h_ref, wo_ref, bo_ref, gm_ref, bm_ref,
                  w1_ref, w3_ref, w2_ref, g_ref, b_ref, o_ref):
    m = _dot(y_ref[...].astype(BF16), wo_ref[...])
    if glu:
        m = m + bo_ref[...]
        d = o_ref.shape[1]
        m = m[:, :d] * (1.0 / (1.0 + jnp.exp(-m[:, d:])))
    x = _layer_norm(ALPHA * h_ref[...] + m, gm_ref[...], bm_ref[...])
    o_ref[...] = _ffn_rows(x, n_chunks, w1_ref, w3_ref, w2_ref, g_ref, b_ref)


def _ffn_chunks(f, n_chunks):
    return n_chunks or (FFN_CHUNKS if f % (FFN_CHUNKS * LANES) == 0 else 1)


def _ffn(h2, w1, w3, w2, g, b, idx, tm=None, n_chunks=None, name="ffn"):
    n, d = h2.shape
    tm = tm or _pick_tile(n, FFN_ROW_TILES)
    (out,) = _row_call(functools.partial(_ffn_body, _ffn_chunks(w1.shape[-1], n_chunks)), [h2],
                       [(w1, idx), (w3, idx), (w2, idx), g.reshape(1, d), b.reshape(1, d)],
                       [(d, F32)], tm, name)
    return out


def _mix_ffn(y2, h2, w_out, b_out, g_mix, b_mix, w1, w3, w2, g, b, idx, tm=None, n_chunks=None, name="mix_ffn"):
    n, d = h2.shape
    tm = tm or _pick_tile(n, FFN_ROW_TILES)
    glu = b_out is not None
    wo = w_out.astype(BF16)
    bo = (b_out if glu else jnp.zeros((wo.shape[1],), F32)).reshape(1, wo.shape[1])
    (out,) = _row_call(functools.partial(_mix_ffn_body, _ffn_chunks(w1.shape[-1], n_chunks), glu), [y2, h2],
                       [wo, bo, g_mix.reshape(1, d), b_mix.reshape(1, d),
                        (w1, idx), (w3, idx), (w2, idx), g.reshape(1, d), b.reshape(1, d)],
                       [(d, F32)], tm, name)
    return out


def _ssd_in_body(h_ref, w_ref, xbc_ref, z_ref, dt_ref):
    r = _dot(h_ref[...].astype(BF16), w_ref[...])
    nx, nz = xbc_ref.shape[1], z_ref.shape[1]
    xbc_ref[...] = r[:, :nx]
    z_ref[...] = r[:, nx:nx + nz]
    dt_ref[...] = r[:, nx + nz:]


def _softplus(x):
    return jnp.maximum(x, 0.0) + jnp.log1p(jnp.exp(-jnp.abs(x)))


def _ssd_body(xbc_ref, dt_ref, z_ref, convw_ref, convb_ref, dtb_ref, alog_ref, dskip_ref, normg_ref,
              y_ref, buf_scr, state_scr, y_scr):
    t = SSD_CHUNK
    d_inner = SSD_HEADS * SSD_HEAD_DIM
    gn = SSD_GROUPS * SSD_STATE
    hp = SSD_HEAD_DIM
    r = SSD_HEADS // SSD_GROUPS

    @pl.when(pl.program_id(1) == 0)
    def _():
        buf_scr[0:SUBLANES, :] = jnp.zeros((SUBLANES, buf_scr.shape[1]), F32)
        state_scr[...] = jnp.zeros(state_scr.shape, F32)

    x = xbc_ref[...]
    buf_scr[SUBLANES:SUBLANES + t, :] = x
    acc = convb_ref[...] + convw_ref[SSD_CONV - 1:SSD_CONV, :] * x
    for j in range(1, SSD_CONV):
        acc = acc + convw_ref[SSD_CONV - 1 - j:SSD_CONV - j, :] * buf_scr[SUBLANES - j:SUBLANES - j + t, :]
    buf_scr[0:SUBLANES, :] = x[t - SUBLANES:t, :]
    xbc = _silu(acc)

    dt = _softplus(dt_ref[...] + dtb_ref[...])
    a = dt * (-jnp.exp(alog_ref[...]))
    rows = lax.broadcasted_iota(jnp.int32, (t, t), 0)
    cols = lax.broadcasted_iota(jnp.int32, (t, t), 1)
    tri = rows >= cols
    a_cum = jnp.dot(tri.astype(F32), a, precision=lax.Precision.HIGHEST, preferred_element_type=F32)
    a_last = a_cum[t - 1:t, :]
    a_cum_t = a_cum.T
    dt_t = dt.T
    w_end_t = (dt * jnp.exp(a_last - a_cum)).T
    chunk_dec = jnp.exp(a_last)

    for g in range(SSD_GROUPS):
        b_f = xbc[:, d_inner + g * SSD_STATE:d_inner + (g + 1) * SSD_STATE]
        c_g = xbc[:, d_inner + gn + g * SSD_STATE:d_inner + gn + (g + 1) * SSD_STATE].astype(BF16)
        b_g = b_f.astype(BF16)
        b_ft = b_f.T
        cb = _dot_nt(c_g, b_g)
        y_off_g = _dot(c_g, state_scr[:, g * r * hp:(g + 1) * r * hp].astype(BF16))
        for k in range(r):
            h = g * r + k
            col = jnp.broadcast_to(a_cum[:, h:h + 1], (t, t))
            decay = jnp.exp(jnp.where(tri, col - a_cum_t[h:h + 1, :], NEG_INF))
            xh = xbc[:, h * hp:(h + 1) * hp]
            xb = xh.astype(BF16)
            y_diag = _dot((cb * decay * dt_t[h:h + 1, :]).astype(BF16), xb)
            y_off = y_off_g[:, k * hp:(k + 1) * hp] * jnp.exp(col[:, :hp])
            y_scr[:, h * hp:(h + 1) * hp] = y_diag + y_off + dskip_ref[:, h * hp:(h + 1) * hp] * xh
            st = state_scr[:, h * hp:(h + 1) * hp]
            b_w = (b_ft * w_end_t[h:h + 1, :]).astype(BF16)
            state_scr[:, h * hp:(h + 1) * hp] = st * chunk_dec[:, h:h + 1] + _dot(b_w, xb)

    y = y_scr[...] * _silu(z_ref[...])
    gs = d_inner // SSD_GROUPS
    for g in range(SSD_GROUPS):
        sl = slice(g * gs, (g + 1) * gs)
        y_ref[:, sl] = _rms(y[:, sl], normg_ref[:, sl]).astype(y_ref.dtype)


def _ssd_mixer(h3, w_in, conv_w, conv_b, dt_bias, a_log, d_skip, norm_g):
    b, lp, d = h3.shape
    n = b * lp
    d_inner = SSD_HEADS * SSD_HEAD_DIM
    conv_ch = d_inner + 2 * SSD_GROUPS * SSD_STATE
    w_z, w_xbc, w_dt = w_in[:, :d_inner], w_in[:, d_inner:d_inner + conv_ch], w_in[:, d_inner + conv_ch:]
    w_dt = jnp.pad(w_dt, ((0, 0), (0, LANES - SSD_HEADS)))
    w_all = jnp.concatenate([w_xbc, w_z, w_dt], axis=1).astype(BF16)
    tm = _pick_tile(n, (512, 384, 256, 128))
    xbc, z, dt = _row_call(_ssd_in_body, [h3.reshape(n, d)], [w_all],
                           [(conv_ch, F32), (d_inner, F32), (LANES, F32)], tm, "ssd_in")
    t = SSD_CHUNK
    pad1 = lambda v: jnp.pad(v, (0, LANES - SSD_HEADS)).reshape(1, LANES)
    consts = [conv_w, conv_b.reshape(1, conv_ch), pad1(dt_bias), pad1(a_log),
              jnp.repeat(d_skip, SSD_HEAD_DIM).reshape(1, d_inner), norm_g.reshape(1, d_inner)]
    row_spec = lambda w: pl.BlockSpec((None, t, w), lambda i, c: (i, c, 0))
    y = pl.pallas_call(
        _ssd_body, grid=(b, lp // t),
        in_specs=[row_spec(conv_ch), row_spec(LANES), row_spec(d_inner)]
        + [pl.BlockSpec(c.shape, lambda i, c_: (0, 0)) for c in consts],
        out_specs=row_spec(d_inner),
        out_shape=jax.ShapeDtypeStruct((b, lp, d_inner), BF16),
        scratch_shapes=[pltpu.VMEM((SUBLANES + t, conv_ch), F32), pltpu.VMEM((SSD_STATE, d_inner), F32),
                        pltpu.VMEM((t, d_inner), F32)],
        name="ssd_scan",
        compiler_params=pltpu.CompilerParams(dimension_semantics=("parallel", "arbitrary"),
                                             vmem_limit_bytes=VMEM_LIMIT),
    )(xbc.reshape(b, lp, conv_ch), dt.reshape(b, lp, LANES), z.reshape(b, lp, d_inner), *consts)
    return y.reshape(n, d_inner)


def _qkv_body(n_q, scale, h_ref, w_ref, qk_ref, vt_ref):
    r = _dot(h_ref[...].astype(BF16), w_ref[...])
    n_qk = qk_ref.shape[1]
    qk_ref[:, :n_q] = (r[:, :n_q] * scale).astype(qk_ref.dtype)
    qk_ref[:, n_q:] = r[:, n_q:n_qk].astype(qk_ref.dtype)
    vt_ref[...] = r[:, n_qk:].T.astype(vt_ref.dtype)


def _causal_mask_pair(s, tq):
    rows = lax.broadcasted_iota(jnp.int32, s.shape, 0)
    cols = lax.broadcasted_iota(jnp.int32, s.shape, 1)
    cols = jnp.where(cols >= tq, cols - tq, cols)
    return jnp.where(rows <= cols, s, NEG_INF)


def _flash_causal(qi, tq, scores, vt_ref, m_scr, l_scr, acc_scr, s_scrs, p_scrs, mx_scrs):
    m_scr[...] = jnp.full(m_scr.shape, NEG_INF, F32)
    l_scr[...] = jnp.zeros(l_scr.shape, F32)
    acc_scr[...] = jnp.zeros(acc_scr.shape, F32)
    p_scrs[1][...] = jnp.zeros(p_scrs[1].shape, p_scrs[1].dtype)

    def softmax(s, s_max, pv):
        m_prev = m_scr[...]
        m_new = jnp.maximum(m_prev, s_max)
        alpha = jnp.exp2(m_prev - m_new)
        p = jnp.exp2(s - m_new)
        l_scr[...] = alpha * l_scr[...] + jnp.sum(p, axis=0, keepdims=True)
        m_scr[...] = m_new
        acc_scr[...] = alpha * (acc_scr[...] + pv)
        return p.astype(BF16)

    def produce(kj, slot):
        s = scores(kj)
        s_scrs[slot][...] = s
        mx_scrs[slot][...] = jnp.max(s, axis=0, keepdims=True)

    def step(kj, cur, nxt):
        pv = _dot(vt_ref[jnp.maximum(kj - 1, 0)], p_scrs[nxt][...])
        p_scrs[cur][...] = softmax(s_scrs[cur][...], mx_scrs[cur][...], pv)
        produce(kj + 1, nxt)

    produce(0, 0)

    def body(t, carry):
        step(2 * t, 0, 1)
        step(2 * t + 1, 1, 0)
        return carry

    lax.fori_loop(0, lax.shift_right_logical(qi, 1), body, 0)
    pl.when((qi & 1) == 1)(functools.partial(step, qi - 1, 0, 1))

    def last(cur, nxt):
        pv = _dot(vt_ref[jnp.maximum(qi - 1, 0)], p_scrs[nxt][...])
        s = _causal_mask_pair(s_scrs[cur][...], tq)
        p = softmax(s, jnp.max(s, axis=0, keepdims=True), pv)
        acc_scr[...] = acc_scr[...] + _dot(vt_ref[qi], p)

    for par in range(2):
        pl.when((qi & 1) == par)(functools.partial(last, par, 1 - par))


def _flash_head(tq, nq, scores, vt_ref, finish, m_all, l_all, acc_all,
                m_scr, l_scr, acc_scr, s_scrs, p_scrs, mx_scrs, exact_ref):
    def init(qi, carry):
        m_all[qi] = jnp.max(scores(0, qi, ATTN_REF_KEYS), axis=0, keepdims=True)
        l_all[qi] = jnp.zeros(l_all.shape[1:], F32)
        acc_all[qi] = jnp.zeros(acc_all.shape[1:], F32)
        return carry

    lax.fori_loop(0, nq, init, 0, unroll=ATTN_EDGE_UNROLL)

    def probs(kj, qi, slot):
        p = jnp.exp2(scores(kj, qi) - m_all[qi])
        l_all[qi] += jnp.sum(p, axis=0, keepdims=True)
        p_scrs[slot][...] = p.astype(BF16)

    def step(prev, cur, prv):
        kjp, qip = prev
        wrap = qip + 1 >= nq
        kj = jnp.where(wrap, kjp + 1, kjp)
        qi = jnp.where(wrap, kjp + 1, qip + 1)
        p = jnp.exp2(scores(kj, qi) - m_all[qi])
        acc_all[qip] += _dot(vt_ref[kjp], p_scrs[prv][...])
        l_all[qi] += jnp.sum(p, axis=0, keepdims=True)
        p_scrs[cur][...] = p.astype(BF16)
        return kj, qi

    n_steps = nq * (nq + 1) // 2 - 1
    probs(0, 0, 0)

    def body(t, prev):
        return step(step(prev, 1, 0), 0, 1)

    prev = lax.fori_loop(0, n_steps // 2, body, (jnp.int32(0), jnp.int32(0)), unroll=ATTN_PAIR_UNROLL)
    if n_steps % 2:
        prev = step(prev, 1, 0)
    acc_all[prev[1]] += _dot(vt_ref[prev[0]], p_scrs[n_steps % 2][...])

    ok = jnp.all(l_all[...] < SOFTMAX_SUM_LIMIT) & jnp.all(jnp.abs(acc_all[...]) < F32_FINITE_LIMIT)
    exact_ref[0] = jnp.where(ok, 0, 1)

    @pl.when(exact_ref[0] == 0)
    def _():
        def fin(qi, carry):
            finish(qi, l_all[qi], acc_all[qi])
            return carry

        lax.fori_loop(0, nq, fin, 0, unroll=ATTN_EDGE_UNROLL)

    @pl.when(exact_ref[0] == 1)
    def _():
        def redo(qi, carry):
            _flash_causal(qi, tq, lambda kj: scores(kj, qi), vt_ref, m_scr, l_scr, acc_scr, s_scrs, p_scrs, mx_scrs)
            finish(qi, l_scr[...], acc_scr[...])
            return carry

        lax.fori_loop(0, nq, redo, 0)


def _flash_scratch(e, tq, nq):
    w = 2 * tq
    return [pltpu.VMEM((nq, 1, w), F32), pltpu.VMEM((nq, 1, w), F32), pltpu.VMEM((nq, e, w), F32),
            pltpu.VMEM((1, w), F32), pltpu.VMEM((1, w), F32), pltpu.VMEM((e, w), F32),
            pltpu.VMEM((tq, w), F32), pltpu.VMEM((tq, w), F32), pltpu.VMEM((tq, w), BF16), pltpu.VMEM((tq, w), BF16),
            pltpu.VMEM((1, w), F32), pltpu.VMEM((1, w), F32), pltpu.SMEM((1,), jnp.int32)]


def _t5_bucket(dist):
    max_exact = REL_BUCKETS // 2
    d = jnp.maximum(dist, 0)
    df = jnp.maximum(d, max_exact).astype(F32)
    large = max_exact + (jnp.log(df / max_exact) / math.log(REL_MAX_DIST / max_exact)
                         * (REL_BUCKETS - max_exact)).astype(jnp.int32)
    large = jnp.minimum(large, REL_BUCKETS - 1)
    return jnp.where(d < max_exact, d, large)


def _diff_attn_body(tq, nq, lam_init, q_ref, k_ref, vt_ref, bias_ref, lam_ref, g_ref, o_ref,
                    m_all, l_all, acc_all, m_scr, l_scr, acc_scr, s0_scr, s1_scr, p0_scr, p1_scr, mx0_scr, mx1_scr,
                    exact_ref, qcat_all):
    lane = lax.broadcasted_iota(jnp.int32, (tq, q_ref.shape[1]), 1)

    def prep(qi, carry):
        q = q_ref[pl.ds(pl.multiple_of(qi * tq, tq), tq), :]
        zero = jnp.zeros_like(q)
        qcat_all[qi, 0:tq, :] = jnp.where(lane < DIFF_HEAD_DIM, q, zero)
        qcat_all[qi, tq:2 * tq, :] = jnp.where(lane >= DIFF_HEAD_DIM, q, zero)
        return carry

    lax.fori_loop(0, nq, prep, 0)

    def scores(kj, qi, rows=tq):
        start = pl.multiple_of(kj * tq, tq)
        bias = bias_ref[jnp.minimum(qi - kj, 2), 0:rows, :]
        return _dot_nt(k_ref[pl.ds(start, rows), :], qcat_all[qi]) + jnp.concatenate([bias, bias], axis=1)

    def finish(qi, l, acc):
        a = acc * (1.0 / l)
        o_t = a[:, :tq] - lam_ref[...] * a[:, tq:]
        o_ref[pl.ds(pl.multiple_of(qi * tq, tq), tq), :] = (
            _rms(o_t.T, g_ref[...]) * (1.0 - lam_init)).astype(o_ref.dtype)

    _flash_head(tq, nq, scores, vt_ref, finish, m_all, l_all, acc_all, m_scr, l_scr, acc_scr,
                (s0_scr, s1_scr), (p0_scr, p1_scr), (mx0_scr, mx1_scr), exact_ref)


def _diff_mixer(h3, w_qkv, lam_q1, lam_k1, lam_q2, lam_k2, subln_g, rel_bias, lam_init):
    b, lp, d = h3.shape
    n = b * lp
    hd = DIFF_HEADS * 2 * DIFF_HEAD_DIM
    tq = _pick_tile(lp, (384, 256, 128))
    assert tq >= REL_MAX_DIST
    nk = lp // tq
    qk, vt = _row_call(functools.partial(_qkv_body, hd, DIFF_HEAD_DIM ** -0.5 * LOG2E), [h3.reshape(n, d)],
                       [w_qkv.astype(BF16)], [(2 * hd, BF16)], tq, "diff_qkv", lp=lp, t_outs=[(hd, BF16)])
    qk = qk.reshape(b, lp, 2 * hd)
    m2 = 2 * tq
    kk = jnp.arange(m2)
    dist = jnp.where(kk < tq, kk, kk - m2)[None, :] + (jnp.arange(3) * tq)[:, None]
    tab = jnp.transpose(rel_bias[_t5_bucket(dist)].astype(F32) * LOG2E, (2, 0, 1))
    tab = jnp.where((jnp.arange(3)[:, None] == 0) & (kk[None, :] >= tq), NEG_INF, tab)
    nh = DIFF_HEADS
    bias = jnp.tile(tab, (1, 1, tq))[..., :tq * (m2 - 1)].reshape(nh, 3, tq, m2 - 1)[..., :tq]
    lam = (jnp.exp(jnp.sum(lam_q1.astype(F32) * lam_k1.astype(F32)))
           - jnp.exp(jnp.sum(lam_q2.astype(F32) * lam_k2.astype(F32))) + lam_init)
    w = 2 * DIFF_HEAD_DIM
    o = pl.pallas_call(
        functools.partial(_diff_attn_body, tq, nk, lam_init),
        grid=(b, nh),
        in_specs=[pl.BlockSpec((None, lp, w), lambda bi, hi: (bi, 0, hi)),
                  pl.BlockSpec((None, lp, w), lambda bi, hi: (bi, 0, nh + hi)),
                  pl.BlockSpec((None, nk, DIFF_V_DIM, tq), lambda bi, hi: (bi, 0, hi, 0)),
                  pl.BlockSpec((None, 3, tq, tq), lambda bi, hi: (hi, 0, 0, 0)),
                  pl.BlockSpec((1, 1), lambda bi, hi: (0, 0)),
                  pl.BlockSpec((1, DIFF_V_DIM), lambda bi, hi: (0, 0))],
        out_specs=pl.BlockSpec((None, lp, DIFF_V_DIM), lambda bi, hi: (bi, 0, hi)),
        out_shape=jax.ShapeDtypeStruct((b, lp, nh * DIFF_V_DIM), BF16),
        scratch_shapes=_flash_scratch(DIFF_V_DIM, tq, nk) + [pltpu.VMEM((nk, 2 * tq, w), BF16)],
        name="diff_attn",
        compiler_params=pltpu.CompilerParams(dimension_semantics=("parallel", "parallel"),
                                             vmem_limit_bytes=VMEM_LIMIT),
    )(qk, qk, vt, bias, lam.reshape(1, 1), subln_g.reshape(1, DIFF_V_DIM))
    return o.reshape(n, nh * DIFF_V_DIM)


def _gelu_tanh(x):
    return 0.5 * x * (1.0 + jnp.tanh(math.sqrt(2.0 / math.pi) * (x + 0.044715 * (x * x * x))))


def _s5_body(bp, x_ref, m_ref, bc_ref, ccre_ref, ccim_ref, are_ref, aim_ref, y_ref, sre_scr, sim_scr):
    rows = x_ref.shape[0]
    gb = m_ref.shape[0]
    kw = m_ref.shape[1]
    p = ccre_ref.shape[1]
    for g in range(gb):
        e = _dot(x_ref[:, g * kw:(g + 1) * kw], bc_ref[g])
        sre_scr[:, g * p:(g + 1) * p] = e[:, :p]
        sim_scr[:, g * p:(g + 1) * p] = e[:, p:]

    ar, ai = are_ref[...], aim_ref[...]
    sub = lax.broadcasted_iota(jnp.int32, ar.shape, 0)
    per_tile = S5_SUBLANES // bp

    def tile_step(r, carry):
        pr, pi = carry
        r0 = pl.multiple_of(r * S5_SUBLANES, S5_SUBLANES)
        er, ei = sre_scr[pl.ds(r0, S5_SUBLANES), :], sim_scr[pl.ds(r0, S5_SUBLANES), :]
        in_r, in_i = pr, pi
        for c in range(per_tile):
            nr = ar * pr - ai * pi + er
            ni = ar * pi + ai * pr + ei
            if per_tile > 1:
                nr = pltpu.roll(nr, bp, axis=0)
                ni = pltpu.roll(ni, bp, axis=0)
            if c + 1 < per_tile:
                sel = (sub >= (c + 1) * bp) & (sub < (c + 2) * bp)
                in_r, in_i = jnp.where(sel, nr, in_r), jnp.where(sel, ni, in_i)
                pr, pi = nr, ni
            else:
                pr, pi = nr, ni
        if per_tile > 1:
            pr = jnp.where(sub < bp, pr, pltpu.roll(pr, bp, axis=0))
            pi = jnp.where(sub < bp, pi, pltpu.roll(pi, bp, axis=0))
        sre_scr[pl.ds(r0, S5_SUBLANES), :] = in_r
        sim_scr[pl.ds(r0, S5_SUBLANES), :] = in_i
        return pr, pi

    zero = jnp.zeros(ar.shape, F32)
    lax.fori_loop(0, rows // S5_SUBLANES, tile_step, (zero, zero), unroll=S5_UNROLL)

    for g in range(gb):
        xg = x_ref[:, g * kw:(g + 1) * kw]
        y = (_dot(xg, m_ref[g]) + _dot(sre_scr[:, g * p:(g + 1) * p].astype(BF16), ccre_ref[g])
             + _dot(sim_scr[:, g * p:(g + 1) * p].astype(BF16), ccim_ref[g]))
        y_ref[:, g * kw:(g + 1) * kw] = _gelu_tanh(y).astype(y_ref.dtype)


def _s5_pack_body(n_batch, h_ref, perm_ref, x_ref):
    t_len = h_ref.shape[1]
    xcat = jnp.concatenate([h_ref[:, t, :].astype(BF16) for t in range(t_len)], axis=1)
    x = _dot(xcat, perm_ref[...]).astype(x_ref.dtype)
    x_ref[...] = jnp.where(pl.program_id(0) < n_batch, x, jnp.zeros_like(x))


def _s5_unpack_body(y_ref, perm_ref, o_ref):
    t_len = o_ref.shape[1]
    res = _dot(y_ref[...], perm_ref[...])
    for t in range(t_len):
        o_ref[:, t, :] = res[:, t * LANES:(t + 1) * LANES]


def _s5_chunk_operators(lam_re, lam_im, log_step, b_re, b_im, c_re, c_im, d_skip):
    hi = lax.Precision.HIGHEST
    t_len = S5_CHUNK
    step = jnp.exp(log_step.astype(F32))[:, None]
    lr = jnp.minimum(lam_re.astype(F32), -1e-4)
    li = lam_im.astype(F32)
    mag = jnp.exp(lr * step)
    ar, ai = mag * jnp.cos(li * step), mag * jnp.sin(li * step)
    den = lr * lr + li * li
    cr = ((ar - 1.0) * lr + ai * li) / den
    ci = (ai * lr - (ar - 1.0) * li) / den
    br, bi = b_re.astype(F32), b_im.astype(F32)
    bbr = cr[..., None] * br - ci[..., None] * bi
    bbi = cr[..., None] * bi + ci[..., None] * br
    k = jnp.arange(t_len + 1, dtype=F32)[:, None, None]
    pmag = jnp.exp(k * (lr * step))
    pr, pi = pmag * jnp.cos(k * (li * step)), pmag * jnp.sin(k * (li * step))
    abr = pr[:t_len, ..., None] * bbr - pi[:t_len, ..., None] * bbi
    abi = pr[:t_len, ..., None] * bbi + pi[:t_len, ..., None] * bbr
    cre, cim = c_re.astype(F32), c_im.astype(F32)
    kern = (jnp.einsum('gop,kgpc->kgoc', cre, abr, precision=hi)
            - jnp.einsum('gop,kgpc->kgoc', cim, abi, precision=hi))
    kern = kern.at[0].add(jax.vmap(jnp.diag)(d_skip.astype(F32)))
    sig = jnp.arange(t_len)
    lag = sig[None, :] - sig[:, None]
    m = jnp.where((lag >= 0)[..., None, None, None], kern[jnp.maximum(lag, 0)], 0.0)
    g, o, c = kern.shape[1:]
    m = jnp.transpose(m, (2, 0, 4, 1, 3)).reshape(g, t_len * c, t_len * o)
    rev = t_len - 1 - sig
    p = bbr.shape[1]
    bc = jnp.concatenate([jnp.transpose(abr[rev], (1, 0, 3, 2)).reshape(g, t_len * c, p),
                          jnp.transpose(abi[rev], (1, 0, 3, 2)).reshape(g, t_len * c, p)], axis=2)
    car = cre[None] * pr[1:, :, None, :] - cim[None] * pi[1:, :, None, :]
    cai = cre[None] * pi[1:, :, None, :] + cim[None] * pr[1:, :, None, :]
    cc_re = jnp.transpose(car, (1, 3, 0, 2)).reshape(g, p, t_len * o)
    cc_im = -jnp.transpose(cai, (1, 3, 0, 2)).reshape(g, p, t_len * o)
    return (m.astype(BF16), bc.astype(BF16), cc_re.astype(BF16), cc_im.astype(BF16),
            pr[t_len].reshape(g * p), pi[t_len].reshape(g * p))


def _s5_mixer(h3, lam_re, lam_im, log_step, b_re, b_im, c_re, c_im, d_skip):
    b, lp, d = h3.shape
    t_len, g, c = S5_CHUNK, S5_GROUPS, S5_GROUP
    assert lp % t_len == 0 and b <= S5_SUBLANES
    bp = 4 if b <= 4 else S5_SUBLANES
    nj = lp // t_len
    m, bc, cc_re, cc_im, at_re, at_im = _s5_chunk_operators(lam_re, lam_im, log_step, b_re, b_im, c_re, c_im, d_skip)
    n_state = g * S5_STATE
    at_re = jnp.broadcast_to(at_re.reshape(1, n_state), (S5_SUBLANES, n_state))
    at_im = jnp.broadcast_to(at_im.reshape(1, n_state), (S5_SUBLANES, n_state))
    rows = nj * bp
    gb = S5_GROUP_BLOCK
    kw = t_len * c
    assert gb * c == LANES
    n_blk = g // gb
    src = jnp.arange(gb * kw)
    tau, r, ch = src // (gb * c), (src // c) % gb, src % c
    perm = jax.nn.one_hot(r * kw + tau * c + ch, gb * kw, dtype=BF16)
    x = pl.pallas_call(
        functools.partial(_s5_pack_body, b),
        grid=(bp, n_blk),
        in_specs=[pl.BlockSpec((None, nj, t_len, LANES), lambda bi, mi: (jnp.minimum(bi, b - 1), 0, 0, mi)),
                  pl.BlockSpec(perm.shape, lambda bi, mi: (0, 0))],
        out_specs=pl.BlockSpec((nj, gb * kw), lambda bi, mi: (0, bi * n_blk + mi)),
        out_shape=jax.ShapeDtypeStruct((nj, bp * g * kw), BF16),
        name="s5_pack",
        compiler_params=pltpu.CompilerParams(dimension_semantics=("parallel", "parallel"),
                                             vmem_limit_bytes=VMEM_LIMIT),
    )(h3.reshape(b, nj, t_len, d), perm).reshape(rows, g * kw)
    y = pl.pallas_call(
        functools.partial(_s5_body, bp), grid=(g // gb,),
        in_specs=[pl.BlockSpec((rows, gb * kw), lambda i: (0, i)),
                  pl.BlockSpec((gb, kw, kw), lambda i: (i, 0, 0)),
                  pl.BlockSpec((gb, kw, 2 * S5_STATE), lambda i: (i, 0, 0)),
                  pl.BlockSpec((gb, S5_STATE, kw), lambda i: (i, 0, 0)),
                  pl.BlockSpec((gb, S5_STATE, kw), lambda i: (i, 0, 0)),
                  pl.BlockSpec((S5_SUBLANES, gb * S5_STATE), lambda i: (0, i)),
                  pl.BlockSpec((S5_SUBLANES, gb * S5_STATE), lambda i: (0, i))],
        out_specs=pl.BlockSpec((rows, gb * kw), lambda i: (0, i)),
        out_shape=jax.ShapeDtypeStruct((rows, g * kw), BF16),
        scratch_shapes=[pltpu.VMEM((rows, gb * S5_STATE), F32), pltpu.VMEM((rows, gb * S5_STATE), F32)],
        name="s5_chunks",
        compiler_params=pltpu.CompilerParams(dimension_semantics=("parallel",), vmem_limit_bytes=VMEM_LIMIT),
    )(x, m, bc, cc_re, cc_im, at_re, at_im)
    y = pl.pallas_call(
        _s5_unpack_body,
        grid=(b, n_blk),
        in_specs=[pl.BlockSpec((nj, gb * kw), lambda bi, mi: (0, bi * n_blk + mi)),
                  pl.BlockSpec(perm.shape, lambda bi, mi: (0, 0))],
        out_specs=pl.BlockSpec((None, nj, t_len, LANES), lambda bi, mi: (bi, 0, 0, mi)),
        out_shape=jax.ShapeDtypeStruct((b, nj, t_len, d), F32),
        name="s5_unpack",
        compiler_params=pltpu.CompilerParams(dimension_semantics=("parallel", "parallel"),
                                             vmem_limit_bytes=VMEM_LIMIT),
    )(y.reshape(nj, bp * g * kw), perm.T)
    return y.reshape(b * lp, d)


def _rope_block(x, cm, s1, s2):
    return (x * cm + pltpu.roll(x, LANES - MLA_ROPE // 2, axis=1) * s1 + pltpu.roll(x, MLA_ROPE // 2, axis=1) * s2)


def _mla_in_body(h_ref, qcm_ref, qs1_ref, qs2_ref, kcm_ref, ks1_ref, ks2_ref,
                 win_ref, qg_ref, kvg_ref, wuq_ref, wuk_ref, wuv_ref, q_ref, k_ref, vt_ref):
    c = _dot(h_ref[...].astype(BF16), win_ref[...])
    c_q = _rms(c[:, :MLA_Q_RANK], qg_ref[...]).astype(BF16)
    c_kv = _rms(c[:, MLA_Q_RANK:MLA_Q_RANK + MLA_KV_RANK], kvg_ref[...]).astype(BF16)
    k_r = _rope_block(c[:, MLA_Q_RANK + MLA_KV_RANK:], kcm_ref[...], ks1_ref[...], ks2_ref[...])
    q = _dot(c_q, wuq_ref[...])
    k = _dot(c_kv, wuk_ref[...])
    qcm, qs1, qs2 = qcm_ref[...], qs1_ref[...], qs2_ref[...]
    for h in range(MLA_HEADS):
        sl = slice(h * LANES, (h + 1) * LANES)
        q_ref[:, sl] = _rope_block(q[:, sl], qcm, qs1, qs2).astype(q_ref.dtype)
        k_ref[:, sl] = (k[:, sl] + k_r).astype(k_ref.dtype)
    vt_ref[...] = _dot(c_kv, wuv_ref[...]).T.astype(vt_ref.dtype)


def _mla_attn_body(tq, nq, q_ref, k_ref, vt_ref, mask_ref, o_ref,
                   m_all, l_all, acc_all, m_scr, l_scr, acc_scr, s0_scr, s1_scr, p0_scr, p1_scr, mx0_scr, mx1_scr,
                   exact_ref):
    def scores(kj, qi, rows=tq):
        k0 = pl.multiple_of(kj * tq, tq)
        q0 = pl.multiple_of(qi * tq, tq)
        mask = mask_ref[jnp.minimum(qi - kj, 1), 0:rows, :]
        return jnp.concatenate(
            [_dot_nt(k_ref[pl.ds(k0, rows), hh * LANES:(hh + 1) * LANES],
                     q_ref[pl.ds(q0, tq), hh * LANES:(hh + 1) * LANES]) + mask for hh in range(2)],
            axis=1)

    row = lax.broadcasted_iota(jnp.int32, (2 * MLA_V, tq), 0)

    def finish(qi, l, acc):
        a = acc * (1.0 / l)
        o_t = jnp.where(row < MLA_V, a[:, :tq], a[:, tq:])
        o_ref[pl.ds(pl.multiple_of(qi * tq, tq), tq), :] = o_t.T.astype(o_ref.dtype)

    _flash_head(tq, nq, scores, vt_ref, finish, m_all, l_all, acc_all, m_scr, l_scr, acc_scr,
                (s0_scr, s1_scr), (p0_scr, p1_scr), (mx0_scr, mx1_scr), exact_ref)


def _mla_mixer(h3, w_in, q_norm_g, kv_norm_g, w_uq, w_ukv):
    b, lp, d = h3.shape
    n = b * lp
    nh = MLA_HEADS
    qk = MLA_NOPE + MLA_ROPE
    half = MLA_ROPE // 2
    w_kr = jnp.pad(w_in[:, MLA_Q_RANK + MLA_KV_RANK:], ((0, 0), (MLA_NOPE, LANES - qk)))
    w_in_all = jnp.concatenate([w_in[:, :MLA_Q_RANK + MLA_KV_RANK], w_kr], axis=1).astype(BF16)
    w_uq_p = jnp.pad(w_uq.reshape(MLA_Q_RANK, nh, qk), ((0, 0), (0, 0), (0, LANES - qk)))
    w_uq_p = w_uq_p.reshape(MLA_Q_RANK, nh * LANES).astype(BF16)
    w_ukv3 = w_ukv.reshape(MLA_KV_RANK, nh, MLA_NOPE + MLA_V)
    w_uk_p = jnp.pad(w_ukv3[:, :, :MLA_NOPE], ((0, 0), (0, 0), (0, LANES - MLA_NOPE)))
    w_uk_p = w_uk_p.reshape(MLA_KV_RANK, nh * LANES).astype(BF16)
    w_uv = w_ukv3[:, :, MLA_NOPE:].reshape(MLA_KV_RANK, nh * MLA_V).astype(BF16)
    pos = jnp.arange(lp, dtype=F32)
    inv_freq = ROPE_BASE ** (-jnp.arange(0, MLA_ROPE, 2, dtype=F32) / MLA_ROPE)
    ang = pos[:, None] * inv_freq[None, :]
    cos, sin = jnp.cos(ang), jnp.sin(ang)
    z = lambda w: jnp.zeros((lp, w), F32)
    cm = jnp.concatenate([jnp.ones((lp, MLA_NOPE), F32), cos, cos, z(LANES - qk)], axis=1)
    s1 = jnp.concatenate([z(MLA_NOPE), -sin, z(LANES - MLA_NOPE - half)], axis=1)
    s2 = jnp.concatenate([z(MLA_NOPE + half), sin, z(LANES - qk)], axis=1)
    scale = qk ** -0.5 * LOG2E
    tq = _pick_tile(lp, (384, 256, 128))
    nk = lp // tq
    q, k, vt = _row_call(
        _mla_in_body, [h3.reshape(n, d)],
        [w_in_all, q_norm_g.reshape(1, MLA_Q_RANK), kv_norm_g.reshape(1, MLA_KV_RANK), w_uq_p, w_uk_p, w_uv],
        [(nh * LANES, BF16), (nh * LANES, BF16)], tq, "mla_in",
        pos_ins=[cm * scale, s1 * scale, s2 * scale, cm, s1, s2], lp=lp, t_outs=[(nh * MLA_V, BF16)])
    q, k = q.reshape(b, lp, nh * LANES), k.reshape(b, lp, nh * LANES)
    ti = jnp.arange(tq)
    mask = jnp.stack([jnp.where(ti[:, None] <= ti[None, :], 0.0, NEG_INF), jnp.zeros((tq, tq))]).astype(F32)
    o = pl.pallas_call(
        functools.partial(_mla_attn_body, tq, nk),
        grid=(b, nh // 2),
        in_specs=[pl.BlockSpec((None, lp, 2 * LANES), lambda bi, hi: (bi, 0, hi)),
                  pl.BlockSpec((None, lp, 2 * LANES), lambda bi, hi: (bi, 0, hi)),
                  pl.BlockSpec((None, nk, 2 * MLA_V, tq), lambda bi, hi: (bi, 0, hi, 0)),
                  pl.BlockSpec((2, tq, tq), lambda bi, hi: (0, 0, 0))],
        out_specs=pl.BlockSpec((None, lp, 2 * MLA_V), lambda bi, hi: (bi, 0, hi)),
        out_shape=jax.ShapeDtypeStruct((b, lp, nh * MLA_V), BF16),
        scratch_shapes=_flash_scratch(2 * MLA_V, tq, nk),
        name="mla_attn",
        compiler_params=pltpu.CompilerParams(dimension_semantics=("parallel", "parallel"),
                                             vmem_limit_bytes=VMEM_LIMIT),
    )(q, k, vt, mask)
    return o.reshape(n, nh * MLA_V)


def kernel(x, meta, rel_bias, ln_g, ln_b, ffn_w1, ffn_w3, ffn_w2, ssd_w_in, ssd_conv_w, ssd_conv_b, ssd_dt_bias, ssd_a_log, ssd_d, ssd_norm_g, ssd_w_out, diff_w_qkv, diff_lam_q1, diff_lam_k1, diff_lam_q2, diff_lam_k2, diff_subln_g, diff_w_out, s5_lam_re, s5_lam_im, s5_log_step, s5_b_re, s5_b_im, s5_c_re, s5_c_im, s5_d, s5_w_glu, s5_b_glu, mla_w_in, mla_q_norm_g, mla_kv_norm_g, mla_w_uq, mla_w_ukv, mla_w_out):
    b, seq, d = x.shape
    l = seq + N_META
    lp = -(-l // SEQ_ALIGN) * SEQ_ALIGN
    n = b * lp
    h = jnp.concatenate([jnp.broadcast_to(meta[None].astype(x.dtype), (b, N_META, d)), x,
                         jnp.zeros((b, lp - l, d), x.dtype)], axis=1).reshape(n, d)
    w1b, w3b, w2b = ffn_w1.astype(BF16), ffn_w3.astype(BF16), ffn_w2.astype(BF16)
    for i in range(DEPTH):
        kind, j = i % 4, i // 4
        h = _ffn(h, w1b, w3b, w2b, ln_g[i, 0], ln_b[i, 0], (i, 0))
        h3 = h.reshape(b, lp, d)
        if kind == 0:
            y = _ssd_mixer(h3, ssd_w_in[j], ssd_conv_w[j], ssd_conv_b[j], ssd_dt_bias[j], ssd_a_log[j],
                           ssd_d[j], ssd_norm_g[j])
            w_out, b_out = ssd_w_out[j], None
        elif kind == 1:
            lam_init = 0.8 - 0.6 * math.exp(-0.3 * i)
            y = _diff_mixer(h3, diff_w_qkv[j], diff_lam_q1[j], diff_lam_k1[j], diff_lam_q2[j], diff_lam_k2[j],
                            diff_subln_g[j], rel_bias, lam_init)
            w_out, b_out = diff_w_out[j], None
        elif kind == 2:
            y = _s5_mixer(h3, s5_lam_re[j], s5_lam_im[j], s5_log_step[j], s5_b_re[j], s5_b_im[j], s5_c_re[j],
                          s5_c_im[j], s5_d[j])
            w_out, b_out = s5_w_glu[j], s5_b_glu[j]
        else:
            y = _mla_mixer(h3, mla_w_in[j], mla_q_norm_g[j], mla_kv_norm_g[j], mla_w_uq[j], mla_w_ukv[j])
            w_out, b_out = mla_w_out[j], None
        h = _mix_ffn(y, h, w_out, b_out, ln_g[i, 1], ln_b[i, 1], w1b, w3b, w2b, ln_g[i, 2], ln_b[i, 2], (i, 1))
    return h.reshape(b, lp, d)[:, N_META:l]
```

```python
import functools
import math

import jax
import jax.numpy as jnp
from jax import lax
from jax.experimental import pallas as pl
from jax.experimental.pallas import tpu as pltpu

F32 = jnp.float32
BF16 = jnp.bfloat16

N_META = 16
DEPTH = 4
ALPHA = (2.0 * DEPTH) ** 0.25
LN_EPS = 1e-5
RMS_EPS = 1e-6
NEG_INF = -1e30
LOG2E = math.log2(math.e)
ATTN_REF_KEYS = 16
ATTN_PAIR_UNROLL = 8
ATTN_EDGE_UNROLL = 4
SOFTMAX_SUM_LIMIT = 2.0 ** 100
F32_FINITE_LIMIT = 3.0e38

SSD_HEAD_DIM = 64
SSD_HEADS = 32
SSD_GROUPS = 8
SSD_STATE = 128
SSD_CONV = 4
SSD_CHUNK = 128

DIFF_HEADS = 8
DIFF_HEAD_DIM = 64
DIFF_V_DIM = 128
REL_BUCKETS = 32
REL_MAX_DIST = 128

S5_GROUP = 16
S5_GROUPS = 64
S5_STATE = 64
S5_CHUNK = 16
S5_GROUP_BLOCK = 8
SUBLANES = 8
S5_SUBLANES = SUBLANES
S5_UNROLL = 4

MLA_HEADS = 16
MLA_Q_RANK = 384
MLA_KV_RANK = 256
MLA_NOPE = 64
MLA_ROPE = 32
MLA_V = 64
ROPE_BASE = 10000.0

FFN_ROW_TILES = (512, 384, 256, 128)
FFN_CHUNKS = 11

LANES = 128
SEQ_ALIGN = 128
VMEM_LIMIT = 56 * 1024 * 1024


def _pick_tile(n, candidates):
    for c in candidates:
        if n % c == 0:
            return c
    raise ValueError(f"no tile for {n}")


def _layer_norm(z, g, b):
    mu = jnp.mean(z, axis=-1, keepdims=True)
    d = z - mu
    var = jnp.mean(d * d, axis=-1, keepdims=True)
    return d * lax.rsqrt(var + LN_EPS) * g + b


def _rms(z, g):
    return z * lax.rsqrt(jnp.mean(z * z, axis=-1, keepdims=True) + RMS_EPS) * g


def _silu(x):
    return x * (1.0 / (1.0 + jnp.exp(-x)))


def _dot(a, b):
    return jnp.dot(a, b, preferred_element_type=F32)


def _dot_nt(a, b):
    return lax.dot_general(a, b, (((1,), (1,)), ((), ())), preferred_element_type=F32)


def _row_call(body, row_ins, const_ins, outs, tm, name, pos_ins=(), lp=None, t_outs=()):
    n = row_ins[0].shape[0]
    assert n % tm == 0
    in_specs = [pl.BlockSpec((tm, a.shape[1]), lambda i: (i, 0)) for a in row_ins]
    if pos_ins or t_outs:
        assert lp % tm == 0
        nb = lp // tm
        in_specs += [pl.BlockSpec((tm, a.shape[1]), lambda i: (i % nb, 0)) for a in pos_ins]
    const_arrays = []
    for c in const_ins:
        arr, lead = c if isinstance(c, tuple) else (c, ())
        const_arrays.append(arr)
        in_specs.append(pl.BlockSpec((None,) * len(lead) + arr.shape[len(lead):],
                                     functools.partial(lambda lead_, i: lead_ + (0, 0), tuple(lead)),
                                     pipeline_mode=pl.Buffered(1)))
    out_specs = [pl.BlockSpec((tm, w), lambda i: (i, 0)) for (w, _) in outs]
    out_shape = [jax.ShapeDtypeStruct((n, w), dt) for (w, dt) in outs]
    out_specs += [pl.BlockSpec((None, None, w, tm), lambda i: (i // nb, i % nb, 0, 0)) for (w, _) in t_outs]
    out_shape += [jax.ShapeDtypeStruct((n // lp, nb, w, tm), dt) for (w, dt) in t_outs]
    res = pl.pallas_call(
        body, grid=(n // tm,), in_specs=in_specs, out_specs=out_specs, out_shape=out_shape, name=name,
        compiler_params=pltpu.CompilerParams(dimension_semantics=("parallel",), vmem_limit_bytes=VMEM_LIMIT),
    )(*row_ins, *pos_ins, *const_arrays)
    return res


def _ffn_rows(x, n_chunks, w1_ref, w3_ref, w2_ref, g_ref, b_ref):
    xb = x.astype(BF16)
    fc = w1_ref.shape[1] // n_chunks
    y = None
    for c in range(n_chunks):
        sl = slice(c * fc, (c + 1) * fc)
        gate = _dot(xb, w1_ref[:, sl])
        up = _dot(xb, w3_ref[:, sl])
        a = (_silu(gate) * up).astype(BF16)
        part = _dot(a, w2_ref[sl, :])
        y = part if y is None else y + part
    return _layer_norm(ALPHA * x + 0.5 * y, g_ref[...], b_ref[...])


def _ffn_body(n_chunks, h_ref, w1_ref, w3_ref, w2_ref, g_ref, b_ref, o_ref):
    o_ref[...] = _ffn_rows(h_ref[...], n_chunks, w1_ref, w3_ref, w2_ref, g_ref, b_ref)


def _mix_ffn_body(n_chunks, glu, y_ref, h_ref, wo_ref, bo_ref, gm_ref, bm_ref,
                  w1_ref, w3_ref, w2_ref, g_ref, b_ref, o_ref):
    m = _dot(y_ref[...].astype(BF16), wo_ref[...])
    if glu:
        m = m + bo_ref[...]
        d = o_ref.shape[1]
        m = m[:, :d] * (1.0 / (1.0 + jnp.exp(-m[:, d:])))
    x = _layer_norm(ALPHA * h_ref[...] + m, gm_ref[...], bm_ref[...])
    o_ref[...] = _ffn_rows(x, n_chunks, w1_ref, w3_ref, w2_ref, g_ref, b_ref)


def _ffn_chunks(f, n_chunks):
    return n_chunks or (FFN_CHUNKS if f % (FFN_CHUNKS * LANES) == 0 else 1)


def _ffn(h2, w1, w3, w2, g, b, idx, tm=None, n_chunks=None, name="ffn"):
    n, d = h2.shape
    tm = tm or _pick_tile(n, FFN_ROW_TILES)
    (out,) = _row_call(functools.partial(_ffn_body, _ffn_chunks(w1.shape[-1], n_chunks)), [h2],
                       [(w1, idx), (w3, idx), (w2, idx), g.reshape(1, d), b.reshape(1, d)],
                       [(d, F32)], tm, name)
    return out


def _mix_ffn(y2, h2, w_out, b_out, g_mix, b_mix, w1, w3, w2, g, b, idx, tm=None, n_chunks=None, name="mix_ffn"):
    n, d = h2.shape
    tm = tm or _pick_tile(n, FFN_ROW_TILES)
    glu = b_out is not None
    wo = w_out.astype(BF16)
    bo = (b_out if glu else jnp.zeros((wo.shape[1],), F32)).reshape(1, wo.shape[1])
    (out,) = _row_call(functools.partial(_mix_ffn_body, _ffn_chunks(w1.shape[-1], n_chunks), glu), [y2, h2],
                       [wo, bo, g_mix.reshape(1, d), b_mix.reshape(1, d),
                        (w1, idx), (w3, idx), (w2, idx), g.reshape(1, d), b.reshape(1, d)],
                       [(d, F32)], tm, name)
    return out


def _ssd_in_body(h_ref, w_ref, xbc_ref, z_ref, dt_ref):
    r = _dot(h_ref[...].astype(BF16), w_ref[...])
    nx, nz = xbc_ref.shape[1], z_ref.shape[1]
    xbc_ref[...] = r[:, :nx]
    z_ref[...] = r[:, nx:nx + nz]
    dt_ref[...] = r[:, nx + nz:]


def _softplus(x):
    return jnp.maximum(x, 0.0) + jnp.log1p(jnp.exp(-jnp.abs(x)))


def _ssd_body(xbc_ref, dt_ref, z_ref, convw_ref, convb_ref, dtb_ref, alog_ref, dskip_ref, normg_ref,
              y_ref, buf_scr, state_scr, y_scr):
    t = SSD_CHUNK
    d_inner = SSD_HEADS * SSD_HEAD_DIM
    gn = SSD_GROUPS * SSD_STATE
    hp = SSD_HEAD_DIM
    r = SSD_HEADS // SSD_GROUPS

    @pl.when(pl.program_id(1) == 0)
    def _():
        buf_scr[0:SUBLANES, :] = jnp.zeros((SUBLANES, buf_scr.shape[1]), F32)
        state_scr[...] = jnp.zeros(state_scr.shape, F32)

    x = xbc_ref[...]
    buf_scr[SUBLANES:SUBLANES + t, :] = x
    acc = convb_ref[...] + convw_ref[SSD_CONV - 1:SSD_CONV, :] * x
    for j in range(1, SSD_CONV):
        acc = acc + convw_ref[SSD_CONV - 1 - j:SSD_CONV - j, :] * buf_scr[SUBLANES - j:SUBLANES - j + t, :]
    buf_scr[0:SUBLANES, :] = x[t - SUBLANES:t, :]
    xbc = _silu(acc)

    dt = _softplus(dt_ref[...] + dtb_ref[...])
    a = dt * (-jnp.exp(alog_ref[...]))
    rows = lax.broadcasted_iota(jnp.int32, (t, t), 0)
    cols = lax.broadcasted_iota(jnp.int32, (t, t), 1)
    tri = rows >= cols
    a_cum = jnp.dot(tri.astype(F32), a, precision=lax.Precision.HIGHEST, preferred_element_type=F32)
    a_last = a_cum[t - 1:t, :]
    a_cum_t = a_cum.T
    dt_t = dt.T
    w_end_t = (dt * jnp.exp(a_last - a_cum)).T
    chunk_dec = jnp.exp(a_last)

    for g in range(SSD_GROUPS):
        b_f = xbc[:, d_inner + g * SSD_STATE:d_inner + (g + 1) * SSD_STATE]
        c_g = xbc[:, d_inner + gn + g * SSD_STATE:d_inner + gn + (g + 1) * SSD_STATE].astype(BF16)
        b_g = b_f.astype(BF16)
        b_ft = b_f.T.astype(BF16)
        cb = _dot_nt(c_g, b_g).astype(BF16)
        y_off_g = _dot(c_g, state_scr[:, g * r * hp:(g + 1) * r * hp].astype(BF16))
        for k in range(r):
            h = g * r + k
            col = jnp.broadcast_to(a_cum[:, h:h + 1], (t, t))
            decay = jnp.exp(jnp.where(tri, col - a_cum_t[h:h + 1, :], NEG_INF))
            xh = xbc[:, h * hp:(h + 1) * hp]
            xb = xh.astype(BF16)
            y_diag = _dot(cb * decay.astype(BF16) * dt_t[h:h + 1, :].astype(BF16), xb)
            y_off = y_off_g[:, k * hp:(k + 1) * hp] * jnp.exp(col[:, :hp])
            y_scr[:, h * hp:(h + 1) * hp] = y_diag + y_off + dskip_ref[:, h * hp:(h + 1) * hp] * xh
            st = state_scr[:, h * hp:(h + 1) * hp]
            b_w = b_ft * w_end_t[h:h + 1, :].astype(BF16)
            state_scr[:, h * hp:(h + 1) * hp] = st * chunk_dec[:, h:h + 1] + _dot(b_w, xb)

    y = y_scr[...] * _silu(z_ref[...])
    gs = d_inner // SSD_GROUPS
    for g in range(SSD_GROUPS):
        sl = slice(g * gs, (g + 1) * gs)
        y_ref[:, sl] = _rms(y[:, sl], normg_ref[:, sl]).astype(y_ref.dtype)


def _ssd_mixer(h3, w_in, conv_w, conv_b, dt_bias, a_log, d_skip, norm_g):
    b, lp, d = h3.shape
    n = b * lp
    d_inner = SSD_HEADS * SSD_HEAD_DIM
    conv_ch = d_inner + 2 * SSD_GROUPS * SSD_STATE
    w_z, w_xbc, w_dt = w_in[:, :d_inner], w_in[:, d_inner:d_inner + conv_ch], w_in[:, d_inner + conv_ch:]
    w_dt = jnp.pad(w_dt, ((0, 0), (0, LANES - SSD_HEADS)))
    w_all = jnp.concatenate([w_xbc, w_z, w_dt], axis=1).astype(BF16)
    tm = _pick_tile(n, (512, 384, 256, 128))
    xbc, z, dt = _row_call(_ssd_in_body, [h3.reshape(n, d)], [w_all],
                           [(conv_ch, F32), (d_inner, F32), (LANES, F32)], tm, "ssd_in")
    t = SSD_CHUNK
    pad1 = lambda v: jnp.pad(v, (0, LANES - SSD_HEADS)).reshape(1, LANES)
    consts = [conv_w, conv_b.reshape(1, conv_ch), pad1(dt_bias), pad1(a_log),
              jnp.repeat(d_skip, SSD_HEAD_DIM).reshape(1, d_inner), norm_g.reshape(1, d_inner)]
    row_spec = lambda w: pl.BlockSpec((None, t, w), lambda i, c: (i, c, 0))
    y = pl.pallas_call(
        _ssd_body, grid=(b, lp // t),
        in_specs=[row_spec(conv_ch), row_spec(LANES), row_spec(d_inner)]
        + [pl.BlockSpec(c.shape, lambda i, c_: (0, 0)) for c in consts],
        out_specs=row_spec(d_inner),
        out_shape=jax.ShapeDtypeStruct((b, lp, d_inner), BF16),
        scratch_shapes=[pltpu.VMEM((SUBLANES + t, conv_ch), F32), pltpu.VMEM((SSD_STATE, d_inner), F32),
                        pltpu.VMEM((t, d_inner), F32)],
        name="ssd_scan",
        compiler_params=pltpu.CompilerParams(dimension_semantics=("parallel", "arbitrary"),
                                             vmem_limit_bytes=VMEM_LIMIT),
    )(xbc.reshape(b, lp, conv_ch), dt.reshape(b, lp, LANES), z.reshape(b, lp, d_inner), *consts)
    return y.reshape(n, d_inner)


def _qkv_body(n_q, scale, h_ref, w_ref, qk_ref, vt_ref):
    r = _dot(h_ref[...].astype(BF16), w_ref[...])
    n_qk = qk_ref.shape[1]
    qk_ref[:, :n_q] = (r[:, :n_q] * scale).astype(qk_ref.dtype)
    qk_ref[:, n_q:] = r[:, n_q:n_qk].astype(qk_ref.dtype)
    vt_ref[...] = r[:, n_qk:].T.astype(vt_ref.dtype)


def _causal_mask_pair(s, tq):
    rows = lax.broadcasted_iota(jnp.int32, s.shape, 0)
    cols = lax.broadcasted_iota(jnp.int32, s.shape, 1)
    cols = jnp.where(cols >= tq, cols - tq, cols)
    return jnp.where(rows <= cols, s, NEG_INF)


def _flash_causal(qi, tq, scores, vt_ref, m_scr, l_scr, acc_scr, s_scrs, p_scrs, mx_scrs):
    m_scr[...] = jnp.full(m_scr.shape, NEG_INF, F32)
    l_scr[...] = jnp.zeros(l_scr.shape, F32)
    acc_scr[...] = jnp.zeros(acc_scr.shape, F32)
    p_scrs[1][...] = jnp.zeros(p_scrs[1].shape, p_scrs[1].dtype)

    def softmax(s, s_max, pv):
        m_prev = m_scr[...]
        m_new = jnp.maximum(m_prev, s_max)
        alpha = jnp.exp2(m_prev - m_new)
        p = jnp.exp2(s - m_new)
        l_scr[...] = alpha * l_scr[...] + jnp.sum(p, axis=0, keepdims=True)
        m_scr[...] = m_new
        acc_scr[...] = alpha * (acc_scr[...] + pv)
        return p.astype(BF16)

    def produce(kj, slot):
        s = scores(kj)
        s_scrs[slot][...] = s
        mx_scrs[slot][...] = jnp.max(s, axis=0, keepdims=True)

    def step(kj, cur, nxt):
        pv = _dot(vt_ref[jnp.maximum(kj - 1, 0)], p_scrs[nxt][...])
        p_scrs[cur][...] = softmax(s_scrs[cur][...], mx_scrs[cur][...], pv)
        produce(kj + 1, nxt)

    produce(0, 0)

    def body(t, carry):
        step(2 * t, 0, 1)
        step(2 * t + 1, 1, 0)
        return carry

    lax.fori_loop(0, lax.shift_right_logical(qi, 1), body, 0)
    pl.when((qi & 1) == 1)(functools.partial(step, qi - 1, 0, 1))

    def last(cur, nxt):
        pv = _dot(vt_ref[jnp.maximum(qi - 1, 0)], p_scrs[nxt][...])
        s = _causal_mask_pair(s_scrs[cur][...], tq)
        p = softmax(s, jnp.max(s, axis=0, keepdims=True), pv)
        acc_scr[...] = acc_scr[...] + _dot(vt_ref[qi], p)

    for par in range(2):
        pl.when((qi & 1) == par)(functools.partial(last, par, 1 - par))


def _flash_head(tq, nq, scores, vt_ref, finish, m_all, l_all, acc_all,
                m_scr, l_scr, acc_scr, s_scrs, p_scrs, mx_scrs, exact_ref):
    def init(qi, carry):
        m_all[qi] = jnp.max(scores(0, qi, ATTN_REF_KEYS), axis=0, keepdims=True)
        l_all[qi] = jnp.zeros(l_all.shape[1:], F32)
        acc_all[qi] = jnp.zeros(acc_all.shape[1:], F32)
        return carry

    lax.fori_loop(0, nq, init, 0, unroll=ATTN_EDGE_UNROLL)

    def probs(kj, qi, slot):
        p = jnp.exp2(scores(kj, qi) - m_all[qi])
        l_all[qi] += jnp.sum(p, axis=0, keepdims=True)
        p_scrs[slot][...] = p.astype(BF16)

    def step(prev, cur, prv):
        kjp, qip = prev
        wrap = qip + 1 >= nq
        kj = jnp.where(wrap, kjp + 1, kjp)
        qi = jnp.where(wrap, kjp + 1, qip + 1)
        p = jnp.exp2(scores(kj, qi) - m_all[qi])
        acc_all[qip] += _dot(vt_ref[kjp], p_scrs[prv][...])
        l_all[qi] += jnp.sum(p, axis=0, keepdims=True)
        p_scrs[cur][...] = p.astype(BF16)
        return kj, qi

    n_steps = nq * (nq + 1) // 2 - 1
    probs(0, 0, 0)

    def body(t, prev):
        return step(step(prev, 1, 0), 0, 1)

    prev = lax.fori_loop(0, n_steps // 2, body, (jnp.int32(0), jnp.int32(0)), unroll=ATTN_PAIR_UNROLL)
    if n_steps % 2:
        prev = step(prev, 1, 0)
    acc_all[prev[1]] += _dot(vt_ref[prev[0]], p_scrs[n_steps % 2][...])

    ok = jnp.all(l_all[...] < SOFTMAX_SUM_LIMIT) & jnp.all(jnp.abs(acc_all[...]) < F32_FINITE_LIMIT)
    exact_ref[0] = jnp.where(ok, 0, 1)

    @pl.when(exact_ref[0] == 0)
    def _():
        def fin(qi, carry):
            finish(qi, l_all[qi], acc_all[qi])
            return carry

        lax.fori_loop(0, nq, fin, 0, unroll=ATTN_EDGE_UNROLL)

    @pl.when(exact_ref[0] == 1)
    def _():
        def redo(qi, carry):
            _flash_causal(qi, tq, lambda kj: scores(kj, qi), vt_ref, m_scr, l_scr, acc_scr, s_scrs, p_scrs, mx_scrs)
            finish(qi, l_scr[...], acc_scr[...])
            return carry

        lax.fori_loop(0, nq, redo, 0)


def _flash_scratch(e, tq, nq):
    w = 2 * tq
    return [pltpu.VMEM((nq, 1, w), F32), pltpu.VMEM((nq, 1, w), F32), pltpu.VMEM((nq, e, w), F32),
            pltpu.VMEM((1, w), F32), pltpu.VMEM((1, w), F32), pltpu.VMEM((e, w), F32),
            pltpu.VMEM((tq, w), F32), pltpu.VMEM((tq, w), F32), pltpu.VMEM((tq, w), BF16), pltpu.VMEM((tq, w), BF16),
            pltpu.VMEM((1, w), F32), pltpu.VMEM((1, w), F32), pltpu.SMEM((1,), jnp.int32)]


def _t5_bucket(dist):
    max_exact = REL_BUCKETS // 2
    d = jnp.maximum(dist, 0)
    df = jnp.maximum(d, max_exact).astype(F32)
    large = max_exact + (jnp.log(df / max_exact) / math.log(REL_MAX_DIST / max_exact)
                         * (REL_BUCKETS - max_exact)).astype(jnp.int32)
    large = jnp.minimum(large, REL_BUCKETS - 1)
    return jnp.where(d < max_exact, d, large)


def _diff_attn_body(tq, nq, lam_init, q_ref, k_ref, vt_ref, bias_ref, lam_ref, g_ref, o_ref,
                    m_all, l_all, acc_all, m_scr, l_scr, acc_scr, s0_scr, s1_scr, p0_scr, p1_scr, mx0_scr, mx1_scr,
                    exact_ref, qcat_all):
    lane = lax.broadcasted_iota(jnp.int32, (tq, q_ref.shape[1]), 1)

    def prep(qi, carry):
        q = q_ref[pl.ds(pl.multiple_of(qi * tq, tq), tq), :]
        zero = jnp.zeros_like(q)
        qcat_all[qi, 0:tq, :] = jnp.where(lane < DIFF_HEAD_DIM, q, zero)
        qcat_all[qi, tq:2 * tq, :] = jnp.where(lane >= DIFF_HEAD_DIM, q, zero)
        return carry

    lax.fori_loop(0, nq, prep, 0)

    def scores(kj, qi, rows=tq):
        start = pl.multiple_of(kj * tq, tq)
        bias = bias_ref[jnp.minimum(qi - kj, 2), 0:rows, :]
        return _dot_nt(k_ref[pl.ds(start, rows), :], qcat_all[qi]) + jnp.concatenate([bias, bias], axis=1)

    def finish(qi, l, acc):
        a = acc * (1.0 / l)
        o_t = a[:, :tq] - lam_ref[...] * a[:, tq:]
        o_ref[pl.ds(pl.multiple_of(qi * tq, tq), tq), :] = (
            _rms(o_t.T, g_ref[...]) * (1.0 - lam_init)).astype(o_ref.dtype)

    _flash_head(tq, nq, scores, vt_ref, finish, m_all, l_all, acc_all, m_scr, l_scr, acc_scr,
                (s0_scr, s1_scr), (p0_scr, p1_scr), (mx0_scr, mx1_scr), exact_ref)


def _diff_mixer(h3, w_qkv, lam_q1, lam_k1, lam_q2, lam_k2, subln_g, rel_bias, lam_init):
    b, lp, d = h3.shape
    n = b * lp
    hd = DIFF_HEADS * 2 * DIFF_HEAD_DIM
    tq = _pick_tile(lp, (384, 256, 128))
    assert tq >= REL_MAX_DIST
    nk = lp // tq
    qk, vt = _row_call(functools.partial(_qkv_body, hd, DIFF_HEAD_DIM ** -0.5 * LOG2E), [h3.reshape(n, d)],
                       [w_qkv.astype(BF16)], [(2 * hd, BF16)], tq, "diff_qkv", lp=lp, t_outs=[(hd, BF16)])
    qk = qk.reshape(b, lp, 2 * hd)
    m2 = 2 * tq
    kk = jnp.arange(m2)
    dist = jnp.where(kk < tq, kk, kk - m2)[None, :] + (jnp.arange(3) * tq)[:, None]
    tab = jnp.transpose(rel_bias[_t5_bucket(dist)].astype(F32) * LOG2E, (2, 0, 1))
    tab = jnp.where((jnp.arange(3)[:, None] == 0) & (kk[None, :] >= tq), NEG_INF, tab)
    nh = DIFF_HEADS
    bias = jnp.tile(tab, (1, 1, tq))[..., :tq * (m2 - 1)].reshape(nh, 3, tq, m2 - 1)[..., :tq]
    lam = (jnp.exp(jnp.sum(lam_q1.astype(F32) * lam_k1.astype(F32)))
           - jnp.exp(jnp.sum(lam_q2.astype(F32) * lam_k2.astype(F32))) + lam_init)
    w = 2 * DIFF_HEAD_DIM
    o = pl.pallas_call(
        functools.partial(_diff_attn_body, tq, nk, lam_init),
        grid=(b, nh),
        in_specs=[pl.BlockSpec((None, lp, w), lambda bi, hi: (bi, 0, hi)),
                  pl.BlockSpec((None, lp, w), lambda bi, hi: (bi, 0, nh + hi)),
                  pl.BlockSpec((None, nk, DIFF_V_DIM, tq), lambda bi, hi: (bi, 0, hi, 0)),
                  pl.BlockSpec((None, 3, tq, tq), lambda bi, hi: (hi, 0, 0, 0)),
                  pl.BlockSpec((1, 1), lambda bi, hi: (0, 0)),
                  pl.BlockSpec((1, DIFF_V_DIM), lambda bi, hi: (0, 0))],
        out_specs=pl.BlockSpec((None, lp, DIFF_V_DIM), lambda bi, hi: (bi, 0, hi)),
        out_shape=jax.ShapeDtypeStruct((b, lp, nh * DIFF_V_DIM), BF16),
        scratch_shapes=_flash_scratch(DIFF_V_DIM, tq, nk) + [pltpu.VMEM((nk, 2 * tq, w), BF16)],
        name="diff_attn",
        compiler_params=pltpu.CompilerParams(dimension_semantics=("parallel", "parallel"),
                                             vmem_limit_bytes=VMEM_LIMIT),
    )(qk, qk, vt, bias, lam.reshape(1, 1), subln_g.reshape(1, DIFF_V_DIM))
    return o.reshape(n, nh * DIFF_V_DIM)


def _gelu_tanh(x):
    return 0.5 * x * (1.0 + jnp.tanh(math.sqrt(2.0 / math.pi) * (x + 0.044715 * (x * x * x))))


def _s5_body(bp, x_ref, m_ref, bc_ref, ccre_ref, ccim_ref, are_ref, aim_ref, y_ref, sre_scr, sim_scr):
    rows = x_ref.shape[0]
    gb = m_ref.shape[0]
    kw = m_ref.shape[1]
    p = ccre_ref.shape[1]
    for g in range(gb):
        e = _dot(x_ref[:, g * kw:(g + 1) * kw], bc_ref[g])
        sre_scr[:, g * p:(g + 1) * p] = e[:, :p]
        sim_scr[:, g * p:(g + 1) * p] = e[:, p:]

    ar, ai = are_ref[...], aim_ref[...]
    sub = lax.broadcasted_iota(jnp.int32, ar.shape, 0)
    per_tile = S5_SUBLANES // bp

    def tile_step(r, carry):
        pr, pi = carry
        r0 = pl.multiple_of(r * S5_SUBLANES, S5_SUBLANES)
        er, ei = sre_scr[pl.ds(r0, S5_SUBLANES), :], sim_scr[pl.ds(r0, S5_SUBLANES), :]
        in_r, in_i = pr, pi
        for c in range(per_tile):
            nr = ar * pr - ai * pi + er
            ni = ar * pi + ai * pr + ei
            if per_tile > 1:
                nr = pltpu.roll(nr, bp, axis=0)
                ni = pltpu.roll(ni, bp, axis=0)
            if c + 1 < per_tile:
                sel = (sub >= (c + 1) * bp) & (sub < (c + 2) * bp)
                in_r, in_i = jnp.where(sel, nr, in_r), jnp.where(sel, ni, in_i)
                pr, pi = nr, ni
            else:
                pr, pi = nr, ni
        if per_tile > 1:
            pr = jnp.where(sub < bp, pr, pltpu.roll(pr, bp, axis=0))
            pi = jnp.where(sub < bp, pi, pltpu.roll(pi, bp, axis=0))
        sre_scr[pl.ds(r0, S5_SUBLANES), :] = in_r
        sim_scr[pl.ds(r0, S5_SUBLANES), :] = in_i
        return pr, pi

    zero = jnp.zeros(ar.shape, F32)
    lax.fori_loop(0, rows // S5_SUBLANES, tile_step, (zero, zero), unroll=S5_UNROLL)

    for g in range(gb):
        xg = x_ref[:, g * kw:(g + 1) * kw]
        y = (_dot(xg, m_ref[g]) + _dot(sre_scr[:, g * p:(g + 1) * p].astype(BF16), ccre_ref[g])
             + _dot(sim_scr[:, g * p:(g + 1) * p].astype(BF16), ccim_ref[g]))
        y_ref[:, g * kw:(g + 1) * kw] = _gelu_tanh(y).astype(y_ref.dtype)


def _s5_pack_body(n_batch, h_ref, perm_ref, x_ref):
    t_len = h_ref.shape[1]
    xcat = jnp.concatenate([h_ref[:, t, :].astype(BF16) for t in range(t_len)], axis=1)
    x = _dot(xcat, perm_ref[...]).astype(x_ref.dtype)
    x_ref[...] = jnp.where(pl.program_id(0) < n_batch, x, jnp.zeros_like(x))


def _s5_unpack_body(y_ref, perm_ref, o_ref):
    t_len = o_ref.shape[1]
    res = _dot(y_ref[...], perm_ref[...])
    for t in range(t_len):
        o_ref[:, t, :] = res[:, t * LANES:(t + 1) * LANES]


def _s5_chunk_operators(lam_re, lam_im, log_step, b_re, b_im, c_re, c_im, d_skip):
    hi = lax.Precision.HIGHEST
    t_len = S5_CHUNK
    step = jnp.exp(log_step.astype(F32))[:, None]
    lr = jnp.minimum(lam_re.astype(F32), -1e-4)
    li = lam_im.astype(F32)
    mag = jnp.exp(lr * step)
    ar, ai = mag * jnp.cos(li * step), mag * jnp.sin(li * step)
    den = lr * lr + li * li
    cr = ((ar - 1.0) * lr + ai * li) / den
    ci = (ai * lr - (ar - 1.0) * li) / den
    br, bi = b_re.astype(F32), b_im.astype(F32)
    bbr = cr[..., None] * br - ci[..., None] * bi
    bbi = cr[..., None] * bi + ci[..., None] * br
    k = jnp.arange(t_len + 1, dtype=F32)[:, None, None]
    pmag = jnp.exp(k * (lr * step))
    pr, pi = pmag * jnp.cos(k * (li * step)), pmag * jnp.sin(k * (li * step))
    abr = pr[:t_len, ..., None] * bbr - pi[:t_len, ..., None] * bbi
    abi = pr[:t_len, ..., None] * bbi + pi[:t_len, ..., None] * bbr
    cre, cim = c_re.astype(F32), c_im.astype(F32)
    kern = (jnp.einsum('gop,kgpc->kgoc', cre, abr, precision=hi)
            - jnp.einsum('gop,kgpc->kgoc', cim, abi, precision=hi))
    kern = kern.at[0].add(jax.vmap(jnp.diag)(d_skip.astype(F32)))
    sig = jnp.arange(t_len)
    lag = sig[None, :] - sig[:, None]
    m = jnp.where((lag >= 0)[..., None, None, None], kern[jnp.maximum(lag, 0)], 0.0)
    g, o, c = kern.shape[1:]
    m = jnp.transpose(m, (2, 0, 4, 1, 3)).reshape(g, t_len * c, t_len * o)
    rev = t_len - 1 - sig
    p = bbr.shape[1]
    bc = jnp.concatenate([jnp.transpose(abr[rev], (1, 0, 3, 2)).reshape(g, t_len * c, p),
                          jnp.transpose(abi[rev], (1, 0, 3, 2)).reshape(g, t_len * c, p)], axis=2)
    car = cre[None] * pr[1:, :, None, :] - cim[None] * pi[1:, :, None, :]
    cai = cre[None] * pi[1:, :, None, :] + cim[None] * pr[1:, :, None, :]
    cc_re = jnp.transpose(car, (1, 3, 0, 2)).reshape(g, p, t_len * o)
    cc_im = -jnp.transpose(cai, (1, 3, 0, 2)).reshape(g, p, t_len * o)
    return (m.astype(BF16), bc.astype(BF16), cc_re.astype(BF16), cc_im.astype(BF16),
            pr[t_len].reshape(g * p), pi[t_len].reshape(g * p))


def _s5_mixer(h3, lam_re, lam_im, log_step, b_re, b_im, c_re, c_im, d_skip):
    b, lp, d = h3.shape
    t_len, g, c = S5_CHUNK, S5_GROUPS, S5_GROUP
    assert lp % t_len == 0 and b <= S5_SUBLANES
    bp = 4 if b <= 4 else S5_SUBLANES
    nj = lp // t_len
    m, bc, cc_re, cc_im, at_re, at_im = _s5_chunk_operators(lam_re, lam_im, log_step, b_re, b_im, c_re, c_im, d_skip)
    n_state = g * S5_STATE
    at_re = jnp.broadcast_to(at_re.reshape(1, n_state), (S5_SUBLANES, n_state))
    at_im = jnp.broadcast_to(at_im.reshape(1, n_state), (S5_SUBLANES, n_state))
    rows = nj * bp
    gb = S5_GROUP_BLOCK
    kw = t_len * c
    assert gb * c == LANES
    n_blk = g // gb
    src = jnp.arange(gb * kw)
    tau, r, ch = src // (gb * c), (src // c) % gb, src % c
    perm = jax.nn.one_hot(r * kw + tau * c + ch, gb * kw, dtype=BF16)
    x = pl.pallas_call(
        functools.partial(_s5_pack_body, b),
        grid=(bp, n_blk),
        in_specs=[pl.BlockSpec((None, nj, t_len, LANES), lambda bi, mi: (jnp.minimum(bi, b - 1), 0, 0, mi)),
                  pl.BlockSpec(perm.shape, lambda bi, mi: (0, 0))],
        out_specs=pl.BlockSpec((nj, gb * kw), lambda bi, mi: (0, bi * n_blk + mi)),
        out_shape=jax.ShapeDtypeStruct((nj, bp * g * kw), BF16),
        name="s5_pack",
        compiler_params=pltpu.CompilerParams(dimension_semantics=("parallel", "parallel"),
                                             vmem_limit_bytes=VMEM_LIMIT),
    )(h3.reshape(b, nj, t_len, d), perm).reshape(rows, g * kw)
    y = pl.pallas_call(
        functools.partial(_s5_body, bp), grid=(g // gb,),
        in_specs=[pl.BlockSpec((rows, gb * kw), lambda i: (0, i)),
                  pl.BlockSpec((gb, kw, kw), lambda i: (i, 0, 0)),
                  pl.BlockSpec((gb, kw, 2 * S5_STATE), lambda i: (i, 0, 0)),
                  pl.BlockSpec((gb, S5_STATE, kw), lambda i: (i, 0, 0)),
                  pl.BlockSpec((gb, S5_STATE, kw), lambda i: (i, 0, 0)),
                  pl.BlockSpec((S5_SUBLANES, gb * S5_STATE), lambda i: (0, i)),
                  pl.BlockSpec((S5_SUBLANES, gb * S5_STATE), lambda i: (0, i))],
        out_specs=pl.BlockSpec((rows, gb * kw), lambda i: (0, i)),
        out_shape=jax.ShapeDtypeStruct((rows, g * kw), BF16),
        scratch_shapes=[pltpu.VMEM((rows, gb * S5_STATE), F32), pltpu.VMEM((rows, gb * S5_STATE), F32)],
        name="s5_chunks",
        compiler_params=pltpu.CompilerParams(dimension_semantics=("parallel",), vmem_limit_bytes=VMEM_LIMIT),
    )(x, m, bc, cc_re, cc_im, at_re, at_im)
    y = pl.pallas_call(
        _s5_unpack_body,
        grid=(b, n_blk),
        in_specs=[pl.BlockSpec((nj, gb * kw), lambda bi, mi: (0, bi * n_blk + mi)),
                  pl.BlockSpec(perm.shape, lambda bi, mi: (0, 0))],
        out_specs=pl.BlockSpec((None, nj, t_len, LANES), lambda bi, mi: (bi, 0, 0, mi)),
        out_shape=jax.ShapeDtypeStruct((b, nj, t_len, d), F32),
        name="s5_unpack",
        compiler_params=pltpu.CompilerParams(dimension_semantics=("parallel", "parallel"),
                                             vmem_limit_bytes=VMEM_LIMIT),
    )(y.reshape(nj, bp * g * kw), perm.T)
    return y.reshape(b * lp, d)


def _rope_block(x, cm, s1, s2):
    return (x * cm + pltpu.roll(x, LANES - MLA_ROPE // 2, axis=1) * s1 + pltpu.roll(x, MLA_ROPE // 2, axis=1) * s2)


def _mla_in_body(h_ref, qcm_ref, qs1_ref, qs2_ref, kcm_ref, ks1_ref, ks2_ref,
                 win_ref, qg_ref, kvg_ref, wuq_ref, wuk_ref, wuv_ref, q_ref, k_ref, vt_ref):
    c = _dot(h_ref[...].astype(BF16), win_ref[...])
    c_q = _rms(c[:, :MLA_Q_RANK], qg_ref[...]).astype(BF16)
    c_kv = _rms(c[:, MLA_Q_RANK:MLA_Q_RANK + MLA_KV_RANK], kvg_ref[...]).astype(BF16)
    k_r = _rope_block(c[:, MLA_Q_RANK + MLA_KV_RANK:], kcm_ref[...], ks1_ref[...], ks2_ref[...])
    q = _dot(c_q, wuq_ref[...])
    k = _dot(c_kv, wuk_ref[...])
    qcm, qs1, qs2 = qcm_ref[...], qs1_ref[...], qs2_ref[...]
    for h in range(MLA_HEADS):
        sl = slice(h * LANES, (h + 1) * LANES)
        q_ref[:, sl] = _rope_block(q[:, sl], qcm, qs1, qs2).astype(q_ref.dtype)
        k_ref[:, sl] = (k[:, sl] + k_r).astype(k_ref.dtype)
    vt_ref[...] = _dot(c_kv, wuv_ref[...]).T.astype(vt_ref.dtype)


def _mla_attn_body(tq, nq, q_ref, k_ref, vt_ref, mask_ref, o_ref,
                   m_all, l_all, acc_all, m_scr, l_scr, acc_scr, s0_scr, s1_scr, p0_scr, p1_scr, mx0_scr, mx1_scr,
                   exact_ref):
    def scores(kj, qi, rows=tq):
        k0 = pl.multiple_of(kj * tq, tq)
        q0 = pl.multiple_of(qi * tq, tq)
        mask = mask_ref[jnp.minimum(qi - kj, 1), 0:rows, :]
        return jnp.concatenate(
            [_dot_nt(k_ref[pl.ds(k0, rows), hh * LANES:(hh + 1) * LANES],
                     q_ref[pl.ds(q0, tq), hh * LANES:(hh + 1) * LANES]) + mask for hh in range(2)],
            axis=1)

    row = lax.broadcasted_iota(jnp.int32, (2 * MLA_V, tq), 0)

    def finish(qi, l, acc):
        a = acc * (1.0 / l)
        o_t = jnp.where(row < MLA_V, a[:, :tq], a[:, tq:])
        o_ref[pl.ds(pl.multiple_of(qi * tq, tq), tq), :] = o_t.T.astype(o_ref.dtype)

    _flash_head(tq, nq, scores, vt_ref, finish, m_all, l_all, acc_all, m_scr, l_scr, acc_scr,
                (s0_scr, s1_scr), (p0_scr, p1_scr), (mx0_scr, mx1_scr), exact_ref)


def _mla_mixer(h3, w_in, q_norm_g, kv_norm_g, w_uq, w_ukv):
    b, lp, d = h3.shape
    n = b * lp
    nh = MLA_HEADS
    qk = MLA_NOPE + MLA_ROPE
    half = MLA_ROPE // 2
    w_kr = jnp.pad(w_in[:, MLA_Q_RANK + MLA_KV_RANK:], ((0, 0), (MLA_NOPE, LANES - qk)))
    w_in_all = jnp.concatenate([w_in[:, :MLA_Q_RANK + MLA_KV_RANK], w_kr], axis=1).astype(BF16)
    w_uq_p = jnp.pad(w_uq.reshape(MLA_Q_RANK, nh, qk), ((0, 0), (0, 0), (0, LANES - qk)))
    w_uq_p = w_uq_p.reshape(MLA_Q_RANK, nh * LANES).astype(BF16)
    w_ukv3 = w_ukv.reshape(MLA_KV_RANK, nh, MLA_NOPE + MLA_V)
    w_uk_p = jnp.pad(w_ukv3[:, :, :MLA_NOPE], ((0, 0), (0, 0), (0, LANES - MLA_NOPE)))
    w_uk_p = w_uk_p.reshape(MLA_KV_RANK, nh * LANES).astype(BF16)
    w_uv = w_ukv3[:, :, MLA_NOPE:].reshape(MLA_KV_RANK, nh * MLA_V).astype(BF16)
    pos = jnp.arange(lp, dtype=F32)
    inv_freq = ROPE_BASE ** (-jnp.arange(0, MLA_ROPE, 2, dtype=F32) / MLA_ROPE)
    ang = pos[:, None] * inv_freq[None, :]
    cos, sin = jnp.cos(ang), jnp.sin(ang)
    z = lambda w: jnp.zeros((lp, w), F32)
    cm = jnp.concatenate([jnp.ones((lp, MLA_NOPE), F32), cos, cos, z(LANES - qk)], axis=1)
    s1 = jnp.concatenate([z(MLA_NOPE), -sin, z(LANES - MLA_NOPE - half)], axis=1)
    s2 = jnp.concatenate([z(MLA_NOPE + half), sin, z(LANES - qk)], axis=1)
    scale = qk ** -0.5 * LOG2E
    tq = _pick_tile(lp, (384, 256, 128))
    nk = lp // tq
    q, k, vt = _row_call(
        _mla_in_body, [h3.reshape(n, d)],
        [w_in_all, q_norm_g.reshape(1, MLA_Q_RANK), kv_norm_g.reshape(1, MLA_KV_RANK), w_uq_p, w_uk_p, w_uv],
        [(nh * LANES, BF16), (nh * LANES, BF16)], tq, "mla_in",
        pos_ins=[cm * scale, s1 * scale, s2 * scale, cm, s1, s2], lp=lp, t_outs=[(nh * MLA_V, BF16)])
    q, k = q.reshape(b, lp, nh * LANES), k.reshape(b, lp, nh * LANES)
    ti = jnp.arange(tq)
    mask = jnp.stack([jnp.where(ti[:, None] <= ti[None, :], 0.0, NEG_INF), jnp.zeros((tq, tq))]).astype(F32)
    o = pl.pallas_call(
        functools.partial(_mla_attn_body, tq, nk),
        grid=(b, nh // 2),
        in_specs=[pl.BlockSpec((None, lp, 2 * LANES), lambda bi, hi: (bi, 0, hi)),
                  pl.BlockSpec((None, lp, 2 * LANES), lambda bi, hi: (bi, 0, hi)),
                  pl.BlockSpec((None, nk, 2 * MLA_V, tq), lambda bi, hi: (bi, 0, hi, 0)),
                  pl.BlockSpec((2, tq, tq), lambda bi, hi: (0, 0, 0))],
        out_specs=pl.BlockSpec((None, lp, 2 * MLA_V), lambda bi, hi: (bi, 0, hi)),
        out_shape=jax.ShapeDtypeStruct((b, lp, nh * MLA_V), BF16),
        scratch_shapes=_flash_scratch(2 * MLA_V, tq, nk),
        name="mla_attn",
        compiler_params=pltpu.CompilerParams(dimension_semantics=("parallel", "parallel"),
                                             vmem_limit_bytes=VMEM_LIMIT),
    )(q, k, vt, mask)
    return o.reshape(n, nh * MLA_V)


def kernel(x, meta, rel_bias, ln_g, ln_b, ffn_w1, ffn_w3, ffn_w2, ssd_w_in, ssd_conv_w, ssd_conv_b, ssd_dt_bias, ssd_a_log, ssd_d, ssd_norm_g, ssd_w_out, diff_w_qkv, diff_lam_q1, diff_lam_k1, diff_lam_q2, diff_lam_k2, diff_subln_g, diff_w_out, s5_lam_re, s5_lam_im, s5_log_step, s5_b_re, s5_b_im, s5_c_re, s5_c_im, s5_d, s5_w_glu, s5_b_glu, mla_w_in, mla_q_norm_g, mla_kv_norm_g, mla_w_uq, mla_w_ukv, mla_w_out):
    b, seq, d = x.shape
    l = seq + N_META
    lp = -(-l // SEQ_ALIGN) * SEQ_ALIGN
    n = b * lp
    h = jnp.concatenate([jnp.broadcast_to(meta[None].astype(x.dtype), (b, N_META, d)), x,
                         jnp.zeros((b, lp - l, d), x.dtype)], axis=1).reshape(n, d)
    w1b, w3b, w2b = ffn_w1.astype(BF16), ffn_w3.astype(BF16), ffn_w2.astype(BF16)
    for i in range(DEPTH):
        kind, j = i % 4, i // 4
        h = _ffn(h, w1b, w3b, w2b, ln_g[i, 0], ln_b[i, 0], (i, 0))
        h3 = h.reshape(b, lp, d)
        if kind == 0:
            y = _ssd_mixer(h3, ssd_w_in[j], ssd_conv_w[j], ssd_conv_b[j], ssd_dt_bias[j], ssd_a_log[j],
                           ssd_d[j], ssd_norm_g[j])
            w_out, b_out = ssd_w_out[j], None
        elif kind == 1:
            lam_init = 0.8 - 0.6 * math.exp(-0.3 * i)
            y = _diff_mixer(h3, diff_w_qkv[j], diff_lam_q1[j], diff_lam_k1[j], diff_lam_q2[j], diff_lam_k2[j],
                            diff_subln_g[j], rel_bias, lam_init)
            w_out, b_out = diff_w_out[j], None
        elif kind == 2:
            y = _s5_mixer(h3, s5_lam_re[j], s5_lam_im[j], s5_log_step[j], s5_b_re[j], s5_b_im[j], s5_c_re[j],
                          s5_c_im[j], s5_d[j])
            w_out, b_out = s5_w_glu[j], s5_b_glu[j]
        else:
            y = _mla_mixer(h3, mla_w_in[j], mla_q_norm_g[j], mla_kv_norm_g[j], mla_w_uq[j], mla_w_ukv[j])
            w_out, b_out = mla_w_out[j], None
        h = _mix_ffn(y, h, w_out, b_out, ln_g[i, 1], ln_b[i, 1], w1b, w3b, w2b, ln_g[i, 2], ln_b[i, 2], (i, 1))
    return h.reshape(b, lp, d)[:, N_META:l]
```
